```python
import math
import jax, jax.numpy as jnp
from jax import lax
import numpy as np

D_MODEL = 1024
BATCH = 4
SEQ = 4096
DEPTH = 2

MLA_HEADS = 8
MLA_NOPE = 64
MLA_ROPE = 32
MLA_V = 64
MLA_Q_RANK = 256
MLA_KV_RANK = 128
ROPE_BASE = 10000.0
NSA_HEADS = 8
NSA_KV_HEADS = 2
NSA_DIM = 64
CMP_LEN = 32
CMP_STRIDE = 16
CMP_HIDDEN = 128
SLC_LEN = 64
SLC_TOPK = 16
WINDOW = 512
REL_BUCKETS = 32
REL_MAX_DIST = 128
D_FF = 2816
CONV_WIDTH = 3
PLE_DIM = 256
Q_BLOCK = 128
EPS = 1e-6
NEG = -1e30
POS_BIG = 1e30

IN_SPLITS = (MLA_Q_RANK, MLA_KV_RANK, MLA_ROPE, NSA_HEADS * NSA_DIM,
             NSA_KV_HEADS * NSA_DIM, NSA_KV_HEADS * NSA_DIM,
             NSA_KV_HEADS * NSA_DIM, NSA_KV_HEADS * NSA_DIM,
             NSA_KV_HEADS * NSA_DIM, NSA_KV_HEADS * NSA_DIM,
             3 * NSA_HEADS)
MIX_OUT = MLA_HEADS * MLA_V + NSA_HEADS * NSA_DIM

kernel_name = "hybrid_mla_nsa_convffn_trunk"


def rms_norm(x, g):
    xf = x.astype(jnp.float32)
    y = xf * lax.rsqrt(jnp.mean(xf * xf, axis=-1, keepdims=True) + EPS)
    return (y * g.astype(jnp.float32)).astype(x.dtype)


def rotary(x, cos, sin):
    half = x.shape[-1] // 2
    x1, x2 = x[..., :half], x[..., half:]
    return jnp.concatenate([x1 * cos - x2 * sin, x2 * cos + x1 * sin], axis=-1).astype(x.dtype)


def rel_bucket(dist):
    n = jnp.maximum(dist, 0)
    max_exact = REL_BUCKETS // 2
    large = max_exact + (jnp.log(jnp.maximum(n, 1).astype(jnp.float32) / max_exact)
                         / math.log(REL_MAX_DIST / max_exact)
                         * (REL_BUCKETS - max_exact)).astype(jnp.int32)
    large = jnp.minimum(large, REL_BUCKETS - 1)
    return jnp.where(n < max_exact, n, large)


def mla_attention(c_q, c_kv, k_rope, positions, q_norm, w_uq, kv_norm, w_ukv):
    B, S, _ = c_q.shape
    f32 = jnp.float32
    q = (rms_norm(c_q, q_norm) @ w_uq).reshape(B, S, MLA_HEADS, MLA_NOPE + MLA_ROPE)
    kv = (rms_norm(c_kv, kv_norm) @ w_ukv).reshape(B, S, MLA_HEADS, MLA_NOPE + MLA_V)
    half = MLA_ROPE // 2
    inv = ROPE_BASE ** (-jnp.arange(half, dtype=f32) / half)
    ang = positions.astype(f32)[..., None] * inv
    cos, sin = jnp.cos(ang), jnp.sin(ang)
    q = jnp.concatenate([q[..., :MLA_NOPE],
                         rotary(q[..., MLA_NOPE:], cos[:, :, None, :], sin[:, :, None, :])], axis=-1)
    k_r = rotary(k_rope, cos, sin)
    k = jnp.concatenate([kv[..., :MLA_NOPE],
                         jnp.broadcast_to(k_r[:, :, None, :], (B, S, MLA_HEADS, MLA_ROPE))], axis=-1)
    v = kv[..., MLA_NOPE:]
    q, k, v = (t.transpose(0, 2, 1, 3) for t in (q, k, v))
    scale = (MLA_NOPE + MLA_ROPE) ** -0.5
    kpos = jnp.arange(S)

    def block(i):
        q0 = i * Q_BLOCK
        qb = lax.dynamic_slice_in_dim(q, q0, Q_BLOCK, axis=2)
        s = jnp.einsum('bhqd,bhkd->bhqk', qb, k, preferred_element_type=f32) * scale
        qpos = q0 + jnp.arange(Q_BLOCK)
        s = jnp.where(kpos[None, :] <= qpos[:, None], s, NEG)
        pr = jax.nn.softmax(s, axis=-1)
        return jnp.einsum('bhqk,bhkd->bhqd', pr.astype(v.dtype), v)

    o = lax.map(block, jnp.arange(S // Q_BLOCK))
    return o.transpose(1, 0, 3, 2, 4).reshape(B, S, MLA_HEADS * MLA_V)


def compress_blocks(x, tok, pos_emb, w1, w2):
    B, H = x.shape[0], x.shape[1]
    blk = x[:, :, tok, :] + pos_emb
    blk = blk.reshape(B, H, tok.shape[0], CMP_LEN * NSA_DIM)
    return jax.nn.gelu(blk @ w1, approximate=True) @ w2


def nsa_attention(q, k_cmp, v_cmp, k_slc, v_slc, k_win, v_win, gate_logits,
                  rel_bias, cmp_pos, cmp_w1, cmp_w2):
    B, S, _ = q.shape
    f32 = jnp.float32
    HK = NSA_KV_HEADS
    G = NSA_HEADS // NSA_KV_HEADS
    D = NSA_DIM
    scale = D ** -0.5
    qg = q.reshape(B, S, HK, G, D).transpose(0, 2, 3, 1, 4)

    def kv_heads(t):
        return t.reshape(B, S, HK, D).transpose(0, 2, 1, 3)

    k_cmp, v_cmp, k_slc, v_slc, k_win, v_win = (
        kv_heads(t) for t in (k_cmp, v_cmp, k_slc, v_slc, k_win, v_win))
    gates = jax.nn.sigmoid(gate_logits.reshape(B, S, HK, G, 3).transpose(0, 2, 3, 1, 4))

    n_cmp = (S - CMP_LEN) // CMP_STRIDE + 1
    cmp_start = jnp.arange(n_cmp) * CMP_STRIDE
    tok = cmp_start[:, None] + jnp.arange(CMP_LEN)[None, :]
    kc = compress_blocks(k_cmp, tok, cmp_pos[0], cmp_w1[0], cmp_w2[0])
    vc = compress_blocks(v_cmp, tok, cmp_pos[1], cmp_w1[1], cmp_w2[1])
    cmp_end = cmp_start + CMP_LEN - 1

    n_slc = S // SLC_LEN
    n_sel = min(SLC_TOPK, n_slc)
    ks_blk = k_slc.reshape(B, HK, n_slc, SLC_LEN, D)
    vs_blk = v_slc.reshape(B, HK, n_slc, SLC_LEN, D)
    slc_start = jnp.arange(n_slc) * SLC_LEN
    overlap = jnp.maximum(
        jnp.minimum(cmp_start[:, None] + CMP_LEN, slc_start[None, :] + SLC_LEN)
        - jnp.maximum(cmp_start[:, None], slc_start[None, :]), 0).astype(f32) / CMP_LEN

    pad = ((0, 0), (0, 0), (WINDOW, 0), (0, 0))
    kw_pad = jnp.pad(k_win, pad)
    vw_pad = jnp.pad(v_win, pad)

    table = rel_bias.astype(f32)
    table_g = table.reshape(REL_BUCKETS, HK, G).transpose(1, 0, 2)

    def head_bias(dist):
        bq = table[rel_bucket(dist)]
        return bq.transpose(2, 0, 1).reshape(HK, G, dist.shape[0], dist.shape[1])

    bi = jnp.arange(B)[:, None, None, None]
    hi = jnp.arange(HK)[None, :, None, None]
    blk_ids = jnp.arange(n_slc)

    def chunk(i):
        q0 = i * Q_BLOCK
        qpos = q0 + jnp.arange(Q_BLOCK)
        qb = lax.dynamic_slice_in_dim(qg, q0, Q_BLOCK, axis=3)
        gb = lax.dynamic_slice_in_dim(gates, q0, Q_BLOCK, axis=3)
        dist_c = qpos[:, None] - cmp_end[None, :]
        s_c = jnp.einsum('bhgqd,bhcd->bhgqc', qb, kc, preferred_element_type=f32) * scale + head_bias(dist_c)
        s_c = jnp.where(dist_c >= 0, s_c, NEG)
        p_c = jax.nn.softmax(s_c, axis=-1) * (qpos >= CMP_LEN - 1)[:, None].astype(f32)
        o_c = jnp.einsum('bhgqc,bhcd->bhgqd', p_c.astype(vc.dtype), vc)
        imp = jnp.einsum('bhgqc,cn->bhqn', p_c, overlap)
        q_blk = qpos // SLC_LEN
        forced = ((blk_ids[None, :] == 0) | (blk_ids[None, :] == q_blk[:, None])
                  | (blk_ids[None, :] == q_blk[:, None] - 1))
        imp = jnp.where(forced, POS_BIG, jnp.where(blk_ids[None, :] > q_blk[:, None], NEG, imp))
        _, sel = lax.top_k(imp, n_sel)
        k_g = ks_blk[bi, hi, sel]
        v_g = vs_blk[bi, hi, sel]
        kpos = sel[..., None] * SLC_LEN + jnp.arange(SLC_LEN)
        dist_s = qpos[:, None, None] - kpos
        bias_s = table_g[hi[..., None], rel_bucket(dist_s)].transpose(0, 1, 5, 2, 3, 4)
        s_s = jnp.einsum('bhgqd,bhqnld->bhgqnl', qb, k_g, preferred_element_type=f32) * scale + bias_s
        s_s = jnp.where((dist_s >= 0)[:, :, None], s_s, NEG)
        p_s = jax.nn.softmax(s_s.reshape(s_s.shape[:4] + (-1,)), axis=-1).reshape(s_s.shape)
        o_s = jnp.einsum('bhgqnl,bhqnld->bhgqd', p_s.astype(v_g.dtype), v_g)
        k_w = lax.dynamic_slice_in_dim(kw_pad, q0, Q_BLOCK + WINDOW, axis=2)
        v_w = lax.dynamic_slice_in_dim(vw_pad, q0, Q_BLOCK + WINDOW, axis=2)
        dist_w = qpos[:, None] - (q0 - WINDOW + jnp.arange(Q_BLOCK + WINDOW))[None, :]
        s_w = jnp.einsum('bhgqd,bhkd->bhgqk', qb, k_w, preferred_element_type=f32) * scale + head_bias(dist_w)
        s_w = jnp.where((dist_w >= 0) & (dist_w < WINDOW), s_w, NEG)
        p_w = jax.nn.softmax(s_w, axis=-1)
        o_w = jnp.einsum('bhgqk,bhkd->bhgqd', p_w.astype(v_w.dtype), v_w)
        out = gb[..., 0:1] * o_c + gb[..., 1:2] * o_s + gb[..., 2:3] * o_w
        return out.astype(q.dtype)

    o = lax.map(chunk, jnp.arange(S // Q_BLOCK))
    return o.transpose(1, 0, 4, 2, 3, 5).reshape(B, S, NSA_HEADS * D)


def conv_ffn(h, w_gate, w_up, conv_w, conv_b, w_down):
    S = h.shape[1]
    g = h @ w_gate
    gp = jnp.pad(g, ((0, 0), (CONV_WIDTH - 1, 0), (0, 0)))
    g = sum(conv_w[k] * gp[:, k:k + S] for k in range(CONV_WIDTH)) + conv_b
    return (jax.nn.gelu(g, approximate=True) * (h @ w_up)) @ w_down


def setup_inputs(seed: int = 0) -> dict:
    key = jax.random.key(seed)
    ks = jax.random.split(key, 26)
    f32 = jnp.float32

    def nrm(k, shape, fan_in):
        return jax.random.normal(k, shape, f32) * fan_in ** -0.5

    def gain(k, shape):
        return 1.0 + 0.05 * jax.random.normal(k, shape, f32)

    n_in = sum(IN_SPLITS)
    return {
        "x": jax.random.normal(ks[0], (BATCH, SEQ, D_MODEL), f32),
        "p": jax.random.normal(ks[1], (DEPTH, BATCH, SEQ, PLE_DIM), f32),
        "positions": jnp.tile(jnp.arange(SEQ, dtype=jnp.int32)[None, :], (BATCH, 1)),
        "rel_bias": 0.5 * jax.random.normal(ks[2], (REL_BUCKETS, NSA_HEADS), f32),
        "attn_pre_norm": gain(ks[3], (DEPTH, D_MODEL)),
        "attn_post_norm": gain(ks[4], (DEPTH, D_MODEL)),
        "ffn_pre_norm": gain(ks[5], (DEPTH, D_MODEL)),
        "ffn_post_norm": gain(ks[6], (DEPTH, D_MODEL)),
        "w_in": nrm(ks[7], (DEPTH, D_MODEL, n_in), D_MODEL),
        "mla_q_norm": gain(ks[8], (DEPTH, MLA_Q_RANK)),
        "mla_w_uq": nrm(ks[9], (DEPTH, MLA_Q_RANK, MLA_HEADS * (MLA_NOPE + MLA_ROPE)), MLA_Q_RANK),
        "mla_kv_norm": gain(ks[10], (DEPTH, MLA_KV_RANK)),
        "mla_w_ukv": nrm(ks[11], (DEPTH, MLA_KV_RANK, MLA_HEADS * (MLA_NOPE + MLA_V)), MLA_KV_RANK),
        "nsa_cmp_pos": 0.1 * jax.random.normal(ks[12], (DEPTH, 2, CMP_LEN, NSA_DIM), f32),
        "nsa_cmp_w1": nrm(ks[13], (DEPTH, 2, CMP_LEN * NSA_DIM, CMP_HIDDEN), CMP_LEN * NSA_DIM),
        "nsa_cmp_w2": nrm(ks[14], (DEPTH, 2, CMP_HIDDEN, NSA_DIM), CMP_HIDDEN),
        "w_o": nrm(ks[15], (DEPTH, MIX_OUT, D_MODEL), MIX_OUT),
        "ffn_w_gate": nrm(ks[16], (DEPTH, D_MODEL, D_FF), D_MODEL),
        "ffn_w_up": nrm(ks[17], (DEPTH, D_MODEL, D_FF), D_MODEL),
        "ffn_conv_w": nrm(ks[18], (DEPTH, CONV_WIDTH, D_FF), CONV_WIDTH),
        "ffn_conv_b": 0.02 * jax.random.normal(ks[19], (DEPTH, D_FF), f32),
        "ffn_w_down": nrm(ks[20], (DEPTH, D_FF, D_MODEL), D_FF),
        "ple_proj": nrm(ks[21], (DEPTH, PLE_DIM, D_MODEL), PLE_DIM),
        "ple_gate": nrm(ks[22], (DEPTH, D_MODEL, D_MODEL), D_MODEL),
    }


def reference(x, p, positions, rel_bias, attn_pre_norm, attn_post_norm, ffn_pre_norm,
              ffn_post_norm, w_in, mla_q_norm, mla_w_uq, mla_kv_norm, mla_w_ukv,
              nsa_cmp_pos, nsa_cmp_w1, nsa_cmp_w2, w_o, ffn_w_gate, ffn_w_up,
              ffn_conv_w, ffn_conv_b, ffn_w_down, ple_proj, ple_gate):
    offsets = np.cumsum(IN_SPLITS)[:-1].tolist()
    for i in range(DEPTH):
        h = rms_norm(x, attn_pre_norm[i])
        z = h @ w_in[i]
        (c_q, c_kv, k_rope, q_nsa, k_cmp, v_cmp, k_slc, v_slc,
         k_win, v_win, g_nsa) = jnp.split(z, offsets, axis=-1)
        o_mla = mla_attention(c_q, c_kv, k_rope, positions, mla_q_norm[i], mla_w_uq[i],
                              mla_kv_norm[i], mla_w_ukv[i])
        o_nsa = nsa_attention(q_nsa, k_cmp, v_cmp, k_slc, v_slc, k_win, v_win, g_nsa,
                              rel_bias, nsa_cmp_pos[i], nsa_cmp_w1[i], nsa_cmp_w2[i])
        y = jnp.concatenate([o_mla, o_nsa], axis=-1) @ w_o[i]
        x = x + rms_norm(y, attn_post_norm[i])
        f = conv_ffn(rms_norm(x, ffn_pre_norm[i]), ffn_w_gate[i], ffn_w_up[i],
                     ffn_conv_w[i], ffn_conv_b[i], ffn_w_down[i])
        x = x + rms_norm(f, ffn_post_norm[i])
        x = x + jax.nn.sigmoid(x @ ple_gate[i]) * (p[i] @ ple_proj[i])
    return x
```

```python
import functools
import math

import numpy as np
import jax
import jax.numpy as jnp
from jax import lax
from jax.experimental import pallas as pl
from jax.experimental.pallas import tpu as pltpu

F32 = jnp.float32
BF16 = jnp.bfloat16

D_MODEL = 1024
DEPTH = 2
MLA_HEADS = 8
MLA_NOPE = 64
MLA_ROPE = 32
MLA_V = 64
MLA_Q_RANK = 256
MLA_KV_RANK = 128
ROPE_BASE = 10000.0
NSA_HEADS = 8
NSA_KV_HEADS = 2
NSA_GROUP = NSA_HEADS // NSA_KV_HEADS
NSA_DIM = 64
CMP_LEN = 32
CMP_STRIDE = 16
CMP_HIDDEN = 128
SLC_LEN = 64
SLC_TOPK = 16
WINDOW = 512
REL_BUCKETS = 32
REL_MAX_DIST = 128
D_FF = 2816
CONV_WIDTH = 3
PLE_DIM = 256
EPS = 1e-6
NEG = -1e30
POS_BIG = 1e30

IN_SPLITS = (MLA_Q_RANK, MLA_KV_RANK, MLA_ROPE, NSA_HEADS * NSA_DIM,
             NSA_KV_HEADS * NSA_DIM, NSA_KV_HEADS * NSA_DIM,
             NSA_KV_HEADS * NSA_DIM, NSA_KV_HEADS * NSA_DIM,
             NSA_KV_HEADS * NSA_DIM, NSA_KV_HEADS * NSA_DIM,
             3 * NSA_HEADS)

LANE = 128
QT = 128
SLC_PAD = 64
FF_CHUNK = 256
VMEM_LIMIT = 56 * 1024 * 1024

C_CQ = 0
C_CKV = 256
C_KR = 384
C_KRROT = 512
C_QN = 640
C_KSLC = 1152
C_VSLC = 1408
C_KWIN = 1664
C_VWIN = 1920
C_KCMP = 2176
C_VCMP = 2304
C_GATE = 2432
C_TOTAL = 2688


def _dot(a, b):
    return jnp.dot(a, b, preferred_element_type=F32)


def _dot_nt(a, b):
    return lax.dot_general(a, b, (((1,), (1,)), ((), ())), preferred_element_type=F32)


def _rms(x, g):
    return x * lax.rsqrt(jnp.mean(x * x, axis=-1, keepdims=True) + EPS) * g


def _gelu_tanh(x):
    return 0.5 * x * (1.0 + jnp.tanh(math.sqrt(2.0 / math.pi) * (x + 0.044715 * (x * x * x))))


def _sigmoid(x):
    return 1.0 / (1.0 + jnp.exp(-x))


def _params(*sem):
    return pltpu.CompilerParams(dimension_semantics=sem, vmem_limit_bytes=VMEM_LIMIT)


def _resident(shape):
    nd = len(shape)
    return pl.BlockSpec(shape, lambda *_: (0,) * nd, pipeline_mode=pl.Buffered(1))


def _rope_kernel(pos_ref, inv_ref, ct_ref, st_ref):
    ang = pos_ref[...].astype(F32) * inv_ref[...]
    lane = lax.broadcasted_iota(jnp.int32, ang.shape, 1)
    rope = (lane >= MLA_NOPE) & (lane < MLA_NOPE + MLA_ROPE)
    ct_ref[...] = jnp.where(rope, jnp.cos(ang), jnp.where(lane < MLA_NOPE, 1.0, 0.0))
    st_ref[...] = jnp.where(rope, jnp.sin(ang), 0.0)


def _rope_tables(positions):
    n = positions.size
    tm = 512
    half = MLA_ROPE // 2
    inv = ROPE_BASE ** (-jnp.arange(half, dtype=F32) / half)
    inv_slot = jnp.concatenate([jnp.zeros((MLA_NOPE,), F32), inv, inv,
                                jnp.zeros((LANE - MLA_NOPE - MLA_ROPE,), F32)])[None, :]
    return pl.pallas_call(
        _rope_kernel,
        grid=(n // tm,),
        in_specs=[pl.BlockSpec((tm, 1), lambda r: (r, 0)),
                  pl.BlockSpec((1, LANE), lambda r: (0, 0))],
        out_specs=[pl.BlockSpec((tm, LANE), lambda r: (r, 0))] * 2,
        out_shape=[jax.ShapeDtypeStruct((n, LANE), F32)] * 2,
        compiler_params=_params("parallel"),
        name="rope_tables",
    )(positions.reshape(n, 1), inv_slot)


def _bucket_np(dist):
    n = np.maximum(dist, 0)
    max_exact = REL_BUCKETS // 2
    large = max_exact + (np.log(np.maximum(n, 1).astype(np.float32) / max_exact)
                         / math.log(REL_MAX_DIST / max_exact)
                         * (REL_BUCKETS - max_exact)).astype(np.int32)
    large = np.minimum(large, REL_BUCKETS - 1)
    return np.where(n < max_exact, n, large).astype(np.int32)


def _bias_kernel(table_ref, bpc_ref, bpt_ref, biasc_ref, tsel_ref, *, nq, ncp):
    h = pl.program_id(0)
    far = table_ref[REL_BUCKETS - 1, h]

    def lookup(bp, sub):
        acc = jnp.full(bp.shape, far - sub, F32)
        for b in range(REL_BUCKETS - 1):
            acc = jnp.where(bp == b, table_ref[b, h] - sub, acc)
        return jnp.where(bp < 0, NEG, acc)

    pat = lookup(bpc_ref[...], 0.0)
    shift = QT // CMP_STRIDE
    for qb in range(nq):
        biasc_ref[0, qb * QT:(qb + 1) * QT, :] = pat[:, ncp - shift * qb: 2 * ncp - shift * qb]
    tsel_ref[0, 0] = lookup(bpt_ref[0], far)
    tsel_ref[0, 1] = lookup(bpt_ref[1], far)


def _bias_tables(rel_bias, seq):
    nq = seq // QT
    ncp = seq // CMP_STRIDE
    i = np.arange(QT)[:, None]
    cprime = np.arange(2 * ncp)[None, :] - ncp
    dist_c = i - CMP_STRIDE * cprime - (CMP_LEN - 1)
    bpc = np.where(dist_c >= 0, _bucket_np(dist_c), -1).astype(np.int32)
    j = np.arange(QT)[None, :]
    d0 = i - j
    d1 = QT + i - j
    bpt = np.stack([np.where(d0 >= 0, _bucket_np(d0), -1), _bucket_np(d1)]).astype(np.int32)
    return pl.pallas_call(
        functools.partial(_bias_kernel, nq=nq, ncp=ncp),
        grid=(NSA_HEADS,),
        in_specs=[pl.BlockSpec(memory_space=pltpu.SMEM),
                  pl.BlockSpec((QT, 2 * ncp), lambda h: (0, 0)),
                  pl.BlockSpec((2, QT, QT), lambda h: (0, 0, 0))],
        out_specs=[pl.BlockSpec((1, seq, ncp), lambda h: (h, 0, 0)),
                   pl.BlockSpec((1, 2, QT, QT), lambda h: (h, 0, 0, 0))],
        out_shape=[jax.ShapeDtypeStruct((NSA_HEADS, seq, ncp), F32),
                   jax.ShapeDtypeStruct((NSA_HEADS, 2, QT, QT), F32)],
        compiler_params=_params("parallel"),
        name="bias_tables",
    )(rel_bias.astype(F32), jnp.asarray(bpc), jnp.asarray(bpt))


def _inproj_kernel(x_ref, g_ref, w1_ref, qn_ref, wqa_ref, wqb_ref, kvn_ref, wk_ref, wv_ref,
                   ct_ref, st_ref,
                   qm_ref, km_ref, vm_ref, zq_ref, nkv_ref, kcmp_ref, vcmp_ref, gate_ref,
                   *, seq, tm):
    h = _rms(x_ref[...], g_ref[...]).astype(BF16)

    def proj(c0, width):
        return _dot(h, w1_ref[:, c0:c0 + width])

    ct = ct_ref[...]
    st = st_ref[...]
    lane = lax.broadcasted_iota(jnp.int32, (tm, LANE), 1)
    ones_hi = (lane >= NSA_DIM).astype(F32)

    cq = _rms(proj(C_CQ, MLA_Q_RANK), qn_ref[...]).astype(BF16)
    qa = _dot(cq, wqa_ref[...])
    qb = _dot(cq, wqb_ref[...])
    scale = (MLA_NOPE + MLA_ROPE) ** -0.5
    cts = ct * scale
    sts = st * scale
    for hh in range(MLA_HEADS):
        sl = slice(hh * LANE, (hh + 1) * LANE)
        qm_ref[:, sl] = (qa[:, sl] * cts + qb[:, sl] * sts).astype(BF16)

    ckv = _rms(proj(C_CKV, MLA_KV_RANK), kvn_ref[...]).astype(BF16)
    kr = proj(C_KR, LANE) * ct + proj(C_KRROT, LANE) * st
    kn = _dot(ckv, wk_ref[...])
    vv = _dot(ckv, wv_ref[...])
    for hh in range(MLA_HEADS):
        sl = slice(hh * LANE, (hh + 1) * LANE)
        km_ref[:, sl] = (kn[:, sl] + kr).astype(BF16)
        vm_ref[:, sl] = (vv[:, sl] + ones_hi).astype(BF16)

    zq_ref[...] = proj(C_QN, NSA_HEADS * NSA_DIM).astype(BF16)

    s0 = (pl.program_id(0) * tm) % seq
    row = lax.broadcasted_iota(jnp.int32, (tm, LANE), 0)
    blk = (s0 + row) // SLC_LEN
    onehot = (lane - NSA_DIM == blk).astype(F32)
    ksl = proj(C_KSLC, 2 * LANE)
    vsl = proj(C_VSLC, 2 * LANE)
    kwn = proj(C_KWIN, 2 * LANE)
    vwn = proj(C_VWIN, 2 * LANE)
    for hk in range(NSA_KV_HEADS):
        sl = slice(hk * LANE, (hk + 1) * LANE)
        nkv_ref[:, 0 * 2 * LANE + hk * LANE: 0 * 2 * LANE + (hk + 1) * LANE] = (ksl[:, sl] + onehot).astype(BF16)
        nkv_ref[:, 1 * 2 * LANE + hk * LANE: 1 * 2 * LANE + (hk + 1) * LANE] = (vsl[:, sl] + ones_hi).astype(BF16)
        nkv_ref[:, 2 * 2 * LANE + hk * LANE: 2 * 2 * LANE + (hk + 1) * LANE] = kwn[:, sl].astype(BF16)
        nkv_ref[:, 3 * 2 * LANE + hk * LANE: 3 * 2 * LANE + (hk + 1) * LANE] = (vwn[:, sl] + ones_hi).astype(BF16)

    kcmp = proj(C_KCMP, LANE)
    vcmp = proj(C_VCMP, LANE)
    for hk in range(NSA_KV_HEADS):
        kcmp_ref[0, hk] = kcmp[:, hk * NSA_DIM:(hk + 1) * NSA_DIM]
        vcmp_ref[0, hk] = vcmp[:, hk * NSA_DIM:(hk + 1) * NSA_DIM]

    gate_ref[...] = _sigmoid(proj(C_GATE, 2 * LANE))


def _inproj(x2, g_pre, w1, qn, wqa, wqb, kvn, wk, wv, ct, st, *, batch, seq):
    n = x2.shape[0]
    tm = 512
    tiles_per_seq = seq // tm
    row = lambda r: (r, 0)
    hm = lambda r: (r // tiles_per_seq, 0, r % tiles_per_seq, 0)
    return pl.pallas_call(
        functools.partial(_inproj_kernel, seq=seq, tm=tm),
        grid=(n // tm,),
        in_specs=[pl.BlockSpec((tm, D_MODEL), row),
                  _resident((1, D_MODEL)),
                  _resident((D_MODEL, C_TOTAL)),
                  _resident((1, MLA_Q_RANK)),
                  _resident((MLA_Q_RANK, MLA_HEADS * LANE)),
                  _resident((MLA_Q_RANK, MLA_HEADS * LANE)),
                  _resident((1, MLA_KV_RANK)),
                  _resident((MLA_KV_RANK, MLA_HEADS * LANE)),
                  _resident((MLA_KV_RANK, MLA_HEADS * LANE)),
                  pl.BlockSpec((tm, LANE), row),
                  pl.BlockSpec((tm, LANE), row)],
        out_specs=[pl.BlockSpec((tm, MLA_HEADS * LANE), row),
                   pl.BlockSpec((tm, MLA_HEADS * LANE), row),
                   pl.BlockSpec((tm, MLA_HEADS * LANE), row),
                   pl.BlockSpec((tm, NSA_HEADS * NSA_DIM), row),
                   pl.BlockSpec((tm, 8 * LANE), row),
                   pl.BlockSpec((1, NSA_KV_HEADS, tm, NSA_DIM), hm),
                   pl.BlockSpec((1, NSA_KV_HEADS, tm, NSA_DIM), hm),
                   pl.BlockSpec((tm, 2 * LANE), row)],
        out_shape=[jax.ShapeDtypeStruct((n, MLA_HEADS * LANE), BF16),
                   jax.ShapeDtypeStruct((n, MLA_HEADS * LANE), BF16),
                   jax.ShapeDtypeStruct((n, MLA_HEADS * LANE), BF16),
                   jax.ShapeDtypeStruct((n, NSA_HEADS * NSA_DIM), BF16),
                   jax.ShapeDtypeStruct((n, 8 * LANE), BF16),
                   jax.ShapeDtypeStruct((batch, NSA_KV_HEADS, seq, NSA_DIM), F32),
                   jax.ShapeDtypeStruct((batch, NSA_KV_HEADS, seq, NSA_DIM), F32),
                   jax.ShapeDtypeStruct((n, 2 * LANE), F32)],
        compiler_params=_params("parallel"),
        name="in_proj",
    )(x2, g_pre, w1, qn, wqa, wqb, kvn, wk, wv, ct, st)


def _compress_kernel(k16_ref, v16_ref, pos_ref, w1_ref, w2_ref, kc_ref, vc_ref):
    half = CMP_STRIDE * NSA_DIM

    def one(x16, j):
        pos = pos_ref[j]
        top = (x16 + pos[:, :half]).astype(BF16)
        bot = (x16 + pos[:, half:]).astype(BF16)
        u = _dot(top, w1_ref[j, :half, :])
        low = _dot(bot, w1_ref[j, half:, :])
        nxt = jnp.concatenate([low[1:], jnp.zeros((1, CMP_HIDDEN), F32)], axis=0)
        hid = _gelu_tanh(u + nxt).astype(BF16)
        return _dot(hid, w2_ref[j])

    kc_ref[0, 0] = one(k16_ref[0, 0], 0).astype(BF16)
    vc_ref[0, 0] = one(v16_ref[0, 0], 1).astype(BF16)


def _compress(kcmp, vcmp, pos, w1, w2, *, batch, seq):
    ncp = seq // CMP_STRIDE
    half = CMP_STRIDE * NSA_DIM
    k16 = kcmp.reshape(batch, NSA_KV_HEADS, ncp, half)
    v16 = vcmp.reshape(batch, NSA_KV_HEADS, ncp, half)
    blk = lambda b, hk: (b, hk, 0, 0)
    return pl.pallas_call(
        _compress_kernel,
        grid=(batch, NSA_KV_HEADS),
        in_specs=[pl.BlockSpec((1, 1, ncp, half), blk),
                  pl.BlockSpec((1, 1, ncp, half), blk),
                  pl.BlockSpec((2, 1, 2 * half), lambda b, hk: (0, 0, 0)),
                  pl.BlockSpec((2, 2 * half, CMP_HIDDEN), lambda b, hk: (0, 0, 0)),
                  pl.BlockSpec((2, CMP_HIDDEN, NSA_DIM), lambda b, hk: (0, 0, 0))],
        out_specs=[pl.BlockSpec((1, 1, ncp, NSA_DIM), blk)] * 2,
        out_shape=[jax.ShapeDtypeStruct((batch, NSA_KV_HEADS, ncp, NSA_DIM), BF16)] * 2,
        compiler_params=_params("parallel", "parallel"),
        name="nsa_compress",
    )(k16, v16, pos, w1, w2)


def _nsa_kernel(zq_ref, kc_ref, vc_ref, ksl_ref, vsl_ref, kw_ref, vw_ref, gate_ref,
                biasc_ref, tsel_ref, ovt_ref, o_ref, imp_ref, acc_ref, m_ref):
    qi = pl.program_id(2)
    G = NSA_GROUP
    q4 = zq_ref[...]
    kc = kc_ref[0, 0]
    vc = vc_ref[0, 0]
    ovt = ovt_ref[...]

    qrow = qi * QT + lax.broadcasted_iota(jnp.int32, (QT, 1), 0)
    rowmask = (qrow >= CMP_LEN - 1).astype(F32)

    o_cmp = []
    imp_t = jnp.zeros((SLC_PAD, QT), F32)
    for g in range(G):
        qg = q4[:, g * NSA_DIM:(g + 1) * NSA_DIM]
        s = _dot_nt(qg, kc) + biasc_ref[g]
        e = jnp.exp(s - jnp.max(s, axis=-1, keepdims=True))
        p = (e / jnp.sum(e, axis=-1, keepdims=True) * rowmask).astype(BF16)
        o_cmp.append(_dot(p, vc))
        imp_t = imp_t + _dot_nt(ovt, p)

    n_id = lax.broadcasted_iota(jnp.int32, (SLC_PAD, QT), 0)
    q_blk = (qi * QT + lax.broadcasted_iota(jnp.int32, (SLC_PAD, QT), 1)) // SLC_LEN
    forced = (n_id == 0) | (n_id == q_blk) | (n_id == q_blk - 1)
    imp = jnp.where(forced, POS_BIG, jnp.where(n_id > q_blk, NEG, imp_t))
    imp_ref[...] = imp
    rank = jnp.zeros((SLC_PAD, QT), jnp.int32)
    for m in range(SLC_PAD):
        other = imp_ref[m:m + 1, :]
        rank = rank + jnp.where(n_id > m, (other >= imp).astype(jnp.int32),
                                (other > imp).astype(jnp.int32))
    selb_t = jnp.where(rank < SLC_TOPK, 0.0, NEG)
    selb = selb_t.T.astype(BF16)
    qaug = [jnp.concatenate([q4[:, g * NSA_DIM:(g + 1) * NSA_DIM], selb], axis=1) for g in range(G)]

    lane = lax.broadcasted_iota(jnp.int32, (1, LANE), 1)
    ti = lax.broadcasted_iota(jnp.int32, (QT, QT), 0)
    tj = lax.broadcasted_iota(jnp.int32, (QT, QT), 1)

    def reset():
        m_ref[...] = jnp.full(m_ref.shape, NEG, F32)
        acc_ref[...] = jnp.zeros(acc_ref.shape, F32)

    def step(g, kt, vt, bias, kvalid=None):
        s = _dot_nt(qaug[g], kt)
        if kvalid is not None:
            s = s * kvalid
        if bias is not None:
            s = s + bias
        m_old = m_ref[g]
        m_new = jnp.maximum(m_old, jnp.max(s, axis=-1, keepdims=True))
        p = jnp.exp(s - m_new).astype(BF16)
        pv = _dot(p, vt)
        if kvalid is not None:
            pv = pv * jnp.where(lane < NSA_DIM, kvalid, 1.0)
        acc_ref[g] = acc_ref[g] * jnp.exp(m_old - m_new) + pv
        m_ref[g] = m_new

    def finish(g):
        a = acc_ref[g]
        return a[:, :NSA_DIM] / a[:, NSA_DIM:NSA_DIM + 1]

    reset()

    def far_body(t, carry):
        off = pl.multiple_of(t * QT, QT)
        kt = ksl_ref[pl.ds(off, QT), :]
        vt = vsl_ref[pl.ds(off, QT), :]
        for g in range(G):
            step(g, kt, vt, None)
        return carry

    lax.fori_loop(0, jnp.maximum(qi - 1, 0), far_body, 0)
    prev = jnp.maximum(qi - 1, 0)
    off_p = pl.multiple_of(prev * QT, QT)
    off_d = pl.multiple_of(qi * QT, QT)
    no_prev = jnp.where(qi == 0, NEG, 0.0)
    kt_p, vt_p = ksl_ref[pl.ds(off_p, QT), :], vsl_ref[pl.ds(off_p, QT), :]
    kt_d, vt_d = ksl_ref[pl.ds(off_d, QT), :], vsl_ref[pl.ds(off_d, QT), :]
    for g in range(G):
        step(g, kt_p, vt_p, tsel_ref[g, 1] + no_prev)
        step(g, kt_d, vt_d, tsel_ref[g, 0])
    o_sel = [finish(g) for g in range(G)]

    reset()
    n_tiles = WINDOW // QT + 1
    anti = jnp.where(tj > ti, 0.0, NEG)
    for w in range(n_tiles):
        tile = qi - (n_tiles - 1) + w
        kvalid = None if w == n_tiles - 1 else (tile >= 0).astype(F32)
        off = pl.multiple_of(jnp.maximum(tile, 0) * QT, QT)
        kt = kw_ref[pl.ds(off, QT), :]
        vt = vw_ref[pl.ds(off, QT), :]
        for g in range(G):
            if w == 0:
                bias = anti
            elif w == n_tiles - 2:
                bias = tsel_ref[g, 1]
            elif w == n_tiles - 1:
                bias = tsel_ref[g, 0]
            else:
                bias = None
            step(g, kt, vt, bias, kvalid)
    o_win = [finish(g) for g in range(G)]

    gate = gate_ref[...]
    outs = []
    for g in range(G):
        outs.append(gate[:, 3 * g:3 * g + 1] * o_cmp[g] + gate[:, 3 * g + 1:3 * g + 2] * o_sel[g]
                    + gate[:, 3 * g + 2:3 * g + 3] * o_win[g])
    o_ref[...] = jnp.concatenate(outs, axis=1).astype(BF16)


def _overlap_t(seq):
    ncp = seq // CMP_STRIDE
    n_cmp = (seq - CMP_LEN) // CMP_STRIDE + 1
    n_slc = seq // SLC_LEN
    cs = np.arange(ncp)[None, :] * CMP_STRIDE
    ss = np.arange(SLC_PAD)[:, None] * SLC_LEN
    ov = np.maximum(np.minimum(cs + CMP_LEN, ss + SLC_LEN) - np.maximum(cs, ss), 0).astype(np.float32) / CMP_LEN
    ov = ov * (np.arange(ncp)[None, :] < n_cmp) * (np.arange(SLC_PAD)[:, None] < n_slc)
    return jnp.asarray(ov, BF16)


def _nsa(zq, kc, vc, nkv, gates, biasc, tsel, *, batch, seq):
    n = zq.shape[0]
    nq = seq // QT
    ncp = seq // CMP_STRIDE
    G = NSA_GROUP
    assert seq // SLC_LEN <= SLC_PAD and seq // SLC_LEN >= SLC_TOPK and seq >= WINDOW
    rowblk = lambda b, hk, qi: (b * nq + qi, hk)
    kvblk = lambda j: (lambda b, hk, qi: (b, 2 * j + hk))
    return pl.pallas_call(
        _nsa_kernel,
        grid=(batch, NSA_KV_HEADS, nq),
        in_specs=[pl.BlockSpec((QT, G * NSA_DIM), rowblk),
                  pl.BlockSpec((1, 1, ncp, NSA_DIM), lambda b, hk, qi: (b, hk, 0, 0)),
                  pl.BlockSpec((1, 1, ncp, NSA_DIM), lambda b, hk, qi: (b, hk, 0, 0)),
                  pl.BlockSpec((seq, LANE), kvblk(0)),
                  pl.BlockSpec((seq, LANE), kvblk(1)),
                  pl.BlockSpec((seq, LANE), kvblk(2)),
                  pl.BlockSpec((seq, LANE), kvblk(3)),
                  pl.BlockSpec((QT, LANE), rowblk),
                  pl.BlockSpec((G, QT, ncp), lambda b, hk, qi: (hk, qi, 0)),
                  pl.BlockSpec((G, 2, QT, QT), lambda b, hk, qi: (hk, 0, 0, 0)),
                  pl.BlockSpec((SLC_PAD, ncp), lambda b, hk, qi: (0, 0))],
        out_specs=pl.BlockSpec((QT, G * NSA_DIM), rowblk),
        out_shape=jax.ShapeDtypeStruct((n, NSA_HEADS * NSA_DIM), BF16),
        scratch_shapes=[pltpu.VMEM((SLC_PAD, QT), F32),
                        pltpu.VMEM((G, QT, LANE), F32),
                        pltpu.VMEM((G, QT, 1), F32)],
        compiler_params=_params("parallel", "parallel", "arbitrary"),
        name="nsa_attention",
    )(zq, kc, vc, nkv, nkv, nkv, nkv, gates, biasc, tsel, _overlap_t(seq))


MLA_TQ = 256


def _mla_kernel(q_ref, k_ref, v_ref, o_ref, acc_ref, m_ref):
    qi = pl.program_id(2)
    tq = MLA_TQ
    HP = 2
    qs = [q_ref[:, hh * LANE:(hh + 1) * LANE] for hh in range(HP)]
    m_ref[...] = jnp.full(m_ref.shape, NEG, F32)
    acc_ref[...] = jnp.zeros(acc_ref.shape, F32)

    def step(hh, off, mask):
        kt = k_ref[pl.ds(off, tq), hh * LANE:(hh + 1) * LANE]
        vt = v_ref[pl.ds(off, tq), hh * LANE:(hh + 1) * LANE]
        s = _dot_nt(qs[hh], kt)
        if mask is not None:
            s = s + mask
        m_old = m_ref[hh]
        m_new = jnp.maximum(m_old, jnp.max(s, axis=-1, keepdims=True))
        p = jnp.exp(s - m_new).astype(BF16)
        acc_ref[hh] = acc_ref[hh] * jnp.exp(m_old - m_new) + _dot(p, vt)
        m_ref[hh] = m_new

    def body(t, carry):
        off = pl.multiple_of(t * tq, tq)
        for hh in range(HP):
            step(hh, off, None)
        return carry

    lax.fori_loop(0, qi, body, 0)
    ti = lax.broadcasted_iota(jnp.int32, (tq, tq), 0)
    tj = lax.broadcasted_iota(jnp.int32, (tq, tq), 1)
    causal = jnp.where(tj <= ti, 0.0, NEG)
    off_d = pl.multiple_of(qi * tq, tq)
    outs = []
    for hh in range(HP):
        step(hh, off_d, causal)
        a = acc_ref[hh]
        outs.append(a[:, :MLA_V] / a[:, MLA_V:MLA_V + 1])
    o_ref[...] = jnp.concatenate(outs, axis=1).astype(BF16)


def _mla(qm, km, vm, *, batch, seq):
    n = qm.shape[0]
    tq = MLA_TQ
    nq = seq // tq
    HP = 2
    rowblk = lambda b, hp, qi: (b * nq + qi, hp)
    kvblk = lambda b, hp, qi: (b, hp)
    return pl.pallas_call(
        _mla_kernel,
        grid=(batch, MLA_HEADS // HP, nq),
        in_specs=[pl.BlockSpec((tq, HP * LANE), rowblk),
                  pl.BlockSpec((seq, HP * LANE), kvblk),
                  pl.BlockSpec((seq, HP * LANE), kvblk)],
        out_specs=pl.BlockSpec((tq, HP * MLA_V), rowblk),
        out_shape=jax.ShapeDtypeStruct((n, MLA_HEADS * MLA_V), BF16),
        scratch_shapes=[pltpu.VMEM((HP, tq, LANE), F32),
                        pltpu.VMEM((HP, tq, 1), F32)],
        compiler_params=_params("parallel", "parallel", "arbitrary"),
        name="mla_attention",
    )(qm, km, vm)


def _tail_kernel(x_ref, om_ref, on_ref, p_ref, wo_ref, g1_ref, g2_ref, g3_ref,
                 wg_ref, wu_ref, cw_ref, cb_ref, wd_ref, pg_ref, pp_ref,
                 o_ref, carry_ref, facc_ref, *, seq, tm):
    n_chunks = D_FF // FF_CHUNK
    half = om_ref.shape[1]
    y = _dot(om_ref[...], wo_ref[:half, :]) + _dot(on_ref[...], wo_ref[half:, :])
    x = x_ref[...] + _rms(y, g1_ref[...])

    h = _rms(x, g2_ref[...]).astype(BF16)

    @pl.when((pl.program_id(0) * tm) % seq == 0)
    def _():
        carry_ref[...] = jnp.zeros(carry_ref.shape, F32)

    row = lax.broadcasted_iota(jnp.int32, (tm, 1), 0)
    facc_ref[...] = jnp.zeros(facc_ref.shape, F32)

    def chunk(c, carry):
        g = _dot(h, wg_ref[c])
        prev = carry_ref[c]
        carry_ref[c] = g[tm - 8:, :]
        g1 = jnp.where(row == 0, prev[7:8, :], pltpu.roll(g, 1, 0))
        g2 = jnp.where(row == 0, prev[6:7, :], jnp.where(row == 1, prev[7:8, :], pltpu.roll(g, 2, 0)))
        cw = cw_ref[c]
        conv = cw[0:1, :] * g2 + cw[1:2, :] * g1 + cw[2:3, :] * g + cb_ref[c]
        act = (_gelu_tanh(conv) * _dot(h, wu_ref[c])).astype(BF16)
        facc_ref[...] += _dot(act, wd_ref[c])
        return carry

    lax.fori_loop(0, n_chunks, chunk, 0)
    x = x + _rms(facc_ref[...], g3_ref[...])

    gate = _sigmoid(_dot(x.astype(BF16), pg_ref[...]))
    o_ref[...] = x + gate * _dot(p_ref[...].astype(BF16), pp_ref[...])


def _tail(x2, om, on, p2, wo, g1, g2, g3, wg, wu, cw, cb, wd, pg, pp, *, seq):
    n = x2.shape[0]
    tm = 512
    n_chunks = D_FF // FF_CHUNK
    row = lambda r: (r, 0)
    half = om.shape[1]
    return pl.pallas_call(
        functools.partial(_tail_kernel, seq=seq, tm=tm),
        grid=(n // tm,),
        in_specs=[pl.BlockSpec((tm, D_MODEL), row),
                  pl.BlockSpec((tm, half), row),
                  pl.BlockSpec((tm, half), row),
                  pl.BlockSpec((tm, PLE_DIM), row),
                  _resident((2 * half, D_MODEL)),
                  _resident((1, D_MODEL)),
                  _resident((1, D_MODEL)),
                  _resident((1, D_MODEL)),
                  _resident((n_chunks, D_MODEL, FF_CHUNK)),
                  _resident((n_chunks, D_MODEL, FF_CHUNK)),
                  _resident((n_chunks, CONV_WIDTH, FF_CHUNK)),
                  _resident((n_chunks, 1, FF_CHUNK)),
                  _resident((n_chunks, FF_CHUNK, D_MODEL)),
                  _resident((D_MODEL, D_MODEL)),
                  _resident((PLE_DIM, D_MODEL))],
        out_specs=pl.BlockSpec((tm, D_MODEL), row),
        out_shape=jax.ShapeDtypeStruct((n, D_MODEL), F32),
        scratch_shapes=[pltpu.VMEM((n_chunks, 8, FF_CHUNK), F32),
                        pltpu.VMEM((tm, D_MODEL), F32)],
        compiler_params=_params("arbitrary"),
        name="layer_tail",
    )(x2, om, on, p2, wo, g1, g2, g3, wg, wu, cw, cb, wd, pg, pp)


def _rot_cols(w):
    half = w.shape[1] // 2
    return jnp.concatenate([-w[:, half:], w[:, :half]], axis=1)


def _prep_inproj(w):
    o = np.cumsum((0,) + IN_SPLITS)
    parts = [w[:, o[j]:o[j + 1]] for j in range(len(IN_SPLITS))]
    cq, ckv, kr, qn, kcmp, vcmp, kslc, vslc, kwin, vwin, gn = parts
    d = w.shape[0]
    z64 = jnp.zeros((d, NSA_DIM), F32)
    z32 = jnp.zeros((d, LANE - MLA_NOPE - MLA_ROPE), F32)

    def slots(m):
        return jnp.concatenate([m[:, :NSA_DIM], z64, m[:, NSA_DIM:], z64], axis=1)

    per = 3 * NSA_GROUP
    gpad = jnp.zeros((d, LANE - per), F32)
    gates = jnp.concatenate([gn[:, :per], gpad, gn[:, per:], gpad], axis=1)
    w1 = jnp.concatenate([cq, ckv,
                          z64, kr, z32,
                          z64, _rot_cols(kr), z32,
                          qn * NSA_DIM ** -0.5,
                          slots(kslc), slots(vslc), slots(kwin), slots(vwin),
                          kcmp, vcmp, gates], axis=1)
    assert w1.shape[1] == C_TOTAL
    return w1.astype(BF16)


def _prep_mla(w_uq, w_ukv):
    r = w_uq.shape[0]
    dq = MLA_NOPE + MLA_ROPE
    z32 = jnp.zeros((r, LANE - dq), F32)
    z64q = jnp.zeros((r, MLA_NOPE), F32)
    qa, qb = [], []
    for h in range(MLA_HEADS):
        blk = w_uq[:, h * dq:(h + 1) * dq]
        qa += [blk, z32]
        qb += [z64q, _rot_cols(blk[:, MLA_NOPE:]), z32]
    rk = w_ukv.shape[0]
    z64 = jnp.zeros((rk, LANE - MLA_NOPE), F32)
    wk, wv = [], []
    dkv = MLA_NOPE + MLA_V
    for h in range(MLA_HEADS):
        blk = w_ukv[:, h * dkv:(h + 1) * dkv]
        wk += [blk[:, :MLA_NOPE], z64]
        wv += [blk[:, MLA_NOPE:], z64]
    cat = lambda xs: jnp.concatenate(xs, axis=1).astype(BF16)
    return cat(qa), cat(qb), cat(wk), cat(wv)


def kernel(x, p, positions, rel_bias, attn_pre_norm, attn_post_norm, ffn_pre_norm, ffn_post_norm,
           w_in, mla_q_norm, mla_w_uq, mla_kv_norm, mla_w_ukv, nsa_cmp_pos, nsa_cmp_w1, nsa_cmp_w2,
           w_o, ffn_w_gate, ffn_w_up, ffn_conv_w, ffn_conv_b, ffn_w_down, ple_proj, ple_gate):
    batch, seq, d = x.shape
    depth = w_in.shape[0]
    n = batch * seq
    n_chunks = D_FF // FF_CHUNK
    x2 = x.reshape(n, d)
    ct, st = _rope_tables(positions)
    biasc, tsel = _bias_tables(rel_bias, seq)
    for i in range(depth):
        w1 = _prep_inproj(w_in[i])
        wqa, wqb, wk, wv = _prep_mla(mla_w_uq[i], mla_w_ukv[i])
        qm, km, vm, zq, nkv, kcmp, vcmp, gates = _inproj(
            x2, attn_pre_norm[i][None, :], w1, mla_q_norm[i][None, :], wqa, wqb,
            mla_kv_norm[i][None, :], wk, wv, ct, st, batch=batch, seq=seq)
        kc, vc = _compress(kcmp, vcmp,
                           nsa_cmp_pos[i].reshape(2, 1, CMP_LEN * NSA_DIM),
                           nsa_cmp_w1[i].astype(BF16), nsa_cmp_w2[i].astype(BF16),
                           batch=batch, seq=seq)
        o_nsa = _nsa(zq, kc, vc, nkv, gates, biasc, tsel, batch=batch, seq=seq)
        o_mla = _mla(qm, km, vm, batch=batch, seq=seq)
        wg = ffn_w_gate[i].astype(BF16).reshape(d, n_chunks, FF_CHUNK).transpose(1, 0, 2)
        wu = ffn_w_up[i].astype(BF16).reshape(d, n_chunks, FF_CHUNK).transpose(1, 0, 2)
        cw = ffn_conv_w[i].reshape(CONV_WIDTH, n_chunks, FF_CHUNK).transpose(1, 0, 2)
        cb = ffn_conv_b[i].reshape(n_chunks, 1, FF_CHUNK)
        wd = ffn_w_down[i].astype(BF16).reshape(n_chunks, FF_CHUNK, d)
        x2 = _tail(x2, o_mla, o_nsa, p[i].reshape(n, PLE_DIM), w_o[i].astype(BF16),
                   attn_post_norm[i][None, :], ffn_pre_norm[i][None, :], ffn_post_norm[i][None, :],
                   wg, wu, cw, cb, wd, ple_gate[i].astype(BF16), ple_proj[i].astype(BF16), seq=seq)
    return x2.reshape(batch, seq, d)
```

```python
import functools
import math

import numpy as np
import jax
import jax.numpy as jnp
from jax import lax
from jax.experimental import pallas as pl
from jax.experimental.pallas import tpu as pltpu

F32 = jnp.float32
BF16 = jnp.bfloat16

D_MODEL = 1024
DEPTH = 2
MLA_HEADS = 8
MLA_NOPE = 64
MLA_ROPE = 32
MLA_V = 64
MLA_Q_RANK = 256
MLA_KV_RANK = 128
ROPE_BASE = 10000.0
NSA_HEADS = 8
NSA_KV_HEADS = 2
NSA_GROUP = NSA_HEADS // NSA_KV_HEADS
NSA_DIM = 64
CMP_LEN = 32
CMP_STRIDE = 16
CMP_HIDDEN = 128
SLC_LEN = 64
SLC_TOPK = 16
WINDOW = 512
REL_BUCKETS = 32
REL_MAX_DIST = 128
D_FF = 2816
CONV_WIDTH = 3
PLE_DIM = 256
EPS = 1e-6
NEG = -1e30
POS_BIG = 1e30

IN_SPLITS = (MLA_Q_RANK, MLA_KV_RANK, MLA_ROPE, NSA_HEADS * NSA_DIM,
             NSA_KV_HEADS * NSA_DIM, NSA_KV_HEADS * NSA_DIM,
             NSA_KV_HEADS * NSA_DIM, NSA_KV_HEADS * NSA_DIM,
             NSA_KV_HEADS * NSA_DIM, NSA_KV_HEADS * NSA_DIM,
             3 * NSA_HEADS)

LANE = 128
QT = 128
SLC_PAD = 64
FF_CHUNK = 256
VMEM_LIMIT = 56 * 1024 * 1024

C_CQ = 0
C_CKV = 256
C_KR = 384
C_KRROT = 512
C_QN = 640
C_KSLC = 1152
C_VSLC = 1408
C_KWIN = 1664
C_VWIN = 1920
C_KCMP = 2176
C_VCMP = 2304
C_GATE = 2432
C_TOTAL = 2688


def _dot(a, b):
    return jnp.dot(a, b, preferred_element_type=F32)


def _dot_nt(a, b):
    return lax.dot_general(a, b, (((1,), (1,)), ((), ())), preferred_element_type=F32)


def _rms(x, g):
    return x * lax.rsqrt(jnp.mean(x * x, axis=-1, keepdims=True) + EPS) * g


def _gelu_tanh(x):
    return 0.5 * x * (1.0 + jnp.tanh(math.sqrt(2.0 / math.pi) * (x + 0.044715 * (x * x * x))))


def _sigmoid(x):
    return 1.0 / (1.0 + jnp.exp(-x))


def _params(*sem):
    return pltpu.CompilerParams(dimension_semantics=sem, vmem_limit_bytes=VMEM_LIMIT)


def _resident(shape):
    nd = len(shape)
    return pl.BlockSpec(shape, lambda *_: (0,) * nd, pipeline_mode=pl.Buffered(1))


def _rope_kernel(pos_ref, inv_ref, ct_ref, st_ref):
    ang = pos_ref[...].astype(F32) * inv_ref[...]
    lane = lax.broadcasted_iota(jnp.int32, ang.shape, 1)
    rope = (lane >= MLA_NOPE) & (lane < MLA_NOPE + MLA_ROPE)
    ct_ref[...] = jnp.where(rope, jnp.cos(ang), jnp.where(lane < MLA_NOPE, 1.0, 0.0))
    st_ref[...] = jnp.where(rope, jnp.sin(ang), 0.0)


def _rope_tables(positions):
    n = positions.size
    tm = 512
    half = MLA_ROPE // 2
    inv = ROPE_BASE ** (-jnp.arange(half, dtype=F32) / half)
    inv_slot = jnp.concatenate([jnp.zeros((MLA_NOPE,), F32), inv, inv,
                                jnp.zeros((LANE - MLA_NOPE - MLA_ROPE,), F32)])[None, :]
    return pl.pallas_call(
        _rope_kernel,
        grid=(n // tm,),
        in_specs=[pl.BlockSpec((tm, 1), lambda r: (r, 0)),
                  pl.BlockSpec((1, LANE), lambda r: (0, 0))],
        out_specs=[pl.BlockSpec((tm, LANE), lambda r: (r, 0))] * 2,
        out_shape=[jax.ShapeDtypeStruct((n, LANE), F32)] * 2,
        compiler_params=_params("parallel"),
        name="rope_tables",
    )(positions.reshape(n, 1), inv_slot)


def _bucket_np(dist):
    n = np.maximum(dist, 0)
    max_exact = REL_BUCKETS // 2
    large = max_exact + (np.log(np.maximum(n, 1).astype(np.float32) / max_exact)
                         / math.log(REL_MAX_DIST / max_exact)
                         * (REL_BUCKETS - max_exact)).astype(np.int32)
    large = np.minimum(large, REL_BUCKETS - 1)
    return np.where(n < max_exact, n, large).astype(np.int32)


def _bias_kernel(table_ref, bpc_ref, bpt_ref, biasc_ref, tsel_ref, *, nq, ncp):
    h = pl.program_id(0)
    far = table_ref[REL_BUCKETS - 1, h]

    def lookup(bp, sub):
        acc = jnp.full(bp.shape, far - sub, F32)
        for b in range(REL_BUCKETS - 1):
            acc = jnp.where(bp == b, table_ref[b, h] - sub, acc)
        return jnp.where(bp < 0, NEG, acc)

    pat = lookup(bpc_ref[...], 0.0)
    shift = QT // CMP_STRIDE
    for qb in range(nq):
        biasc_ref[0, qb * QT:(qb + 1) * QT, :] = pat[:, ncp - shift * qb: 2 * ncp - shift * qb]
    tsel_ref[0, 0] = lookup(bpt_ref[0], far)
    tsel_ref[0, 1] = lookup(bpt_ref[1], far)


def _bias_tables(rel_bias, seq):
    nq = seq // QT
    ncp = seq // CMP_STRIDE
    i = np.arange(QT)[:, None]
    cprime = np.arange(2 * ncp)[None, :] - ncp
    dist_c = i - CMP_STRIDE * cprime - (CMP_LEN - 1)
    bpc = np.where(dist_c >= 0, _bucket_np(dist_c), -1).astype(np.int32)
    j = np.arange(QT)[None, :]
    d0 = i - j
    d1 = QT + i - j
    bpt = np.stack([np.where(d0 >= 0, _bucket_np(d0), -1), _bucket_np(d1)]).astype(np.int32)
    return pl.pallas_call(
        functools.partial(_bias_kernel, nq=nq, ncp=ncp),
        grid=(NSA_HEADS,),
        in_specs=[pl.BlockSpec(memory_space=pltpu.SMEM),
                  pl.BlockSpec((QT, 2 * ncp), lambda h: (0, 0)),
                  pl.BlockSpec((2, QT, QT), lambda h: (0, 0, 0))],
        out_specs=[pl.BlockSpec((1, seq, ncp), lambda h: (h, 0, 0)),
                   pl.BlockSpec((1, 2, QT, QT), lambda h: (h // NSA_GROUP, 0, h % NSA_GROUP, 0))],
        out_shape=[jax.ShapeDtypeStruct((NSA_HEADS, seq, ncp), F32),
                   jax.ShapeDtypeStruct((NSA_KV_HEADS, 2, NSA_GROUP * QT, QT), F32)],
        compiler_params=_params("parallel"),
        name="bias_tables",
    )(rel_bias.astype(F32), jnp.asarray(bpc), jnp.asarray(bpt))


def _inproj_kernel(x_ref, g_ref, w1_ref, qn_ref, wqa_ref, wqb_ref, kvn_ref, wk_ref, wv_ref,
                   ct_ref, st_ref,
                   qm_ref, km_ref, vm_ref, zq_ref, nkv_ref, kcmp_ref, vcmp_ref, gate_ref,
                   *, seq, tm):
    h = _rms(x_ref[...], g_ref[...]).astype(BF16)

    def proj(c0, width):
        return _dot(h, w1_ref[:, c0:c0 + width])

    ct = ct_ref[...]
    st = st_ref[...]
    lane = lax.broadcasted_iota(jnp.int32, (tm, LANE), 1)
    ones_hi = (lane >= NSA_DIM).astype(F32)

    cq = _rms(proj(C_CQ, MLA_Q_RANK), qn_ref[...]).astype(BF16)
    qa = _dot(cq, wqa_ref[...])
    qb = _dot(cq, wqb_ref[...])
    scale = (MLA_NOPE + MLA_ROPE) ** -0.5
    cts = ct * scale
    sts = st * scale
    for hh in range(MLA_HEADS):
        sl = slice(hh * LANE, (hh + 1) * LANE)
        qm_ref[:, sl] = (qa[:, sl] * cts + qb[:, sl] * sts).astype(BF16)

    ckv = _rms(proj(C_CKV, MLA_KV_RANK), kvn_ref[...]).astype(BF16)
    kr = proj(C_KR, LANE) * ct + proj(C_KRROT, LANE) * st
    kn = _dot(ckv, wk_ref[...])
    vv = _dot(ckv, wv_ref[...])
    for hh in range(MLA_HEADS):
        sl = slice(hh * LANE, (hh + 1) * LANE)
        km_ref[:, sl] = (kn[:, sl] + kr).astype(BF16)
        vm_ref[:, sl] = (vv[:, sl] + ones_hi).astype(BF16)

    zq_ref[...] = proj(C_QN, NSA_HEADS * NSA_DIM).astype(BF16)

    s0 = (pl.program_id(0) * tm) % seq
    row = lax.broadcasted_iota(jnp.int32, (tm, LANE), 0)
    blk = (s0 + row) // SLC_LEN
    onehot = (lane - NSA_DIM == blk).astype(F32)
    ksl = proj(C_KSLC, 2 * LANE)
    vsl = proj(C_VSLC, 2 * LANE)
    kwn = proj(C_KWIN, 2 * LANE)
    vwn = proj(C_VWIN, 2 * LANE)
    for hk in range(NSA_KV_HEADS):
        sl = slice(hk * LANE, (hk + 1) * LANE)
        nkv_ref[:, 0 * 2 * LANE + hk * LANE: 0 * 2 * LANE + (hk + 1) * LANE] = (ksl[:, sl] + onehot).astype(BF16)
        nkv_ref[:, 1 * 2 * LANE + hk * LANE: 1 * 2 * LANE + (hk + 1) * LANE] = (vsl[:, sl] + ones_hi).astype(BF16)
        nkv_ref[:, 2 * 2 * LANE + hk * LANE: 2 * 2 * LANE + (hk + 1) * LANE] = kwn[:, sl].astype(BF16)
        nkv_ref[:, 3 * 2 * LANE + hk * LANE: 3 * 2 * LANE + (hk + 1) * LANE] = (vwn[:, sl] + ones_hi).astype(BF16)

    kcmp = proj(C_KCMP, LANE)
    vcmp = proj(C_VCMP, LANE)
    for hk in range(NSA_KV_HEADS):
        kcmp_ref[0, hk] = kcmp[:, hk * NSA_DIM:(hk + 1) * NSA_DIM]
        vcmp_ref[0, hk] = vcmp[:, hk * NSA_DIM:(hk + 1) * NSA_DIM]

    gate_ref[...] = _sigmoid(proj(C_GATE, 2 * LANE))


def _inproj(x2, g_pre, w1, qn, wqa, wqb, kvn, wk, wv, ct, st, *, batch, seq):
    n = x2.shape[0]
    tm = 512
    tiles_per_seq = seq // tm
    row = lambda r: (r, 0)
    hm = lambda r: (r // tiles_per_seq, 0, r % tiles_per_seq, 0)
    return pl.pallas_call(
        functools.partial(_inproj_kernel, seq=seq, tm=tm),
        grid=(n // tm,),
        in_specs=[pl.BlockSpec((tm, D_MODEL), row),
                  _resident((1, D_MODEL)),
                  _resident((D_MODEL, C_TOTAL)),
                  _resident((1, MLA_Q_RANK)),
                  _resident((MLA_Q_RANK, MLA_HEADS * LANE)),
                  _resident((MLA_Q_RANK, MLA_HEADS * LANE)),
                  _resident((1, MLA_KV_RANK)),
                  _resident((MLA_KV_RANK, MLA_HEADS * LANE)),
                  _resident((MLA_KV_RANK, MLA_HEADS * LANE)),
                  pl.BlockSpec((tm, LANE), row),
                  pl.BlockSpec((tm, LANE), row)],
        out_specs=[pl.BlockSpec((tm, MLA_HEADS * LANE), row),
                   pl.BlockSpec((tm, MLA_HEADS * LANE), row),
                   pl.BlockSpec((tm, MLA_HEADS * LANE), row),
                   pl.BlockSpec((tm, NSA_HEADS * NSA_DIM), row),
                   pl.BlockSpec((tm, 8 * LANE), row),
                   pl.BlockSpec((1, NSA_KV_HEADS, tm, NSA_DIM), hm),
                   pl.BlockSpec((1, NSA_KV_HEADS, tm, NSA_DIM), hm),
                   pl.BlockSpec((tm, 2 * LANE), row)],
        out_shape=[jax.ShapeDtypeStruct((n, MLA_HEADS * LANE), BF16),
                   jax.ShapeDtypeStruct((n, MLA_HEADS * LANE), BF16),
                   jax.ShapeDtypeStruct((n, MLA_HEADS * LANE), BF16),
                   jax.ShapeDtypeStruct((n, NSA_HEADS * NSA_DIM), BF16),
                   jax.ShapeDtypeStruct((n, 8 * LANE), BF16),
                   jax.ShapeDtypeStruct((batch, NSA_KV_HEADS, seq, NSA_DIM), F32),
                   jax.ShapeDtypeStruct((batch, NSA_KV_HEADS, seq, NSA_DIM), F32),
                   jax.ShapeDtypeStruct((n, 2 * LANE), F32)],
        compiler_params=_params("parallel"),
        name="in_proj",
    )(x2, g_pre, w1, qn, wqa, wqb, kvn, wk, wv, ct, st)


def _compress_kernel(k16_ref, v16_ref, pos_ref, w1_ref, w2_ref, kc_ref, vc_ref):
    half = CMP_STRIDE * NSA_DIM

    def one(x16, j):
        pos = pos_ref[j]
        top = (x16 + pos[:, :half]).astype(BF16)
        bot = (x16 + pos[:, half:]).astype(BF16)
        u = _dot(top, w1_ref[j, :half, :])
        low = _dot(bot, w1_ref[j, half:, :])
        nxt = jnp.concatenate([low[1:], jnp.zeros((1, CMP_HIDDEN), F32)], axis=0)
        hid = _gelu_tanh(u + nxt).astype(BF16)
        return _dot(hid, w2_ref[j])

    kc_ref[0, 0] = one(k16_ref[0, 0], 0).astype(BF16)
    vc_ref[0, 0] = one(v16_ref[0, 0], 1).astype(BF16)


def _compress(kcmp, vcmp, pos, w1, w2, *, batch, seq):
    ncp = seq // CMP_STRIDE
    half = CMP_STRIDE * NSA_DIM
    k16 = kcmp.reshape(batch, NSA_KV_HEADS, ncp, half)
    v16 = vcmp.reshape(batch, NSA_KV_HEADS, ncp, half)
    blk = lambda b, hk: (b, hk, 0, 0)
    return pl.pallas_call(
        _compress_kernel,
        grid=(batch, NSA_KV_HEADS),
        in_specs=[pl.BlockSpec((1, 1, ncp, half), blk),
                  pl.BlockSpec((1, 1, ncp, half), blk),
                  pl.BlockSpec((2, 1, 2 * half), lambda b, hk: (0, 0, 0)),
                  pl.BlockSpec((2, 2 * half, CMP_HIDDEN), lambda b, hk: (0, 0, 0)),
                  pl.BlockSpec((2, CMP_HIDDEN, NSA_DIM), lambda b, hk: (0, 0, 0))],
        out_specs=[pl.BlockSpec((1, 1, ncp, NSA_DIM), blk)] * 2,
        out_shape=[jax.ShapeDtypeStruct((batch, NSA_KV_HEADS, ncp, NSA_DIM), BF16)] * 2,
        compiler_params=_params("parallel", "parallel"),
        name="nsa_compress",
    )(k16, v16, pos, w1, w2)


def _nsa_kernel(zq_ref, kc_ref, vc_ref, ksl_ref, vsl_ref, kw_ref, vw_ref, gate_ref,
                biasc_ref, tsel_ref, ovt_ref, o_ref, imp_ref, acc_ref, m_ref):
    qi = pl.program_id(2)
    G = NSA_GROUP
    M = G * QT
    q4 = zq_ref[...]
    qs = jnp.concatenate([q4[:, g * NSA_DIM:(g + 1) * NSA_DIM] for g in range(G)], axis=0)
    kc = kc_ref[0, 0]
    vc = vc_ref[0, 0]
    ovt = ovt_ref[...]
    ncp = kc.shape[0]

    qrow = qi * QT + (lax.broadcasted_iota(jnp.int32, (M, 1), 0) & (QT - 1))
    rowmask = (qrow >= CMP_LEN - 1).astype(F32)

    s = _dot_nt(qs, kc) + biasc_ref[...].reshape(M, ncp)
    e = jnp.exp(s - jnp.max(s, axis=-1, keepdims=True))
    p_cmp = (e / jnp.sum(e, axis=-1, keepdims=True) * rowmask).astype(BF16)
    o_cmp = _dot(p_cmp, vc)
    imp_t = jnp.zeros((SLC_PAD, QT), F32)
    for g in range(G):
        imp_t = imp_t + _dot_nt(ovt, p_cmp[g * QT:(g + 1) * QT])

    n_id = lax.broadcasted_iota(jnp.int32, (SLC_PAD, QT), 0)
    q_blk = (qi * QT + lax.broadcasted_iota(jnp.int32, (SLC_PAD, QT), 1)) // SLC_LEN
    forced = (n_id == 0) | (n_id == q_blk) | (n_id == q_blk - 1)
    imp = jnp.where(forced, POS_BIG, jnp.where(n_id > q_blk, NEG, imp_t))
    imp_ref[...] = imp
    rank = jnp.zeros((SLC_PAD, QT), jnp.int32)
    for m in range(SLC_PAD):
        other = imp_ref[m:m + 1, :]
        rank = rank + jnp.where(n_id > m, (other >= imp).astype(jnp.int32),
                                (other > imp).astype(jnp.int32))
    selb_t = jnp.where(rank < SLC_TOPK, 0.0, NEG)
    selb = selb_t.T.astype(BF16)
    qaug = jnp.concatenate([qs, jnp.concatenate([selb] * G, axis=0)], axis=1)

    def softmax_update(s_tiles, v_rows):
        m_old = m_ref[...]
        tile_max = s_tiles[0]
        for st in s_tiles[1:]:
            tile_max = jnp.maximum(tile_max, st)
        m_new = jnp.maximum(m_old, jnp.max(tile_max, axis=-1, keepdims=True))
        p = jnp.concatenate([jnp.exp(st - m_new).astype(BF16) for st in s_tiles], axis=1)
        pv = _dot(p, v_rows)
        acc_ref[...] = acc_ref[...] * jnp.exp(m_old - m_new) + pv
        m_ref[...] = m_new

    def reset():
        m_ref[...] = jnp.full(m_ref.shape, NEG, F32)
        acc_ref[...] = jnp.zeros(acc_ref.shape, F32)

    def finish():
        a = acc_ref[...]
        return a[:, :NSA_DIM] / a[:, NSA_DIM:NSA_DIM + 1]

    def scores(k_rows):
        n_t = k_rows.shape[0] // QT
        sc = _dot_nt(qaug, k_rows)
        return [sc[:, j * QT:(j + 1) * QT] for j in range(n_t)]

    reset()
    n_far = jnp.maximum(qi - 1, 0)
    FAR = 4

    def far_big(t, carry):
        off = pl.multiple_of(t * (FAR * QT), FAR * QT)
        softmax_update(scores(ksl_ref[pl.ds(off, FAR * QT), :]), vsl_ref[pl.ds(off, FAR * QT), :])
        return carry

    def far_small(t, carry):
        off = pl.multiple_of(t * QT, QT)
        softmax_update(scores(ksl_ref[pl.ds(off, QT), :]), vsl_ref[pl.ds(off, QT), :])
        return carry

    n_big = n_far // FAR
    lax.fori_loop(0, n_big, far_big, 0)
    lax.fori_loop(n_big * FAR, n_far, far_small, 0)

    off_p = pl.multiple_of(n_far * QT, QT)
    off_d = pl.multiple_of(qi * QT, QT)
    no_prev = jnp.where(qi == 0, NEG, 0.0)
    k2 = jnp.concatenate([ksl_ref[pl.ds(off_p, QT), :], ksl_ref[pl.ds(off_d, QT), :]], axis=0)
    v2 = jnp.concatenate([vsl_ref[pl.ds(off_p, QT), :], vsl_ref[pl.ds(off_d, QT), :]], axis=0)
    s_p, s_d = scores(k2)
    softmax_update([s_p + (tsel_ref[0, 1] + no_prev), s_d + tsel_ref[0, 0]], v2)
    o_sel = finish()

    reset()
    n_tiles = WINDOW // QT + 1
    lane = lax.broadcasted_iota(jnp.int32, (QT, LANE), 1)
    pad_v = jnp.where(lane >= NSA_DIM, 1.0, 0.0).astype(BF16)
    ti = lax.broadcasted_iota(jnp.int32, (M, QT), 0) & (QT - 1)
    tj = lax.broadcasted_iota(jnp.int32, (M, QT), 1)
    k_rows, v_rows = [], []
    for w in range(n_tiles):
        tile = qi - (n_tiles - 1) + w
        off = pl.multiple_of(jnp.maximum(tile, 0) * QT, QT)
        kt = kw_ref[pl.ds(off, QT), :]
        vt = vw_ref[pl.ds(off, QT), :]
        if w < n_tiles - 1:
            kt = jnp.where(tile >= 0, kt, jnp.zeros_like(kt))
            vt = jnp.where(tile >= 0, vt, pad_v)
        k_rows.append(kt)
        v_rows.append(vt)
    s_w = scores(jnp.concatenate(k_rows, axis=0))
    s_w[0] = s_w[0] + jnp.where(tj > ti, 0.0, NEG)
    s_w[n_tiles - 2] = s_w[n_tiles - 2] + tsel_ref[0, 1]
    s_w[n_tiles - 1] = s_w[n_tiles - 1] + tsel_ref[0, 0]
    softmax_update(s_w, jnp.concatenate(v_rows, axis=0))
    o_win = finish()

    gate = gate_ref[...]
    outs = []
    for g in range(G):
        rows = slice(g * QT, (g + 1) * QT)
        outs.append(gate[:, 3 * g:3 * g + 1] * o_cmp[rows] + gate[:, 3 * g + 1:3 * g + 2] * o_sel[rows]
                    + gate[:, 3 * g + 2:3 * g + 3] * o_win[rows])
    o_ref[...] = jnp.concatenate(outs, axis=1).astype(BF16)


def _overlap_t(seq):
    ncp = seq // CMP_STRIDE
    n_cmp = (seq - CMP_LEN) // CMP_STRIDE + 1
    n_slc = seq // SLC_LEN
    cs = np.arange(ncp)[None, :] * CMP_STRIDE
    ss = np.arange(SLC_PAD)[:, None] * SLC_LEN
    ov = np.maximum(np.minimum(cs + CMP_LEN, ss + SLC_LEN) - np.maximum(cs, ss), 0).astype(np.float32) / CMP_LEN
    ov = ov * (np.arange(ncp)[None, :] < n_cmp) * (np.arange(SLC_PAD)[:, None] < n_slc)
    return jnp.asarray(ov, BF16)


def _nsa(zq, kc, vc, nkv, gates, biasc, tsel, *, batch, seq):
    n = zq.shape[0]
    nq = seq // QT
    ncp = seq // CMP_STRIDE
    G = NSA_GROUP
    assert seq // SLC_LEN <= SLC_PAD and seq // SLC_LEN >= SLC_TOPK and seq >= WINDOW
    rowblk = lambda b, hk, qi: (b * nq + qi, hk)
    kvblk = lambda j: (lambda b, hk, qi: (b, 2 * j + hk))
    return pl.pallas_call(
        _nsa_kernel,
        grid=(batch, NSA_KV_HEADS, nq),
        in_specs=[pl.BlockSpec((QT, G * NSA_DIM), rowblk),
                  pl.BlockSpec((1, 1, ncp, NSA_DIM), lambda b, hk, qi: (b, hk, 0, 0)),
                  pl.BlockSpec((1, 1, ncp, NSA_DIM), lambda b, hk, qi: (b, hk, 0, 0)),
                  pl.BlockSpec((seq, LANE), kvblk(0)),
                  pl.BlockSpec((seq, LANE), kvblk(1)),
                  pl.BlockSpec((seq, LANE), kvblk(2)),
                  pl.BlockSpec((seq, LANE), kvblk(3)),
                  pl.BlockSpec((QT, LANE), rowblk),
                  pl.BlockSpec((G, QT, ncp), lambda b, hk, qi: (hk, qi, 0)),
                  pl.BlockSpec((1, 2, G * QT, QT), lambda b, hk, qi: (hk, 0, 0, 0)),
                  pl.BlockSpec((SLC_PAD, ncp), lambda b, hk, qi: (0, 0))],
        out_specs=pl.BlockSpec((QT, G * NSA_DIM), rowblk),
        out_shape=jax.ShapeDtypeStruct((n, NSA_HEADS * NSA_DIM), BF16),
        scratch_shapes=[pltpu.VMEM((SLC_PAD, QT), F32),
                        pltpu.VMEM((G * QT, LANE), F32),
                        pltpu.VMEM((G * QT, LANE), F32)],
        compiler_params=_params("parallel", "parallel", "arbitrary"),
        name="nsa_attention",
    )(zq, kc, vc, nkv, nkv, nkv, nkv, gates, biasc, tsel, _overlap_t(seq))


MLA_TQ = 512


def _mla_kernel(q_ref, k_ref, v_ref, o_ref, acc_ref, m_ref):
    qi = pl.program_id(2)
    tq = MLA_TQ
    HP = 2
    qs = [q_ref[:, hh * LANE:(hh + 1) * LANE] for hh in range(HP)]
    m_ref[...] = jnp.full(m_ref.shape, NEG, F32)
    acc_ref[...] = jnp.zeros(acc_ref.shape, F32)

    def step(hh, off, mask):
        kt = k_ref[pl.ds(off, tq), hh * LANE:(hh + 1) * LANE]
        vt = v_ref[pl.ds(off, tq), hh * LANE:(hh + 1) * LANE]
        s = _dot_nt(qs[hh], kt)
        if mask is not None:
            s = s + mask
        tiles = [s[:, j * LANE:(j + 1) * LANE] for j in range(tq // LANE)]
        tile_max = tiles[0]
        for st in tiles[1:]:
            tile_max = jnp.maximum(tile_max, st)
        m_old = m_ref[hh]
        m_new = jnp.maximum(m_old, jnp.max(tile_max, axis=-1, keepdims=True))
        p = jnp.concatenate([jnp.exp(st - m_new).astype(BF16) for st in tiles], axis=1)
        acc_ref[hh] = acc_ref[hh] * jnp.exp(m_old - m_new) + _dot(p, vt)
        m_ref[hh] = m_new

    def body(t, carry):
        off = pl.multiple_of(t * tq, tq)
        for hh in range(HP):
            step(hh, off, None)
        return carry

    lax.fori_loop(0, qi, body, 0)
    ti = lax.broadcasted_iota(jnp.int32, (tq, tq), 0)
    tj = lax.broadcasted_iota(jnp.int32, (tq, tq), 1)
    causal = jnp.where(tj <= ti, 0.0, NEG)
    off_d = pl.multiple_of(qi * tq, tq)
    outs = []
    for hh in range(HP):
        step(hh, off_d, causal)
        a = acc_ref[hh]
        outs.append(a[:, :MLA_V] / a[:, MLA_V:MLA_V + 1])
    o_ref[...] = jnp.concatenate(outs, axis=1).astype(BF16)


def _mla(qm, km, vm, *, batch, seq):
    n = qm.shape[0]
    tq = MLA_TQ
    nq = seq // tq
    HP = 2
    rowblk = lambda b, hp, qi: (b * nq + qi, hp)
    kvblk = lambda b, hp, qi: (b, hp)
    return pl.pallas_call(
        _mla_kernel,
        grid=(batch, MLA_HEADS // HP, nq),
        in_specs=[pl.BlockSpec((tq, HP * LANE), rowblk),
                  pl.BlockSpec((seq, HP * LANE), kvblk),
                  pl.BlockSpec((seq, HP * LANE), kvblk)],
        out_specs=pl.BlockSpec((tq, HP * MLA_V), rowblk),
        out_shape=jax.ShapeDtypeStruct((n, MLA_HEADS * MLA_V), BF16),
        scratch_shapes=[pltpu.VMEM((HP, tq, LANE), F32),
                        pltpu.VMEM((HP, tq, LANE), F32)],
        compiler_params=_params("parallel", "parallel", "arbitrary"),
        name="mla_attention",
    )(qm, km, vm)


def _tail_kernel(x_ref, om_ref, on_ref, p_ref, wo_ref, g1_ref, g2_ref, g3_ref,
                 wg_ref, wu_ref, cw_ref, cb_ref, wd_ref, pg_ref, pp_ref,
                 o_ref, carry_ref, facc_ref, *, seq, tm):
    n_chunks = D_FF // FF_CHUNK
    half = om_ref.shape[1]
    y = _dot(om_ref[...], wo_ref[:half, :]) + _dot(on_ref[...], wo_ref[half:, :])
    x = x_ref[...] + _rms(y, g1_ref[...])

    h = _rms(x, g2_ref[...]).astype(BF16)

    @pl.when((pl.program_id(0) * tm) % seq == 0)
    def _():
        carry_ref[...] = jnp.zeros(carry_ref.shape, F32)

    row = lax.broadcasted_iota(jnp.int32, (tm, 1), 0)
    facc_ref[...] = jnp.zeros(facc_ref.shape, F32)

    def chunk(c, carry):
        g = _dot(h, wg_ref[c])
        prev = carry_ref[c]
        carry_ref[c] = g[tm - 8:, :]
        g1 = jnp.where(row == 0, prev[7:8, :], pltpu.roll(g, 1, 0))
        g2 = jnp.where(row == 0, prev[6:7, :], jnp.where(row == 1, prev[7:8, :], pltpu.roll(g, 2, 0)))
        cw = cw_ref[c]
        conv = cw[0:1, :] * g2 + cw[1:2, :] * g1 + cw[2:3, :] * g + cb_ref[c]
        act = (_gelu_tanh(conv) * _dot(h, wu_ref[c])).astype(BF16)
        facc_ref[...] += _dot(act, wd_ref[c])
        return carry

    lax.fori_loop(0, n_chunks, chunk, 0)
    x = x + _rms(facc_ref[...], g3_ref[...])

    gate = _sigmoid(_dot(x.astype(BF16), pg_ref[...]))
    o_ref[...] = x + gate * _dot(p_ref[...].astype(BF16), pp_ref[...])


def _tail(x2, om, on, p2, wo, g1, g2, g3, wg, wu, cw, cb, wd, pg, pp, *, seq):
    n = x2.shape[0]
    tm = 512
    n_chunks = D_FF // FF_CHUNK
    row = lambda r: (r, 0)
    half = om.shape[1]
    return pl.pallas_call(
        functools.partial(_tail_kernel, seq=seq, tm=tm),
        grid=(n // tm,),
        in_specs=[pl.BlockSpec((tm, D_MODEL), row),
                  pl.BlockSpec((tm, half), row),
                  pl.BlockSpec((tm, half), row),
                  pl.BlockSpec((tm, PLE_DIM), row),
                  _resident((2 * half, D_MODEL)),
                  _resident((1, D_MODEL)),
                  _resident((1, D_MODEL)),
                  _resident((1, D_MODEL)),
                  _resident((n_chunks, D_MODEL, FF_CHUNK)),
                  _resident((n_chunks, D_MODEL, FF_CHUNK)),
                  _resident((n_chunks, CONV_WIDTH, FF_CHUNK)),
                  _resident((n_chunks, 1, FF_CHUNK)),
                  _resident((n_chunks, FF_CHUNK, D_MODEL)),
                  _resident((D_MODEL, D_MODEL)),
                  _resident((PLE_DIM, D_MODEL))],
        out_specs=pl.BlockSpec((tm, D_MODEL), row),
        out_shape=jax.ShapeDtypeStruct((n, D_MODEL), F32),
        scratch_shapes=[pltpu.VMEM((n_chunks, 8, FF_CHUNK), F32),
                        pltpu.VMEM((tm, D_MODEL), F32)],
        compiler_params=_params("arbitrary"),
        name="layer_tail",
    )(x2, om, on, p2, wo, g1, g2, g3, wg, wu, cw, cb, wd, pg, pp)


def _rot_cols(w):
    half = w.shape[1] // 2
    return jnp.concatenate([-w[:, half:], w[:, :half]], axis=1)


def _prep_inproj(w):
    o = np.cumsum((0,) + IN_SPLITS)
    parts = [w[:, o[j]:o[j + 1]] for j in range(len(IN_SPLITS))]
    cq, ckv, kr, qn, kcmp, vcmp, kslc, vslc, kwin, vwin, gn = parts
    d = w.shape[0]
    z64 = jnp.zeros((d, NSA_DIM), F32)
    z32 = jnp.zeros((d, LANE - MLA_NOPE - MLA_ROPE), F32)

    def slots(m):
        return jnp.concatenate([m[:, :NSA_DIM], z64, m[:, NSA_DIM:], z64], axis=1)

    per = 3 * NSA_GROUP
    gpad = jnp.zeros((d, LANE - per), F32)
    gates = jnp.concatenate([gn[:, :per], gpad, gn[:, per:], gpad], axis=1)
    w1 = jnp.concatenate([cq, ckv,
                          z64, kr, z32,
                          z64, _rot_cols(kr), z32,
                          qn * NSA_DIM ** -0.5,
                          slots(kslc), slots(vslc), slots(kwin), slots(vwin),
                          kcmp, vcmp, gates], axis=1)
    assert w1.shape[1] == C_TOTAL
    return w1.astype(BF16)


def _prep_mla(w_uq, w_ukv):
    r = w_uq.shape[0]
    dq = MLA_NOPE + MLA_ROPE
    z32 = jnp.zeros((r, LANE - dq), F32)
    z64q = jnp.zeros((r, MLA_NOPE), F32)
    qa, qb = [], []
    for h in range(MLA_HEADS):
        blk = w_uq[:, h * dq:(h + 1) * dq]
        qa += [blk, z32]
        qb += [z64q, _rot_cols(blk[:, MLA_NOPE:]), z32]
    rk = w_ukv.shape[0]
    z64 = jnp.zeros((rk, LANE - MLA_NOPE), F32)
    wk, wv = [], []
    dkv = MLA_NOPE + MLA_V
    for h in range(MLA_HEADS):
        blk = w_ukv[:, h * dkv:(h + 1) * dkv]
        wk += [blk[:, :MLA_NOPE], z64]
        wv += [blk[:, MLA_NOPE:], z64]
    cat = lambda xs: jnp.concatenate(xs, axis=1).astype(BF16)
    return cat(qa), cat(qb), cat(wk), cat(wv)


def kernel(x, p, positions, rel_bias, attn_pre_norm, attn_post_norm, ffn_pre_norm, ffn_post_norm,
           w_in, mla_q_norm, mla_w_uq, mla_kv_norm, mla_w_ukv, nsa_cmp_pos, nsa_cmp_w1, nsa_cmp_w2,
           w_o, ffn_w_gate, ffn_w_up, ffn_conv_w, ffn_conv_b, ffn_w_down, ple_proj, ple_gate):
    batch, seq, d = x.shape
    depth = w_in.shape[0]
    n = batch * seq
    n_chunks = D_FF // FF_CHUNK
    x2 = x.reshape(n, d)
    ct, st = _rope_tables(positions)
    biasc, tsel = _bias_tables(rel_bias, seq)
    for i in range(depth):
        w1 = _prep_inproj(w_in[i])
        wqa, wqb, wk, wv = _prep_mla(mla_w_uq[i], mla_w_ukv[i])
        qm, km, vm, zq, nkv, kcmp, vcmp, gates = _inproj(
            x2, attn_pre_norm[i][None, :], w1, mla_q_norm[i][None, :], wqa, wqb,
            mla_kv_norm[i][None, :], wk, wv, ct, st, batch=batch, seq=seq)
        kc, vc = _compress(kcmp, vcmp,
                           nsa_cmp_pos[i].reshape(2, 1, CMP_LEN * NSA_DIM),
                           nsa_cmp_w1[i].astype(BF16), nsa_cmp_w2[i].astype(BF16),
                           batch=batch, seq=seq)
        o_nsa = _nsa(zq, kc, vc, nkv, gates, biasc, tsel, batch=batch, seq=seq)
        o_mla = _mla(qm, km, vm, batch=batch, seq=seq)
        wg = ffn_w_gate[i].astype(BF16).reshape(d, n_chunks, FF_CHUNK).transpose(1, 0, 2)
        wu = ffn_w_up[i].astype(BF16).reshape(d, n_chunks, FF_CHUNK).transpose(1, 0, 2)
        cw = ffn_conv_w[i].reshape(CONV_WIDTH, n_chunks, FF_CHUNK).transpose(1, 0, 2)
        cb = ffn_conv_b[i].reshape(n_chunks, 1, FF_CHUNK)
        wd = ffn_w_down[i].astype(BF16).reshape(n_chunks, FF_CHUNK, d)
        x2 = _tail(x2, o_mla, o_nsa, p[i].reshape(n, PLE_DIM), w_o[i].astype(BF16),
                   attn_post_norm[i][None, :], ffn_pre_norm[i][None, :], ffn_post_norm[i][None, :],
                   wg, wu, cw, cb, wd, ple_gate[i].astype(BF16), ple_proj[i].astype(BF16), seq=seq)
    return x2.reshape(batch, seq, d)
```

```python
import functools
import math

import numpy as np
import jax
import jax.numpy as jnp
from jax import lax
from jax.experimental import pallas as pl
from jax.experimental.pallas import tpu as pltpu

F32 = jnp.float32
BF16 = jnp.bfloat16

D_MODEL = 1024
DEPTH = 2
MLA_HEADS = 8
MLA_NOPE = 64
MLA_ROPE = 32
MLA_V = 64
MLA_Q_RANK = 256
MLA_KV_RANK = 128
ROPE_BASE = 10000.0
NSA_HEADS = 8
NSA_KV_HEADS = 2
NSA_GROUP = NSA_HEADS // NSA_KV_HEADS
NSA_DIM = 64
CMP_LEN = 32
CMP_STRIDE = 16
CMP_HIDDEN = 128
SLC_LEN = 64
SLC_TOPK = 16
WINDOW = 512
REL_BUCKETS = 32
REL_MAX_DIST = 128
D_FF = 2816
CONV_WIDTH = 3
PLE_DIM = 256
EPS = 1e-6
NEG = -1e30
POS_BIG = 1e30
LOG2E = math.log2(math.e)

IN_SPLITS = (MLA_Q_RANK, MLA_KV_RANK, MLA_ROPE, NSA_HEADS * NSA_DIM,
             NSA_KV_HEADS * NSA_DIM, NSA_KV_HEADS * NSA_DIM,
             NSA_KV_HEADS * NSA_DIM, NSA_KV_HEADS * NSA_DIM,
             NSA_KV_HEADS * NSA_DIM, NSA_KV_HEADS * NSA_DIM,
             3 * NSA_HEADS)

LANE = 128
QT = 128
SLC_PAD = 64
FF_CHUNK = 256
VMEM_LIMIT = 56 * 1024 * 1024

VROWS = 80

C_CQ = 0
C_CKV = 256
C_KR = 384
C_KRROT = 512
C_KSLC = 640
C_KWIN = 896
C_KCMP = 1152
C_VCMP = 1280
C_TOTAL = 1408
R_QN = 0
R_VSLC = 512
R_VWIN = R_VSLC + NSA_KV_HEADS * VROWS
R_GATE = R_VWIN + NSA_KV_HEADS * VROWS
GATE_ROWS = 16
R_TOTAL = R_GATE + NSA_KV_HEADS * GATE_ROWS


def _dot(a, b):
    return jnp.dot(a, b, preferred_element_type=F32)


def _dot_nt(a, b):
    return lax.dot_general(a, b, (((1,), (1,)), ((), ())), preferred_element_type=F32)


def _rms(x, g):
    return x * lax.rsqrt(jnp.mean(x * x, axis=-1, keepdims=True) + EPS) * g


def _gelu_tanh(x):
    return 0.5 * x * (1.0 + jnp.tanh(math.sqrt(2.0 / math.pi) * (x + 0.044715 * (x * x * x))))


def _sigmoid(x):
    return 1.0 / (1.0 + jnp.exp(-x))


def _params(*sem):
    return pltpu.CompilerParams(dimension_semantics=sem, vmem_limit_bytes=VMEM_LIMIT)


def _resident(shape):
    nd = len(shape)
    return pl.BlockSpec(shape, lambda *_: (0,) * nd, pipeline_mode=pl.Buffered(1))


def _rope_kernel(pos_ref, inv_ref, ct_ref, st_ref):
    ang = pos_ref[...].astype(F32) * inv_ref[...]
    lane = lax.broadcasted_iota(jnp.int32, ang.shape, 1)
    rope = (lane >= MLA_NOPE) & (lane < MLA_NOPE + MLA_ROPE)
    ct_ref[...] = jnp.where(rope, jnp.cos(ang), jnp.where(lane < MLA_NOPE, 1.0, 0.0))
    st_ref[...] = jnp.where(rope, jnp.sin(ang), 0.0)


def _rope_tables(positions):
    n = positions.size
    tm = 512
    half = MLA_ROPE // 2
    inv = ROPE_BASE ** (-jnp.arange(half, dtype=F32) / half)
    inv_slot = jnp.concatenate([jnp.zeros((MLA_NOPE,), F32), inv, inv,
                                jnp.zeros((LANE - MLA_NOPE - MLA_ROPE,), F32)])[None, :]
    return pl.pallas_call(
        _rope_kernel,
        grid=(n // tm,),
        in_specs=[pl.BlockSpec((tm, 1), lambda r: (r, 0)),
                  pl.BlockSpec((1, LANE), lambda r: (0, 0))],
        out_specs=[pl.BlockSpec((tm, LANE), lambda r: (r, 0))] * 2,
        out_shape=[jax.ShapeDtypeStruct((n, LANE), F32)] * 2,
        compiler_params=_params("parallel"),
        name="rope_tables",
    )(positions.reshape(n, 1), inv_slot)


def _rope_t_kernel(pos_ref, inv_ref, ct_ref, st_ref):
    ang = inv_ref[...] * pos_ref[...].astype(F32)
    row = lax.broadcasted_iota(jnp.int32, ang.shape, 0)
    rope = (row >= MLA_NOPE) & (row < MLA_NOPE + MLA_ROPE)
    ct_ref[...] = jnp.where(rope, jnp.cos(ang), jnp.where(row < MLA_NOPE, 1.0, 0.0))
    st_ref[...] = jnp.where(rope, jnp.sin(ang), 0.0)


def _rope_tables_t(positions):
    n = positions.size
    tn = 2048
    half = MLA_ROPE // 2
    inv = ROPE_BASE ** (-jnp.arange(half, dtype=F32) / half)
    inv_slot = jnp.concatenate([jnp.zeros((MLA_NOPE,), F32), inv, inv,
                                jnp.zeros((LANE - MLA_NOPE - MLA_ROPE,), F32)])[:, None]
    return pl.pallas_call(
        _rope_t_kernel,
        grid=(n // tn,),
        in_specs=[pl.BlockSpec((1, tn), lambda r: (0, r)),
                  pl.BlockSpec((LANE, 1), lambda r: (0, 0))],
        out_specs=[pl.BlockSpec((LANE, tn), lambda r: (0, r))] * 2,
        out_shape=[jax.ShapeDtypeStruct((LANE, n), F32)] * 2,
        compiler_params=_params("parallel"),
        name="rope_tables_t",
    )(positions.reshape(1, n), inv_slot)


def _bucket_np(dist):
    n = np.maximum(dist, 0)
    max_exact = REL_BUCKETS // 2
    large = max_exact + (np.log(np.maximum(n, 1).astype(np.float32) / max_exact)
                         / math.log(REL_MAX_DIST / max_exact)
                         * (REL_BUCKETS - max_exact)).astype(np.int32)
    large = np.minimum(large, REL_BUCKETS - 1)
    return np.where(n < max_exact, n, large).astype(np.int32)


def _bias_kernel(table_ref, bpc_ref, bpt_ref, pat_ref, tsel_ref):
    h = pl.program_id(0)
    far = table_ref[REL_BUCKETS - 1, h]

    def lookup(bp, sub):
        acc = jnp.full(bp.shape, far - sub, F32)
        for b in range(REL_BUCKETS - 1):
            acc = jnp.where(bp == b, table_ref[b, h] - sub, acc)
        return jnp.where(bp < 0, NEG, acc * LOG2E)

    pat_ref[0] = lookup(bpc_ref[...], 0.0)
    tsel_ref[0, 0] = lookup(bpt_ref[0], far)
    tsel_ref[0, 1] = lookup(bpt_ref[1], far)


def _bias_tables(rel_bias, seq):
    ncp = seq // CMP_STRIDE
    i = np.arange(QT)[None, :]
    cprime = np.arange(2 * ncp)[:, None] - ncp
    dist_c = i - CMP_STRIDE * cprime - (CMP_LEN - 1)
    bpc = np.where(dist_c >= 0, _bucket_np(dist_c), -1).astype(np.int32)
    j = np.arange(QT)[:, None]
    d0 = i - j
    d1 = QT + i - j
    bpt = np.stack([np.where(d0 >= 0, _bucket_np(d0), -1), _bucket_np(d1)]).astype(np.int32)
    return pl.pallas_call(
        _bias_kernel,
        grid=(NSA_HEADS,),
        in_specs=[pl.BlockSpec(memory_space=pltpu.SMEM),
                  pl.BlockSpec((2 * ncp, QT), lambda h: (0, 0)),
                  pl.BlockSpec((2, QT, QT), lambda h: (0, 0, 0))],
        out_specs=[pl.BlockSpec((1, 2 * ncp, QT), lambda h: (h, 0, 0)),
                   pl.BlockSpec((1, 2, QT, QT), lambda h: (h // NSA_GROUP, 0, 0, h % NSA_GROUP))],
        out_shape=[jax.ShapeDtypeStruct((NSA_HEADS, 2 * ncp, QT), F32),
                   jax.ShapeDtypeStruct((NSA_KV_HEADS, 2, QT, NSA_GROUP * QT), F32)],
        compiler_params=_params("parallel"),
        name="bias_tables",
    )(rel_bias.astype(F32), jnp.asarray(bpc), jnp.asarray(bpt))


def _inproj_kernel(x_ref, g_ref, w1_ref, w1t_ref, qn_ref, wqa_ref, wqb_ref, kvn_ref, wk_ref, wvt_ref,
                   ct_ref, st_ref, ctt_ref, stt_ref, ones_n_ref, ones_m_ref,
                   qm_ref, km_ref, vm_ref, zq_ref, nk_ref, nv_ref, kcmp_ref, vcmp_ref, gate_ref,
                   *, seq, tm):
    h = _rms(x_ref[...], g_ref[...]).astype(BF16)

    def proj(c0, width):
        return _dot(h, w1_ref[:, c0:c0 + width])

    cq = _rms(proj(C_CQ, MLA_Q_RANK), qn_ref[...]).astype(BF16)
    qa = _dot_nt(wqa_ref[...], cq)
    qb = _dot_nt(wqb_ref[...], cq)
    scale = (MLA_NOPE + MLA_ROPE) ** -0.5 * LOG2E
    cts = ctt_ref[...] * scale
    sts = stt_ref[...] * scale
    for hh in range(MLA_HEADS):
        sl = slice(hh * LANE, (hh + 1) * LANE)
        qm_ref[sl, :] = (qa[sl] * cts + qb[sl] * sts).astype(BF16)

    ckv = _rms(proj(C_CKV, MLA_KV_RANK), kvn_ref[...]).astype(BF16)
    kr = proj(C_KR, LANE) * ct_ref[...] + proj(C_KRROT, LANE) * st_ref[...]
    kn = _dot(ckv, wk_ref[...])
    for hh in range(MLA_HEADS):
        sl = slice(hh * LANE, (hh + 1) * LANE)
        km_ref[:, sl] = (kn[:, sl] + kr).astype(BF16)
    vm_ref[...] = (_dot_nt(wvt_ref[...], ckv) + ones_m_ref[...]).astype(BF16)

    zt = _dot_nt(w1t_ref[...], h)
    zq_ref[...] = (zt[R_QN:R_VSLC] * LOG2E).astype(BF16)
    nv_ref[...] = (zt[R_VSLC:R_GATE] + ones_n_ref[...]).astype(BF16)
    gate_ref[...] = _sigmoid(zt[R_GATE:R_TOTAL])

    lane = lax.broadcasted_iota(jnp.int32, (tm, LANE), 1)
    s0 = (pl.program_id(0) * tm) % seq
    row = lax.broadcasted_iota(jnp.int32, (tm, LANE), 0)
    onehot = (lane - NSA_DIM == (s0 + row) // SLC_LEN).astype(F32)
    ksl = proj(C_KSLC, 2 * LANE)
    for hk in range(NSA_KV_HEADS):
        sl = slice(hk * LANE, (hk + 1) * LANE)
        nk_ref[:, sl] = (ksl[:, sl] + onehot).astype(BF16)
    nk_ref[:, 2 * LANE:] = proj(C_KWIN, 2 * LANE).astype(BF16)

    kcmp = proj(C_KCMP, LANE)
    vcmp = proj(C_VCMP, LANE)
    for hk in range(NSA_KV_HEADS):
        kcmp_ref[0, hk] = kcmp[:, hk * NSA_DIM:(hk + 1) * NSA_DIM]
        vcmp_ref[0, hk] = vcmp[:, hk * NSA_DIM:(hk + 1) * NSA_DIM]


def _ones_rows(n_slots):
    r = np.arange(n_slots * VROWS) % VROWS
    return jnp.asarray((r >= NSA_DIM).astype(np.float32)[:, None])


def _inproj(x2, g_pre, w1, w1t, qn, wqa, wqb, kvn, wk, wvt, ct, st, ctt, stt, *, batch, seq):
    n = x2.shape[0]
    tm = 512
    tiles_per_seq = seq // tm
    row = lambda r: (r, 0)
    col = lambda r: (0, r)
    hm = lambda r: (r // tiles_per_seq, 0, r % tiles_per_seq, 0)
    nv_rows = 2 * NSA_KV_HEADS * VROWS
    vm_rows = MLA_HEADS * VROWS
    return pl.pallas_call(
        functools.partial(_inproj_kernel, seq=seq, tm=tm),
        grid=(n // tm,),
        in_specs=[pl.BlockSpec((tm, D_MODEL), row),
                  _resident((1, D_MODEL)),
                  _resident((D_MODEL, C_TOTAL)),
                  _resident((R_TOTAL, D_MODEL)),
                  _resident((1, MLA_Q_RANK)),
                  _resident((MLA_HEADS * LANE, MLA_Q_RANK)),
                  _resident((MLA_HEADS * LANE, MLA_Q_RANK)),
                  _resident((1, MLA_KV_RANK)),
                  _resident((MLA_KV_RANK, MLA_HEADS * LANE)),
                  _resident((vm_rows, MLA_KV_RANK)),
                  pl.BlockSpec((tm, LANE), row),
                  pl.BlockSpec((tm, LANE), row),
                  pl.BlockSpec((LANE, tm), col),
                  pl.BlockSpec((LANE, tm), col),
                  _resident((nv_rows, 1)),
                  _resident((vm_rows, 1))],
        out_specs=[pl.BlockSpec((MLA_HEADS * LANE, tm), col),
                   pl.BlockSpec((tm, MLA_HEADS * LANE), row),
                   pl.BlockSpec((vm_rows, tm), col),
                   pl.BlockSpec((NSA_HEADS * NSA_DIM, tm), col),
                   pl.BlockSpec((tm, 4 * LANE), row),
                   pl.BlockSpec((nv_rows, tm), col),
                   pl.BlockSpec((1, NSA_KV_HEADS, tm, NSA_DIM), hm),
                   pl.BlockSpec((1, NSA_KV_HEADS, tm, NSA_DIM), hm),
                   pl.BlockSpec((NSA_KV_HEADS * GATE_ROWS, tm), col)],
        out_shape=[jax.ShapeDtypeStruct((MLA_HEADS * LANE, n), BF16),
                   jax.ShapeDtypeStruct((n, MLA_HEADS * LANE), BF16),
                   jax.ShapeDtypeStruct((vm_rows, n), BF16),
                   jax.ShapeDtypeStruct((NSA_HEADS * NSA_DIM, n), BF16),
                   jax.ShapeDtypeStruct((n, 4 * LANE), BF16),
                   jax.ShapeDtypeStruct((nv_rows, n), BF16),
                   jax.ShapeDtypeStruct((batch, NSA_KV_HEADS, seq, NSA_DIM), F32),
                   jax.ShapeDtypeStruct((batch, NSA_KV_HEADS, seq, NSA_DIM), F32),
                   jax.ShapeDtypeStruct((NSA_KV_HEADS * GATE_ROWS, n), F32)],
        compiler_params=_params("parallel"),
        name="in_proj",
    )(x2, g_pre, w1, w1t, qn, wqa, wqb, kvn, wk, wvt, ct, st, ctt, stt,
      _ones_rows(2 * NSA_KV_HEADS), _ones_rows(MLA_HEADS))


def _compress_kernel(k16_ref, v16_ref, pos_ref, w1_ref, w2k_ref, w2vt_ref, kc_ref, vc_ref):
    half = CMP_STRIDE * NSA_DIM

    def hidden(x16, j):
        pos = pos_ref[j]
        top = (x16 + pos[:, :half]).astype(BF16)
        bot = (x16 + pos[:, half:]).astype(BF16)
        u = _dot(top, w1_ref[j, :half, :])
        low = _dot(bot, w1_ref[j, half:, :])
        nxt = jnp.concatenate([low[1:], jnp.zeros((1, CMP_HIDDEN), F32)], axis=0)
        return _gelu_tanh(u + nxt).astype(BF16)

    kc_ref[0, 0] = _dot(hidden(k16_ref[0, 0], 0), w2k_ref[...]).astype(BF16)
    vc_ref[0, 0] = _dot_nt(w2vt_ref[...], hidden(v16_ref[0, 0], 1)).astype(BF16)


def _compress(kcmp, vcmp, pos, w1, w2k, w2vt, *, batch, seq):
    ncp = seq // CMP_STRIDE
    half = CMP_STRIDE * NSA_DIM
    k16 = kcmp.reshape(batch, NSA_KV_HEADS, ncp, half)
    v16 = vcmp.reshape(batch, NSA_KV_HEADS, ncp, half)
    blk = lambda b, hk: (b, hk, 0, 0)
    return pl.pallas_call(
        _compress_kernel,
        grid=(batch, NSA_KV_HEADS),
        in_specs=[pl.BlockSpec((1, 1, ncp, half), blk),
                  pl.BlockSpec((1, 1, ncp, half), blk),
                  pl.BlockSpec((2, 1, 2 * half), lambda b, hk: (0, 0, 0)),
                  pl.BlockSpec((2, 2 * half, CMP_HIDDEN), lambda b, hk: (0, 0, 0)),
                  pl.BlockSpec((CMP_HIDDEN, NSA_DIM), lambda b, hk: (0, 0)),
                  pl.BlockSpec((NSA_DIM, CMP_HIDDEN), lambda b, hk: (0, 0))],
        out_specs=[pl.BlockSpec((1, 1, ncp, NSA_DIM), blk),
                   pl.BlockSpec((1, 1, NSA_DIM, ncp), blk)],
        out_shape=[jax.ShapeDtypeStruct((batch, NSA_KV_HEADS, ncp, NSA_DIM), BF16),
                   jax.ShapeDtypeStruct((batch, NSA_KV_HEADS, NSA_DIM, ncp), BF16)],
        compiler_params=_params("parallel", "parallel"),
        name="nsa_compress",
    )(k16, v16, pos, w1, w2k, w2vt)


def _nsa_kernel(zq_ref, kc_ref, vc_ref, ksl_ref, vsl_ref, kw_ref, vw_ref, gate_ref,
                pat_ref, tsel_ref, ovt_ref, o_ref, imp_ref, acc_ref, m_ref):
    qi = pl.program_id(2)
    G = NSA_GROUP
    M = G * QT
    q4 = zq_ref[...]
    qs = jnp.concatenate([q4[g * NSA_DIM:(g + 1) * NSA_DIM, :] for g in range(G)], axis=1)
    kc = kc_ref[0, 0]
    vc_t = vc_ref[0, 0]
    ovt = ovt_ref[...]
    ncp = kc.shape[0]
    qcol = qi * QT + (lax.broadcasted_iota(jnp.int32, (1, M), 1) & (QT - 1))

    off_b = pl.multiple_of(ncp - (QT // CMP_STRIDE) * qi, QT // CMP_STRIDE)
    bias = jnp.concatenate([pat_ref[g, pl.ds(off_b, ncp), :] for g in range(G)], axis=1)
    s = _dot(kc, qs) + bias
    e = jnp.exp2(s - jnp.max(s, axis=0, keepdims=True))
    scale = jnp.where(qcol >= CMP_LEN - 1, 1.0 / jnp.sum(e, axis=0, keepdims=True), 0.0)
    p_cmp = (e * scale).astype(BF16)
    o_cmp = _dot(vc_t, p_cmp)
    imp_t = jnp.zeros((SLC_PAD, QT), F32)
    for g in range(G):
        imp_t = imp_t + _dot(ovt, p_cmp[:, g * QT:(g + 1) * QT])

    n_id = lax.broadcasted_iota(jnp.int32, (SLC_PAD, QT), 0)
    q_blk = (qi * QT + lax.broadcasted_iota(jnp.int32, (SLC_PAD, QT), 1)) // SLC_LEN
    forced = (n_id == 0) | (n_id == q_blk) | (n_id == q_blk - 1)
    imp = jnp.where(forced, POS_BIG, jnp.where(n_id > q_blk, NEG, imp_t))
    imp_ref[...] = imp
    SUB = 8
    slabs = [imp[v * SUB:(v + 1) * SUB] for v in range(SLC_PAD // SUB)]
    ranks = [jnp.zeros((SUB, QT), jnp.int32) for _ in slabs]
    sub_id = lax.broadcasted_iota(jnp.int32, (SUB, QT), 0)
    for m in range(SLC_PAD):
        other = imp_ref[m:m + 1, :]
        for v, slab in enumerate(slabs):
            if v > m // SUB:
                beats = (other >= slab).astype(jnp.int32)
            elif v < m // SUB:
                beats = (other > slab).astype(jnp.int32)
            else:
                beats = jnp.where(sub_id > m % SUB, (other >= slab).astype(jnp.int32),
                                  (other > slab).astype(jnp.int32))
            ranks[v] = ranks[v] + beats
    rank = jnp.concatenate(ranks, axis=0)
    selb = jnp.where(rank < SLC_TOPK, 0.0, NEG).astype(BF16)
    qaug = jnp.concatenate([qs, jnp.concatenate([selb] * G, axis=1)], axis=0)

    def reset():
        m_ref[...] = jnp.full(m_ref.shape, NEG, F32)
        acc_ref[...] = jnp.zeros(acc_ref.shape, F32)

    def update(s, v_t):
        m_old = m_ref[...]
        m_new = jnp.maximum(m_old, jnp.max(s, axis=0, keepdims=True))
        p = jnp.exp2(s - m_new).astype(BF16)
        acc_ref[...] = acc_ref[...] * jnp.exp2(m_old - m_new) + _dot(v_t, p)
        m_ref[...] = m_new

    def finish():
        a = acc_ref[...]
        return a[:NSA_DIM] / a[NSA_DIM:NSA_DIM + 1]

    reset()
    n_far = jnp.maximum(qi - 1, 0)
    FAR = 4

    def far_big(t, carry):
        off = pl.multiple_of(t * (FAR * QT), FAR * QT)
        update(_dot(ksl_ref[pl.ds(off, FAR * QT), :], qaug), vsl_ref[:, pl.ds(off, FAR * QT)])
        return carry

    def far_small(t, carry):
        off = pl.multiple_of(t * QT, QT)
        update(_dot(ksl_ref[pl.ds(off, QT), :], qaug), vsl_ref[:, pl.ds(off, QT)])
        return carry

    n_big = n_far // FAR
    lax.fori_loop(0, n_big, far_big, 0)
    lax.fori_loop(n_big * FAR, n_far, far_small, 0)

    off_p = pl.multiple_of(n_far * QT, QT)
    off_d = pl.multiple_of(qi * QT, QT)
    no_prev = jnp.where(qi == 0, NEG, 0.0)
    k2 = jnp.concatenate([ksl_ref[pl.ds(off_p, QT), :], ksl_ref[pl.ds(off_d, QT), :]], axis=0)
    v2 = jnp.concatenate([vsl_ref[:, pl.ds(off_p, QT)], vsl_ref[:, pl.ds(off_d, QT)]], axis=1)
    b2 = jnp.concatenate([tsel_ref[0, 1] + no_prev, tsel_ref[0, 0]], axis=0)
    update(_dot(k2, qaug) + b2, v2)
    o_sel = finish()

    reset()
    n_tiles = WINDOW // QT + 1
    pad_v = jnp.where(lax.broadcasted_iota(jnp.int32, (VROWS, QT), 0) >= NSA_DIM, 1.0, 0.0).astype(BF16)
    key_j = lax.broadcasted_iota(jnp.int32, (QT, M), 0)
    qry_i = lax.broadcasted_iota(jnp.int32, (QT, M), 1) & (QT - 1)
    k_rows, v_cols = [], []
    for w in range(n_tiles):
        tile = qi - (n_tiles - 1) + w
        off = pl.multiple_of(jnp.maximum(tile, 0) * QT, QT)
        kt = kw_ref[pl.ds(off, QT), :]
        vt = vw_ref[:, pl.ds(off, QT)]
        if w < n_tiles - 1:
            kt = jnp.where(tile >= 0, kt, jnp.zeros_like(kt))
            vt = jnp.where(tile >= 0, vt, pad_v)
        k_rows.append(kt)
        v_cols.append(vt)
    s_w = _dot(jnp.concatenate(k_rows, axis=0), qaug)
    pieces = [s_w[w * QT:(w + 1) * QT] for w in range(n_tiles)]
    pieces[0] = pieces[0] + jnp.where(key_j > qry_i, 0.0, NEG)
    pieces[n_tiles - 2] = pieces[n_tiles - 2] + tsel_ref[0, 1]
    pieces[n_tiles - 1] = pieces[n_tiles - 1] + tsel_ref[0, 0]
    update(jnp.concatenate(pieces, axis=0), jnp.concatenate(v_cols, axis=1))
    o_win = finish()

    gate = gate_ref[...]
    outs = []
    for g in range(G):
        cols = slice(g * QT, (g + 1) * QT)
        o_t = (gate[3 * g:3 * g + 1, :] * o_cmp[:, cols] + gate[3 * g + 1:3 * g + 2, :] * o_sel[:, cols]
               + gate[3 * g + 2:3 * g + 3, :] * o_win[:, cols])
        outs.append(o_t.T)
    o_ref[...] = jnp.concatenate(outs, axis=1).astype(BF16)


def _overlap_t(seq):
    ncp = seq // CMP_STRIDE
    n_cmp = (seq - CMP_LEN) // CMP_STRIDE + 1
    n_slc = seq // SLC_LEN
    cs = np.arange(ncp)[None, :] * CMP_STRIDE
    ss = np.arange(SLC_PAD)[:, None] * SLC_LEN
    ov = np.maximum(np.minimum(cs + CMP_LEN, ss + SLC_LEN) - np.maximum(cs, ss), 0).astype(np.float32) / CMP_LEN
    ov = ov * (np.arange(ncp)[None, :] < n_cmp) * (np.arange(SLC_PAD)[:, None] < n_slc)
    return jnp.asarray(ov, BF16)


def _nsa(zq_t, kc, vc_t, nk, nv_t, gates_t, pat, tsel, *, batch, seq):
    n = nk.shape[0]
    nq = seq // QT
    ncp = seq // CMP_STRIDE
    G = NSA_GROUP
    assert seq // SLC_LEN <= SLC_PAD and seq // SLC_LEN >= SLC_TOPK and seq >= WINDOW
    qcol = lambda b, hk, qi: (hk, b * nq + qi)
    return pl.pallas_call(
        _nsa_kernel,
        grid=(batch, NSA_KV_HEADS, nq),
        in_specs=[pl.BlockSpec((G * NSA_DIM, QT), qcol),
                  pl.BlockSpec((1, 1, ncp, NSA_DIM), lambda b, hk, qi: (b, hk, 0, 0)),
                  pl.BlockSpec((1, 1, NSA_DIM, ncp), lambda b, hk, qi: (b, hk, 0, 0)),
                  pl.BlockSpec((seq, LANE), lambda b, hk, qi: (b, hk)),
                  pl.BlockSpec((VROWS, seq), lambda b, hk, qi: (hk, b)),
                  pl.BlockSpec((seq, LANE), lambda b, hk, qi: (b, NSA_KV_HEADS + hk)),
                  pl.BlockSpec((VROWS, seq), lambda b, hk, qi: (NSA_KV_HEADS + hk, b)),
                  pl.BlockSpec((GATE_ROWS, QT), qcol),
                  pl.BlockSpec((G, 2 * ncp, QT), lambda b, hk, qi: (hk, 0, 0)),
                  pl.BlockSpec((1, 2, QT, G * QT), lambda b, hk, qi: (hk, 0, 0, 0)),
                  pl.BlockSpec((SLC_PAD, ncp), lambda b, hk, qi: (0, 0))],
        out_specs=pl.BlockSpec((QT, G * NSA_DIM), lambda b, hk, qi: (b * nq + qi, hk)),
        out_shape=jax.ShapeDtypeStruct((n, NSA_HEADS * NSA_DIM), BF16),
        scratch_shapes=[pltpu.VMEM((SLC_PAD, QT), F32),
                        pltpu.VMEM((VROWS, G * QT), F32),
                        pltpu.VMEM((1, G * QT), F32)],
        compiler_params=_params("parallel", "parallel", "arbitrary"),
        name="nsa_attention",
    )(zq_t, kc, vc_t, nk, nv_t, nk, nv_t, gates_t, pat, tsel, _overlap_t(seq))


MLA_TQ = 512


def _mla_kernel(q_ref, k_ref, v_ref, o_ref, acc_ref, m_ref, sa_ref, sb_ref):
    qi = pl.program_id(2)
    tq = MLA_TQ
    HP = 2
    qs = [q_ref[hh * LANE:(hh + 1) * LANE, :] for hh in range(HP)]
    m_ref[...] = jnp.full(m_ref.shape, NEG, F32)
    acc_ref[...] = jnp.zeros(acc_ref.shape, F32)

    def qk(dst_ref, off):
        for hh in range(HP):
            dst_ref[hh] = _dot(k_ref[pl.ds(off, tq), hh * LANE:(hh + 1) * LANE], qs[hh])

    def consume(src_ref, off, mask):
        for hh in range(HP):
            s = src_ref[hh]
            if mask is not None:
                s = s + mask
            vt = v_ref[hh * VROWS:(hh + 1) * VROWS, pl.ds(off, tq)]
            m_old = m_ref[hh]
            m_new = jnp.maximum(m_old, jnp.max(s, axis=0, keepdims=True))
            p = jnp.exp2(s - m_new).astype(BF16)
            acc_ref[hh] = acc_ref[hh] * jnp.exp2(m_old - m_new) + _dot(vt, p)
            m_ref[hh] = m_new

    key_j = lax.broadcasted_iota(jnp.int32, (tq, tq), 0)
    qry_i = lax.broadcasted_iota(jnp.int32, (tq, tq), 1)
    causal = jnp.where(key_j <= qry_i, 0.0, NEG)
    qk(sa_ref, 0)

    def pair(u, carry):
        off = pl.multiple_of(u * (2 * tq), 2 * tq)
        qk(sb_ref, off + tq)
        consume(sa_ref, off, None)
        qk(sa_ref, off + 2 * tq)
        consume(sb_ref, off + tq, None)
        return carry

    lax.fori_loop(0, qi // 2, pair, 0)
    base = pl.multiple_of((qi // 2) * (2 * tq), 2 * tq)

    @pl.when(qi % 2 == 0)
    def _():
        consume(sa_ref, base, causal)

    @pl.when(qi % 2 == 1)
    def _():
        qk(sb_ref, base + tq)
        consume(sa_ref, base, None)
        consume(sb_ref, base + tq, causal)

    outs = []
    for hh in range(HP):
        a = acc_ref[hh]
        outs.append((a[:MLA_V] / a[MLA_V:MLA_V + 1]).T)
    o_ref[...] = jnp.concatenate(outs, axis=1).astype(BF16)


def _mla(qm_t, km, vm_t, *, batch, seq):
    n = km.shape[0]
    tq = MLA_TQ
    nq = seq // tq
    HP = 2
    return pl.pallas_call(
        _mla_kernel,
        grid=(batch, MLA_HEADS // HP, nq),
        in_specs=[pl.BlockSpec((HP * LANE, tq), lambda b, hp, qi: (hp, b * nq + qi)),
                  pl.BlockSpec((seq, HP * LANE), lambda b, hp, qi: (b, hp)),
                  pl.BlockSpec((HP * VROWS, seq), lambda b, hp, qi: (hp, b))],
        out_specs=pl.BlockSpec((tq, HP * MLA_V), lambda b, hp, qi: (b * nq + qi, hp)),
        out_shape=jax.ShapeDtypeStruct((n, MLA_HEADS * MLA_V), BF16),
        scratch_shapes=[pltpu.VMEM((HP, VROWS, tq), F32),
                        pltpu.VMEM((HP, 1, tq), F32),
                        pltpu.VMEM((HP, tq, tq), F32),
                        pltpu.VMEM((HP, tq, tq), F32)],
        compiler_params=_params("parallel", "parallel", "arbitrary"),
        name="mla_attention",
    )(qm_t, km, vm_t)


def _tail_kernel(x_ref, om_ref, on_ref, p_ref, wo_ref, g1_ref, g2_ref, g3_ref,
                 wg_ref, wu_ref, cw_ref, cb_ref, wd_ref, pg_ref, pp_ref,
                 o_ref, carry_ref, facc_ref, *, seq, tm):
    n_chunks = D_FF // FF_CHUNK
    half = om_ref.shape[1]
    y = _dot(om_ref[...], wo_ref[:half, :]) + _dot(on_ref[...], wo_ref[half:, :])
    x = x_ref[...] + _rms(y, g1_ref[...])

    h = _rms(x, g2_ref[...]).astype(BF16)

    @pl.when((pl.program_id(0) * tm) % seq == 0)
    def _():
        carry_ref[...] = jnp.zeros(carry_ref.shape, F32)

    row = lax.broadcasted_iota(jnp.int32, (tm, 1), 0)
    facc_ref[...] = jnp.zeros(facc_ref.shape, F32)

    def chunk(c, carry):
        g = _dot(h, wg_ref[c])
        prev = carry_ref[c]
        carry_ref[c] = g[tm - 8:, :]
        g1 = jnp.where(row == 0, prev[7:8, :], pltpu.roll(g, 1, 0))
        g2 = jnp.where(row == 0, prev[6:7, :], jnp.where(row == 1, prev[7:8, :], pltpu.roll(g, 2, 0)))
        cw = cw_ref[c]
        conv = cw[0:1, :] * g2 + cw[1:2, :] * g1 + cw[2:3, :] * g + cb_ref[c]
        act = (_gelu_tanh(conv) * _dot(h, wu_ref[c])).astype(BF16)
        facc_ref[...] += _dot(act, wd_ref[c])
        return carry

    lax.fori_loop(0, n_chunks, chunk, 0, unroll=True)
    x = x + _rms(facc_ref[...], g3_ref[...])

    gate = _sigmoid(_dot(x.astype(BF16), pg_ref[...]))
    o_ref[...] = x + gate * _dot(p_ref[...].astype(BF16), pp_ref[...])


def _tail(x2, om, on, p2, wo, g1, g2, g3, wg, wu, cw, cb, wd, pg, pp, *, seq):
    n = x2.shape[0]
    tm = 512
    n_chunks = D_FF // FF_CHUNK
    row = lambda r: (r, 0)
    half = om.shape[1]
    return pl.pallas_call(
        functools.partial(_tail_kernel, seq=seq, tm=tm),
        grid=(n // tm,),
        in_specs=[pl.BlockSpec((tm, D_MODEL), row),
                  pl.BlockSpec((tm, half), row),
                  pl.BlockSpec((tm, half), row),
                  pl.BlockSpec((tm, PLE_DIM), row),
                  _resident((2 * half, D_MODEL)),
                  _resident((1, D_MODEL)),
                  _resident((1, D_MODEL)),
                  _resident((1, D_MODEL)),
                  _resident((n_chunks, D_MODEL, FF_CHUNK)),
                  _resident((n_chunks, D_MODEL, FF_CHUNK)),
                  _resident((n_chunks, CONV_WIDTH, FF_CHUNK)),
                  _resident((n_chunks, 1, FF_CHUNK)),
                  _resident((n_chunks, FF_CHUNK, D_MODEL)),
                  _resident((D_MODEL, D_MODEL)),
                  _resident((PLE_DIM, D_MODEL))],
        out_specs=pl.BlockSpec((tm, D_MODEL), row),
        out_shape=jax.ShapeDtypeStruct((n, D_MODEL), F32),
        scratch_shapes=[pltpu.VMEM((n_chunks, 8, FF_CHUNK), F32),
                        pltpu.VMEM((tm, D_MODEL), F32)],
        compiler_params=_params("arbitrary"),
        name="layer_tail",
    )(x2, om, on, p2, wo, g1, g2, g3, wg, wu, cw, cb, wd, pg, pp)


def _rot_cols(w):
    half = w.shape[1] // 2
    return jnp.concatenate([-w[:, half:], w[:, :half]], axis=1)


def _prep_inproj(w):
    o = np.cumsum((0,) + IN_SPLITS)
    parts = [w[:, o[j]:o[j + 1]] for j in range(len(IN_SPLITS))]
    cq, ckv, kr, qn, kcmp, vcmp, kslc, vslc, kwin, vwin, gn = parts
    d = w.shape[0]
    z64 = jnp.zeros((d, NSA_DIM), F32)
    z32 = jnp.zeros((d, LANE - MLA_NOPE - MLA_ROPE), F32)

    def slots(m):
        return jnp.concatenate([m[:, :NSA_DIM], z64, m[:, NSA_DIM:], z64], axis=1)

    w1 = jnp.concatenate([cq, ckv,
                          z64, kr, z32,
                          z64, _rot_cols(kr), z32,
                          slots(kslc), slots(kwin), kcmp, vcmp], axis=1)
    assert w1.shape[1] == C_TOTAL

    def vslots(m):
        zp = jnp.zeros((VROWS - NSA_DIM, d), F32)
        return jnp.concatenate([m[:, :NSA_DIM].T, zp, m[:, NSA_DIM:].T, zp], axis=0)

    per = 3 * NSA_GROUP
    gp = jnp.zeros((GATE_ROWS - per, d), F32)
    w1t = jnp.concatenate([qn.T * NSA_DIM ** -0.5, vslots(vslc), vslots(vwin),
                           gn[:, :per].T, gp, gn[:, per:].T, gp], axis=0)
    assert w1t.shape[0] == R_TOTAL
    return w1.astype(BF16), w1t.astype(BF16)


def _prep_mla(w_uq, w_ukv):
    r = w_uq.shape[0]
    dq = MLA_NOPE + MLA_ROPE
    z32 = jnp.zeros((r, LANE - dq), F32)
    z64q = jnp.zeros((r, MLA_NOPE), F32)
    qa, qb = [], []
    for h in range(MLA_HEADS):
        blk = w_uq[:, h * dq:(h + 1) * dq]
        qa += [blk, z32]
        qb += [z64q, _rot_cols(blk[:, MLA_NOPE:]), z32]
    rk = w_ukv.shape[0]
    z64 = jnp.zeros((rk, LANE - MLA_NOPE), F32)
    zv = jnp.zeros((rk, VROWS - MLA_V), F32)
    wk, wv = [], []
    dkv = MLA_NOPE + MLA_V
    for h in range(MLA_HEADS):
        blk = w_ukv[:, h * dkv:(h + 1) * dkv]
        wk += [blk[:, :MLA_NOPE], z64]
        wv += [blk[:, MLA_NOPE:], zv]
    cat = lambda xs: jnp.concatenate(xs, axis=1).astype(BF16)
    return cat(qa).T, cat(qb).T, cat(wk), cat(wv).T


def kernel(x, p, positions, rel_bias, attn_pre_norm, attn_post_norm, ffn_pre_norm, ffn_post_norm,
           w_in, mla_q_norm, mla_w_uq, mla_kv_norm, mla_w_ukv, nsa_cmp_pos, nsa_cmp_w1, nsa_cmp_w2,
           w_o, ffn_w_gate, ffn_w_up, ffn_conv_w, ffn_conv_b, ffn_w_down, ple_proj, ple_gate):
    batch, seq, d = x.shape
    depth = w_in.shape[0]
    n = batch * seq
    n_chunks = D_FF // FF_CHUNK
    x2 = x.reshape(n, d)
    ct, st = _rope_tables(positions)
    ctt, stt = _rope_tables_t(positions)
    pat, tsel = _bias_tables(rel_bias, seq)
    for i in range(depth):
        w1, w1t = _prep_inproj(w_in[i])
        wqa, wqb, wk, wvt = _prep_mla(mla_w_uq[i], mla_w_ukv[i])
        qm_t, km, vm_t, zq_t, nk, nv_t, kcmp, vcmp, gates_t = _inproj(
            x2, attn_pre_norm[i][None, :], w1, w1t, mla_q_norm[i][None, :], wqa, wqb,
            mla_kv_norm[i][None, :], wk, wvt, ct, st, ctt, stt, batch=batch, seq=seq)
        kc, vc_t = _compress(kcmp, vcmp,
                             nsa_cmp_pos[i].reshape(2, 1, CMP_LEN * NSA_DIM),
                             nsa_cmp_w1[i].astype(BF16), nsa_cmp_w2[i, 0].astype(BF16),
                             nsa_cmp_w2[i, 1].T.astype(BF16), batch=batch, seq=seq)
        o_nsa = _nsa(zq_t, kc, vc_t, nk, nv_t, gates_t, pat, tsel, batch=batch, seq=seq)
        o_mla = _mla(qm_t, km, vm_t, batch=batch, seq=seq)
        wg = ffn_w_gate[i].astype(BF16).reshape(d, n_chunks, FF_CHUNK).transpose(1, 0, 2)
        wu = ffn_w_up[i].astype(BF16).reshape(d, n_chunks, FF_CHUNK).transpose(1, 0, 2)
        cw = ffn_conv_w[i].reshape(CONV_WIDTH, n_chunks, FF_CHUNK).transpose(1, 0, 2)
        cb = ffn_conv_b[i].reshape(n_chunks, 1, FF_CHUNK)
        wd = ffn_w_down[i].astype(BF16).reshape(n_chunks, FF_CHUNK, d)
        x2 = _tail(x2, o_mla, o_nsa, p[i].reshape(n, PLE_DIM), w_o[i].astype(BF16),
                   attn_post_norm[i][None, :], ffn_pre_norm[i][None, :], ffn_post_norm[i][None, :],
                   wg, wu, cw, cb, wd, ple_gate[i].astype(BF16), ple_proj[i].astype(BF16), seq=seq)
    return x2.reshape(batch, seq, d)
```

```python
import functools
import math

import numpy as np
import jax
import jax.numpy as jnp
from jax import lax
from jax.experimental import pallas as pl
from jax.experimental.pallas import tpu as pltpu

F32 = jnp.float32
BF16 = jnp.bfloat16

D_MODEL = 1024
DEPTH = 2
MLA_HEADS = 8
MLA_NOPE = 64
MLA_ROPE = 32
MLA_V = 64
MLA_Q_RANK = 256
MLA_KV_RANK = 128
ROPE_BASE = 10000.0
NSA_HEADS = 8
NSA_KV_HEADS = 2
NSA_GROUP = NSA_HEADS // NSA_KV_HEADS
NSA_DIM = 64
CMP_LEN = 32
CMP_STRIDE = 16
CMP_HIDDEN = 128
SLC_LEN = 64
SLC_TOPK = 16
WINDOW = 512
REL_BUCKETS = 32
REL_MAX_DIST = 128
D_FF = 2816
CONV_WIDTH = 3
PLE_DIM = 256
EPS = 1e-6
NEG = -1e30
POS_BIG = 1e30
LOG2E = math.log2(math.e)

IN_SPLITS = (MLA_Q_RANK, MLA_KV_RANK, MLA_ROPE, NSA_HEADS * NSA_DIM,
             NSA_KV_HEADS * NSA_DIM, NSA_KV_HEADS * NSA_DIM,
             NSA_KV_HEADS * NSA_DIM, NSA_KV_HEADS * NSA_DIM,
             NSA_KV_HEADS * NSA_DIM, NSA_KV_HEADS * NSA_DIM,
             3 * NSA_HEADS)

LANE = 128
QT = 128
SLC_PAD = 64
FF_CHUNK = 256
VMEM_LIMIT = 56 * 1024 * 1024

VROWS = 80

C_CQ = 0
C_CKV = 256
C_KR = 384
C_KRROT = 512
C_KSLC = 640
C_KWIN = 896
C_KCMP = 1152
C_VCMP = 1280
C_TOTAL = 1408
R_QN = 0
R_VSLC = 512
R_VWIN = R_VSLC + NSA_KV_HEADS * VROWS
R_GATE = R_VWIN + NSA_KV_HEADS * VROWS
GATE_ROWS = 16
R_TOTAL = R_GATE + NSA_KV_HEADS * GATE_ROWS


def _dot(a, b):
    return jnp.dot(a, b, preferred_element_type=F32)


def _dot_nt(a, b):
    return lax.dot_general(a, b, (((1,), (1,)), ((), ())), preferred_element_type=F32)


def _rms(x, g):
    return x * lax.rsqrt(jnp.mean(x * x, axis=-1, keepdims=True) + EPS) * g


def _gelu_tanh(x):
    return 0.5 * x * (1.0 + jnp.tanh(math.sqrt(2.0 / math.pi) * (x + 0.044715 * (x * x * x))))


def _sigmoid(x):
    return 1.0 / (1.0 + jnp.exp(-x))


def _params(*sem):
    return pltpu.CompilerParams(dimension_semantics=sem, vmem_limit_bytes=VMEM_LIMIT)


def _resident(shape):
    nd = len(shape)
    return pl.BlockSpec(shape, lambda *_: (0,) * nd, pipeline_mode=pl.Buffered(1))


def _rope_kernel(pos_ref, inv_ref, ctt_ref, stt_ref, ct_ref, st_ref):
    ang = inv_ref[...] * pos_ref[...].astype(F32)
    row = lax.broadcasted_iota(jnp.int32, ang.shape, 0)
    rope = (row >= MLA_NOPE) & (row < MLA_NOPE + MLA_ROPE)
    ct = jnp.where(rope, jnp.cos(ang), jnp.where(row < MLA_NOPE, 1.0, 0.0))
    st = jnp.where(rope, jnp.sin(ang), 0.0)
    ctt_ref[...] = ct
    stt_ref[...] = st
    ct_ref[...] = ct.T
    st_ref[...] = st.T


def _rope_tables(positions):
    n = positions.size
    tn = 2048
    half = MLA_ROPE // 2
    inv = ROPE_BASE ** (-jnp.arange(half, dtype=F32) / half)
    inv_slot = jnp.concatenate([jnp.zeros((MLA_NOPE,), F32), inv, inv,
                                jnp.zeros((LANE - MLA_NOPE - MLA_ROPE,), F32)])[:, None]
    return pl.pallas_call(
        _rope_kernel,
        grid=(n // tn,),
        in_specs=[pl.BlockSpec((1, tn), lambda r: (0, r)),
                  pl.BlockSpec((LANE, 1), lambda r: (0, 0))],
        out_specs=[pl.BlockSpec((LANE, tn), lambda r: (0, r))] * 2
                  + [pl.BlockSpec((tn, LANE), lambda r: (r, 0))] * 2,
        out_shape=[jax.ShapeDtypeStruct((LANE, n), F32)] * 2 + [jax.ShapeDtypeStruct((n, LANE), F32)] * 2,
        compiler_params=_params("parallel"),
        name="rope_tables",
    )(positions.reshape(1, n), inv_slot)


def _bucket_np(dist):
    n = np.maximum(dist, 0)
    max_exact = REL_BUCKETS // 2
    large = max_exact + (np.log(np.maximum(n, 1).astype(np.float32) / max_exact)
                         / math.log(REL_MAX_DIST / max_exact)
                         * (REL_BUCKETS - max_exact)).astype(np.int32)
    large = np.minimum(large, REL_BUCKETS - 1)
    return np.where(n < max_exact, n, large).astype(np.int32)


def _bias_kernel(table_ref, bpc_ref, bpt_ref, pat_ref, tsel_ref):
    h = pl.program_id(0)
    far = table_ref[REL_BUCKETS - 1, h]

    def lookup(bp, sub):
        acc = jnp.full(bp.shape, far - sub, F32)
        for b in range(REL_BUCKETS - 1):
            acc = jnp.where(bp == b, table_ref[b, h] - sub, acc)
        return jnp.where(bp < 0, NEG, acc * LOG2E)

    pat_ref[0] = lookup(bpc_ref[...], 0.0)
    tsel_ref[0, 0] = lookup(bpt_ref[0], far)
    tsel_ref[0, 1] = lookup(bpt_ref[1], far)


def _bias_tables(rel_bias, seq):
    ncp = seq // CMP_STRIDE
    i = np.arange(QT)[None, :]
    cprime = np.arange(2 * ncp)[:, None] - ncp
    dist_c = i - CMP_STRIDE * cprime - (CMP_LEN - 1)
    bpc = np.where(dist_c >= 0, _bucket_np(dist_c), -1).astype(np.int32)
    j = np.arange(QT)[:, None]
    d0 = i - j
    d1 = QT + i - j
    bpt = np.stack([np.where(d0 >= 0, _bucket_np(d0), -1), _bucket_np(d1)]).astype(np.int32)
    return pl.pallas_call(
        _bias_kernel,
        grid=(NSA_HEADS,),
        in_specs=[pl.BlockSpec(memory_space=pltpu.SMEM),
                  pl.BlockSpec((2 * ncp, QT), lambda h: (0, 0)),
                  pl.BlockSpec((2, QT, QT), lambda h: (0, 0, 0))],
        out_specs=[pl.BlockSpec((1, 2 * ncp, QT), lambda h: (h, 0, 0)),
                   pl.BlockSpec((1, 2, QT, QT), lambda h: (h // NSA_GROUP, 0, 0, h % NSA_GROUP))],
        out_shape=[jax.ShapeDtypeStruct((NSA_HEADS, 2 * ncp, QT), F32),
                   jax.ShapeDtypeStruct((NSA_KV_HEADS, 2, QT, NSA_GROUP * QT), F32)],
        compiler_params=_params("parallel"),
        name="bias_tables",
    )(rel_bias.astype(F32), jnp.asarray(bpc), jnp.asarray(bpt))


def _inproj_kernel(x_ref, g_ref, w1_ref, w1t_ref, qn_ref, wqa_ref, wqb_ref, kvn_ref, wk_ref, wvt_ref,
                   ct_ref, st_ref, ctt_ref, stt_ref, ones_n_ref, ones_m_ref,
                   qm_ref, km_ref, vm_ref, zq_ref, nk_ref, nv_ref, kcmp_ref, vcmp_ref, gate_ref,
                   *, seq, tm):
    h = _rms(x_ref[...], g_ref[...]).astype(BF16)

    def proj(c0, width):
        return _dot(h, w1_ref[:, c0:c0 + width])

    cq = _rms(proj(C_CQ, MLA_Q_RANK), qn_ref[...]).astype(BF16)
    qa = _dot_nt(wqa_ref[...], cq)
    qb = _dot_nt(wqb_ref[...], cq)
    scale = (MLA_NOPE + MLA_ROPE) ** -0.5 * LOG2E
    cts = ctt_ref[...] * scale
    sts = stt_ref[...] * scale
    for hh in range(MLA_HEADS):
        sl = slice(hh * LANE, (hh + 1) * LANE)
        qm_ref[sl, :] = (qa[sl] * cts + qb[sl] * sts).astype(BF16)

    ckv = _rms(proj(C_CKV, MLA_KV_RANK), kvn_ref[...]).astype(BF16)
    kr = proj(C_KR, LANE) * ct_ref[...] + proj(C_KRROT, LANE) * st_ref[...]
    kn = _dot(ckv, wk_ref[...])
    for hh in range(MLA_HEADS):
        sl = slice(hh * LANE, (hh + 1) * LANE)
        km_ref[:, sl] = (kn[:, sl] + kr).astype(BF16)
    vm_ref[...] = (_dot_nt(wvt_ref[...], ckv) + ones_m_ref[...]).astype(BF16)

    zt = _dot_nt(w1t_ref[...], h)
    zq_ref[...] = (zt[R_QN:R_VSLC] * LOG2E).astype(BF16)
    nv_ref[...] = (zt[R_VSLC:R_GATE] + ones_n_ref[...]).astype(BF16)
    gate_ref[...] = _sigmoid(zt[R_GATE:R_TOTAL])

    lane = lax.broadcasted_iota(jnp.int32, (tm, LANE), 1)
    s0 = (pl.program_id(0) * tm) % seq
    row = lax.broadcasted_iota(jnp.int32, (tm, LANE), 0)
    onehot = (lane - NSA_DIM == (s0 + row) // SLC_LEN).astype(F32)
    ksl = proj(C_KSLC, 2 * LANE)
    for hk in range(NSA_KV_HEADS):
        sl = slice(hk * LANE, (hk + 1) * LANE)
        nk_ref[:, sl] = (ksl[:, sl] + onehot).astype(BF16)
    nk_ref[:, 2 * LANE:] = proj(C_KWIN, 2 * LANE).astype(BF16)

    kcmp = proj(C_KCMP, LANE)
    vcmp = proj(C_VCMP, LANE)
    for hk in range(NSA_KV_HEADS):
        kcmp_ref[0, hk] = kcmp[:, hk * NSA_DIM:(hk + 1) * NSA_DIM]
        vcmp_ref[0, hk] = vcmp[:, hk * NSA_DIM:(hk + 1) * NSA_DIM]


def _ones_rows(n_slots):
    r = np.arange(n_slots * VROWS) % VROWS
    return jnp.asarray((r >= NSA_DIM).astype(np.float32)[:, None])


def _inproj(x2, g_pre, w1, w1t, qn, wqa, wqb, kvn, wk, wvt, ct, st, ctt, stt, *, batch, seq):
    n = x2.shape[0]
    tm = 512
    tiles_per_seq = seq // tm
    row = lambda r: (r, 0)
    col = lambda r: (0, r)
    hm = lambda r: (r // tiles_per_seq, 0, r % tiles_per_seq, 0)
    nv_rows = 2 * NSA_KV_HEADS * VROWS
    vm_rows = MLA_HEADS * VROWS
    return pl.pallas_call(
        functools.partial(_inproj_kernel, seq=seq, tm=tm),
        grid=(n // tm,),
        in_specs=[pl.BlockSpec((tm, D_MODEL), row),
                  _resident((1, D_MODEL)),
                  _resident((D_MODEL, C_TOTAL)),
                  _resident((R_TOTAL, D_MODEL)),
                  _resident((1, MLA_Q_RANK)),
                  _resident((MLA_HEADS * LANE, MLA_Q_RANK)),
                  _resident((MLA_HEADS * LANE, MLA_Q_RANK)),
                  _resident((1, MLA_KV_RANK)),
                  _resident((MLA_KV_RANK, MLA_HEADS * LANE)),
                  _resident((vm_rows, MLA_KV_RANK)),
                  pl.BlockSpec((tm, LANE), row),
                  pl.BlockSpec((tm, LANE), row),
                  pl.BlockSpec((LANE, tm), col),
                  pl.BlockSpec((LANE, tm), col),
                  _resident((nv_rows, 1)),
                  _resident((vm_rows, 1))],
        out_specs=[pl.BlockSpec((MLA_HEADS * LANE, tm), col),
                   pl.BlockSpec((tm, MLA_HEADS * LANE), row),
                   pl.BlockSpec((vm_rows, tm), col),
                   pl.BlockSpec((NSA_HEADS * NSA_DIM, tm), col),
                   pl.BlockSpec((tm, 4 * LANE), row),
                   pl.BlockSpec((nv_rows, tm), col),
                   pl.BlockSpec((1, NSA_KV_HEADS, tm, NSA_DIM), hm),
                   pl.BlockSpec((1, NSA_KV_HEADS, tm, NSA_DIM), hm),
                   pl.BlockSpec((NSA_KV_HEADS * GATE_ROWS, tm), col)],
        out_shape=[jax.ShapeDtypeStruct((MLA_HEADS * LANE, n), BF16),
                   jax.ShapeDtypeStruct((n, MLA_HEADS * LANE), BF16),
                   jax.ShapeDtypeStruct((vm_rows, n), BF16),
                   jax.ShapeDtypeStruct((NSA_HEADS * NSA_DIM, n), BF16),
                   jax.ShapeDtypeStruct((n, 4 * LANE), BF16),
                   jax.ShapeDtypeStruct((nv_rows, n), BF16),
                   jax.ShapeDtypeStruct((batch, NSA_KV_HEADS, seq, NSA_DIM), F32),
                   jax.ShapeDtypeStruct((batch, NSA_KV_HEADS, seq, NSA_DIM), F32),
                   jax.ShapeDtypeStruct((NSA_KV_HEADS * GATE_ROWS, n), F32)],
        compiler_params=_params("parallel"),
        name="in_proj",
    )(x2, g_pre, w1, w1t, qn, wqa, wqb, kvn, wk, wvt, ct, st, ctt, stt,
      _ones_rows(2 * NSA_KV_HEADS), _ones_rows(MLA_HEADS))


def _compress_kernel(k16_ref, v16_ref, pos_ref, w1_ref, w2k_ref, w2vt_ref, kc_ref, vc_ref):
    half = CMP_STRIDE * NSA_DIM

    def hidden(x16, j):
        pos = pos_ref[j]
        top = (x16 + pos[:, :half]).astype(BF16)
        bot = (x16 + pos[:, half:]).astype(BF16)
        u = _dot(top, w1_ref[j, :half, :])
        low = _dot(bot, w1_ref[j, half:, :])
        nxt = jnp.concatenate([low[1:], jnp.zeros((1, CMP_HIDDEN), F32)], axis=0)
        return _gelu_tanh(u + nxt).astype(BF16)

    kc_ref[0, 0] = _dot(hidden(k16_ref[0, 0], 0), w2k_ref[...]).astype(BF16)
    vc_ref[0, 0] = _dot_nt(w2vt_ref[...], hidden(v16_ref[0, 0], 1)).astype(BF16)


def _compress(kcmp, vcmp, pos, w1, w2k, w2vt, *, batch, seq):
    ncp = seq // CMP_STRIDE
    half = CMP_STRIDE * NSA_DIM
    k16 = kcmp.reshape(batch, NSA_KV_HEADS, ncp, half)
    v16 = vcmp.reshape(batch, NSA_KV_HEADS, ncp, half)
    blk = lambda b, hk: (b, hk, 0, 0)
    return pl.pallas_call(
        _compress_kernel,
        grid=(batch, NSA_KV_HEADS),
        in_specs=[pl.BlockSpec((1, 1, ncp, half), blk),
                  pl.BlockSpec((1, 1, ncp, half), blk),
                  pl.BlockSpec((2, 1, 2 * half), lambda b, hk: (0, 0, 0)),
                  pl.BlockSpec((2, 2 * half, CMP_HIDDEN), lambda b, hk: (0, 0, 0)),
                  pl.BlockSpec((CMP_HIDDEN, NSA_DIM), lambda b, hk: (0, 0)),
                  pl.BlockSpec((NSA_DIM, CMP_HIDDEN), lambda b, hk: (0, 0))],
        out_specs=[pl.BlockSpec((1, 1, ncp, NSA_DIM), blk),
                   pl.BlockSpec((1, 1, NSA_DIM, ncp), blk)],
        out_shape=[jax.ShapeDtypeStruct((batch, NSA_KV_HEADS, ncp, NSA_DIM), BF16),
                   jax.ShapeDtypeStruct((batch, NSA_KV_HEADS, NSA_DIM, ncp), BF16)],
        compiler_params=_params("parallel", "parallel"),
        name="nsa_compress",
    )(k16, v16, pos, w1, w2k, w2vt)


def _nsa_kernel(zq_ref, kc_ref, vc_ref, ksl_ref, vsl_ref, kw_ref, vw_ref, gate_ref,
                pat_ref, tsel_ref, ovt_ref, o_ref, imp_ref, acc_ref, m_ref, sa_ref, sb_ref):
    qi = pl.program_id(2)
    G = NSA_GROUP
    M = G * QT
    q4 = zq_ref[...]
    qs = jnp.concatenate([q4[g * NSA_DIM:(g + 1) * NSA_DIM, :] for g in range(G)], axis=1)
    kc = kc_ref[0, 0]
    vc_t = vc_ref[0, 0]
    ovt = ovt_ref[...]
    ncp = kc.shape[0]
    qcol = qi * QT + (lax.broadcasted_iota(jnp.int32, (1, M), 1) & (QT - 1))

    n_tiles = WINDOW // QT + 1
    qw = jnp.concatenate([qs, jnp.zeros_like(qs)], axis=0)
    pad_v = jnp.where(lax.broadcasted_iota(jnp.int32, (VROWS, QT), 0) >= NSA_DIM, 1.0, 0.0).astype(BF16)
    key_j = lax.broadcasted_iota(jnp.int32, (QT, M), 0)
    qry_i = lax.broadcasted_iota(jnp.int32, (QT, M), 1) & (QT - 1)
    k_rows, v_cols = [], []
    for w in range(n_tiles):
        tile = qi - (n_tiles - 1) + w
        off = pl.multiple_of(jnp.maximum(tile, 0) * QT, QT)
        kt = kw_ref[pl.ds(off, QT), :]
        vt = vw_ref[:, pl.ds(off, QT)]
        if w < n_tiles - 1:
            kt = jnp.where(tile >= 0, kt, jnp.zeros_like(kt))
            vt = jnp.where(tile >= 0, vt, pad_v)
        k_rows.append(kt)
        v_cols.append(vt)
    s_w = _dot(jnp.concatenate(k_rows, axis=0), qw)
    pieces = [s_w[w * QT:(w + 1) * QT] for w in range(n_tiles)]
    pieces[0] = pieces[0] + jnp.where(key_j > qry_i, 0.0, NEG)
    pieces[n_tiles - 2] = pieces[n_tiles - 2] + tsel_ref[0, 1]
    pieces[n_tiles - 1] = pieces[n_tiles - 1] + tsel_ref[0, 0]
    s_w = jnp.concatenate(pieces, axis=0)
    p_w = jnp.exp2(s_w - jnp.max(s_w, axis=0, keepdims=True)).astype(BF16)
    a_w = _dot(jnp.concatenate(v_cols, axis=1), p_w)
    o_win = a_w[:NSA_DIM] / a_w[NSA_DIM:NSA_DIM + 1]

    off_b = pl.multiple_of(ncp - (QT // CMP_STRIDE) * qi, QT // CMP_STRIDE)
    bias = jnp.concatenate([pat_ref[g, pl.ds(off_b, ncp), :] for g in range(G)], axis=1)
    s = _dot(kc, qs) + bias
    e = jnp.exp2(s - jnp.max(s, axis=0, keepdims=True))
    scale = jnp.where(qcol >= CMP_LEN - 1, 1.0 / jnp.sum(e, axis=0, keepdims=True), 0.0)
    p_cmp = (e * scale).astype(BF16)
    o_cmp = _dot(vc_t, p_cmp)
    imp_t = jnp.zeros((SLC_PAD, QT), F32)
    for g in range(G):
        imp_t = imp_t + _dot(ovt, p_cmp[:, g * QT:(g + 1) * QT])

    n_id = lax.broadcasted_iota(jnp.int32, (SLC_PAD, QT), 0)
    q_blk = (qi * QT + lax.broadcasted_iota(jnp.int32, (SLC_PAD, QT), 1)) // SLC_LEN
    forced = (n_id == 0) | (n_id == q_blk) | (n_id == q_blk - 1)
    imp = jnp.where(forced, POS_BIG, jnp.where(n_id > q_blk, NEG, imp_t))
    imp_ref[...] = imp
    SUB = 8
    slabs = [imp[v * SUB:(v + 1) * SUB] for v in range(SLC_PAD // SUB)]
    ranks = [jnp.zeros((SUB, QT), jnp.int32) for _ in slabs]
    sub_id = lax.broadcasted_iota(jnp.int32, (SUB, QT), 0)
    for m in range(SLC_PAD):
        other = imp_ref[m:m + 1, :]
        for v, slab in enumerate(slabs):
            if v > m // SUB:
                beats = (other >= slab).astype(jnp.int32)
            elif v < m // SUB:
                beats = (other > slab).astype(jnp.int32)
            else:
                beats = jnp.where(sub_id > m % SUB, (other >= slab).astype(jnp.int32),
                                  (other > slab).astype(jnp.int32))
            ranks[v] = ranks[v] + beats
    rank = jnp.concatenate(ranks, axis=0)
    selb = jnp.where(rank < SLC_TOPK, 0.0, NEG).astype(BF16)
    qaug = jnp.concatenate([qs, jnp.concatenate([selb] * G, axis=1)], axis=0)

    TK = 4 * QT
    n_t = qi // 4 + 1
    r = qi % 4
    t0_b = tsel_ref[0, 0]
    t1_b = tsel_ref[0, 1]
    last_bias = [jnp.where(r == j, t0_b, jnp.where(r == j + 1, t1_b, jnp.where(r < j, NEG, 0.0)))
                 for j in range(4)]
    prev_bias = [None, None, None, jnp.where(r == 0, t1_b, 0.0)]
    m_ref[...] = jnp.full(m_ref.shape, NEG, F32)
    acc_ref[...] = jnp.zeros(acc_ref.shape, F32)

    def qk(dst_ref, t):
        off = pl.multiple_of(t * TK, TK)
        dst_ref[...] = _dot(ksl_ref[pl.ds(off, TK), :], qaug)

    def consume(src_ref, t, bias):
        off = pl.multiple_of(t * TK, TK)
        s = src_ref[...]
        if bias is not None:
            s = jnp.concatenate([s[j * QT:(j + 1) * QT] if bias[j] is None else s[j * QT:(j + 1) * QT] + bias[j]
                                 for j in range(4)], axis=0)
        m_old = m_ref[...]
        m_new = jnp.maximum(m_old, jnp.max(s, axis=0, keepdims=True))
        p = jnp.exp2(s - m_new).astype(BF16)
        acc_ref[...] = acc_ref[...] * jnp.exp2(m_old - m_new) + _dot(vsl_ref[:, pl.ds(off, TK)], p)
        m_ref[...] = m_new

    n_plain = jnp.maximum(n_t - 2, 0)
    qk(sa_ref, 0)

    def pair(u, carry):
        t = 2 * u
        qk(sb_ref, t + 1)
        consume(sa_ref, t, None)
        qk(sa_ref, t + 2)
        consume(sb_ref, t + 1, None)
        return carry

    lax.fori_loop(0, n_plain // 2, pair, 0)
    tb = (n_plain // 2) * 2

    @pl.when(n_t == 1)
    def _():
        consume(sa_ref, 0, last_bias)

    @pl.when((n_t >= 2) & (n_plain % 2 == 0))
    def _():
        qk(sb_ref, tb + 1)
        consume(sa_ref, tb, prev_bias)
        consume(sb_ref, tb + 1, last_bias)

    @pl.when(n_plain % 2 == 1)
    def _():
        qk(sb_ref, tb + 1)
        consume(sa_ref, tb, None)
        qk(sa_ref, tb + 2)
        consume(sb_ref, tb + 1, prev_bias)
        consume(sa_ref, tb + 2, last_bias)

    a_s = acc_ref[...]
    o_sel = a_s[:NSA_DIM] / a_s[NSA_DIM:NSA_DIM + 1]

    gate = gate_ref[...]
    outs = []
    for g in range(G):
        cols = slice(g * QT, (g + 1) * QT)
        o_t = (gate[3 * g:3 * g + 1, :] * o_cmp[:, cols] + gate[3 * g + 1:3 * g + 2, :] * o_sel[:, cols]
               + gate[3 * g + 2:3 * g + 3, :] * o_win[:, cols])
        outs.append(o_t.T)
    o_ref[...] = jnp.concatenate(outs, axis=1).astype(BF16)


def _overlap_t(seq):
    ncp = seq // CMP_STRIDE
    n_cmp = (seq - CMP_LEN) // CMP_STRIDE + 1
    n_slc = seq // SLC_LEN
    cs = np.arange(ncp)[None, :] * CMP_STRIDE
    ss = np.arange(SLC_PAD)[:, None] * SLC_LEN
    ov = np.maximum(np.minimum(cs + CMP_LEN, ss + SLC_LEN) - np.maximum(cs, ss), 0).astype(np.float32) / CMP_LEN
    ov = ov * (np.arange(ncp)[None, :] < n_cmp) * (np.arange(SLC_PAD)[:, None] < n_slc)
    return jnp.asarray(ov, BF16)


def _nsa(zq_t, kc, vc_t, nk, nv_t, gates_t, pat, tsel, *, batch, seq):
    n = nk.shape[0]
    nq = seq // QT
    ncp = seq // CMP_STRIDE
    G = NSA_GROUP
    assert seq // SLC_LEN <= SLC_PAD and seq // SLC_LEN >= SLC_TOPK and seq >= WINDOW
    qcol = lambda b, hk, qi: (hk, b * nq + qi)
    return pl.pallas_call(
        _nsa_kernel,
        grid=(batch, NSA_KV_HEADS, nq),
        in_specs=[pl.BlockSpec((G * NSA_DIM, QT), qcol),
                  pl.BlockSpec((1, 1, ncp, NSA_DIM), lambda b, hk, qi: (b, hk, 0, 0)),
                  pl.BlockSpec((1, 1, NSA_DIM, ncp), lambda b, hk, qi: (b, hk, 0, 0)),
                  pl.BlockSpec((seq, LANE), lambda b, hk, qi: (b, hk)),
                  pl.BlockSpec((VROWS, seq), lambda b, hk, qi: (hk, b)),
                  pl.BlockSpec((seq, LANE), lambda b, hk, qi: (b, NSA_KV_HEADS + hk)),
                  pl.BlockSpec((VROWS, seq), lambda b, hk, qi: (NSA_KV_HEADS + hk, b)),
                  pl.BlockSpec((GATE_ROWS, QT), qcol),
                  pl.BlockSpec((G, 2 * ncp, QT), lambda b, hk, qi: (hk, 0, 0)),
                  pl.BlockSpec((1, 2, QT, G * QT), lambda b, hk, qi: (hk, 0, 0, 0)),
                  pl.BlockSpec((SLC_PAD, ncp), lambda b, hk, qi: (0, 0))],
        out_specs=pl.BlockSpec((QT, G * NSA_DIM), lambda b, hk, qi: (b * nq + qi, hk)),
        out_shape=jax.ShapeDtypeStruct((n, NSA_HEADS * NSA_DIM), BF16),
        scratch_shapes=[pltpu.VMEM((SLC_PAD, QT), F32),
                        pltpu.VMEM((VROWS, G * QT), F32),
                        pltpu.VMEM((1, G * QT), F32),
                        pltpu.VMEM((4 * QT, G * QT), F32),
                        pltpu.VMEM((4 * QT, G * QT), F32)],
        compiler_params=_params("parallel", "parallel", "arbitrary"),
        name="nsa_attention",
    )(zq_t, kc, vc_t, nk, nv_t, nk, nv_t, gates_t, pat, tsel, _overlap_t(seq))


MLA_TQ = 512


def _mla_kernel(q_ref, k_ref, v_ref, o_ref, acc_ref, m_ref, sa_ref, sb_ref):
    qi = pl.program_id(2)
    tq = MLA_TQ
    HP = 2
    qs = [q_ref[hh * LANE:(hh + 1) * LANE, :] for hh in range(HP)]
    m_ref[...] = jnp.full(m_ref.shape, NEG, F32)
    acc_ref[...] = jnp.zeros(acc_ref.shape, F32)

    def qk(dst_ref, off):
        for hh in range(HP):
            dst_ref[hh] = _dot(k_ref[pl.ds(off, tq), hh * LANE:(hh + 1) * LANE], qs[hh])

    def consume(src_ref, off, mask):
        for hh in range(HP):
            s = src_ref[hh]
            if mask is not None:
                s = s + mask
            vt = v_ref[hh * VROWS:(hh + 1) * VROWS, pl.ds(off, tq)]
            m_old = m_ref[hh]
            m_new = jnp.maximum(m_old, jnp.max(s, axis=0, keepdims=True))
            p = jnp.exp2(s - m_new).astype(BF16)
            acc_ref[hh] = acc_ref[hh] * jnp.exp2(m_old - m_new) + _dot(vt, p)
            m_ref[hh] = m_new

    key_j = lax.broadcasted_iota(jnp.int32, (tq, tq), 0)
    qry_i = lax.broadcasted_iota(jnp.int32, (tq, tq), 1)
    causal = jnp.where(key_j <= qry_i, 0.0, NEG)
    qk(sa_ref, 0)

    def pair(u, carry):
        off = pl.multiple_of(u * (2 * tq), 2 * tq)
        qk(sb_ref, off + tq)
        consume(sa_ref, off, None)
        qk(sa_ref, off + 2 * tq)
        consume(sb_ref, off + tq, None)
        return carry

    lax.fori_loop(0, qi // 2, pair, 0)
    base = pl.multiple_of((qi // 2) * (2 * tq), 2 * tq)

    @pl.when(qi % 2 == 0)
    def _():
        consume(sa_ref, base, causal)

    @pl.when(qi % 2 == 1)
    def _():
        qk(sb_ref, base + tq)
        consume(sa_ref, base, None)
        consume(sb_ref, base + tq, causal)

    outs = []
    for hh in range(HP):
        a = acc_ref[hh]
        outs.append((a[:MLA_V] / a[MLA_V:MLA_V + 1]).T)
    o_ref[...] = jnp.concatenate(outs, axis=1).astype(BF16)


def _mla(qm_t, km, vm_t, *, batch, seq):
    n = km.shape[0]
    tq = MLA_TQ
    nq = seq // tq
    HP = 2
    return pl.pallas_call(
        _mla_kernel,
        grid=(batch, MLA_HEADS // HP, nq),
        in_specs=[pl.BlockSpec((HP * LANE, tq), lambda b, hp, qi: (hp, b * nq + qi)),
                  pl.BlockSpec((seq, HP * LANE), lambda b, hp, qi: (b, hp)),
                  pl.BlockSpec((HP * VROWS, seq), lambda b, hp, qi: (hp, b))],
        out_specs=pl.BlockSpec((tq, HP * MLA_V), lambda b, hp, qi: (b * nq + qi, hp)),
        out_shape=jax.ShapeDtypeStruct((n, MLA_HEADS * MLA_V), BF16),
        scratch_shapes=[pltpu.VMEM((HP, VROWS, tq), F32),
                        pltpu.VMEM((HP, 1, tq), F32),
                        pltpu.VMEM((HP, tq, tq), F32),
                        pltpu.VMEM((HP, tq, tq), F32)],
        compiler_params=_params("parallel", "parallel", "arbitrary"),
        name="mla_attention",
    )(qm_t, km, vm_t)


def _tail_kernel(x_ref, om_ref, on_ref, p_ref, wo_ref, g1_ref, g2_ref, g3_ref,
                 wg_ref, wu_ref, cw_ref, cb_ref, wd_ref, pg_ref, pp_ref,
                 o_ref, carry_ref, act_ref, *, seq, tm):
    n_chunks = D_FF // FF_CHUNK
    half = om_ref.shape[1]
    y = _dot(om_ref[...], wo_ref[:half, :]) + _dot(on_ref[...], wo_ref[half:, :])
    x = x_ref[...] + _rms(y, g1_ref[...])

    h = _rms(x, g2_ref[...]).astype(BF16)

    @pl.when((pl.program_id(0) * tm) % seq == 0)
    def _():
        carry_ref[...] = jnp.zeros(carry_ref.shape, F32)

    SUB = 8
    row8 = lax.broadcasted_iota(jnp.int32, (SUB, 1), 0)
    for c in range(n_chunks):
        cols = slice(c * FF_CHUNK, (c + 1) * FF_CHUNK)
        g = _dot(h, wg_ref[:, cols])
        prev = carry_ref[:, cols]
        carry_ref[:, cols] = g[tm - SUB:, :]
        r1 = pltpu.roll(g, 1, 0)
        r2 = pltpu.roll(g, 2, 0)
        top1 = jnp.where(row8 == 0, prev[7:8, :], r1[:SUB])
        top2 = jnp.where(row8 == 0, prev[6:7, :], jnp.where(row8 == 1, prev[7:8, :], r2[:SUB]))
        g1 = jnp.concatenate([top1, r1[SUB:]], axis=0)
        g2 = jnp.concatenate([top2, r2[SUB:]], axis=0)
        conv = (cw_ref[0:1, cols] * g2 + cw_ref[1:2, cols] * g1 + cw_ref[2:3, cols] * g
                + cb_ref[:, cols])
        act_ref[:, cols] = (_gelu_tanh(conv) * _dot(h, wu_ref[:, cols])).astype(BF16)
    x = x + _rms(_dot(act_ref[...], wd_ref[...]), g3_ref[...])

    gate = _sigmoid(_dot(x.astype(BF16), pg_ref[...]))
    o_ref[...] = x + gate * _dot(p_ref[...].astype(BF16), pp_ref[...])


def _tail(x2, om, on, p2, wo, g1, g2, g3, wg, wu, cw, cb, wd, pg, pp, *, seq):
    n = x2.shape[0]
    tm = 512
    row = lambda r: (r, 0)
    half = om.shape[1]
    return pl.pallas_call(
        functools.partial(_tail_kernel, seq=seq, tm=tm),
        grid=(n // tm,),
        in_specs=[pl.BlockSpec((tm, D_MODEL), row),
                  pl.BlockSpec((tm, half), row),
                  pl.BlockSpec((tm, half), row),
                  pl.BlockSpec((tm, PLE_DIM), row),
                  _resident((2 * half, D_MODEL)),
                  _resident((1, D_MODEL)),
                  _resident((1, D_MODEL)),
                  _resident((1, D_MODEL)),
                  _resident((D_MODEL, D_FF)),
                  _resident((D_MODEL, D_FF)),
                  _resident((CONV_WIDTH, D_FF)),
                  _resident((1, D_FF)),
                  _resident((D_FF, D_MODEL)),
                  _resident((D_MODEL, D_MODEL)),
                  _resident((PLE_DIM, D_MODEL))],
        out_specs=pl.BlockSpec((tm, D_MODEL), row),
        out_shape=jax.ShapeDtypeStruct((n, D_MODEL), F32),
        scratch_shapes=[pltpu.VMEM((8, D_FF), F32),
                        pltpu.VMEM((tm, D_FF), BF16)],
        compiler_params=_params("arbitrary"),
        name="layer_tail",
    )(x2, om, on, p2, wo, g1, g2, g3, wg, wu, cw, cb, wd, pg, pp)


def _rot_cols(w):
    half = w.shape[1] // 2
    return jnp.concatenate([-w[:, half:], w[:, :half]], axis=1)


def _prep_inproj(w):
    o = np.cumsum((0,) + IN_SPLITS)
    parts = [w[:, o[j]:o[j + 1]] for j in range(len(IN_SPLITS))]
    cq, ckv, kr, qn, kcmp, vcmp, kslc, vslc, kwin, vwin, gn = parts
    d = w.shape[0]
    z64 = jnp.zeros((d, NSA_DIM), F32)
    z32 = jnp.zeros((d, LANE - MLA_NOPE - MLA_ROPE), F32)

    def slots(m):
        return jnp.concatenate([m[:, :NSA_DIM], z64, m[:, NSA_DIM:], z64], axis=1)

    w1 = jnp.concatenate([cq, ckv,
                          z64, kr, z32,
                          z64, _rot_cols(kr), z32,
                          slots(kslc), slots(kwin), kcmp, vcmp], axis=1)
    assert w1.shape[1] == C_TOTAL

    def vslots(m):
        zp = jnp.zeros((VROWS - NSA_DIM, d), F32)
        return jnp.concatenate([m[:, :NSA_DIM].T, zp, m[:, NSA_DIM:].T, zp], axis=0)

    per = 3 * NSA_GROUP
    gp = jnp.zeros((GATE_ROWS - per, d), F32)
    w1t = jnp.concatenate([qn.T * NSA_DIM ** -0.5, vslots(vslc), vslots(vwin),
                           gn[:, :per].T, gp, gn[:, per:].T, gp], axis=0)
    assert w1t.shape[0] == R_TOTAL
    return w1.astype(BF16), w1t.astype(BF16)


def _prep_mla(w_uq, w_ukv):
    r = w_uq.shape[0]
    dq = MLA_NOPE + MLA_ROPE
    z32 = jnp.zeros((r, LANE - dq), F32)
    z64q = jnp.zeros((r, MLA_NOPE), F32)
    qa, qb = [], []
    for h in range(MLA_HEADS):
        blk = w_uq[:, h * dq:(h + 1) * dq]
        qa += [blk, z32]
        qb += [z64q, _rot_cols(blk[:, MLA_NOPE:]), z32]
    rk = w_ukv.shape[0]
    z64 = jnp.zeros((rk, LANE - MLA_NOPE), F32)
    zv = jnp.zeros((rk, VROWS - MLA_V), F32)
    wk, wv = [], []
    dkv = MLA_NOPE + MLA_V
    for h in range(MLA_HEADS):
        blk = w_ukv[:, h * dkv:(h + 1) * dkv]
        wk += [blk[:, :MLA_NOPE], z64]
        wv += [blk[:, MLA_NOPE:], zv]
    cat = lambda xs: jnp.concatenate(xs, axis=1).astype(BF16)
    return cat(qa).T, cat(qb).T, cat(wk), cat(wv).T


def kernel(x, p, positions, rel_bias, attn_pre_norm, attn_post_norm, ffn_pre_norm, ffn_post_norm,
           w_in, mla_q_norm, mla_w_uq, mla_kv_norm, mla_w_ukv, nsa_cmp_pos, nsa_cmp_w1, nsa_cmp_w2,
           w_o, ffn_w_gate, ffn_w_up, ffn_conv_w, ffn_conv_b, ffn_w_down, ple_proj, ple_gate):
    batch, seq, d = x.shape
    depth = w_in.shape[0]
    n = batch * seq
    x2 = x.reshape(n, d)
    ctt, stt, ct, st = _rope_tables(positions)
    pat, tsel = _bias_tables(rel_bias, seq)
    for i in range(depth):
        w1, w1t = _prep_inproj(w_in[i])
        wqa, wqb, wk, wvt = _prep_mla(mla_w_uq[i], mla_w_ukv[i])
        qm_t, km, vm_t, zq_t, nk, nv_t, kcmp, vcmp, gates_t = _inproj(
            x2, attn_pre_norm[i][None, :], w1, w1t, mla_q_norm[i][None, :], wqa, wqb,
            mla_kv_norm[i][None, :], wk, wvt, ct, st, ctt, stt, batch=batch, seq=seq)
        kc, vc_t = _compress(kcmp, vcmp,
                             nsa_cmp_pos[i].reshape(2, 1, CMP_LEN * NSA_DIM),
                             nsa_cmp_w1[i].astype(BF16), nsa_cmp_w2[i, 0].astype(BF16),
                             nsa_cmp_w2[i, 1].T.astype(BF16), batch=batch, seq=seq)
        o_nsa = _nsa(zq_t, kc, vc_t, nk, nv_t, gates_t, pat, tsel, batch=batch, seq=seq)
        o_mla = _mla(qm_t, km, vm_t, batch=batch, seq=seq)
        wg = ffn_w_gate[i].astype(BF16)
        wu = ffn_w_up[i].astype(BF16)
        cw = ffn_conv_w[i]
        cb = ffn_conv_b[i][None, :]
        wd = ffn_w_down[i].astype(BF16)
        x2 = _tail(x2, o_mla, o_nsa, p[i].reshape(n, PLE_DIM), w_o[i].astype(BF16),
                   attn_post_norm[i][None, :], ffn_pre_norm[i][None, :], ffn_post_norm[i][None, :],
                   wg, wu, cw, cb, wd, ple_gate[i].astype(BF16), ple_proj[i].astype(BF16), seq=seq)
    return x2.reshape(batch, seq, d)
```

```python
import functools
import math

import numpy as np
import jax
import jax.numpy as jnp
from jax import lax
from jax.experimental import pallas as pl
from jax.experimental.pallas import tpu as pltpu

F32 = jnp.float32
BF16 = jnp.bfloat16

D_MODEL = 1024
DEPTH = 2
MLA_HEADS = 8
MLA_NOPE = 64
MLA_ROPE = 32
MLA_V = 64
MLA_Q_RANK = 256
MLA_KV_RANK = 128
ROPE_BASE = 10000.0
NSA_HEADS = 8
NSA_KV_HEADS = 2
NSA_GROUP = NSA_HEADS // NSA_KV_HEADS
NSA_DIM = 64
CMP_LEN = 32
CMP_STRIDE = 16
CMP_HIDDEN = 128
SLC_LEN = 64
SLC_TOPK = 16
WINDOW = 512
REL_BUCKETS = 32
REL_MAX_DIST = 128
D_FF = 2816
CONV_WIDTH = 3
PLE_DIM = 256
EPS = 1e-6
NEG = -1e30
POS_BIG = 1e30
LOG2E = math.log2(math.e)

IN_SPLITS = (MLA_Q_RANK, MLA_KV_RANK, MLA_ROPE, NSA_HEADS * NSA_DIM,
             NSA_KV_HEADS * NSA_DIM, NSA_KV_HEADS * NSA_DIM,
             NSA_KV_HEADS * NSA_DIM, NSA_KV_HEADS * NSA_DIM,
             NSA_KV_HEADS * NSA_DIM, NSA_KV_HEADS * NSA_DIM,
             3 * NSA_HEADS)

LANE = 128
QT = 128
SLC_PAD = 64
FF_CHUNK = 256
VMEM_LIMIT = 56 * 1024 * 1024

VROWS = 80

C_CQ = 0
C_CKV = 256
C_KR = 384
C_KRROT = 512
C_KSLC = 640
C_KWIN = 896
C_KCMP = 1152
C_VCMP = 1280
C_TOTAL = 1408
R_QN = 0
R_VSLC = 512
R_VWIN = R_VSLC + NSA_KV_HEADS * VROWS
R_GATE = R_VWIN + NSA_KV_HEADS * VROWS
GATE_ROWS = 16
R_TOTAL = R_GATE + NSA_KV_HEADS * GATE_ROWS


def _dot(a, b):
    return jnp.dot(a, b, preferred_element_type=F32)


def _dot_nt(a, b):
    return lax.dot_general(a, b, (((1,), (1,)), ((), ())), preferred_element_type=F32)


def _rms(x, g):
    return x * lax.rsqrt(jnp.mean(x * x, axis=-1, keepdims=True) + EPS) * g


def _gelu_tanh(x):
    return 0.5 * x * (1.0 + jnp.tanh(math.sqrt(2.0 / math.pi) * (x + 0.044715 * (x * x * x))))


def _sigmoid(x):
    return 1.0 / (1.0 + jnp.exp(-x))


def _params(*sem):
    return pltpu.CompilerParams(dimension_semantics=sem, vmem_limit_bytes=VMEM_LIMIT)


def _resident(shape):
    nd = len(shape)
    return pl.BlockSpec(shape, lambda *_: (0,) * nd, pipeline_mode=pl.Buffered(1))


def _rope_kernel(pos_ref, inv_ref, ctt_ref, stt_ref, ct_ref, st_ref):
    ang = inv_ref[...] * pos_ref[...].astype(F32)
    row = lax.broadcasted_iota(jnp.int32, ang.shape, 0)
    rope = (row >= MLA_NOPE) & (row < MLA_NOPE + MLA_ROPE)
    ct = jnp.where(rope, jnp.cos(ang), jnp.where(row < MLA_NOPE, 1.0, 0.0))
    st = jnp.where(rope, jnp.sin(ang), 0.0)
    ctt_ref[...] = ct
    stt_ref[...] = st
    ct_ref[...] = ct.T
    st_ref[...] = st.T


def _rope_tables(positions):
    n = positions.size
    tn = 2048
    half = MLA_ROPE // 2
    inv = ROPE_BASE ** (-jnp.arange(half, dtype=F32) / half)
    inv_slot = jnp.concatenate([jnp.zeros((MLA_NOPE,), F32), inv, inv,
                                jnp.zeros((LANE - MLA_NOPE - MLA_ROPE,), F32)])[:, None]
    return pl.pallas_call(
        _rope_kernel,
        grid=(n // tn,),
        in_specs=[pl.BlockSpec((1, tn), lambda r: (0, r)),
                  pl.BlockSpec((LANE, 1), lambda r: (0, 0))],
        out_specs=[pl.BlockSpec((LANE, tn), lambda r: (0, r))] * 2
                  + [pl.BlockSpec((tn, LANE), lambda r: (r, 0))] * 2,
        out_shape=[jax.ShapeDtypeStruct((LANE, n), F32)] * 2 + [jax.ShapeDtypeStruct((n, LANE), F32)] * 2,
        compiler_params=_params("parallel"),
        name="rope_tables",
    )(positions.reshape(1, n), inv_slot)


def _bucket_np(dist):
    n = np.maximum(dist, 0)
    max_exact = REL_BUCKETS // 2
    large = max_exact + (np.log(np.maximum(n, 1).astype(np.float32) / max_exact)
                         / math.log(REL_MAX_DIST / max_exact)
                         * (REL_BUCKETS - max_exact)).astype(np.int32)
    large = np.minimum(large, REL_BUCKETS - 1)
    return np.where(n < max_exact, n, large).astype(np.int32)


def _bias_kernel(table_ref, bpc_ref, bpt_ref, pat_ref, tsel_ref):
    h = pl.program_id(0)
    far = table_ref[REL_BUCKETS - 1, h]

    def lookup(bp, sub):
        acc = jnp.full(bp.shape, far - sub, F32)
        for b in range(REL_BUCKETS - 1):
            acc = jnp.where(bp == b, table_ref[b, h] - sub, acc)
        return jnp.where(bp < 0, NEG, acc * LOG2E)

    pat_ref[0] = lookup(bpc_ref[...], 0.0)
    tsel_ref[0, 0] = lookup(bpt_ref[0], far)
    tsel_ref[0, 1] = lookup(bpt_ref[1], far)


def _bias_tables(rel_bias, seq):
    ncp = seq // CMP_STRIDE
    i = np.arange(QT)[None, :]
    cprime = np.arange(2 * ncp)[:, None] - ncp
    dist_c = i - CMP_STRIDE * cprime - (CMP_LEN - 1)
    bpc = np.where(dist_c >= 0, _bucket_np(dist_c), -1).astype(np.int32)
    j = np.arange(QT)[:, None]
    d0 = i - j
    d1 = QT + i - j
    bpt = np.stack([np.where(d0 >= 0, _bucket_np(d0), -1), _bucket_np(d1)]).astype(np.int32)
    return pl.pallas_call(
        _bias_kernel,
        grid=(NSA_HEADS,),
        in_specs=[pl.BlockSpec(memory_space=pltpu.SMEM),
                  pl.BlockSpec((2 * ncp, QT), lambda h: (0, 0)),
                  pl.BlockSpec((2, QT, QT), lambda h: (0, 0, 0))],
        out_specs=[pl.BlockSpec((1, 2 * ncp, QT), lambda h: (h, 0, 0)),
                   pl.BlockSpec((1, 2, QT, QT), lambda h: (h // NSA_GROUP, 0, 0, h % NSA_GROUP))],
        out_shape=[jax.ShapeDtypeStruct((NSA_HEADS, 2 * ncp, QT), F32),
                   jax.ShapeDtypeStruct((NSA_KV_HEADS, 2, QT, NSA_GROUP * QT), F32)],
        compiler_params=_params("parallel"),
        name="bias_tables",
    )(rel_bias.astype(F32), jnp.asarray(bpc), jnp.asarray(bpt))


def _inproj_kernel(x_ref, g_ref, w1_ref, w1t_ref, qn_ref, wqa_ref, wqb_ref, kvn_ref, wk_ref, wvt_ref,
                   ct_ref, st_ref, ctt_ref, stt_ref, ones_n_ref, ones_m_ref,
                   qm_ref, km_ref, vm_ref, zq_ref, nk_ref, nv_ref, kcmp_ref, vcmp_ref, gate_ref,
                   *, seq, tm):
    h = _rms(x_ref[...], g_ref[...]).astype(BF16)

    def proj(c0, width):
        return _dot(h, w1_ref[:, c0:c0 + width])

    cq = _rms(proj(C_CQ, MLA_Q_RANK), qn_ref[...]).astype(BF16)
    qa = _dot_nt(wqa_ref[...], cq)
    qb = _dot_nt(wqb_ref[...], cq)
    scale = (MLA_NOPE + MLA_ROPE) ** -0.5 * LOG2E
    cts = ctt_ref[...] * scale
    sts = stt_ref[...] * scale
    for hh in range(MLA_HEADS):
        sl = slice(hh * LANE, (hh + 1) * LANE)
        qm_ref[sl, :] = (qa[sl] * cts + qb[sl] * sts).astype(BF16)

    ckv = _rms(proj(C_CKV, MLA_KV_RANK), kvn_ref[...]).astype(BF16)
    kr = proj(C_KR, LANE) * ct_ref[...] + proj(C_KRROT, LANE) * st_ref[...]
    kn = _dot(ckv, wk_ref[...])
    for hh in range(MLA_HEADS):
        sl = slice(hh * LANE, (hh + 1) * LANE)
        km_ref[:, sl] = (kn[:, sl] + kr).astype(BF16)
    vm_ref[...] = (_dot_nt(wvt_ref[...], ckv) + ones_m_ref[...]).astype(BF16)

    zt = _dot_nt(w1t_ref[...], h)
    zq_ref[...] = (zt[R_QN:R_VSLC] * LOG2E).astype(BF16)
    nv_ref[...] = (zt[R_VSLC:R_GATE] + ones_n_ref[...]).astype(BF16)
    gate_ref[...] = _sigmoid(zt[R_GATE:R_TOTAL])

    lane = lax.broadcasted_iota(jnp.int32, (tm, LANE), 1)
    s0 = (pl.program_id(0) * tm) % seq
    row = lax.broadcasted_iota(jnp.int32, (tm, LANE), 0)
    onehot = (lane - NSA_DIM == (s0 + row) // SLC_LEN).astype(F32)
    ksl = proj(C_KSLC, 2 * LANE)
    for hk in range(NSA_KV_HEADS):
        sl = slice(hk * LANE, (hk + 1) * LANE)
        nk_ref[:, sl] = (ksl[:, sl] + onehot).astype(BF16)
    nk_ref[:, 2 * LANE:] = proj(C_KWIN, 2 * LANE).astype(BF16)

    kcmp = proj(C_KCMP, LANE)
    vcmp = proj(C_VCMP, LANE)
    for hk in range(NSA_KV_HEADS):
        kcmp_ref[0, hk] = kcmp[:, hk * NSA_DIM:(hk + 1) * NSA_DIM]
        vcmp_ref[0, hk] = vcmp[:, hk * NSA_DIM:(hk + 1) * NSA_DIM]


def _ones_rows(n_slots):
    r = np.arange(n_slots * VROWS) % VROWS
    return jnp.asarray((r >= NSA_DIM).astype(np.float32)[:, None])


def _inproj(x2, g_pre, w1, w1t, qn, wqa, wqb, kvn, wk, wvt, ct, st, ctt, stt, *, batch, seq):
    n = x2.shape[0]
    tm = 512
    tiles_per_seq = seq // tm
    row = lambda r: (r, 0)
    col = lambda r: (0, r)
    hm = lambda r: (r // tiles_per_seq, 0, r % tiles_per_seq, 0)
    nv_rows = 2 * NSA_KV_HEADS * VROWS
    vm_rows = MLA_HEADS * VROWS
    return pl.pallas_call(
        functools.partial(_inproj_kernel, seq=seq, tm=tm),
        grid=(n // tm,),
        in_specs=[pl.BlockSpec((tm, D_MODEL), row),
                  _resident((1, D_MODEL)),
                  _resident((D_MODEL, C_TOTAL)),
                  _resident((R_TOTAL, D_MODEL)),
                  _resident((1, MLA_Q_RANK)),
                  _resident((MLA_HEADS * LANE, MLA_Q_RANK)),
                  _resident((MLA_HEADS * LANE, MLA_Q_RANK)),
                  _resident((1, MLA_KV_RANK)),
                  _resident((MLA_KV_RANK, MLA_HEADS * LANE)),
                  _resident((vm_rows, MLA_KV_RANK)),
                  pl.BlockSpec((tm, LANE), row),
                  pl.BlockSpec((tm, LANE), row),
                  pl.BlockSpec((LANE, tm), col),
                  pl.BlockSpec((LANE, tm), col),
                  _resident((nv_rows, 1)),
                  _resident((vm_rows, 1))],
        out_specs=[pl.BlockSpec((MLA_HEADS * LANE, tm), col),
                   pl.BlockSpec((tm, MLA_HEADS * LANE), row),
                   pl.BlockSpec((vm_rows, tm), col),
                   pl.BlockSpec((NSA_HEADS * NSA_DIM, tm), col),
                   pl.BlockSpec((tm, 4 * LANE), row),
                   pl.BlockSpec((nv_rows, tm), col),
                   pl.BlockSpec((1, NSA_KV_HEADS, tm, NSA_DIM), hm),
                   pl.BlockSpec((1, NSA_KV_HEADS, tm, NSA_DIM), hm),
                   pl.BlockSpec((NSA_KV_HEADS * GATE_ROWS, tm), col)],
        out_shape=[jax.ShapeDtypeStruct((MLA_HEADS * LANE, n), BF16),
                   jax.ShapeDtypeStruct((n, MLA_HEADS * LANE), BF16),
                   jax.ShapeDtypeStruct((vm_rows, n), BF16),
                   jax.ShapeDtypeStruct((NSA_HEADS * NSA_DIM, n), BF16),
                   jax.ShapeDtypeStruct((n, 4 * LANE), BF16),
                   jax.ShapeDtypeStruct((nv_rows, n), BF16),
                   jax.ShapeDtypeStruct((batch, NSA_KV_HEADS, seq, NSA_DIM), F32),
                   jax.ShapeDtypeStruct((batch, NSA_KV_HEADS, seq, NSA_DIM), F32),
                   jax.ShapeDtypeStruct((NSA_KV_HEADS * GATE_ROWS, n), F32)],
        compiler_params=_params("parallel"),
        name="in_proj",
    )(x2, g_pre, w1, w1t, qn, wqa, wqb, kvn, wk, wvt, ct, st, ctt, stt,
      _ones_rows(2 * NSA_KV_HEADS), _ones_rows(MLA_HEADS))


def _compress_kernel(k16_ref, v16_ref, pos_ref, w1_ref, w2k_ref, w2vt_ref, kc_ref, vc_ref):
    half = CMP_STRIDE * NSA_DIM

    def hidden(x16, j):
        pos = pos_ref[j]
        top = (x16 + pos[:, :half]).astype(BF16)
        bot = (x16 + pos[:, half:]).astype(BF16)
        u = _dot(top, w1_ref[j, :half, :])
        low = _dot(bot, w1_ref[j, half:, :])
        nxt = jnp.concatenate([low[1:], jnp.zeros((1, CMP_HIDDEN), F32)], axis=0)
        return _gelu_tanh(u + nxt).astype(BF16)

    kc_ref[0, 0] = _dot(hidden(k16_ref[0, 0], 0), w2k_ref[...]).astype(BF16)
    vc_ref[0, 0] = _dot_nt(w2vt_ref[...], hidden(v16_ref[0, 0], 1)).astype(BF16)


def _compress(kcmp, vcmp, pos, w1, w2k, w2vt, *, batch, seq):
    ncp = seq // CMP_STRIDE
    half = CMP_STRIDE * NSA_DIM
    k16 = kcmp.reshape(batch, NSA_KV_HEADS, ncp, half)
    v16 = vcmp.reshape(batch, NSA_KV_HEADS, ncp, half)
    blk = lambda b, hk: (b, hk, 0, 0)
    return pl.pallas_call(
        _compress_kernel,
        grid=(batch, NSA_KV_HEADS),
        in_specs=[pl.BlockSpec((1, 1, ncp, half), blk),
                  pl.BlockSpec((1, 1, ncp, half), blk),
                  pl.BlockSpec((2, 1, 2 * half), lambda b, hk: (0, 0, 0)),
                  pl.BlockSpec((2, 2 * half, CMP_HIDDEN), lambda b, hk: (0, 0, 0)),
                  pl.BlockSpec((CMP_HIDDEN, NSA_DIM), lambda b, hk: (0, 0)),
                  pl.BlockSpec((NSA_DIM, CMP_HIDDEN), lambda b, hk: (0, 0))],
        out_specs=[pl.BlockSpec((1, 1, ncp, NSA_DIM), blk),
                   pl.BlockSpec((1, 1, NSA_DIM, ncp), blk)],
        out_shape=[jax.ShapeDtypeStruct((batch, NSA_KV_HEADS, ncp, NSA_DIM), BF16),
                   jax.ShapeDtypeStruct((batch, NSA_KV_HEADS, NSA_DIM, ncp), BF16)],
        compiler_params=_params("parallel", "parallel"),
        name="nsa_compress",
    )(k16, v16, pos, w1, w2k, w2vt)


def _nsa_kernel(zq_ref, kc_ref, vc_ref, ksl_ref, vsl_ref, kw_ref, vw_ref, gate_ref,
                pat_ref, tsel_ref, ovt_ref, o_ref, imp_ref, acc_ref, m_ref, sa_ref, sb_ref):
    qi = pl.program_id(2)
    G = NSA_GROUP
    M = G * QT
    q4 = zq_ref[...]
    qs = jnp.concatenate([q4[g * NSA_DIM:(g + 1) * NSA_DIM, :] for g in range(G)], axis=1)
    kc = kc_ref[0, 0]
    vc_t = vc_ref[0, 0]
    ovt = ovt_ref[...]
    ncp = kc.shape[0]
    qcol = qi * QT + (lax.broadcasted_iota(jnp.int32, (1, M), 1) & (QT - 1))

    n_tiles = WINDOW // QT + 1
    qw = jnp.concatenate([qs, jnp.zeros_like(qs)], axis=0)
    pad_v = jnp.where(lax.broadcasted_iota(jnp.int32, (VROWS, QT), 0) >= NSA_DIM, 1.0, 0.0).astype(BF16)
    k_rows, v_cols = [], []
    for w in range(n_tiles):
        tile = qi - (n_tiles - 1) + w
        off = pl.multiple_of(jnp.maximum(tile, 0) * QT, QT)
        kt = kw_ref[pl.ds(off, QT), :]
        vt = vw_ref[:, pl.ds(off, QT)]
        if w < n_tiles - 1:
            kt = jnp.where(tile >= 0, kt, jnp.zeros_like(kt))
            vt = jnp.where(tile >= 0, vt, pad_v)
        k_rows.append(kt)
        v_cols.append(vt)
    s_w = _dot(jnp.concatenate(k_rows, axis=0), qw)

    off_b = pl.multiple_of(ncp - (QT // CMP_STRIDE) * qi, QT // CMP_STRIDE)
    bias = jnp.concatenate([pat_ref[g, pl.ds(off_b, ncp), :] for g in range(G)], axis=1)
    s = _dot(kc, qs) + bias
    e = jnp.exp2(s - jnp.max(s, axis=0, keepdims=True))
    scale = jnp.where(qcol >= CMP_LEN - 1, 1.0 / jnp.sum(e, axis=0, keepdims=True), 0.0)
    p_cmp = (e * scale).astype(BF16)

    key_j = lax.broadcasted_iota(jnp.int32, (QT, M), 0)
    qry_i = lax.broadcasted_iota(jnp.int32, (QT, M), 1) & (QT - 1)
    pieces = [s_w[w * QT:(w + 1) * QT] for w in range(n_tiles)]
    pieces[0] = pieces[0] + jnp.where(key_j > qry_i, 0.0, NEG)
    pieces[n_tiles - 2] = pieces[n_tiles - 2] + tsel_ref[0, 1]
    pieces[n_tiles - 1] = pieces[n_tiles - 1] + tsel_ref[0, 0]
    s_w = jnp.concatenate(pieces, axis=0)
    p_w = jnp.exp2(s_w - jnp.max(s_w, axis=0, keepdims=True)).astype(BF16)

    o_cmp = _dot(vc_t, p_cmp)
    imp_t = jnp.zeros((SLC_PAD, QT), F32)
    for g in range(G):
        imp_t = imp_t + _dot(ovt, p_cmp[:, g * QT:(g + 1) * QT])

    a_w = _dot(jnp.concatenate(v_cols, axis=1), p_w)
    o_win = a_w[:NSA_DIM] / a_w[NSA_DIM:NSA_DIM + 1]

    n_id = lax.broadcasted_iota(jnp.int32, (SLC_PAD, QT), 0)
    q_blk = (qi * QT + lax.broadcasted_iota(jnp.int32, (SLC_PAD, QT), 1)) // SLC_LEN
    forced = (n_id == 0) | (n_id == q_blk) | (n_id == q_blk - 1)
    imp = jnp.where(forced, POS_BIG, jnp.where(n_id > q_blk, NEG, imp_t))
    imp_ref[...] = imp
    SUB = 8
    slabs = [imp[v * SUB:(v + 1) * SUB] for v in range(SLC_PAD // SUB)]
    ranks = [jnp.zeros((SUB, QT), jnp.int32) for _ in slabs]
    sub_id = lax.broadcasted_iota(jnp.int32, (SUB, QT), 0)
    for m in range(SLC_PAD):
        other = imp_ref[m:m + 1, :]
        for v, slab in enumerate(slabs):
            if v > m // SUB:
                beats = (other >= slab).astype(jnp.int32)
            elif v < m // SUB:
                beats = (other > slab).astype(jnp.int32)
            else:
                beats = jnp.where(sub_id > m % SUB, (other >= slab).astype(jnp.int32),
                                  (other > slab).astype(jnp.int32))
            ranks[v] = ranks[v] + beats
    rank = jnp.concatenate(ranks, axis=0)
    selb = jnp.where(rank < SLC_TOPK, 0.0, NEG).astype(BF16)
    qaug = jnp.concatenate([qs, jnp.concatenate([selb] * G, axis=1)], axis=0)

    TK = 4 * QT
    n_t = qi // 4 + 1
    r = qi % 4
    t0_b = tsel_ref[0, 0]
    t1_b = tsel_ref[0, 1]
    last_bias = [jnp.where(r == j, t0_b, jnp.where(r == j + 1, t1_b, jnp.where(r < j, NEG, 0.0)))
                 for j in range(4)]
    prev_bias = [None, None, None, jnp.where(r == 0, t1_b, 0.0)]
    m_ref[...] = jnp.full(m_ref.shape, NEG, F32)
    acc_ref[...] = jnp.zeros(acc_ref.shape, F32)

    def qk(dst_ref, t):
        off = pl.multiple_of(t * TK, TK)
        dst_ref[...] = _dot(ksl_ref[pl.ds(off, TK), :], qaug)

    def consume(src_ref, t, bias):
        off = pl.multiple_of(t * TK, TK)
        s = src_ref[...]
        if bias is not None:
            s = jnp.concatenate([s[j * QT:(j + 1) * QT] if bias[j] is None else s[j * QT:(j + 1) * QT] + bias[j]
                                 for j in range(4)], axis=0)
        m_old = m_ref[...]
        m_new = jnp.maximum(m_old, jnp.max(s, axis=0, keepdims=True))
        p = jnp.exp2(s - m_new).astype(BF16)
        acc_ref[...] = acc_ref[...] * jnp.exp2(m_old - m_new) + _dot(vsl_ref[:, pl.ds(off, TK)], p)
        m_ref[...] = m_new

    n_plain = jnp.maximum(n_t - 2, 0)
    qk(sa_ref, 0)

    def pair(u, carry):
        t = 2 * u
        qk(sb_ref, t + 1)
        consume(sa_ref, t, None)
        qk(sa_ref, t + 2)
        consume(sb_ref, t + 1, None)
        return carry

    lax.fori_loop(0, n_plain // 2, pair, 0)
    tb = (n_plain // 2) * 2

    @pl.when(n_t == 1)
    def _():
        consume(sa_ref, 0, last_bias)

    @pl.when((n_t >= 2) & (n_plain % 2 == 0))
    def _():
        qk(sb_ref, tb + 1)
        consume(sa_ref, tb, prev_bias)
        consume(sb_ref, tb + 1, last_bias)

    @pl.when(n_plain % 2 == 1)
    def _():
        qk(sb_ref, tb + 1)
        consume(sa_ref, tb, None)
        qk(sa_ref, tb + 2)
        consume(sb_ref, tb + 1, prev_bias)
        consume(sa_ref, tb + 2, last_bias)

    a_s = acc_ref[...]
    o_sel = a_s[:NSA_DIM] / a_s[NSA_DIM:NSA_DIM + 1]

    gate = gate_ref[...]
    outs = []
    for g in range(G):
        cols = slice(g * QT, (g + 1) * QT)
        o_t = (gate[3 * g:3 * g + 1, :] * o_cmp[:, cols] + gate[3 * g + 1:3 * g + 2, :] * o_sel[:, cols]
               + gate[3 * g + 2:3 * g + 3, :] * o_win[:, cols])
        outs.append(o_t.T)
    o_ref[...] = jnp.concatenate(outs, axis=1).astype(BF16)


def _overlap_t(seq):
    ncp = seq // CMP_STRIDE
    n_cmp = (seq - CMP_LEN) // CMP_STRIDE + 1
    n_slc = seq // SLC_LEN
    cs = np.arange(ncp)[None, :] * CMP_STRIDE
    ss = np.arange(SLC_PAD)[:, None] * SLC_LEN
    ov = np.maximum(np.minimum(cs + CMP_LEN, ss + SLC_LEN) - np.maximum(cs, ss), 0).astype(np.float32) / CMP_LEN
    ov = ov * (np.arange(ncp)[None, :] < n_cmp) * (np.arange(SLC_PAD)[:, None] < n_slc)
    return jnp.asarray(ov, BF16)


def _nsa(zq_t, kc, vc_t, nk, nv_t, gates_t, pat, tsel, *, batch, seq):
    n = nk.shape[0]
    nq = seq // QT
    ncp = seq // CMP_STRIDE
    G = NSA_GROUP
    assert seq // SLC_LEN <= SLC_PAD and seq // SLC_LEN >= SLC_TOPK and seq >= WINDOW
    qcol = lambda b, hk, qi: (hk, b * nq + qi)
    return pl.pallas_call(
        _nsa_kernel,
        grid=(batch, NSA_KV_HEADS, nq),
        in_specs=[pl.BlockSpec((G * NSA_DIM, QT), qcol),
                  pl.BlockSpec((1, 1, ncp, NSA_DIM), lambda b, hk, qi: (b, hk, 0, 0)),
                  pl.BlockSpec((1, 1, NSA_DIM, ncp), lambda b, hk, qi: (b, hk, 0, 0)),
                  pl.BlockSpec((seq, LANE), lambda b, hk, qi: (b, hk)),
                  pl.BlockSpec((VROWS, seq), lambda b, hk, qi: (hk, b)),
                  pl.BlockSpec((seq, LANE), lambda b, hk, qi: (b, NSA_KV_HEADS + hk)),
                  pl.BlockSpec((VROWS, seq), lambda b, hk, qi: (NSA_KV_HEADS + hk, b)),
                  pl.BlockSpec((GATE_ROWS, QT), qcol),
                  pl.BlockSpec((G, 2 * ncp, QT), lambda b, hk, qi: (hk, 0, 0)),
                  pl.BlockSpec((1, 2, QT, G * QT), lambda b, hk, qi: (hk, 0, 0, 0)),
                  pl.BlockSpec((SLC_PAD, ncp), lambda b, hk, qi: (0, 0))],
        out_specs=pl.BlockSpec((QT, G * NSA_DIM), lambda b, hk, qi: (b * nq + qi, hk)),
        out_shape=jax.ShapeDtypeStruct((n, NSA_HEADS * NSA_DIM), BF16),
        scratch_shapes=[pltpu.VMEM((SLC_PAD, QT), F32),
                        pltpu.VMEM((VROWS, G * QT), F32),
                        pltpu.VMEM((1, G * QT), F32),
                        pltpu.VMEM((4 * QT, G * QT), F32),
                        pltpu.VMEM((4 * QT, G * QT), F32)],
        compiler_params=_params("parallel", "parallel", "arbitrary"),
        name="nsa_attention",
    )(zq_t, kc, vc_t, nk, nv_t, nk, nv_t, gates_t, pat, tsel, _overlap_t(seq))


MLA_TQ = 512


def _mla_kernel(q_ref, k_ref, v_ref, o_ref, acc_ref, m_ref, sa_ref, sb_ref):
    qi = pl.program_id(2)
    tq = MLA_TQ
    HP = 2
    qs = [q_ref[hh * LANE:(hh + 1) * LANE, :] for hh in range(HP)]
    m_ref[...] = jnp.full(m_ref.shape, NEG, F32)
    acc_ref[...] = jnp.zeros(acc_ref.shape, F32)

    def qk(dst_ref, off, hh):
        dst_ref[hh] = _dot(k_ref[pl.ds(off, tq), hh * LANE:(hh + 1) * LANE], qs[hh])

    def consume(src_ref, off, mask, hh):
        s = src_ref[hh]
        if mask is not None:
            s = s + mask
        vt = v_ref[hh * VROWS:(hh + 1) * VROWS, pl.ds(off, tq)]
        m_old = m_ref[hh]
        m_new = jnp.maximum(m_old, jnp.max(s, axis=0, keepdims=True))
        p = jnp.exp2(s - m_new).astype(BF16)
        acc_ref[hh] = acc_ref[hh] * jnp.exp2(m_old - m_new) + _dot(vt, p)
        m_ref[hh] = m_new

    def stage(nxt_ref, nxt_off, cur_ref, cur_off, mask):
        for hh in range(HP):
            if nxt_ref is not None:
                qk(nxt_ref, nxt_off, hh)
            consume(cur_ref, cur_off, mask, hh)

    key_j = lax.broadcasted_iota(jnp.int32, (tq, tq), 0)
    qry_i = lax.broadcasted_iota(jnp.int32, (tq, tq), 1)
    causal = jnp.where(key_j <= qry_i, 0.0, NEG)
    for hh in range(HP):
        qk(sa_ref, 0, hh)

    def pair(u, carry):
        off = pl.multiple_of(u * (2 * tq), 2 * tq)
        stage(sb_ref, off + tq, sa_ref, off, None)
        stage(sa_ref, off + 2 * tq, sb_ref, off + tq, None)
        return carry

    lax.fori_loop(0, qi // 2, pair, 0)
    base = pl.multiple_of((qi // 2) * (2 * tq), 2 * tq)

    @pl.when(qi % 2 == 0)
    def _():
        stage(None, None, sa_ref, base, causal)

    @pl.when(qi % 2 == 1)
    def _():
        stage(sb_ref, base + tq, sa_ref, base, None)
        stage(None, None, sb_ref, base + tq, causal)

    outs = []
    for hh in range(HP):
        a = acc_ref[hh]
        outs.append((a[:MLA_V] / a[MLA_V:MLA_V + 1]).T)
    o_ref[...] = jnp.concatenate(outs, axis=1).astype(BF16)


def _mla(qm_t, km, vm_t, *, batch, seq):
    n = km.shape[0]
    tq = MLA_TQ
    nq = seq // tq
    HP = 2
    return pl.pallas_call(
        _mla_kernel,
        grid=(batch, MLA_HEADS // HP, nq),
        in_specs=[pl.BlockSpec((HP * LANE, tq), lambda b, hp, qi: (hp, b * nq + qi)),
                  pl.BlockSpec((seq, HP * LANE), lambda b, hp, qi: (b, hp)),
                  pl.BlockSpec((HP * VROWS, seq), lambda b, hp, qi: (hp, b))],
        out_specs=pl.BlockSpec((tq, HP * MLA_V), lambda b, hp, qi: (b * nq + qi, hp)),
        out_shape=jax.ShapeDtypeStruct((n, MLA_HEADS * MLA_V), BF16),
        scratch_shapes=[pltpu.VMEM((HP, VROWS, tq), F32),
                        pltpu.VMEM((HP, 1, tq), F32),
                        pltpu.VMEM((HP, tq, tq), F32),
                        pltpu.VMEM((HP, tq, tq), F32)],
        compiler_params=_params("parallel", "parallel", "arbitrary"),
        name="mla_attention",
    )(qm_t, km, vm_t)


def _tail_kernel(x_ref, om_ref, on_ref, p_ref, wo_ref, g1_ref, g2_ref, g3_ref,
                 wg_ref, wu_ref, cw_ref, cb_ref, wd_ref, pg_ref, pp_ref,
                 o_ref, carry_ref, act_ref, *, seq, tm):
    n_chunks = D_FF // FF_CHUNK
    half = om_ref.shape[1]
    y = _dot(om_ref[...], wo_ref[:half, :]) + _dot(on_ref[...], wo_ref[half:, :])
    x = x_ref[...] + _rms(y, g1_ref[...])

    h = _rms(x, g2_ref[...]).astype(BF16)

    @pl.when((pl.program_id(0) * tm) % seq == 0)
    def _():
        carry_ref[...] = jnp.zeros(carry_ref.shape, F32)

    SUB = 8
    row8 = lax.broadcasted_iota(jnp.int32, (SUB, 1), 0)
    for c in range(n_chunks):
        cols = slice(c * FF_CHUNK, (c + 1) * FF_CHUNK)
        g = _dot(h, wg_ref[:, cols])
        prev = carry_ref[:, cols]
        carry_ref[:, cols] = g[tm - SUB:, :]
        r1 = pltpu.roll(g, 1, 0)
        r2 = pltpu.roll(g, 2, 0)
        top1 = jnp.where(row8 == 0, prev[7:8, :], r1[:SUB])
        top2 = jnp.where(row8 == 0, prev[6:7, :], jnp.where(row8 == 1, prev[7:8, :], r2[:SUB]))
        g1 = jnp.concatenate([top1, r1[SUB:]], axis=0)
        g2 = jnp.concatenate([top2, r2[SUB:]], axis=0)
        conv = (cw_ref[0:1, cols] * g2 + cw_ref[1:2, cols] * g1 + cw_ref[2:3, cols] * g
                + cb_ref[:, cols])
        act_ref[:, cols] = (_gelu_tanh(conv) * _dot(h, wu_ref[:, cols])).astype(BF16)
    x = x + _rms(_dot(act_ref[...], wd_ref[...]), g3_ref[...])

    gate = _sigmoid(_dot(x.astype(BF16), pg_ref[...]))
    o_ref[...] = x + gate * _dot(p_ref[...].astype(BF16), pp_ref[...])


def _tail(x2, om, on, p2, wo, g1, g2, g3, wg, wu, cw, cb, wd, pg, pp, *, seq):
    n = x2.shape[0]
    tm = 512
    row = lambda r: (r, 0)
    half = om.shape[1]
    return pl.pallas_call(
        functools.partial(_tail_kernel, seq=seq, tm=tm),
        grid=(n // tm,),
        in_specs=[pl.BlockSpec((tm, D_MODEL), row),
                  pl.BlockSpec((tm, half), row),
                  pl.BlockSpec((tm, half), row),
                  pl.BlockSpec((tm, PLE_DIM), row),
                  _resident((2 * half, D_MODEL)),
                  _resident((1, D_MODEL)),
                  _resident((1, D_MODEL)),
                  _resident((1, D_MODEL)),
                  _resident((D_MODEL, D_FF)),
                  _resident((D_MODEL, D_FF)),
                  _resident((CONV_WIDTH, D_FF)),
                  _resident((1, D_FF)),
                  _resident((D_FF, D_MODEL)),
                  _resident((D_MODEL, D_MODEL)),
                  _resident((PLE_DIM, D_MODEL))],
        out_specs=pl.BlockSpec((tm, D_MODEL), row),
        out_shape=jax.ShapeDtypeStruct((n, D_MODEL), F32),
        scratch_shapes=[pltpu.VMEM((8, D_FF), F32),
                        pltpu.VMEM((tm, D_FF), BF16)],
        compiler_params=_params("arbitrary"),
        name="layer_tail",
    )(x2, om, on, p2, wo, g1, g2, g3, wg, wu, cw, cb, wd, pg, pp)


def _rot_cols(w):
    half = w.shape[1] // 2
    return jnp.concatenate([-w[:, half:], w[:, :half]], axis=1)


def _prep_inproj(w):
    o = np.cumsum((0,) + IN_SPLITS)
    parts = [w[:, o[j]:o[j + 1]] for j in range(len(IN_SPLITS))]
    cq, ckv, kr, qn, kcmp, vcmp, kslc, vslc, kwin, vwin, gn = parts
    d = w.shape[0]
    z64 = jnp.zeros((d, NSA_DIM), F32)
    z32 = jnp.zeros((d, LANE - MLA_NOPE - MLA_ROPE), F32)

    def slots(m):
        return jnp.concatenate([m[:, :NSA_DIM], z64, m[:, NSA_DIM:], z64], axis=1)

    w1 = jnp.concatenate([cq, ckv,
                          z64, kr, z32,
                          z64, _rot_cols(kr), z32,
                          slots(kslc), slots(kwin), kcmp, vcmp], axis=1)
    assert w1.shape[1] == C_TOTAL

    def vslots(m):
        zp = jnp.zeros((VROWS - NSA_DIM, d), F32)
        return jnp.concatenate([m[:, :NSA_DIM].T, zp, m[:, NSA_DIM:].T, zp], axis=0)

    per = 3 * NSA_GROUP
    gp = jnp.zeros((GATE_ROWS - per, d), F32)
    w1t = jnp.concatenate([qn.T * NSA_DIM ** -0.5, vslots(vslc), vslots(vwin),
                           gn[:, :per].T, gp, gn[:, per:].T, gp], axis=0)
    assert w1t.shape[0] == R_TOTAL
    return w1.astype(BF16), w1t.astype(BF16)


def _prep_mla(w_uq, w_ukv):
    r = w_uq.shape[0]
    dq = MLA_NOPE + MLA_ROPE
    z32 = jnp.zeros((r, LANE - dq), F32)
    z64q = jnp.zeros((r, MLA_NOPE), F32)
    qa, qb = [], []
    for h in range(MLA_HEADS):
        blk = w_uq[:, h * dq:(h + 1) * dq]
        qa += [blk, z32]
        qb += [z64q, _rot_cols(blk[:, MLA_NOPE:]), z32]
    rk = w_ukv.shape[0]
    z64 = jnp.zeros((rk, LANE - MLA_NOPE), F32)
    zv = jnp.zeros((rk, VROWS - MLA_V), F32)
    wk, wv = [], []
    dkv = MLA_NOPE + MLA_V
    for h in range(MLA_HEADS):
        blk = w_ukv[:, h * dkv:(h + 1) * dkv]
        wk += [blk[:, :MLA_NOPE], z64]
        wv += [blk[:, MLA_NOPE:], zv]
    cat = lambda xs: jnp.concatenate(xs, axis=1).astype(BF16)
    return cat(qa).T, cat(qb).T, cat(wk), cat(wv).T


def kernel(x, p, positions, rel_bias, attn_pre_norm, attn_post_norm, ffn_pre_norm, ffn_post_norm,
           w_in, mla_q_norm, mla_w_uq, mla_kv_norm, mla_w_ukv, nsa_cmp_pos, nsa_cmp_w1, nsa_cmp_w2,
           w_o, ffn_w_gate, ffn_w_up, ffn_conv_w, ffn_conv_b, ffn_w_down, ple_proj, ple_gate):
    batch, seq, d = x.shape
    depth = w_in.shape[0]
    n = batch * seq
    x2 = x.reshape(n, d)
    ctt, stt, ct, st = _rope_tables(positions)
    pat, tsel = _bias_tables(rel_bias, seq)
    for i in range(depth):
        w1, w1t = _prep_inproj(w_in[i])
        wqa, wqb, wk, wvt = _prep_mla(mla_w_uq[i], mla_w_ukv[i])
        qm_t, km, vm_t, zq_t, nk, nv_t, kcmp, vcmp, gates_t = _inproj(
            x2, attn_pre_norm[i][None, :], w1, w1t, mla_q_norm[i][None, :], wqa, wqb,
            mla_kv_norm[i][None, :], wk, wvt, ct, st, ctt, stt, batch=batch, seq=seq)
        kc, vc_t = _compress(kcmp, vcmp,
                             nsa_cmp_pos[i].reshape(2, 1, CMP_LEN * NSA_DIM),
                             nsa_cmp_w1[i].astype(BF16), nsa_cmp_w2[i, 0].astype(BF16),
                             nsa_cmp_w2[i, 1].T.astype(BF16), batch=batch, seq=seq)
        o_nsa = _nsa(zq_t, kc, vc_t, nk, nv_t, gates_t, pat, tsel, batch=batch, seq=seq)
        o_mla = _mla(qm_t, km, vm_t, batch=batch, seq=seq)
        wg = ffn_w_gate[i].astype(BF16)
        wu = ffn_w_up[i].astype(BF16)
        cw = ffn_conv_w[i]
        cb = ffn_conv_b[i][None, :]
        wd = ffn_w_down[i].astype(BF16)
        x2 = _tail(x2, o_mla, o_nsa, p[i].reshape(n, PLE_DIM), w_o[i].astype(BF16),
                   attn_post_norm[i][None, :], ffn_pre_norm[i][None, :], ffn_post_norm[i][None, :],
                   wg, wu, cw, cb, wd, ple_gate[i].astype(BF16), ple_proj[i].astype(BF16), seq=seq)
    return x2.reshape(batch, seq, d)
```

```python
import functools
import math

import numpy as np
import jax
import jax.numpy as jnp
from jax import lax
from jax.experimental import pallas as pl
from jax.experimental.pallas import tpu as pltpu

F32 = jnp.float32
BF16 = jnp.bfloat16

D_MODEL = 1024
DEPTH = 2
MLA_HEADS = 8
MLA_NOPE = 64
MLA_ROPE = 32
MLA_V = 64
MLA_Q_RANK = 256
MLA_KV_RANK = 128
ROPE_BASE = 10000.0
NSA_HEADS = 8
NSA_KV_HEADS = 2
NSA_GROUP = NSA_HEADS // NSA_KV_HEADS
NSA_DIM = 64
CMP_LEN = 32
CMP_STRIDE = 16
CMP_HIDDEN = 128
SLC_LEN = 64
SLC_TOPK = 16
WINDOW = 512
REL_BUCKETS = 32
REL_MAX_DIST = 128
D_FF = 2816
CONV_WIDTH = 3
PLE_DIM = 256
EPS = 1e-6
NEG = -1e30
POS_BIG = 1e30
LOG2E = math.log2(math.e)

IN_SPLITS = (MLA_Q_RANK, MLA_KV_RANK, MLA_ROPE, NSA_HEADS * NSA_DIM,
             NSA_KV_HEADS * NSA_DIM, NSA_KV_HEADS * NSA_DIM,
             NSA_KV_HEADS * NSA_DIM, NSA_KV_HEADS * NSA_DIM,
             NSA_KV_HEADS * NSA_DIM, NSA_KV_HEADS * NSA_DIM,
             3 * NSA_HEADS)

LANE = 128
QT = 256
KP = 128
SLC_PAD = 64
FF_CHUNK = 256
VMEM_LIMIT = 56 * 1024 * 1024

VROWS = 80

C_CQ = 0
C_CKV = 256
C_KR = 384
C_KRROT = 512
C_KSLC = 640
C_KWIN = 896
C_KCMP = 1152
C_VCMP = 1280
C_TOTAL = 1408
R_QN = 0
R_VSLC = 512
R_VWIN = R_VSLC + NSA_KV_HEADS * VROWS
R_GATE = R_VWIN + NSA_KV_HEADS * VROWS
GATE_ROWS = 16
R_TOTAL = R_GATE + NSA_KV_HEADS * GATE_ROWS


def _dot(a, b):
    return jnp.dot(a, b, preferred_element_type=F32)


def _dot_nt(a, b):
    return lax.dot_general(a, b, (((1,), (1,)), ((), ())), preferred_element_type=F32)


def _rms(x, g):
    return x * lax.rsqrt(jnp.mean(x * x, axis=-1, keepdims=True) + EPS) * g


def _gelu_tanh(x):
    return 0.5 * x * (1.0 + jnp.tanh(math.sqrt(2.0 / math.pi) * (x + 0.044715 * (x * x * x))))


def _sigmoid(x):
    return 1.0 / (1.0 + jnp.exp(-x))


def _params(*sem):
    return pltpu.CompilerParams(dimension_semantics=sem, vmem_limit_bytes=VMEM_LIMIT)


def _resident(shape):
    nd = len(shape)
    return pl.BlockSpec(shape, lambda *_: (0,) * nd, pipeline_mode=pl.Buffered(1))


def _rope_kernel(pos_ref, inv_ref, ctt_ref, stt_ref, ct_ref, st_ref):
    ang = inv_ref[...] * pos_ref[...].astype(F32)
    row = lax.broadcasted_iota(jnp.int32, ang.shape, 0)
    rope = (row >= MLA_NOPE) & (row < MLA_NOPE + MLA_ROPE)
    ct = jnp.where(rope, jnp.cos(ang), jnp.where(row < MLA_NOPE, 1.0, 0.0))
    st = jnp.where(rope, jnp.sin(ang), 0.0)
    ctt_ref[...] = ct
    stt_ref[...] = st
    ct_ref[...] = ct.T
    st_ref[...] = st.T


def _rope_tables(positions):
    n = positions.size
    tn = 2048
    half = MLA_ROPE // 2
    inv = ROPE_BASE ** (-jnp.arange(half, dtype=F32) / half)
    inv_slot = jnp.concatenate([jnp.zeros((MLA_NOPE,), F32), inv, inv,
                                jnp.zeros((LANE - MLA_NOPE - MLA_ROPE,), F32)])[:, None]
    return pl.pallas_call(
        _rope_kernel,
        grid=(n // tn,),
        in_specs=[pl.BlockSpec((1, tn), lambda r: (0, r)),
                  pl.BlockSpec((LANE, 1), lambda r: (0, 0))],
        out_specs=[pl.BlockSpec((LANE, tn), lambda r: (0, r))] * 2
                  + [pl.BlockSpec((tn, LANE), lambda r: (r, 0))] * 2,
        out_shape=[jax.ShapeDtypeStruct((LANE, n), F32)] * 2 + [jax.ShapeDtypeStruct((n, LANE), F32)] * 2,
        compiler_params=_params("parallel"),
        name="rope_tables",
    )(positions.reshape(1, n), inv_slot)


def _bucket_np(dist):
    n = np.maximum(dist, 0)
    max_exact = REL_BUCKETS // 2
    large = max_exact + (np.log(np.maximum(n, 1).astype(np.float32) / max_exact)
                         / math.log(REL_MAX_DIST / max_exact)
                         * (REL_BUCKETS - max_exact)).astype(np.int32)
    large = np.minimum(large, REL_BUCKETS - 1)
    return np.where(n < max_exact, n, large).astype(np.int32)


def _bias_kernel(table_ref, bpc_ref, bpu_ref, pat_ref, u_ref):
    h = pl.program_id(0)
    far = table_ref[REL_BUCKETS - 1, h]

    def lookup(bp, sub):
        acc = jnp.full(bp.shape, far - sub, F32)
        for b in range(REL_BUCKETS - 1):
            acc = jnp.where(bp == b, table_ref[b, h] - sub, acc)
        return jnp.where(bp < 0, NEG, acc * LOG2E)

    pat_ref[0] = lookup(bpc_ref[...], 0.0)
    for d in range(3):
        u_ref[0, d] = lookup(bpu_ref[d], far)


def _bias_tables(rel_bias, seq):
    ncp = seq // CMP_STRIDE
    i = np.arange(QT)[None, :]
    cprime = np.arange(2 * ncp)[:, None] - ncp
    dist_c = i - CMP_STRIDE * cprime - (CMP_LEN - 1)
    bpc = np.where(dist_c >= 0, _bucket_np(dist_c), -1).astype(np.int32)
    j = np.arange(KP)[:, None]
    bpu = []
    for delta in (-1, 0, 1):
        dist = i - KP * delta - j
        bpu.append(np.where(dist >= 0, _bucket_np(dist), -1))
    bpu = np.stack(bpu).astype(np.int32)
    return pl.pallas_call(
        _bias_kernel,
        grid=(NSA_HEADS,),
        in_specs=[pl.BlockSpec(memory_space=pltpu.SMEM),
                  pl.BlockSpec((2 * ncp, QT), lambda h: (0, 0)),
                  pl.BlockSpec((3, KP, QT), lambda h: (0, 0, 0))],
        out_specs=[pl.BlockSpec((1, 2 * ncp, QT), lambda h: (h, 0, 0)),
                   pl.BlockSpec((1, 3, KP, QT), lambda h: (h // NSA_GROUP, 0, 0, h % NSA_GROUP))],
        out_shape=[jax.ShapeDtypeStruct((NSA_HEADS, 2 * ncp, QT), F32),
                   jax.ShapeDtypeStruct((NSA_KV_HEADS, 3, KP, NSA_GROUP * QT), F32)],
        compiler_params=_params("parallel"),
        name="bias_tables",
    )(rel_bias.astype(F32), jnp.asarray(bpc), jnp.asarray(bpu))


def _inproj_kernel(x_ref, g_ref, w1_ref, w1t_ref, qn_ref, wqa_ref, wqb_ref, kvn_ref, wk_ref, wvt_ref,
                   ct_ref, st_ref, ctt_ref, stt_ref, ones_n_ref, ones_m_ref,
                   qm_ref, km_ref, vm_ref, zq_ref, nk_ref, nv_ref, kcmp_ref, vcmp_ref, gate_ref,
                   *, seq, tm):
    h = _rms(x_ref[...], g_ref[...]).astype(BF16)

    def proj(c0, width):
        return _dot(h, w1_ref[:, c0:c0 + width])

    cq = _rms(proj(C_CQ, MLA_Q_RANK), qn_ref[...]).astype(BF16)
    qa = _dot_nt(wqa_ref[...], cq)
    qb = _dot_nt(wqb_ref[...], cq)
    scale = (MLA_NOPE + MLA_ROPE) ** -0.5 * LOG2E
    cts = ctt_ref[...] * scale
    sts = stt_ref[...] * scale
    for hh in range(MLA_HEADS):
        sl = slice(hh * LANE, (hh + 1) * LANE)
        qm_ref[sl, :] = (qa[sl] * cts + qb[sl] * sts).astype(BF16)

    ckv = _rms(proj(C_CKV, MLA_KV_RANK), kvn_ref[...]).astype(BF16)
    kr = proj(C_KR, LANE) * ct_ref[...] + proj(C_KRROT, LANE) * st_ref[...]
    kn = _dot(ckv, wk_ref[...])
    for hh in range(MLA_HEADS):
        sl = slice(hh * LANE, (hh + 1) * LANE)
        km_ref[:, sl] = (kn[:, sl] + kr).astype(BF16)
    vm_ref[...] = (_dot_nt(wvt_ref[...], ckv) + ones_m_ref[...]).astype(BF16)

    zt = _dot_nt(w1t_ref[...], h)
    zq_ref[...] = (zt[R_QN:R_VSLC] * LOG2E).astype(BF16)
    nv_ref[...] = (zt[R_VSLC:R_GATE] + ones_n_ref[...]).astype(BF16)
    gate_ref[...] = _sigmoid(zt[R_GATE:R_TOTAL])

    lane = lax.broadcasted_iota(jnp.int32, (tm, LANE), 1)
    s0 = (pl.program_id(0) * tm) % seq
    row = lax.broadcasted_iota(jnp.int32, (tm, LANE), 0)
    onehot = (lane - NSA_DIM == (s0 + row) // SLC_LEN).astype(F32)
    ksl = proj(C_KSLC, 2 * LANE)
    for hk in range(NSA_KV_HEADS):
        sl = slice(hk * LANE, (hk + 1) * LANE)
        nk_ref[:, sl] = (ksl[:, sl] + onehot).astype(BF16)
    nk_ref[:, 2 * LANE:] = proj(C_KWIN, 2 * LANE).astype(BF16)

    kcmp = proj(C_KCMP, LANE)
    vcmp = proj(C_VCMP, LANE)
    for hk in range(NSA_KV_HEADS):
        kcmp_ref[0, hk] = kcmp[:, hk * NSA_DIM:(hk + 1) * NSA_DIM]
        vcmp_ref[0, hk] = vcmp[:, hk * NSA_DIM:(hk + 1) * NSA_DIM]


def _ones_rows(n_slots):
    r = np.arange(n_slots * VROWS) % VROWS
    return jnp.asarray((r >= NSA_DIM).astype(np.float32)[:, None])


def _inproj(x2, g_pre, w1, w1t, qn, wqa, wqb, kvn, wk, wvt, ct, st, ctt, stt, *, batch, seq):
    n = x2.shape[0]
    tm = 512
    tiles_per_seq = seq // tm
    row = lambda r: (r, 0)
    col = lambda r: (0, r)
    hm = lambda r: (r // tiles_per_seq, 0, r % tiles_per_seq, 0)
    nv_rows = 2 * NSA_KV_HEADS * VROWS
    vm_rows = MLA_HEADS * VROWS
    return pl.pallas_call(
        functools.partial(_inproj_kernel, seq=seq, tm=tm),
        grid=(n // tm,),
        in_specs=[pl.BlockSpec((tm, D_MODEL), row),
                  _resident((1, D_MODEL)),
                  _resident((D_MODEL, C_TOTAL)),
                  _resident((R_TOTAL, D_MODEL)),
                  _resident((1, MLA_Q_RANK)),
                  _resident((MLA_HEADS * LANE, MLA_Q_RANK)),
                  _resident((MLA_HEADS * LANE, MLA_Q_RANK)),
                  _resident((1, MLA_KV_RANK)),
                  _resident((MLA_KV_RANK, MLA_HEADS * LANE)),
                  _resident((vm_rows, MLA_KV_RANK)),
                  pl.BlockSpec((tm, LANE), row),
                  pl.BlockSpec((tm, LANE), row),
                  pl.BlockSpec((LANE, tm), col),
                  pl.BlockSpec((LANE, tm), col),
                  _resident((nv_rows, 1)),
                  _resident((vm_rows, 1))],
        out_specs=[pl.BlockSpec((MLA_HEADS * LANE, tm), col),
                   pl.BlockSpec((tm, MLA_HEADS * LANE), row),
                   pl.BlockSpec((vm_rows, tm), col),
                   pl.BlockSpec((NSA_HEADS * NSA_DIM, tm), col),
                   pl.BlockSpec((tm, 4 * LANE), row),
                   pl.BlockSpec((nv_rows, tm), col),
                   pl.BlockSpec((1, NSA_KV_HEADS, tm, NSA_DIM), hm),
                   pl.BlockSpec((1, NSA_KV_HEADS, tm, NSA_DIM), hm),
                   pl.BlockSpec((NSA_KV_HEADS * GATE_ROWS, tm), col)],
        out_shape=[jax.ShapeDtypeStruct((MLA_HEADS * LANE, n), BF16),
                   jax.ShapeDtypeStruct((n, MLA_HEADS * LANE), BF16),
                   jax.ShapeDtypeStruct((vm_rows, n), BF16),
                   jax.ShapeDtypeStruct((NSA_HEADS * NSA_DIM, n), BF16),
                   jax.ShapeDtypeStruct((n, 4 * LANE), BF16),
                   jax.ShapeDtypeStruct((nv_rows, n), BF16),
                   jax.ShapeDtypeStruct((batch, NSA_KV_HEADS, seq, NSA_DIM), F32),
                   jax.ShapeDtypeStruct((batch, NSA_KV_HEADS, seq, NSA_DIM), F32),
                   jax.ShapeDtypeStruct((NSA_KV_HEADS * GATE_ROWS, n), F32)],
        compiler_params=_params("parallel"),
        name="in_proj",
    )(x2, g_pre, w1, w1t, qn, wqa, wqb, kvn, wk, wvt, ct, st, ctt, stt,
      _ones_rows(2 * NSA_KV_HEADS), _ones_rows(MLA_HEADS))


def _compress_kernel(k16_ref, v16_ref, pos_ref, w1_ref, w2k_ref, w2vt_ref, kc_ref, vc_ref):
    half = CMP_STRIDE * NSA_DIM

    def hidden(x16, j):
        pos = pos_ref[j]
        top = (x16 + pos[:, :half]).astype(BF16)
        bot = (x16 + pos[:, half:]).astype(BF16)
        u = _dot(top, w1_ref[j, :half, :])
        low = _dot(bot, w1_ref[j, half:, :])
        nxt = jnp.concatenate([low[1:], jnp.zeros((1, CMP_HIDDEN), F32)], axis=0)
        return _gelu_tanh(u + nxt).astype(BF16)

    kc_ref[0, 0] = _dot(hidden(k16_ref[0, 0], 0), w2k_ref[...]).astype(BF16)
    vc_ref[0, 0] = _dot_nt(w2vt_ref[...], hidden(v16_ref[0, 0], 1)).astype(BF16)


def _compress(kcmp, vcmp, pos, w1, w2k, w2vt, *, batch, seq):
    ncp = seq // CMP_STRIDE
    half = CMP_STRIDE * NSA_DIM
    k16 = kcmp.reshape(batch, NSA_KV_HEADS, ncp, half)
    v16 = vcmp.reshape(batch, NSA_KV_HEADS, ncp, half)
    blk = lambda b, hk: (b, hk, 0, 0)
    return pl.pallas_call(
        _compress_kernel,
        grid=(batch, NSA_KV_HEADS),
        in_specs=[pl.BlockSpec((1, 1, ncp, half), blk),
                  pl.BlockSpec((1, 1, ncp, half), blk),
                  pl.BlockSpec((2, 1, 2 * half), lambda b, hk: (0, 0, 0)),
                  pl.BlockSpec((2, 2 * half, CMP_HIDDEN), lambda b, hk: (0, 0, 0)),
                  pl.BlockSpec((CMP_HIDDEN, NSA_DIM), lambda b, hk: (0, 0)),
                  pl.BlockSpec((NSA_DIM, CMP_HIDDEN), lambda b, hk: (0, 0))],
        out_specs=[pl.BlockSpec((1, 1, ncp, NSA_DIM), blk),
                   pl.BlockSpec((1, 1, NSA_DIM, ncp), blk)],
        out_shape=[jax.ShapeDtypeStruct((batch, NSA_KV_HEADS, ncp, NSA_DIM), BF16),
                   jax.ShapeDtypeStruct((batch, NSA_KV_HEADS, NSA_DIM, ncp), BF16)],
        compiler_params=_params("parallel", "parallel"),
        name="nsa_compress",
    )(k16, v16, pos, w1, w2k, w2vt)


def _nsa_kernel(zq_ref, kc_ref, vc_ref, ksl_ref, vsl_ref, kw_ref, vw_ref, gate_ref,
                pat_ref, u_ref, ovt_ref, o_ref, imp_ref, acc_ref, m_ref, sa_ref, sb_ref):
    qi = pl.program_id(2)
    G = NSA_GROUP
    M = G * QT
    PPT = QT // KP
    q4 = zq_ref[...]
    qs = jnp.concatenate([q4[g * NSA_DIM:(g + 1) * NSA_DIM, :] for g in range(G)], axis=1)
    kc = kc_ref[0, 0]
    vc_t = vc_ref[0, 0]
    ovt = ovt_ref[...]
    ncp = kc.shape[0]
    qcol = qi * QT + (lax.broadcasted_iota(jnp.int32, (1, M), 1) & (QT - 1))
    u_m1 = u_ref[0, 0]
    u_0 = u_ref[0, 1]
    u_p1 = u_ref[0, 2]

    n_back = WINDOW // KP
    n_w = n_back + PPT
    qw = jnp.concatenate([qs, jnp.zeros_like(qs)], axis=0)
    pad_v = jnp.where(lax.broadcasted_iota(jnp.int32, (VROWS, KP), 0) >= NSA_DIM, 1.0, 0.0).astype(BF16)
    k_rows, v_cols = [], []
    for w in range(n_w):
        piece = PPT * qi - n_back + w
        off = pl.multiple_of(jnp.maximum(piece, 0) * KP, KP)
        kt = kw_ref[pl.ds(off, KP), :]
        vt = vw_ref[:, pl.ds(off, KP)]
        if w < n_back:
            kt = jnp.where(piece >= 0, kt, jnp.zeros_like(kt))
            vt = jnp.where(piece >= 0, vt, pad_v)
        k_rows.append(kt)
        v_cols.append(vt)
    s_w = _dot(jnp.concatenate(k_rows, axis=0), qw)

    slide = QT // CMP_STRIDE
    off_b = pl.multiple_of(ncp - slide * qi, slide)
    bias = jnp.concatenate([pat_ref[g, pl.ds(off_b, ncp), :] for g in range(G)], axis=1)
    s = _dot(kc, qs) + bias
    e = jnp.exp2(s - jnp.max(s, axis=0, keepdims=True))
    scale = jnp.where(qcol >= CMP_LEN - 1, 1.0 / jnp.sum(e, axis=0, keepdims=True), 0.0)
    p_cmp = (e * scale).astype(BF16)

    key_j = lax.broadcasted_iota(jnp.int32, (KP, M), 0)
    qry_i = lax.broadcasted_iota(jnp.int32, (KP, M), 1) & (QT - 1)
    pieces = [s_w[w * KP:(w + 1) * KP] for w in range(n_w)]
    for w in range(PPT):
        pieces[w] = pieces[w] + jnp.where(key_j > qry_i - w * KP, 0.0, NEG)
    pieces[n_back - 1] = pieces[n_back - 1] + u_m1
    pieces[n_back] = pieces[n_back] + u_0
    pieces[n_back + 1] = pieces[n_back + 1] + u_p1
    s_w = jnp.concatenate(pieces, axis=0)
    p_w = jnp.exp2(s_w - jnp.max(s_w, axis=0, keepdims=True)).astype(BF16)

    o_cmp = _dot(vc_t, p_cmp)
    imp_t = jnp.zeros((SLC_PAD, QT), F32)
    for g in range(G):
        imp_t = imp_t + _dot(ovt, p_cmp[:, g * QT:(g + 1) * QT])

    a_w = _dot(jnp.concatenate(v_cols, axis=1), p_w)
    o_win = a_w[:NSA_DIM] / a_w[NSA_DIM:NSA_DIM + 1]

    n_id = lax.broadcasted_iota(jnp.int32, (SLC_PAD, QT), 0)
    q_blk = (qi * QT + lax.broadcasted_iota(jnp.int32, (SLC_PAD, QT), 1)) // SLC_LEN
    forced = (n_id == 0) | (n_id == q_blk) | (n_id == q_blk - 1)
    imp = jnp.where(forced, POS_BIG, jnp.where(n_id > q_blk, NEG, imp_t))
    imp_ref[...] = imp
    SUB = 8
    slabs = [imp[v * SUB:(v + 1) * SUB] for v in range(SLC_PAD // SUB)]
    ranks = [jnp.zeros((SUB, QT), jnp.int32) for _ in slabs]
    sub_id = lax.broadcasted_iota(jnp.int32, (SUB, QT), 0)
    for m in range(SLC_PAD):
        other = imp_ref[m:m + 1, :]
        for v, slab in enumerate(slabs):
            if v > m // SUB:
                beats = (other >= slab).astype(jnp.int32)
            elif v < m // SUB:
                beats = (other > slab).astype(jnp.int32)
            else:
                beats = jnp.where(sub_id > m % SUB, (other >= slab).astype(jnp.int32),
                                  (other > slab).astype(jnp.int32))
            ranks[v] = ranks[v] + beats
    rank = jnp.concatenate(ranks, axis=0)
    selb = jnp.where(rank < SLC_TOPK, 0.0, NEG).astype(BF16)
    qaug = jnp.concatenate([qs, jnp.concatenate([selb] * G, axis=1)], axis=0)

    PIECES = 4
    TK = PIECES * KP
    n_t = (PPT * qi + PPT - 1) // PIECES + 1
    at_start = (PPT * qi) % PIECES == 0
    last_bias = [jnp.where(at_start, u_0, 0.0), jnp.where(at_start, u_p1, u_m1),
                 jnp.where(at_start, NEG, u_0), jnp.where(at_start, NEG, u_p1)]
    prev_bias = [None, None, None, jnp.where(at_start, u_m1, 0.0)]
    m_ref[...] = jnp.full(m_ref.shape, NEG, F32)
    acc_ref[...] = jnp.zeros(acc_ref.shape, F32)

    def qk(dst_ref, t):
        off = pl.multiple_of(t * TK, TK)
        dst_ref[...] = _dot(ksl_ref[pl.ds(off, TK), :], qaug)

    def consume(src_ref, t, bias):
        off = pl.multiple_of(t * TK, TK)
        s = src_ref[...]
        if bias is not None:
            s = jnp.concatenate([s[j * KP:(j + 1) * KP] if bias[j] is None else s[j * KP:(j + 1) * KP] + bias[j]
                                 for j in range(PIECES)], axis=0)
        m_old = m_ref[...]
        m_new = jnp.maximum(m_old, jnp.max(s, axis=0, keepdims=True))
        p = jnp.exp2(s - m_new).astype(BF16)
        acc_ref[...] = acc_ref[...] * jnp.exp2(m_old - m_new) + _dot(vsl_ref[:, pl.ds(off, TK)], p)
        m_ref[...] = m_new

    n_plain = jnp.maximum(n_t - 2, 0)
    qk(sa_ref, 0)

    def pair(u, carry):
        t = 2 * u
        qk(sb_ref, t + 1)
        consume(sa_ref, t, None)
        qk(sa_ref, t + 2)
        consume(sb_ref, t + 1, None)
        return carry

    lax.fori_loop(0, n_plain // 2, pair, 0)
    tb = (n_plain // 2) * 2

    @pl.when(n_t == 1)
    def _():
        consume(sa_ref, 0, last_bias)

    @pl.when((n_t >= 2) & (n_plain % 2 == 0))
    def _():
        qk(sb_ref, tb + 1)
        consume(sa_ref, tb, prev_bias)
        consume(sb_ref, tb + 1, last_bias)

    @pl.when(n_plain % 2 == 1)
    def _():
        qk(sb_ref, tb + 1)
        consume(sa_ref, tb, None)
        qk(sa_ref, tb + 2)
        consume(sb_ref, tb + 1, prev_bias)
        consume(sa_ref, tb + 2, last_bias)

    a_s = acc_ref[...]
    o_sel = a_s[:NSA_DIM] / a_s[NSA_DIM:NSA_DIM + 1]

    gate = gate_ref[...]
    outs = []
    for g in range(G):
        cols = slice(g * QT, (g + 1) * QT)
        o_t = (gate[3 * g:3 * g + 1, :] * o_cmp[:, cols] + gate[3 * g + 1:3 * g + 2, :] * o_sel[:, cols]
               + gate[3 * g + 2:3 * g + 3, :] * o_win[:, cols])
        outs.append(o_t.T)
    o_ref[...] = jnp.concatenate(outs, axis=1).astype(BF16)


def _overlap_t(seq):
    ncp = seq // CMP_STRIDE
    n_cmp = (seq - CMP_LEN) // CMP_STRIDE + 1
    n_slc = seq // SLC_LEN
    cs = np.arange(ncp)[None, :] * CMP_STRIDE
    ss = np.arange(SLC_PAD)[:, None] * SLC_LEN
    ov = np.maximum(np.minimum(cs + CMP_LEN, ss + SLC_LEN) - np.maximum(cs, ss), 0).astype(np.float32) / CMP_LEN
    ov = ov * (np.arange(ncp)[None, :] < n_cmp) * (np.arange(SLC_PAD)[:, None] < n_slc)
    return jnp.asarray(ov, BF16)


def _nsa(zq_t, kc, vc_t, nk, nv_t, gates_t, pat, utab, *, batch, seq):
    n = nk.shape[0]
    nq = seq // QT
    ncp = seq // CMP_STRIDE
    G = NSA_GROUP
    assert seq // SLC_LEN <= SLC_PAD and seq // SLC_LEN >= SLC_TOPK and seq >= WINDOW
    qcol = lambda b, hk, qi: (hk, b * nq + qi)
    return pl.pallas_call(
        _nsa_kernel,
        grid=(batch, NSA_KV_HEADS, nq),
        in_specs=[pl.BlockSpec((G * NSA_DIM, QT), qcol),
                  pl.BlockSpec((1, 1, ncp, NSA_DIM), lambda b, hk, qi: (b, hk, 0, 0)),
                  pl.BlockSpec((1, 1, NSA_DIM, ncp), lambda b, hk, qi: (b, hk, 0, 0)),
                  pl.BlockSpec((seq, LANE), lambda b, hk, qi: (b, hk)),
                  pl.BlockSpec((VROWS, seq), lambda b, hk, qi: (hk, b)),
                  pl.BlockSpec((seq, LANE), lambda b, hk, qi: (b, NSA_KV_HEADS + hk)),
                  pl.BlockSpec((VROWS, seq), lambda b, hk, qi: (NSA_KV_HEADS + hk, b)),
                  pl.BlockSpec((GATE_ROWS, QT), qcol),
                  pl.BlockSpec((G, 2 * ncp, QT), lambda b, hk, qi: (hk, 0, 0)),
                  pl.BlockSpec((1, 3, KP, G * QT), lambda b, hk, qi: (hk, 0, 0, 0)),
                  pl.BlockSpec((SLC_PAD, ncp), lambda b, hk, qi: (0, 0))],
        out_specs=pl.BlockSpec((QT, G * NSA_DIM), lambda b, hk, qi: (b * nq + qi, hk)),
        out_shape=jax.ShapeDtypeStruct((n, NSA_HEADS * NSA_DIM), BF16),
        scratch_shapes=[pltpu.VMEM((SLC_PAD, QT), F32),
                        pltpu.VMEM((VROWS, G * QT), F32),
                        pltpu.VMEM((1, G * QT), F32),
                        pltpu.VMEM((4 * KP, G * QT), F32),
                        pltpu.VMEM((4 * KP, G * QT), F32)],
        compiler_params=_params("parallel", "parallel", "arbitrary"),
        name="nsa_attention",
    )(zq_t, kc, vc_t, nk, nv_t, nk, nv_t, gates_t, pat, utab, _overlap_t(seq))


MLA_TQ = 512


def _mla_kernel(q_ref, k_ref, v_ref, o_ref, acc_ref, m_ref, sa_ref, sb_ref):
    qi = pl.program_id(2)
    tq = MLA_TQ
    HP = 2
    qs = [q_ref[hh * LANE:(hh + 1) * LANE, :] for hh in range(HP)]
    m_ref[...] = jnp.full(m_ref.shape, NEG, F32)
    acc_ref[...] = jnp.zeros(acc_ref.shape, F32)

    def qk(dst_ref, off, hh):
        dst_ref[hh] = _dot(k_ref[pl.ds(off, tq), hh * LANE:(hh + 1) * LANE], qs[hh])

    def consume(src_ref, off, mask, hh):
        s = src_ref[hh]
        if mask is not None:
            s = s + mask
        vt = v_ref[hh * VROWS:(hh + 1) * VROWS, pl.ds(off, tq)]
        m_old = m_ref[hh]
        m_new = jnp.maximum(m_old, jnp.max(s, axis=0, keepdims=True))
        p = jnp.exp2(s - m_new).astype(BF16)
        acc_ref[hh] = acc_ref[hh] * jnp.exp2(m_old - m_new) + _dot(vt, p)
        m_ref[hh] = m_new

    def stage(nxt_ref, nxt_off, cur_ref, cur_off, mask):
        for hh in range(HP):
            if nxt_ref is not None:
                qk(nxt_ref, nxt_off, hh)
            consume(cur_ref, cur_off, mask, hh)

    key_j = lax.broadcasted_iota(jnp.int32, (tq, tq), 0)
    qry_i = lax.broadcasted_iota(jnp.int32, (tq, tq), 1)
    causal = jnp.where(key_j <= qry_i, 0.0, NEG)
    for hh in range(HP):
        qk(sa_ref, 0, hh)

    def pair(u, carry):
        off = pl.multiple_of(u * (2 * tq), 2 * tq)
        stage(sb_ref, off + tq, sa_ref, off, None)
        stage(sa_ref, off + 2 * tq, sb_ref, off + tq, None)
        return carry

    lax.fori_loop(0, qi // 2, pair, 0)
    base = pl.multiple_of((qi // 2) * (2 * tq), 2 * tq)

    @pl.when(qi % 2 == 0)
    def _():
        stage(None, None, sa_ref, base, causal)

    @pl.when(qi % 2 == 1)
    def _():
        stage(sb_ref, base + tq, sa_ref, base, None)
        stage(None, None, sb_ref, base + tq, causal)

    outs = []
    for hh in range(HP):
        a = acc_ref[hh]
        outs.append((a[:MLA_V] / a[MLA_V:MLA_V + 1]).T)
    o_ref[...] = jnp.concatenate(outs, axis=1).astype(BF16)


def _mla(qm_t, km, vm_t, *, batch, seq):
    n = km.shape[0]
    tq = MLA_TQ
    nq = seq // tq
    HP = 2
    return pl.pallas_call(
        _mla_kernel,
        grid=(batch, MLA_HEADS // HP, nq),
        in_specs=[pl.BlockSpec((HP * LANE, tq), lambda b, hp, qi: (hp, b * nq + qi)),
                  pl.BlockSpec((seq, HP * LANE), lambda b, hp, qi: (b, hp)),
                  pl.BlockSpec((HP * VROWS, seq), lambda b, hp, qi: (hp, b))],
        out_specs=pl.BlockSpec((tq, HP * MLA_V), lambda b, hp, qi: (b * nq + qi, hp)),
        out_shape=jax.ShapeDtypeStruct((n, MLA_HEADS * MLA_V), BF16),
        scratch_shapes=[pltpu.VMEM((HP, VROWS, tq), F32),
                        pltpu.VMEM((HP, 1, tq), F32),
                        pltpu.VMEM((HP, tq, tq), F32),
                        pltpu.VMEM((HP, tq, tq), F32)],
        compiler_params=_params("parallel", "parallel", "arbitrary"),
        name="mla_attention",
    )(qm_t, km, vm_t)


def _tail_kernel(x_ref, om_ref, on_ref, p_ref, wo_ref, g1_ref, g2_ref, g3_ref,
                 wg_ref, wu_ref, cw_ref, cb_ref, wd_ref, pg_ref, pp_ref,
                 o_ref, carry_ref, act_ref, *, seq, tm):
    n_chunks = D_FF // FF_CHUNK
    half = om_ref.shape[1]
    y = _dot(om_ref[...], wo_ref[:half, :]) + _dot(on_ref[...], wo_ref[half:, :])
    x = x_ref[...] + _rms(y, g1_ref[...])

    h = _rms(x, g2_ref[...]).astype(BF16)

    @pl.when((pl.program_id(0) * tm) % seq == 0)
    def _():
        carry_ref[...] = jnp.zeros(carry_ref.shape, F32)

    SUB = 8
    row8 = lax.broadcasted_iota(jnp.int32, (SUB, 1), 0)
    for c in range(n_chunks):
        cols = slice(c * FF_CHUNK, (c + 1) * FF_CHUNK)
        g = _dot(h, wg_ref[:, cols])
        prev = carry_ref[:, cols]
        carry_ref[:, cols] = g[tm - SUB:, :]
        r1 = pltpu.roll(g, 1, 0)
        r2 = pltpu.roll(g, 2, 0)
        top1 = jnp.where(row8 == 0, prev[7:8, :], r1[:SUB])
        top2 = jnp.where(row8 == 0, prev[6:7, :], jnp.where(row8 == 1, prev[7:8, :], r2[:SUB]))
        g1 = jnp.concatenate([top1, r1[SUB:]], axis=0)
        g2 = jnp.concatenate([top2, r2[SUB:]], axis=0)
        conv = (cw_ref[0:1, cols] * g2 + cw_ref[1:2, cols] * g1 + cw_ref[2:3, cols] * g
                + cb_ref[:, cols])
        act_ref[:, cols] = (_gelu_tanh(conv) * _dot(h, wu_ref[:, cols])).astype(BF16)
    x = x + _rms(_dot(act_ref[...], wd_ref[...]), g3_ref[...])

    gate = _sigmoid(_dot(x.astype(BF16), pg_ref[...]))
    o_ref[...] = x + gate * _dot(p_ref[...].astype(BF16), pp_ref[...])


def _tail(x2, om, on, p2, wo, g1, g2, g3, wg, wu, cw, cb, wd, pg, pp, *, seq):
    n = x2.shape[0]
    tm = 512
    row = lambda r: (r, 0)
    half = om.shape[1]
    return pl.pallas_call(
        functools.partial(_tail_kernel, seq=seq, tm=tm),
        grid=(n // tm,),
        in_specs=[pl.BlockSpec((tm, D_MODEL), row),
                  pl.BlockSpec((tm, half), row),
                  pl.BlockSpec((tm, half), row),
                  pl.BlockSpec((tm, PLE_DIM), row),
                  _resident((2 * half, D_MODEL)),
                  _resident((1, D_MODEL)),
                  _resident((1, D_MODEL)),
                  _resident((1, D_MODEL)),
                  _resident((D_MODEL, D_FF)),
                  _resident((D_MODEL, D_FF)),
                  _resident((CONV_WIDTH, D_FF)),
                  _resident((1, D_FF)),
                  _resident((D_FF, D_MODEL)),
                  _resident((D_MODEL, D_MODEL)),
                  _resident((PLE_DIM, D_MODEL))],
        out_specs=pl.BlockSpec((tm, D_MODEL), row),
        out_shape=jax.ShapeDtypeStruct((n, D_MODEL), F32),
        scratch_shapes=[pltpu.VMEM((8, D_FF), F32),
                        pltpu.VMEM((tm, D_FF), BF16)],
        compiler_params=_params("arbitrary"),
        name="layer_tail",
    )(x2, om, on, p2, wo, g1, g2, g3, wg, wu, cw, cb, wd, pg, pp)


def _rot_cols(w):
    half = w.shape[1] // 2
    return jnp.concatenate([-w[:, half:], w[:, :half]], axis=1)


def _prep_inproj(w):
    o = np.cumsum((0,) + IN_SPLITS)
    parts = [w[:, o[j]:o[j + 1]] for j in range(len(IN_SPLITS))]
    cq, ckv, kr, qn, kcmp, vcmp, kslc, vslc, kwin, vwin, gn = parts
    d = w.shape[0]
    z64 = jnp.zeros((d, NSA_DIM), F32)
    z32 = jnp.zeros((d, LANE - MLA_NOPE - MLA_ROPE), F32)

    def slots(m):
        return jnp.concatenate([m[:, :NSA_DIM], z64, m[:, NSA_DIM:], z64], axis=1)

    w1 = jnp.concatenate([cq, ckv,
                          z64, kr, z32,
                          z64, _rot_cols(kr), z32,
                          slots(kslc), slots(kwin), kcmp, vcmp], axis=1)
    assert w1.shape[1] == C_TOTAL

    def vslots(m):
        zp = jnp.zeros((VROWS - NSA_DIM, d), F32)
        return jnp.concatenate([m[:, :NSA_DIM].T, zp, m[:, NSA_DIM:].T, zp], axis=0)

    per = 3 * NSA_GROUP
    gp = jnp.zeros((GATE_ROWS - per, d), F32)
    w1t = jnp.concatenate([qn.T * NSA_DIM ** -0.5, vslots(vslc), vslots(vwin),
                           gn[:, :per].T, gp, gn[:, per:].T, gp], axis=0)
    assert w1t.shape[0] == R_TOTAL
    return w1.astype(BF16), w1t.astype(BF16)


def _prep_mla(w_uq, w_ukv):
    r = w_uq.shape[0]
    dq = MLA_NOPE + MLA_ROPE
    z32 = jnp.zeros((r, LANE - dq), F32)
    z64q = jnp.zeros((r, MLA_NOPE), F32)
    qa, qb = [], []
    for h in range(MLA_HEADS):
        blk = w_uq[:, h * dq:(h + 1) * dq]
        qa += [blk, z32]
        qb += [z64q, _rot_cols(blk[:, MLA_NOPE:]), z32]
    rk = w_ukv.shape[0]
    z64 = jnp.zeros((rk, LANE - MLA_NOPE), F32)
    zv = jnp.zeros((rk, VROWS - MLA_V), F32)
    wk, wv = [], []
    dkv = MLA_NOPE + MLA_V
    for h in range(MLA_HEADS):
        blk = w_ukv[:, h * dkv:(h + 1) * dkv]
        wk += [blk[:, :MLA_NOPE], z64]
        wv += [blk[:, MLA_NOPE:], zv]
    cat = lambda xs: jnp.concatenate(xs, axis=1).astype(BF16)
    return cat(qa).T, cat(qb).T, cat(wk), cat(wv).T


def kernel(x, p, positions, rel_bias, attn_pre_norm, attn_post_norm, ffn_pre_norm, ffn_post_norm,
           w_in, mla_q_norm, mla_w_uq, mla_kv_norm, mla_w_ukv, nsa_cmp_pos, nsa_cmp_w1, nsa_cmp_w2,
           w_o, ffn_w_gate, ffn_w_up, ffn_conv_w, ffn_conv_b, ffn_w_down, ple_proj, ple_gate):
    batch, seq, d = x.shape
    depth = w_in.shape[0]
    n = batch * seq
    x2 = x.reshape(n, d)
    ctt, stt, ct, st = _rope_tables(positions)
    pat, utab = _bias_tables(rel_bias, seq)
    for i in range(depth):
        w1, w1t = _prep_inproj(w_in[i])
        wqa, wqb, wk, wvt = _prep_mla(mla_w_uq[i], mla_w_ukv[i])
        qm_t, km, vm_t, zq_t, nk, nv_t, kcmp, vcmp, gates_t = _inproj(
            x2, attn_pre_norm[i][None, :], w1, w1t, mla_q_norm[i][None, :], wqa, wqb,
            mla_kv_norm[i][None, :], wk, wvt, ct, st, ctt, stt, batch=batch, seq=seq)
        kc, vc_t = _compress(kcmp, vcmp,
                             nsa_cmp_pos[i].reshape(2, 1, CMP_LEN * NSA_DIM),
                             nsa_cmp_w1[i].astype(BF16), nsa_cmp_w2[i, 0].astype(BF16),
                             nsa_cmp_w2[i, 1].T.astype(BF16), batch=batch, seq=seq)
        o_nsa = _nsa(zq_t, kc, vc_t, nk, nv_t, gates_t, pat, utab, batch=batch, seq=seq)
        o_mla = _mla(qm_t, km, vm_t, batch=batch, seq=seq)
        wg = ffn_w_gate[i].astype(BF16)
        wu = ffn_w_up[i].astype(BF16)
        cw = ffn_conv_w[i]
        cb = ffn_conv_b[i][None, :]
        wd = ffn_w_down[i].astype(BF16)
        x2 = _tail(x2, o_mla, o_nsa, p[i].reshape(n, PLE_DIM), w_o[i].astype(BF16),
                   attn_post_norm[i][None, :], ffn_pre_norm[i][None, :], ffn_post_norm[i][None, :],
                   wg, wu, cw, cb, wd, ple_gate[i].astype(BF16), ple_proj[i].astype(BF16), seq=seq)
    return x2.reshape(batch, seq, d)
```

```python
import functools
import math

import numpy as np
import jax
import jax.numpy as jnp
from jax import lax
from jax.experimental import pallas as pl
from jax.experimental.pallas import tpu as pltpu

F32 = jnp.float32
BF16 = jnp.bfloat16

D_MODEL = 1024
DEPTH = 2
MLA_HEADS = 8
MLA_NOPE = 64
MLA_ROPE = 32
MLA_V = 64
MLA_Q_RANK = 256
MLA_KV_RANK = 128
ROPE_BASE = 10000.0
NSA_HEADS = 8
NSA_KV_HEADS = 2
NSA_GROUP = NSA_HEADS // NSA_KV_HEADS
NSA_DIM = 64
CMP_LEN = 32
CMP_STRIDE = 16
CMP_HIDDEN = 128
SLC_LEN = 64
SLC_TOPK = 16
WINDOW = 512
REL_BUCKETS = 32
REL_MAX_DIST = 128
D_FF = 2816
CONV_WIDTH = 3
PLE_DIM = 256
EPS = 1e-6
NEG = -1e30
POS_BIG = 1e30
LOG2E = math.log2(math.e)

IN_SPLITS = (MLA_Q_RANK, MLA_KV_RANK, MLA_ROPE, NSA_HEADS * NSA_DIM,
             NSA_KV_HEADS * NSA_DIM, NSA_KV_HEADS * NSA_DIM,
             NSA_KV_HEADS * NSA_DIM, NSA_KV_HEADS * NSA_DIM,
             NSA_KV_HEADS * NSA_DIM, NSA_KV_HEADS * NSA_DIM,
             3 * NSA_HEADS)

LANE = 128
QT = 256
KP = 128
SLC_PAD = 64
FF_CHUNK = 256
VMEM_LIMIT = 56 * 1024 * 1024

VROWS = 80

C_CQ = 0
C_CKV = 256
C_KR = 384
C_KRROT = 512
C_KSLC = 640
C_KWIN = 896
C_KCMP = 1152
C_VCMP = 1280
C_TOTAL = 1408
R_QN = 0
R_VSLC = 512
R_VWIN = R_VSLC + NSA_KV_HEADS * VROWS
R_GATE = R_VWIN + NSA_KV_HEADS * VROWS
GATE_ROWS = 16
R_TOTAL = R_GATE + NSA_KV_HEADS * GATE_ROWS


def _dot(a, b):
    return jnp.dot(a, b, preferred_element_type=F32)


def _dot_nt(a, b):
    return lax.dot_general(a, b, (((1,), (1,)), ((), ())), preferred_element_type=F32)


def _rms(x, g):
    return x * lax.rsqrt(jnp.mean(x * x, axis=-1, keepdims=True) + EPS) * g


def _gelu_tanh(x):
    return 0.5 * x * (1.0 + jnp.tanh(math.sqrt(2.0 / math.pi) * (x + 0.044715 * (x * x * x))))


def _sigmoid(x):
    return 1.0 / (1.0 + jnp.exp(-x))


def _params(*sem):
    return pltpu.CompilerParams(dimension_semantics=sem, vmem_limit_bytes=VMEM_LIMIT)


def _resident(shape):
    nd = len(shape)
    return pl.BlockSpec(shape, lambda *_: (0,) * nd, pipeline_mode=pl.Buffered(1))


def _rope_kernel(pos_ref, inv_ref, ctt_ref, stt_ref, ct_ref, st_ref):
    ang = inv_ref[...] * pos_ref[...].astype(F32)
    row = lax.broadcasted_iota(jnp.int32, ang.shape, 0)
    rope = (row >= MLA_NOPE) & (row < MLA_NOPE + MLA_ROPE)
    ct = jnp.where(rope, jnp.cos(ang), jnp.where(row < MLA_NOPE, 1.0, 0.0))
    st = jnp.where(rope, jnp.sin(ang), 0.0)
    ctt_ref[...] = ct
    stt_ref[...] = st
    ct_ref[...] = ct.T
    st_ref[...] = st.T


def _rope_tables(positions):
    n = positions.size
    tn = 2048
    half = MLA_ROPE // 2
    inv = ROPE_BASE ** (-jnp.arange(half, dtype=F32) / half)
    inv_slot = jnp.concatenate([jnp.zeros((MLA_NOPE,), F32), inv, inv,
                                jnp.zeros((LANE - MLA_NOPE - MLA_ROPE,), F32)])[:, None]
    return pl.pallas_call(
        _rope_kernel,
        grid=(n // tn,),
        in_specs=[pl.BlockSpec((1, tn), lambda r: (0, r)),
                  pl.BlockSpec((LANE, 1), lambda r: (0, 0))],
        out_specs=[pl.BlockSpec((LANE, tn), lambda r: (0, r))] * 2
                  + [pl.BlockSpec((tn, LANE), lambda r: (r, 0))] * 2,
        out_shape=[jax.ShapeDtypeStruct((LANE, n), F32)] * 2 + [jax.ShapeDtypeStruct((n, LANE), F32)] * 2,
        compiler_params=_params("parallel"),
        name="rope_tables",
    )(positions.reshape(1, n), inv_slot)


def _bucket_np(dist):
    n = np.maximum(dist, 0)
    max_exact = REL_BUCKETS // 2
    large = max_exact + (np.log(np.maximum(n, 1).astype(np.float32) / max_exact)
                         / math.log(REL_MAX_DIST / max_exact)
                         * (REL_BUCKETS - max_exact)).astype(np.int32)
    large = np.minimum(large, REL_BUCKETS - 1)
    return np.where(n < max_exact, n, large).astype(np.int32)


def _bias_kernel(table_ref, bpc_ref, bpu_ref, pat_ref, u_ref):
    h = pl.program_id(0)
    far = table_ref[REL_BUCKETS - 1, h]

    def lookup(bp, sub):
        acc = jnp.full(bp.shape, far - sub, F32)
        for b in range(REL_BUCKETS - 1):
            acc = jnp.where(bp == b, table_ref[b, h] - sub, acc)
        return jnp.where(bp < 0, NEG, acc * LOG2E)

    pat_ref[0] = lookup(bpc_ref[...], 0.0)
    for d in range(3):
        u_ref[0, d] = lookup(bpu_ref[d], far)


def _bias_tables(rel_bias, seq):
    ncp = seq // CMP_STRIDE
    i = np.arange(QT)[None, :]
    cprime = np.arange(2 * ncp)[:, None] - ncp
    dist_c = i - CMP_STRIDE * cprime - (CMP_LEN - 1)
    bpc = np.where(dist_c >= 0, _bucket_np(dist_c), -1).astype(np.int32)
    j = np.arange(KP)[:, None]
    bpu = []
    for delta in (-1, 0, 1):
        dist = i - KP * delta - j
        bpu.append(np.where(dist >= 0, _bucket_np(dist), -1))
    bpu = np.stack(bpu).astype(np.int32)
    return pl.pallas_call(
        _bias_kernel,
        grid=(NSA_HEADS,),
        in_specs=[pl.BlockSpec(memory_space=pltpu.SMEM),
                  pl.BlockSpec((2 * ncp, QT), lambda h: (0, 0)),
                  pl.BlockSpec((3, KP, QT), lambda h: (0, 0, 0))],
        out_specs=[pl.BlockSpec((1, 2 * ncp, QT), lambda h: (h, 0, 0)),
                   pl.BlockSpec((1, 3, KP, QT), lambda h: (h // NSA_GROUP, 0, 0, h % NSA_GROUP))],
        out_shape=[jax.ShapeDtypeStruct((NSA_HEADS, 2 * ncp, QT), F32),
                   jax.ShapeDtypeStruct((NSA_KV_HEADS, 3, KP, NSA_GROUP * QT), F32)],
        compiler_params=_params("parallel"),
        name="bias_tables",
    )(rel_bias.astype(F32), jnp.asarray(bpc), jnp.asarray(bpu))


def _inproj_kernel(x_ref, g_ref, w1_ref, w1t_ref, qn_ref, wqa_ref, wqb_ref, kvn_ref, wk_ref, wvt_ref,
                   ct_ref, st_ref, ctt_ref, stt_ref, ones_n_ref, ones_m_ref,
                   qm_ref, km_ref, vm_ref, zq_ref, nk_ref, nv_ref, kcmp_ref, vcmp_ref, gate_ref,
                   *, seq, tm):
    h = _rms(x_ref[...], g_ref[...]).astype(BF16)

    def proj(c0, width):
        return _dot(h, w1_ref[:, c0:c0 + width])

    cq = _rms(proj(C_CQ, MLA_Q_RANK), qn_ref[...]).astype(BF16)
    qa = _dot_nt(wqa_ref[...], cq)
    qb = _dot_nt(wqb_ref[...], cq)
    scale = (MLA_NOPE + MLA_ROPE) ** -0.5 * LOG2E
    cts = ctt_ref[...] * scale
    sts = stt_ref[...] * scale
    for hh in range(MLA_HEADS):
        sl = slice(hh * LANE, (hh + 1) * LANE)
        qm_ref[sl, :] = (qa[sl] * cts + qb[sl] * sts).astype(BF16)

    ckv = _rms(proj(C_CKV, MLA_KV_RANK), kvn_ref[...]).astype(BF16)
    kr = proj(C_KR, LANE) * ct_ref[...] + proj(C_KRROT, LANE) * st_ref[...]
    kn = _dot(ckv, wk_ref[...])
    for hh in range(MLA_HEADS):
        sl = slice(hh * LANE, (hh + 1) * LANE)
        km_ref[:, sl] = (kn[:, sl] + kr).astype(BF16)
    vm_ref[...] = (_dot_nt(wvt_ref[...], ckv) + ones_m_ref[...]).astype(BF16)

    zt = _dot_nt(w1t_ref[...], h)
    zq_ref[...] = (zt[R_QN:R_VSLC] * LOG2E).astype(BF16)
    nv_ref[...] = (zt[R_VSLC:R_GATE] + ones_n_ref[...]).astype(BF16)
    gate_ref[...] = _sigmoid(zt[R_GATE:R_TOTAL])

    lane = lax.broadcasted_iota(jnp.int32, (tm, LANE), 1)
    s0 = (pl.program_id(0) * tm) % seq
    row = lax.broadcasted_iota(jnp.int32, (tm, LANE), 0)
    onehot = (lane - NSA_DIM == (s0 + row) // SLC_LEN).astype(F32)
    ksl = proj(C_KSLC, 2 * LANE)
    for hk in range(NSA_KV_HEADS):
        sl = slice(hk * LANE, (hk + 1) * LANE)
        nk_ref[:, sl] = (ksl[:, sl] + onehot).astype(BF16)
    nk_ref[:, 2 * LANE:] = proj(C_KWIN, 2 * LANE).astype(BF16)

    kcmp = proj(C_KCMP, LANE)
    vcmp = proj(C_VCMP, LANE)
    for hk in range(NSA_KV_HEADS):
        kcmp_ref[0, hk] = kcmp[:, hk * NSA_DIM:(hk + 1) * NSA_DIM]
        vcmp_ref[0, hk] = vcmp[:, hk * NSA_DIM:(hk + 1) * NSA_DIM]


def _ones_rows(n_slots):
    r = np.arange(n_slots * VROWS) % VROWS
    return jnp.asarray((r >= NSA_DIM).astype(np.float32)[:, None])


def _inproj(x2, g_pre, w1, w1t, qn, wqa, wqb, kvn, wk, wvt, ct, st, ctt, stt, *, batch, seq):
    n = x2.shape[0]
    tm = 1024
    tiles_per_seq = seq // tm
    row = lambda r: (r, 0)
    col = lambda r: (0, r)
    hm = lambda r: (r // tiles_per_seq, 0, r % tiles_per_seq, 0)
    nv_rows = 2 * NSA_KV_HEADS * VROWS
    vm_rows = MLA_HEADS * VROWS
    return pl.pallas_call(
        functools.partial(_inproj_kernel, seq=seq, tm=tm),
        grid=(n // tm,),
        in_specs=[pl.BlockSpec((tm, D_MODEL), row),
                  _resident((1, D_MODEL)),
                  _resident((D_MODEL, C_TOTAL)),
                  _resident((R_TOTAL, D_MODEL)),
                  _resident((1, MLA_Q_RANK)),
                  _resident((MLA_HEADS * LANE, MLA_Q_RANK)),
                  _resident((MLA_HEADS * LANE, MLA_Q_RANK)),
                  _resident((1, MLA_KV_RANK)),
                  _resident((MLA_KV_RANK, MLA_HEADS * LANE)),
                  _resident((vm_rows, MLA_KV_RANK)),
                  pl.BlockSpec((tm, LANE), row),
                  pl.BlockSpec((tm, LANE), row),
                  pl.BlockSpec((LANE, tm), col),
                  pl.BlockSpec((LANE, tm), col),
                  _resident((nv_rows, 1)),
                  _resident((vm_rows, 1))],
        out_specs=[pl.BlockSpec((MLA_HEADS * LANE, tm), col),
                   pl.BlockSpec((tm, MLA_HEADS * LANE), row),
                   pl.BlockSpec((vm_rows, tm), col),
                   pl.BlockSpec((NSA_HEADS * NSA_DIM, tm), col),
                   pl.BlockSpec((tm, 4 * LANE), row),
                   pl.BlockSpec((nv_rows, tm), col),
                   pl.BlockSpec((1, NSA_KV_HEADS, tm, NSA_DIM), hm),
                   pl.BlockSpec((1, NSA_KV_HEADS, tm, NSA_DIM), hm),
                   pl.BlockSpec((NSA_KV_HEADS * GATE_ROWS, tm), col)],
        out_shape=[jax.ShapeDtypeStruct((MLA_HEADS * LANE, n), BF16),
                   jax.ShapeDtypeStruct((n, MLA_HEADS * LANE), BF16),
                   jax.ShapeDtypeStruct((vm_rows, n), BF16),
                   jax.ShapeDtypeStruct((NSA_HEADS * NSA_DIM, n), BF16),
                   jax.ShapeDtypeStruct((n, 4 * LANE), BF16),
                   jax.ShapeDtypeStruct((nv_rows, n), BF16),
                   jax.ShapeDtypeStruct((batch, NSA_KV_HEADS, seq, NSA_DIM), F32),
                   jax.ShapeDtypeStruct((batch, NSA_KV_HEADS, seq, NSA_DIM), F32),
                   jax.ShapeDtypeStruct((NSA_KV_HEADS * GATE_ROWS, n), F32)],
        compiler_params=_params("parallel"),
        name="in_proj",
    )(x2, g_pre, w1, w1t, qn, wqa, wqb, kvn, wk, wvt, ct, st, ctt, stt,
      _ones_rows(2 * NSA_KV_HEADS), _ones_rows(MLA_HEADS))


def _compress_kernel(k16_ref, v16_ref, pos_ref, w1_ref, w2k_ref, w2vt_ref, kc_ref, vc_ref):
    half = CMP_STRIDE * NSA_DIM

    def hidden(x16, j):
        pos = pos_ref[j]
        top = (x16 + pos[:, :half]).astype(BF16)
        bot = (x16 + pos[:, half:]).astype(BF16)
        u = _dot(top, w1_ref[j, :half, :])
        low = _dot(bot, w1_ref[j, half:, :])
        nxt = jnp.concatenate([low[1:], jnp.zeros((1, CMP_HIDDEN), F32)], axis=0)
        return _gelu_tanh(u + nxt).astype(BF16)

    kc_ref[0, 0] = _dot(hidden(k16_ref[0, 0], 0), w2k_ref[...]).astype(BF16)
    vc_ref[0, 0] = _dot_nt(w2vt_ref[...], hidden(v16_ref[0, 0], 1)).astype(BF16)


def _compress(kcmp, vcmp, pos, w1, w2k, w2vt, *, batch, seq):
    ncp = seq // CMP_STRIDE
    half = CMP_STRIDE * NSA_DIM
    k16 = kcmp.reshape(batch, NSA_KV_HEADS, ncp, half)
    v16 = vcmp.reshape(batch, NSA_KV_HEADS, ncp, half)
    blk = lambda b, hk: (b, hk, 0, 0)
    return pl.pallas_call(
        _compress_kernel,
        grid=(batch, NSA_KV_HEADS),
        in_specs=[pl.BlockSpec((1, 1, ncp, half), blk),
                  pl.BlockSpec((1, 1, ncp, half), blk),
                  pl.BlockSpec((2, 1, 2 * half), lambda b, hk: (0, 0, 0)),
                  pl.BlockSpec((2, 2 * half, CMP_HIDDEN), lambda b, hk: (0, 0, 0)),
                  pl.BlockSpec((CMP_HIDDEN, NSA_DIM), lambda b, hk: (0, 0)),
                  pl.BlockSpec((NSA_DIM, CMP_HIDDEN), lambda b, hk: (0, 0))],
        out_specs=[pl.BlockSpec((1, 1, ncp, NSA_DIM), blk),
                   pl.BlockSpec((1, 1, NSA_DIM, ncp), blk)],
        out_shape=[jax.ShapeDtypeStruct((batch, NSA_KV_HEADS, ncp, NSA_DIM), BF16),
                   jax.ShapeDtypeStruct((batch, NSA_KV_HEADS, NSA_DIM, ncp), BF16)],
        compiler_params=_params("parallel", "parallel"),
        name="nsa_compress",
    )(k16, v16, pos, w1, w2k, w2vt)


def _nsa_kernel(zq_ref, kc_ref, vc_ref, ksl_ref, vsl_ref, kw_ref, vw_ref, gate_ref,
                pat_ref, u_ref, ovt_ref, o_ref, imp_ref, acc_ref, m_ref, sa_ref, sb_ref):
    qi = pl.program_id(2)
    G = NSA_GROUP
    M = G * QT
    PPT = QT // KP
    q4 = zq_ref[...]
    qs = jnp.concatenate([q4[g * NSA_DIM:(g + 1) * NSA_DIM, :] for g in range(G)], axis=1)
    kc = kc_ref[0, 0]
    vc_t = vc_ref[0, 0]
    ovt = ovt_ref[...]
    ncp = kc.shape[0]
    qcol = qi * QT + (lax.broadcasted_iota(jnp.int32, (1, M), 1) & (QT - 1))
    u_m1 = u_ref[0, 0]
    u_0 = u_ref[0, 1]
    u_p1 = u_ref[0, 2]

    n_back = WINDOW // KP
    n_w = n_back + PPT
    qw = jnp.concatenate([qs, jnp.zeros_like(qs)], axis=0)
    pad_v = jnp.where(lax.broadcasted_iota(jnp.int32, (VROWS, KP), 0) >= NSA_DIM, 1.0, 0.0).astype(BF16)
    k_rows, v_cols = [], []
    for w in range(n_w):
        piece = PPT * qi - n_back + w
        off = pl.multiple_of(jnp.maximum(piece, 0) * KP, KP)
        kt = kw_ref[pl.ds(off, KP), :]
        vt = vw_ref[:, pl.ds(off, KP)]
        if w < n_back:
            kt = jnp.where(piece >= 0, kt, jnp.zeros_like(kt))
            vt = jnp.where(piece >= 0, vt, pad_v)
        k_rows.append(kt)
        v_cols.append(vt)
    s_w = _dot(jnp.concatenate(k_rows, axis=0), qw)

    slide = QT // CMP_STRIDE
    off_b = pl.multiple_of(ncp - slide * qi, slide)
    bias = jnp.concatenate([pat_ref[g, pl.ds(off_b, ncp), :] for g in range(G)], axis=1)
    s = _dot(kc, qs) + bias
    e = jnp.exp2(s - jnp.max(s, axis=0, keepdims=True))
    scale = jnp.where(qcol >= CMP_LEN - 1, 1.0 / jnp.sum(e, axis=0, keepdims=True), 0.0)
    p_cmp = (e * scale).astype(BF16)

    key_j = lax.broadcasted_iota(jnp.int32, (KP, M), 0)
    qry_i = lax.broadcasted_iota(jnp.int32, (KP, M), 1) & (QT - 1)
    pieces = [s_w[w * KP:(w + 1) * KP] for w in range(n_w)]
    for w in range(PPT):
        pieces[w] = pieces[w] + jnp.where(key_j > qry_i - w * KP, 0.0, NEG)
    pieces[n_back - 1] = pieces[n_back - 1] + u_m1
    pieces[n_back] = pieces[n_back] + u_0
    pieces[n_back + 1] = pieces[n_back + 1] + u_p1
    s_w = jnp.concatenate(pieces, axis=0)
    p_w = jnp.exp2(s_w - jnp.max(s_w, axis=0, keepdims=True)).astype(BF16)

    o_cmp = _dot(vc_t, p_cmp)
    imp_t = jnp.zeros((SLC_PAD, QT), F32)
    for g in range(G):
        imp_t = imp_t + _dot(ovt, p_cmp[:, g * QT:(g + 1) * QT])

    a_w = _dot(jnp.concatenate(v_cols, axis=1), p_w)
    o_win = a_w[:NSA_DIM] / a_w[NSA_DIM:NSA_DIM + 1]

    n_id = lax.broadcasted_iota(jnp.int32, (SLC_PAD, QT), 0)
    q_blk = (qi * QT + lax.broadcasted_iota(jnp.int32, (SLC_PAD, QT), 1)) // SLC_LEN
    forced = (n_id == 0) | (n_id == q_blk) | (n_id == q_blk - 1)
    imp = jnp.where(forced, POS_BIG, jnp.where(n_id > q_blk, NEG, imp_t))
    imp_ref[...] = imp
    SUB = 8
    slabs = [imp[v * SUB:(v + 1) * SUB] for v in range(SLC_PAD // SUB)]
    ranks = [jnp.zeros((SUB, QT), jnp.int32) for _ in slabs]
    sub_id = lax.broadcasted_iota(jnp.int32, (SUB, QT), 0)
    for m in range(SLC_PAD):
        other = imp_ref[m:m + 1, :]
        for v, slab in enumerate(slabs):
            if v > m // SUB:
                beats = (other >= slab).astype(jnp.int32)
            elif v < m // SUB:
                beats = (other > slab).astype(jnp.int32)
            else:
                beats = jnp.where(sub_id > m % SUB, (other >= slab).astype(jnp.int32),
                                  (other > slab).astype(jnp.int32))
            ranks[v] = ranks[v] + beats
    rank = jnp.concatenate(ranks, axis=0)
    selb = jnp.where(rank < SLC_TOPK, 0.0, NEG).astype(BF16)
    qaug = jnp.concatenate([qs, jnp.concatenate([selb] * G, axis=1)], axis=0)

    PIECES = 4
    TK = PIECES * KP
    n_t = (PPT * qi + PPT - 1) // PIECES + 1
    at_start = (PPT * qi) % PIECES == 0
    last_bias = [jnp.where(at_start, u_0, 0.0), jnp.where(at_start, u_p1, u_m1),
                 jnp.where(at_start, NEG, u_0), jnp.where(at_start, NEG, u_p1)]
    prev_bias = [None, None, None, jnp.where(at_start, u_m1, 0.0)]
    m_ref[...] = jnp.full(m_ref.shape, NEG, F32)
    acc_ref[...] = jnp.zeros(acc_ref.shape, F32)

    def qk(dst_ref, t):
        off = pl.multiple_of(t * TK, TK)
        dst_ref[...] = _dot(ksl_ref[pl.ds(off, TK), :], qaug)

    def consume(src_ref, t, bias):
        off = pl.multiple_of(t * TK, TK)
        s = src_ref[...]
        if bias is not None:
            s = jnp.concatenate([s[j * KP:(j + 1) * KP] if bias[j] is None else s[j * KP:(j + 1) * KP] + bias[j]
                                 for j in range(PIECES)], axis=0)
        m_old = m_ref[...]
        m_new = jnp.maximum(m_old, jnp.max(s, axis=0, keepdims=True))
        p = jnp.exp2(s - m_new).astype(BF16)
        acc_ref[...] = acc_ref[...] * jnp.exp2(m_old - m_new) + _dot(vsl_ref[:, pl.ds(off, TK)], p)
        m_ref[...] = m_new

    n_plain = jnp.maximum(n_t - 2, 0)
    qk(sa_ref, 0)

    def pair(u, carry):
        t = 2 * u
        qk(sb_ref, t + 1)
        consume(sa_ref, t, None)
        qk(sa_ref, t + 2)
        consume(sb_ref, t + 1, None)
        return carry

    lax.fori_loop(0, n_plain // 2, pair, 0)
    tb = (n_plain // 2) * 2

    @pl.when(n_t == 1)
    def _():
        consume(sa_ref, 0, last_bias)

    @pl.when((n_t >= 2) & (n_plain % 2 == 0))
    def _():
        qk(sb_ref, tb + 1)
        consume(sa_ref, tb, prev_bias)
        consume(sb_ref, tb + 1, last_bias)

    @pl.when(n_plain % 2 == 1)
    def _():
        qk(sb_ref, tb + 1)
        consume(sa_ref, tb, None)
        qk(sa_ref, tb + 2)
        consume(sb_ref, tb + 1, prev_bias)
        consume(sa_ref, tb + 2, last_bias)

    a_s = acc_ref[...]
    o_sel = a_s[:NSA_DIM] / a_s[NSA_DIM:NSA_DIM + 1]

    gate = gate_ref[...]
    outs = []
    for g in range(G):
        cols = slice(g * QT, (g + 1) * QT)
        o_t = (gate[3 * g:3 * g + 1, :] * o_cmp[:, cols] + gate[3 * g + 1:3 * g + 2, :] * o_sel[:, cols]
               + gate[3 * g + 2:3 * g + 3, :] * o_win[:, cols])
        outs.append(o_t.T)
    o_ref[...] = jnp.concatenate(outs, axis=1).astype(BF16)


def _overlap_t(seq):
    ncp = seq // CMP_STRIDE
    n_cmp = (seq - CMP_LEN) // CMP_STRIDE + 1
    n_slc = seq // SLC_LEN
    cs = np.arange(ncp)[None, :] * CMP_STRIDE
    ss = np.arange(SLC_PAD)[:, None] * SLC_LEN
    ov = np.maximum(np.minimum(cs + CMP_LEN, ss + SLC_LEN) - np.maximum(cs, ss), 0).astype(np.float32) / CMP_LEN
    ov = ov * (np.arange(ncp)[None, :] < n_cmp) * (np.arange(SLC_PAD)[:, None] < n_slc)
    return jnp.asarray(ov, BF16)


def _nsa(zq_t, kc, vc_t, nk, nv_t, gates_t, pat, utab, *, batch, seq):
    n = nk.shape[0]
    nq = seq // QT
    ncp = seq // CMP_STRIDE
    G = NSA_GROUP
    assert seq // SLC_LEN <= SLC_PAD and seq // SLC_LEN >= SLC_TOPK and seq >= WINDOW
    qcol = lambda b, hk, qi: (hk, b * nq + qi)
    return pl.pallas_call(
        _nsa_kernel,
        grid=(batch, NSA_KV_HEADS, nq),
        in_specs=[pl.BlockSpec((G * NSA_DIM, QT), qcol),
                  pl.BlockSpec((1, 1, ncp, NSA_DIM), lambda b, hk, qi: (b, hk, 0, 0)),
                  pl.BlockSpec((1, 1, NSA_DIM, ncp), lambda b, hk, qi: (b, hk, 0, 0)),
                  pl.BlockSpec((seq, LANE), lambda b, hk, qi: (b, hk)),
                  pl.BlockSpec((VROWS, seq), lambda b, hk, qi: (hk, b)),
                  pl.BlockSpec((seq, LANE), lambda b, hk, qi: (b, NSA_KV_HEADS + hk)),
                  pl.BlockSpec((VROWS, seq), lambda b, hk, qi: (NSA_KV_HEADS + hk, b)),
                  pl.BlockSpec((GATE_ROWS, QT), qcol),
                  pl.BlockSpec((G, 2 * ncp, QT), lambda b, hk, qi: (hk, 0, 0)),
                  pl.BlockSpec((1, 3, KP, G * QT), lambda b, hk, qi: (hk, 0, 0, 0)),
                  pl.BlockSpec((SLC_PAD, ncp), lambda b, hk, qi: (0, 0))],
        out_specs=pl.BlockSpec((QT, G * NSA_DIM), lambda b, hk, qi: (b * nq + qi, hk)),
        out_shape=jax.ShapeDtypeStruct((n, NSA_HEADS * NSA_DIM), BF16),
        scratch_shapes=[pltpu.VMEM((SLC_PAD, QT), F32),
                        pltpu.VMEM((VROWS, G * QT), F32),
                        pltpu.VMEM((1, G * QT), F32),
                        pltpu.VMEM((4 * KP, G * QT), F32),
                        pltpu.VMEM((4 * KP, G * QT), F32)],
        compiler_params=_params("parallel", "parallel", "arbitrary"),
        name="nsa_attention",
    )(zq_t, kc, vc_t, nk, nv_t, nk, nv_t, gates_t, pat, utab, _overlap_t(seq))


MLA_TQ = 512
MLA_HP = 4


def _mla_kernel(q_ref, k_ref, v_ref, o_ref, acc_ref, m_ref, sa_ref, sb_ref):
    qi = pl.program_id(2)
    tq = MLA_TQ
    HP = MLA_HP
    qs = [q_ref[hh * LANE:(hh + 1) * LANE, :] for hh in range(HP)]
    m_ref[...] = jnp.full(m_ref.shape, NEG, F32)
    acc_ref[...] = jnp.zeros(acc_ref.shape, F32)

    def qk(dst_ref, off, hh):
        dst_ref[hh] = _dot(k_ref[pl.ds(off, tq), hh * LANE:(hh + 1) * LANE], qs[hh])

    def consume(src_ref, off, mask, hh):
        s = src_ref[hh]
        if mask is not None:
            s = s + mask
        vt = v_ref[hh * VROWS:(hh + 1) * VROWS, pl.ds(off, tq)]
        m_old = m_ref[hh]
        m_new = jnp.maximum(m_old, jnp.max(s, axis=0, keepdims=True))
        p = jnp.exp2(s - m_new).astype(BF16)
        acc_ref[hh] = acc_ref[hh] * jnp.exp2(m_old - m_new) + _dot(vt, p)
        m_ref[hh] = m_new

    def stage(nxt_ref, nxt_off, cur_ref, cur_off, mask):
        for hh in range(HP):
            if nxt_ref is not None:
                qk(nxt_ref, nxt_off, hh)
            consume(cur_ref, cur_off, mask, hh)

    key_j = lax.broadcasted_iota(jnp.int32, (tq, tq), 0)
    qry_i = lax.broadcasted_iota(jnp.int32, (tq, tq), 1)
    causal = jnp.where(key_j <= qry_i, 0.0, NEG)
    for hh in range(HP):
        qk(sa_ref, 0, hh)

    def pair(u, carry):
        off = pl.multiple_of(u * (2 * tq), 2 * tq)
        stage(sb_ref, off + tq, sa_ref, off, None)
        stage(sa_ref, off + 2 * tq, sb_ref, off + tq, None)
        return carry

    lax.fori_loop(0, qi // 2, pair, 0)
    base = pl.multiple_of((qi // 2) * (2 * tq), 2 * tq)

    @pl.when(qi % 2 == 0)
    def _():
        stage(None, None, sa_ref, base, causal)

    @pl.when(qi % 2 == 1)
    def _():
        stage(sb_ref, base + tq, sa_ref, base, None)
        stage(None, None, sb_ref, base + tq, causal)

    outs = []
    for hh in range(HP):
        a = acc_ref[hh]
        outs.append((a[:MLA_V] / a[MLA_V:MLA_V + 1]).T)
    o_ref[...] = jnp.concatenate(outs, axis=1).astype(BF16)


def _mla(qm_t, km, vm_t, *, batch, seq):
    n = km.shape[0]
    tq = MLA_TQ
    nq = seq // tq
    HP = MLA_HP
    return pl.pallas_call(
        _mla_kernel,
        grid=(batch, MLA_HEADS // HP, nq),
        in_specs=[pl.BlockSpec((HP * LANE, tq), lambda b, hp, qi: (hp, b * nq + qi)),
                  pl.BlockSpec((seq, HP * LANE), lambda b, hp, qi: (b, hp)),
                  pl.BlockSpec((HP * VROWS, seq), lambda b, hp, qi: (hp, b))],
        out_specs=pl.BlockSpec((tq, HP * MLA_V), lambda b, hp, qi: (b * nq + qi, hp)),
        out_shape=jax.ShapeDtypeStruct((n, MLA_HEADS * MLA_V), BF16),
        scratch_shapes=[pltpu.VMEM((HP, VROWS, tq), F32),
                        pltpu.VMEM((HP, 1, tq), F32),
                        pltpu.VMEM((HP, tq, tq), F32),
                        pltpu.VMEM((HP, tq, tq), F32)],
        compiler_params=_params("parallel", "parallel", "arbitrary"),
        name="mla_attention",
    )(qm_t, km, vm_t)


def _tail_kernel(x_ref, om_ref, on_ref, p_ref, wo_ref, g1_ref, g2_ref, g3_ref,
                 wg_ref, wu_ref, cw_ref, cb_ref, wd_ref, pg_ref, pp_ref,
                 o_ref, carry_ref, act_ref, *, seq, tm):
    n_chunks = D_FF // FF_CHUNK
    half = om_ref.shape[1]
    y = _dot(om_ref[...], wo_ref[:half, :]) + _dot(on_ref[...], wo_ref[half:, :])
    x = x_ref[...] + _rms(y, g1_ref[...])

    h = _rms(x, g2_ref[...]).astype(BF16)

    @pl.when((pl.program_id(0) * tm) % seq == 0)
    def _():
        carry_ref[...] = jnp.zeros(carry_ref.shape, F32)

    SUB = 8
    row8 = lax.broadcasted_iota(jnp.int32, (SUB, 1), 0)
    for c in range(n_chunks):
        cols = slice(c * FF_CHUNK, (c + 1) * FF_CHUNK)
        g = _dot(h, wg_ref[:, cols])
        prev = carry_ref[:, cols]
        carry_ref[:, cols] = g[tm - SUB:, :]
        r1 = pltpu.roll(g, 1, 0)
        r2 = pltpu.roll(g, 2, 0)
        top1 = jnp.where(row8 == 0, prev[7:8, :], r1[:SUB])
        top2 = jnp.where(row8 == 0, prev[6:7, :], jnp.where(row8 == 1, prev[7:8, :], r2[:SUB]))
        g1 = jnp.concatenate([top1, r1[SUB:]], axis=0)
        g2 = jnp.concatenate([top2, r2[SUB:]], axis=0)
        conv = (cw_ref[0:1, cols] * g2 + cw_ref[1:2, cols] * g1 + cw_ref[2:3, cols] * g
                + cb_ref[:, cols])
        act_ref[:, cols] = (_gelu_tanh(conv) * _dot(h, wu_ref[:, cols])).astype(BF16)
    x = x + _rms(_dot(act_ref[...], wd_ref[...]), g3_ref[...])

    gate = _sigmoid(_dot(x.astype(BF16), pg_ref[...]))
    o_ref[...] = x + gate * _dot(p_ref[...].astype(BF16), pp_ref[...])


def _tail(x2, om, on, p2, wo, g1, g2, g3, wg, wu, cw, cb, wd, pg, pp, *, seq):
    n = x2.shape[0]
    tm = 1024
    row = lambda r: (r, 0)
    half = om.shape[1]
    return pl.pallas_call(
        functools.partial(_tail_kernel, seq=seq, tm=tm),
        grid=(n // tm,),
        in_specs=[pl.BlockSpec((tm, D_MODEL), row),
                  pl.BlockSpec((tm, half), row),
                  pl.BlockSpec((tm, half), row),
                  pl.BlockSpec((tm, PLE_DIM), row),
                  _resident((2 * half, D_MODEL)),
                  _resident((1, D_MODEL)),
                  _resident((1, D_MODEL)),
                  _resident((1, D_MODEL)),
                  _resident((D_MODEL, D_FF)),
                  _resident((D_MODEL, D_FF)),
                  _resident((CONV_WIDTH, D_FF)),
                  _resident((1, D_FF)),
                  _resident((D_FF, D_MODEL)),
                  _resident((D_MODEL, D_MODEL)),
                  _resident((PLE_DIM, D_MODEL))],
        out_specs=pl.BlockSpec((tm, D_MODEL), row),
        out_shape=jax.ShapeDtypeStruct((n, D_MODEL), F32),
        scratch_shapes=[pltpu.VMEM((8, D_FF), F32),
                        pltpu.VMEM((tm, D_FF), BF16)],
        compiler_params=_params("arbitrary"),
        name="layer_tail",
    )(x2, om, on, p2, wo, g1, g2, g3, wg, wu, cw, cb, wd, pg, pp)


def _rot_rows(w):
    half = w.shape[-2] // 2
    return jnp.concatenate([-w[..., half:, :], w[..., :half, :]], axis=-2)


def _prep_inproj(w):
    wt = w.T
    o = np.cumsum((0,) + IN_SPLITS)
    cq, ckv, kr, qn, kcmp, vcmp, kslc, vslc, kwin, vwin, gn = [wt[o[j]:o[j + 1]] for j in range(len(IN_SPLITS))]
    d = w.shape[0]

    def zeros(rows):
        return jnp.zeros((rows, d), F32)

    def kslots(m):
        return jnp.concatenate([m[:NSA_DIM], zeros(LANE - NSA_DIM), m[NSA_DIM:], zeros(LANE - NSA_DIM)], axis=0)

    def vslots(m):
        return jnp.concatenate([m[:NSA_DIM], zeros(VROWS - NSA_DIM), m[NSA_DIM:], zeros(VROWS - NSA_DIM)], axis=0)

    pad_rope = zeros(LANE - MLA_NOPE - MLA_ROPE)
    w1 = jnp.concatenate([cq, ckv,
                          zeros(MLA_NOPE), kr, pad_rope,
                          zeros(MLA_NOPE), _rot_rows(kr), pad_rope,
                          kslots(kslc), kslots(kwin), kcmp, vcmp], axis=0)
    assert w1.shape[0] == C_TOTAL
    per = 3 * NSA_GROUP
    w1t = jnp.concatenate([qn * NSA_DIM ** -0.5, vslots(vslc), vslots(vwin),
                           gn[:per], zeros(GATE_ROWS - per), gn[per:], zeros(GATE_ROWS - per)], axis=0)
    assert w1t.shape[0] == R_TOTAL
    return w1.astype(BF16).T, w1t.astype(BF16)


def _prep_mla(w_uq, w_ukv):
    r = w_uq.shape[0]
    dq = MLA_NOPE + MLA_ROPE
    ut = w_uq.T.reshape(MLA_HEADS, dq, r)
    pad = jnp.zeros((MLA_HEADS, LANE - dq, r), F32)
    wqa_t = jnp.concatenate([ut, pad], axis=1).reshape(MLA_HEADS * LANE, r)
    wqb_t = jnp.concatenate([jnp.zeros((MLA_HEADS, MLA_NOPE, r), F32), _rot_rows(ut[:, MLA_NOPE:]), pad],
                            axis=1).reshape(MLA_HEADS * LANE, r)
    rk = w_ukv.shape[0]
    kv = w_ukv.reshape(rk, MLA_HEADS, 2, MLA_NOPE)
    wk = jnp.concatenate([kv[:, :, 0, :], jnp.zeros((rk, MLA_HEADS, LANE - MLA_NOPE), F32)],
                         axis=-1).reshape(rk, MLA_HEADS * LANE)
    vt = kv[:, :, 1, :].transpose(1, 2, 0)
    wvt = jnp.concatenate([vt, jnp.zeros((MLA_HEADS, VROWS - MLA_V, rk), F32)], axis=1).reshape(MLA_HEADS * VROWS, rk)
    return wqa_t.astype(BF16), wqb_t.astype(BF16), wk.astype(BF16), wvt.astype(BF16)


def kernel(x, p, positions, rel_bias, attn_pre_norm, attn_post_norm, ffn_pre_norm, ffn_post_norm,
           w_in, mla_q_norm, mla_w_uq, mla_kv_norm, mla_w_ukv, nsa_cmp_pos, nsa_cmp_w1, nsa_cmp_w2,
           w_o, ffn_w_gate, ffn_w_up, ffn_conv_w, ffn_conv_b, ffn_w_down, ple_proj, ple_gate):
    batch, seq, d = x.shape
    depth = w_in.shape[0]
    n = batch * seq
    x2 = x.reshape(n, d)
    ctt, stt, ct, st = _rope_tables(positions)
    pat, utab = _bias_tables(rel_bias, seq)
    for i in range(depth):
        w1, w1t = _prep_inproj(w_in[i])
        wqa, wqb, wk, wvt = _prep_mla(mla_w_uq[i], mla_w_ukv[i])
        qm_t, km, vm_t, zq_t, nk, nv_t, kcmp, vcmp, gates_t = _inproj(
            x2, attn_pre_norm[i][None, :], w1, w1t, mla_q_norm[i][None, :], wqa, wqb,
            mla_kv_norm[i][None, :], wk, wvt, ct, st, ctt, stt, batch=batch, seq=seq)
        kc, vc_t = _compress(kcmp, vcmp,
                             nsa_cmp_pos[i].reshape(2, 1, CMP_LEN * NSA_DIM),
                             nsa_cmp_w1[i].astype(BF16), nsa_cmp_w2[i, 0].astype(BF16),
                             nsa_cmp_w2[i, 1].T.astype(BF16), batch=batch, seq=seq)
        o_nsa = _nsa(zq_t, kc, vc_t, nk, nv_t, gates_t, pat, utab, batch=batch, seq=seq)
        o_mla = _mla(qm_t, km, vm_t, batch=batch, seq=seq)
        wg = ffn_w_gate[i].astype(BF16)
        wu = ffn_w_up[i].astype(BF16)
        cw = ffn_conv_w[i]
        cb = ffn_conv_b[i][None, :]
        wd = ffn_w_down[i].astype(BF16)
        x2 = _tail(x2, o_mla, o_nsa, p[i].reshape(n, PLE_DIM), w_o[i].astype(BF16),
                   attn_post_norm[i][None, :], ffn_pre_norm[i][None, :], ffn_post_norm[i][None, :],
                   wg, wu, cw, cb, wd, ple_gate[i].astype(BF16), ple_proj[i].astype(BF16), seq=seq)
    return x2.reshape(batch, seq, d)
```

```python
import functools
import math

import numpy as np
import jax
import jax.numpy as jnp
from jax import lax
from jax.experimental import pallas as pl
from jax.experimental.pallas import tpu as pltpu

F32 = jnp.float32
BF16 = jnp.bfloat16

D_MODEL = 1024
DEPTH = 2
MLA_HEADS = 8
MLA_NOPE = 64
MLA_ROPE = 32
MLA_V = 64
MLA_Q_RANK = 256
MLA_KV_RANK = 128
ROPE_BASE = 10000.0
NSA_HEADS = 8
NSA_KV_HEADS = 2
NSA_GROUP = NSA_HEADS // NSA_KV_HEADS
NSA_DIM = 64
CMP_LEN = 32
CMP_STRIDE = 16
CMP_HIDDEN = 128
SLC_LEN = 64
SLC_TOPK = 16
WINDOW = 512
REL_BUCKETS = 32
REL_MAX_DIST = 128
D_FF = 2816
CONV_WIDTH = 3
PLE_DIM = 256
EPS = 1e-6
NEG = -1e30
POS_BIG = 1e30
LOG2E = math.log2(math.e)

IN_SPLITS = (MLA_Q_RANK, MLA_KV_RANK, MLA_ROPE, NSA_HEADS * NSA_DIM,
             NSA_KV_HEADS * NSA_DIM, NSA_KV_HEADS * NSA_DIM,
             NSA_KV_HEADS * NSA_DIM, NSA_KV_HEADS * NSA_DIM,
             NSA_KV_HEADS * NSA_DIM, NSA_KV_HEADS * NSA_DIM,
             3 * NSA_HEADS)

LANE = 128
QT = 256
KP = 128
SLC_PAD = 64
FF_CHUNK = 256
VMEM_LIMIT = 56 * 1024 * 1024

VROWS = 80

C_CQ = 0
C_CKV = 256
C_KR = 384
C_KRROT = 512
C_KSLC = 640
C_KWIN = 896
C_KCMP = 1152
C_VCMP = 1280
C_TOTAL = 1408
R_QN = 0
R_VSLC = 512
R_VWIN = R_VSLC + NSA_KV_HEADS * VROWS
R_GATE = R_VWIN + NSA_KV_HEADS * VROWS
GATE_ROWS = 16
R_TOTAL = R_GATE + NSA_KV_HEADS * GATE_ROWS


def _dot(a, b):
    return jnp.dot(a, b, preferred_element_type=F32)


def _dot_nt(a, b):
    return lax.dot_general(a, b, (((1,), (1,)), ((), ())), preferred_element_type=F32)


def _rms(x, g):
    return x * lax.rsqrt(jnp.mean(x * x, axis=-1, keepdims=True) + EPS) * g


def _gelu_tanh(x):
    return 0.5 * x * (1.0 + jnp.tanh(math.sqrt(2.0 / math.pi) * (x + 0.044715 * (x * x * x))))


def _sigmoid(x):
    return 1.0 / (1.0 + jnp.exp(-x))


def _params(*sem):
    return pltpu.CompilerParams(dimension_semantics=sem, vmem_limit_bytes=VMEM_LIMIT)


def _resident(shape):
    nd = len(shape)
    return pl.BlockSpec(shape, lambda *_: (0,) * nd, pipeline_mode=pl.Buffered(1))


def _layer_resident(layer, shape):
    nd = len(shape)
    return pl.BlockSpec((None,) + tuple(shape), lambda *_: (layer,) + (0,) * nd, pipeline_mode=pl.Buffered(1))


def _rope_kernel(pos_ref, inv_ref, ctt_ref, stt_ref, ct_ref, st_ref):
    ang = inv_ref[...] * pos_ref[...].astype(F32)
    row = lax.broadcasted_iota(jnp.int32, ang.shape, 0)
    rope = (row >= MLA_NOPE) & (row < MLA_NOPE + MLA_ROPE)
    ct = jnp.where(rope, jnp.cos(ang), jnp.where(row < MLA_NOPE, 1.0, 0.0))
    st = jnp.where(rope, jnp.sin(ang), 0.0)
    ctt_ref[...] = ct
    stt_ref[...] = st
    ct_ref[...] = ct.T
    st_ref[...] = st.T


def _rope_tables(positions):
    n = positions.size
    tn = 2048
    half = MLA_ROPE // 2
    inv = ROPE_BASE ** (-jnp.arange(half, dtype=F32) / half)
    inv_slot = jnp.concatenate([jnp.zeros((MLA_NOPE,), F32), inv, inv,
                                jnp.zeros((LANE - MLA_NOPE - MLA_ROPE,), F32)])[:, None]
    return pl.pallas_call(
        _rope_kernel,
        grid=(n // tn,),
        in_specs=[pl.BlockSpec((1, tn), lambda r: (0, r)),
                  pl.BlockSpec((LANE, 1), lambda r: (0, 0))],
        out_specs=[pl.BlockSpec((LANE, tn), lambda r: (0, r))] * 2
                  + [pl.BlockSpec((tn, LANE), lambda r: (r, 0))] * 2,
        out_shape=[jax.ShapeDtypeStruct((LANE, n), F32)] * 2 + [jax.ShapeDtypeStruct((n, LANE), F32)] * 2,
        compiler_params=_params("parallel"),
        name="rope_tables",
    )(positions.reshape(1, n), inv_slot)


def _bucket_np(dist):
    n = np.maximum(dist, 0)
    max_exact = REL_BUCKETS // 2
    large = max_exact + (np.log(np.maximum(n, 1).astype(np.float32) / max_exact)
                         / math.log(REL_MAX_DIST / max_exact)
                         * (REL_BUCKETS - max_exact)).astype(np.int32)
    large = np.minimum(large, REL_BUCKETS - 1)
    return np.where(n < max_exact, n, large).astype(np.int32)


def _bias_kernel(table_ref, bpc_ref, bpu_ref, pat_ref, u_ref):
    h = pl.program_id(0)
    far = table_ref[REL_BUCKETS - 1, h]

    def lookup(bp, sub):
        acc = jnp.full(bp.shape, far - sub, F32)
        for b in range(REL_BUCKETS - 1):
            acc = jnp.where(bp == b, table_ref[b, h] - sub, acc)
        return jnp.where(bp < 0, NEG, acc * LOG2E)

    pat_ref[0] = lookup(bpc_ref[...], 0.0)
    for d in range(3):
        u_ref[0, d] = lookup(bpu_ref[d], far)


def _bias_tables(rel_bias, seq):
    ncp = seq // CMP_STRIDE
    i = np.arange(QT)[None, :]
    cprime = np.arange(2 * ncp)[:, None] - ncp
    dist_c = i - CMP_STRIDE * cprime - (CMP_LEN - 1)
    bpc = np.where(dist_c >= 0, _bucket_np(dist_c), -1).astype(np.int32)
    j = np.arange(KP)[:, None]
    bpu = []
    for delta in (-1, 0, 1):
        dist = i - KP * delta - j
        bpu.append(np.where(dist >= 0, _bucket_np(dist), -1))
    bpu = np.stack(bpu).astype(np.int32)
    return pl.pallas_call(
        _bias_kernel,
        grid=(NSA_HEADS,),
        in_specs=[pl.BlockSpec(memory_space=pltpu.SMEM),
                  pl.BlockSpec((2 * ncp, QT), lambda h: (0, 0)),
                  pl.BlockSpec((3, KP, QT), lambda h: (0, 0, 0))],
        out_specs=[pl.BlockSpec((1, 2 * ncp, QT), lambda h: (h, 0, 0)),
                   pl.BlockSpec((1, 3, KP, QT), lambda h: (h // NSA_GROUP, 0, 0, h % NSA_GROUP))],
        out_shape=[jax.ShapeDtypeStruct((NSA_HEADS, 2 * ncp, QT), F32),
                   jax.ShapeDtypeStruct((NSA_KV_HEADS, 3, KP, NSA_GROUP * QT), F32)],
        compiler_params=_params("parallel"),
        name="bias_tables",
    )(rel_bias.astype(F32), jnp.asarray(bpc), jnp.asarray(bpu))


def _inproj_kernel(x_ref, g_ref, w1_ref, w1t_ref, qn_ref, wqa_ref, wqb_ref, kvn_ref, wk_ref, wvt_ref,
                   ct_ref, st_ref, ctt_ref, stt_ref, ones_n_ref, ones_m_ref,
                   qm_ref, km_ref, vm_ref, zq_ref, nk_ref, nv_ref, kcmp_ref, vcmp_ref, gate_ref,
                   *, seq, tm):
    h = _rms(x_ref[...], g_ref[...]).astype(BF16)

    def proj(c0, width):
        return _dot(h, w1_ref[:, c0:c0 + width])

    cq = _rms(proj(C_CQ, MLA_Q_RANK), qn_ref[...]).astype(BF16)
    qa = _dot_nt(wqa_ref[...], cq)
    qb = _dot_nt(wqb_ref[...], cq)
    scale = (MLA_NOPE + MLA_ROPE) ** -0.5 * LOG2E
    cts = ctt_ref[...] * scale
    sts = stt_ref[...] * scale
    for hh in range(MLA_HEADS):
        sl = slice(hh * LANE, (hh + 1) * LANE)
        qm_ref[sl, :] = (qa[sl] * cts + qb[sl] * sts).astype(BF16)

    ckv = _rms(proj(C_CKV, MLA_KV_RANK), kvn_ref[...]).astype(BF16)
    kr = proj(C_KR, LANE) * ct_ref[...] + proj(C_KRROT, LANE) * st_ref[...]
    kn = _dot(ckv, wk_ref[...])
    for hh in range(MLA_HEADS):
        sl = slice(hh * LANE, (hh + 1) * LANE)
        km_ref[:, sl] = (kn[:, sl] + kr).astype(BF16)
    vm_ref[...] = (_dot_nt(wvt_ref[...], ckv) + ones_m_ref[...]).astype(BF16)

    zt = _dot_nt(w1t_ref[...], h)
    zq_ref[...] = (zt[R_QN:R_VSLC] * LOG2E).astype(BF16)
    nv_ref[...] = (zt[R_VSLC:R_GATE] + ones_n_ref[...]).astype(BF16)
    gate_ref[...] = _sigmoid(zt[R_GATE:R_TOTAL])

    lane = lax.broadcasted_iota(jnp.int32, (tm, LANE), 1)
    s0 = (pl.program_id(0) * tm) % seq
    row = lax.broadcasted_iota(jnp.int32, (tm, LANE), 0)
    onehot = (lane - NSA_DIM == (s0 + row) // SLC_LEN).astype(F32)
    ksl = proj(C_KSLC, 2 * LANE)
    for hk in range(NSA_KV_HEADS):
        sl = slice(hk * LANE, (hk + 1) * LANE)
        nk_ref[:, sl] = (ksl[:, sl] + onehot).astype(BF16)
    nk_ref[:, 2 * LANE:] = proj(C_KWIN, 2 * LANE).astype(BF16)

    kcmp = proj(C_KCMP, LANE)
    vcmp = proj(C_VCMP, LANE)
    for hk in range(NSA_KV_HEADS):
        kcmp_ref[0, hk] = kcmp[:, hk * NSA_DIM:(hk + 1) * NSA_DIM]
        vcmp_ref[0, hk] = vcmp[:, hk * NSA_DIM:(hk + 1) * NSA_DIM]


def _ones_rows(n_slots):
    r = np.arange(n_slots * VROWS) % VROWS
    return jnp.asarray((r >= NSA_DIM).astype(np.float32)[:, None])


def _inproj(x2, g_pre, w1, w1t, qn, wqa, wqb, kvn, wk, wvt, ct, st, ctt, stt, *, batch, seq):
    n = x2.shape[0]
    tm = 1024
    tiles_per_seq = seq // tm
    row = lambda r: (r, 0)
    col = lambda r: (0, r)
    hm = lambda r: (r // tiles_per_seq, 0, r % tiles_per_seq, 0)
    nv_rows = 2 * NSA_KV_HEADS * VROWS
    vm_rows = MLA_HEADS * VROWS
    return pl.pallas_call(
        functools.partial(_inproj_kernel, seq=seq, tm=tm),
        grid=(n // tm,),
        in_specs=[pl.BlockSpec((tm, D_MODEL), row),
                  _resident((1, D_MODEL)),
                  _resident((D_MODEL, C_TOTAL)),
                  _resident((R_TOTAL, D_MODEL)),
                  _resident((1, MLA_Q_RANK)),
                  _resident((MLA_HEADS * LANE, MLA_Q_RANK)),
                  _resident((MLA_HEADS * LANE, MLA_Q_RANK)),
                  _resident((1, MLA_KV_RANK)),
                  _resident((MLA_KV_RANK, MLA_HEADS * LANE)),
                  _resident((vm_rows, MLA_KV_RANK)),
                  pl.BlockSpec((tm, LANE), row),
                  pl.BlockSpec((tm, LANE), row),
                  pl.BlockSpec((LANE, tm), col),
                  pl.BlockSpec((LANE, tm), col),
                  _resident((nv_rows, 1)),
                  _resident((vm_rows, 1))],
        out_specs=[pl.BlockSpec((MLA_HEADS * LANE, tm), col),
                   pl.BlockSpec((tm, MLA_HEADS * LANE), row),
                   pl.BlockSpec((vm_rows, tm), col),
                   pl.BlockSpec((NSA_HEADS * NSA_DIM, tm), col),
                   pl.BlockSpec((tm, 4 * LANE), row),
                   pl.BlockSpec((nv_rows, tm), col),
                   pl.BlockSpec((1, NSA_KV_HEADS, tm, NSA_DIM), hm),
                   pl.BlockSpec((1, NSA_KV_HEADS, tm, NSA_DIM), hm),
                   pl.BlockSpec((NSA_KV_HEADS * GATE_ROWS, tm), col)],
        out_shape=[jax.ShapeDtypeStruct((MLA_HEADS * LANE, n), BF16),
                   jax.ShapeDtypeStruct((n, MLA_HEADS * LANE), BF16),
                   jax.ShapeDtypeStruct((vm_rows, n), BF16),
                   jax.ShapeDtypeStruct((NSA_HEADS * NSA_DIM, n), BF16),
                   jax.ShapeDtypeStruct((n, 4 * LANE), BF16),
                   jax.ShapeDtypeStruct((nv_rows, n), BF16),
                   jax.ShapeDtypeStruct((batch, NSA_KV_HEADS, seq, NSA_DIM), F32),
                   jax.ShapeDtypeStruct((batch, NSA_KV_HEADS, seq, NSA_DIM), F32),
                   jax.ShapeDtypeStruct((NSA_KV_HEADS * GATE_ROWS, n), F32)],
        compiler_params=_params("parallel"),
        name="in_proj",
    )(x2, g_pre, w1, w1t, qn, wqa, wqb, kvn, wk, wvt, ct, st, ctt, stt,
      _ones_rows(2 * NSA_KV_HEADS), _ones_rows(MLA_HEADS))


def _compress_kernel(k_ref, v_ref, pos_ref, w1_ref, w2k_ref, w2vt_ref, kc_ref, vc_ref):
    ncp = kc_ref.shape[2]

    def hidden(x_ref, j):
        u = jnp.zeros((ncp, CMP_HIDDEN), F32)
        low = jnp.zeros((ncp, CMP_HIDDEN), F32)
        for l in range(CMP_STRIDE):
            x = x_ref[0, 0, pl.ds(l, ncp, stride=CMP_STRIDE), :]
            top = (x + pos_ref[j, l:l + 1, :]).astype(BF16)
            bot = (x + pos_ref[j, CMP_STRIDE + l:CMP_STRIDE + l + 1, :]).astype(BF16)
            u = u + _dot(top, w1_ref[j, l * NSA_DIM:(l + 1) * NSA_DIM, :])
            low = low + _dot(bot, w1_ref[j, (CMP_STRIDE + l) * NSA_DIM:(CMP_STRIDE + l + 1) * NSA_DIM, :])
        nxt = jnp.concatenate([low[1:], jnp.zeros((1, CMP_HIDDEN), F32)], axis=0)
        return _gelu_tanh(u + nxt).astype(BF16)

    kc_ref[0, 0] = _dot(hidden(k_ref, 0), w2k_ref[...]).astype(BF16)
    vc_ref[0, 0] = _dot_nt(w2vt_ref[...], hidden(v_ref, 1)).astype(BF16)


def _compress(kcmp, vcmp, pos, w1, w2k, w2vt, *, batch, seq):
    ncp = seq // CMP_STRIDE
    blk = lambda b, hk: (b, hk, 0, 0)
    return pl.pallas_call(
        _compress_kernel,
        grid=(batch, NSA_KV_HEADS),
        in_specs=[pl.BlockSpec((1, 1, seq, NSA_DIM), blk),
                  pl.BlockSpec((1, 1, seq, NSA_DIM), blk),
                  pl.BlockSpec((2, CMP_LEN, NSA_DIM), lambda b, hk: (0, 0, 0)),
                  pl.BlockSpec((2, CMP_LEN * NSA_DIM, CMP_HIDDEN), lambda b, hk: (0, 0, 0)),
                  pl.BlockSpec((CMP_HIDDEN, NSA_DIM), lambda b, hk: (0, 0)),
                  pl.BlockSpec((NSA_DIM, CMP_HIDDEN), lambda b, hk: (0, 0))],
        out_specs=[pl.BlockSpec((1, 1, ncp, NSA_DIM), blk),
                   pl.BlockSpec((1, 1, NSA_DIM, ncp), blk)],
        out_shape=[jax.ShapeDtypeStruct((batch, NSA_KV_HEADS, ncp, NSA_DIM), BF16),
                   jax.ShapeDtypeStruct((batch, NSA_KV_HEADS, NSA_DIM, ncp), BF16)],
        compiler_params=_params("parallel", "parallel"),
        name="nsa_compress",
    )(kcmp, vcmp, pos, w1, w2k, w2vt)


def _nsa_kernel(zq_ref, kc_ref, vc_ref, ksl_ref, vsl_ref, kw_ref, vw_ref, gate_ref,
                pat_ref, u_ref, ovt_ref, o_ref, imp_ref, acc_ref, m_ref, sa_ref, sb_ref):
    qi = pl.program_id(2)
    G = NSA_GROUP
    M = G * QT
    PPT = QT // KP
    q4 = zq_ref[...]
    qs = jnp.concatenate([q4[g * NSA_DIM:(g + 1) * NSA_DIM, :] for g in range(G)], axis=1)
    kc = kc_ref[0, 0]
    vc_t = vc_ref[0, 0]
    ovt = ovt_ref[...]
    ncp = kc.shape[0]
    qcol = qi * QT + (lax.broadcasted_iota(jnp.int32, (1, M), 1) & (QT - 1))
    u_m1 = u_ref[0, 0]
    u_0 = u_ref[0, 1]
    u_p1 = u_ref[0, 2]

    n_back = WINDOW // KP
    n_w = n_back + PPT
    qw = jnp.concatenate([qs, jnp.zeros_like(qs)], axis=0)
    pad_v = jnp.where(lax.broadcasted_iota(jnp.int32, (VROWS, KP), 0) >= NSA_DIM, 1.0, 0.0).astype(BF16)
    k_rows, v_cols = [], []
    for w in range(n_w):
        piece = PPT * qi - n_back + w
        off = pl.multiple_of(jnp.maximum(piece, 0) * KP, KP)
        kt = kw_ref[pl.ds(off, KP), :]
        vt = vw_ref[:, pl.ds(off, KP)]
        if w < n_back:
            kt = jnp.where(piece >= 0, kt, jnp.zeros_like(kt))
            vt = jnp.where(piece >= 0, vt, pad_v)
        k_rows.append(kt)
        v_cols.append(vt)
    s_w = _dot(jnp.concatenate(k_rows, axis=0), qw)

    slide = QT // CMP_STRIDE
    off_b = pl.multiple_of(ncp - slide * qi, slide)
    bias = jnp.concatenate([pat_ref[g, pl.ds(off_b, ncp), :] for g in range(G)], axis=1)
    s = _dot(kc, qs) + bias
    e = jnp.exp2(s - jnp.max(s, axis=0, keepdims=True))
    scale = jnp.where(qcol >= CMP_LEN - 1, 1.0 / jnp.sum(e, axis=0, keepdims=True), 0.0)
    p_cmp = (e * scale).astype(BF16)

    key_j = lax.broadcasted_iota(jnp.int32, (KP, M), 0)
    qry_i = lax.broadcasted_iota(jnp.int32, (KP, M), 1) & (QT - 1)
    pieces = [s_w[w * KP:(w + 1) * KP] for w in range(n_w)]
    for w in range(PPT):
        pieces[w] = pieces[w] + jnp.where(key_j > qry_i - w * KP, 0.0, NEG)
    pieces[n_back - 1] = pieces[n_back - 1] + u_m1
    pieces[n_back] = pieces[n_back] + u_0
    pieces[n_back + 1] = pieces[n_back + 1] + u_p1
    s_w = jnp.concatenate(pieces, axis=0)
    p_w = jnp.exp2(s_w - jnp.max(s_w, axis=0, keepdims=True)).astype(BF16)

    o_cmp = _dot(vc_t, p_cmp)
    imp_t = jnp.zeros((SLC_PAD, QT), F32)
    for g in range(G):
        imp_t = imp_t + _dot(ovt, p_cmp[:, g * QT:(g + 1) * QT])

    a_w = _dot(jnp.concatenate(v_cols, axis=1), p_w)
    o_win = a_w[:NSA_DIM] / a_w[NSA_DIM:NSA_DIM + 1]

    n_id = lax.broadcasted_iota(jnp.int32, (SLC_PAD, QT), 0)
    q_blk = (qi * QT + lax.broadcasted_iota(jnp.int32, (SLC_PAD, QT), 1)) // SLC_LEN
    forced = (n_id == 0) | (n_id == q_blk) | (n_id == q_blk - 1)
    imp = jnp.where(forced, POS_BIG, jnp.where(n_id > q_blk, NEG, imp_t))
    imp_ref[...] = imp
    SUB = 8
    slabs = [imp[v * SUB:(v + 1) * SUB] for v in range(SLC_PAD // SUB)]
    ranks = [jnp.zeros((SUB, QT), jnp.int32) for _ in slabs]
    sub_id = lax.broadcasted_iota(jnp.int32, (SUB, QT), 0)
    for m in range(SLC_PAD):
        other = imp_ref[m:m + 1, :]
        for v, slab in enumerate(slabs):
            if v > m // SUB:
                beats = (other >= slab).astype(jnp.int32)
            elif v < m // SUB:
                beats = (other > slab).astype(jnp.int32)
            else:
                beats = jnp.where(sub_id > m % SUB, (other >= slab).astype(jnp.int32),
                                  (other > slab).astype(jnp.int32))
            ranks[v] = ranks[v] + beats
    rank = jnp.concatenate(ranks, axis=0)
    selb = jnp.where(rank < SLC_TOPK, 0.0, NEG).astype(BF16)
    qaug = jnp.concatenate([qs, jnp.concatenate([selb] * G, axis=1)], axis=0)

    PIECES = 4
    TK = PIECES * KP
    n_t = (PPT * qi + PPT - 1) // PIECES + 1
    at_start = (PPT * qi) % PIECES == 0
    last_bias = [jnp.where(at_start, u_0, 0.0), jnp.where(at_start, u_p1, u_m1),
                 jnp.where(at_start, NEG, u_0), jnp.where(at_start, NEG, u_p1)]
    prev_bias = [None, None, None, jnp.where(at_start, u_m1, 0.0)]
    m_ref[...] = jnp.full(m_ref.shape, NEG, F32)
    acc_ref[...] = jnp.zeros(acc_ref.shape, F32)

    HALVES = 2
    MH = M // HALVES

    def qk(dst_ref, t, hf):
        off = pl.multiple_of(t * TK, TK)
        lanes = slice(hf * MH, (hf + 1) * MH)
        dst_ref[:, lanes] = _dot(ksl_ref[pl.ds(off, TK), :], qaug[:, lanes])

    def consume(src_ref, t, bias, hf):
        off = pl.multiple_of(t * TK, TK)
        lanes = slice(hf * MH, (hf + 1) * MH)
        s = src_ref[:, lanes]
        if bias is not None:
            s = jnp.concatenate([s[j * KP:(j + 1) * KP] if bias[j] is None
                                 else s[j * KP:(j + 1) * KP] + bias[j][:, lanes]
                                 for j in range(PIECES)], axis=0)
        m_old = m_ref[:, lanes]
        m_new = jnp.maximum(m_old, jnp.max(s, axis=0, keepdims=True))
        p = jnp.exp2(s - m_new).astype(BF16)
        acc_ref[:, lanes] = acc_ref[:, lanes] * jnp.exp2(m_old - m_new) + _dot(vsl_ref[:, pl.ds(off, TK)], p)
        m_ref[:, lanes] = m_new

    def stage(nxt_ref, nxt_t, cur_ref, cur_t, bias):
        for hf in range(HALVES):
            if nxt_ref is not None:
                qk(nxt_ref, nxt_t, hf)
            consume(cur_ref, cur_t, bias, hf)

    n_plain = jnp.maximum(n_t - 2, 0)
    for hf in range(HALVES):
        qk(sa_ref, 0, hf)

    def pair(u, carry):
        t = 2 * u
        stage(sb_ref, t + 1, sa_ref, t, None)
        stage(sa_ref, t + 2, sb_ref, t + 1, None)
        return carry

    lax.fori_loop(0, n_plain // 2, pair, 0)
    tb = (n_plain // 2) * 2

    @pl.when(n_t == 1)
    def _():
        stage(None, None, sa_ref, 0, last_bias)

    @pl.when((n_t >= 2) & (n_plain % 2 == 0))
    def _():
        stage(sb_ref, tb + 1, sa_ref, tb, prev_bias)
        stage(None, None, sb_ref, tb + 1, last_bias)

    @pl.when(n_plain % 2 == 1)
    def _():
        stage(sb_ref, tb + 1, sa_ref, tb, None)
        stage(sa_ref, tb + 2, sb_ref, tb + 1, prev_bias)
        stage(None, None, sa_ref, tb + 2, last_bias)

    a_s = acc_ref[...]
    o_sel = a_s[:NSA_DIM] / a_s[NSA_DIM:NSA_DIM + 1]

    gate = gate_ref[...]
    outs = []
    for g in range(G):
        cols = slice(g * QT, (g + 1) * QT)
        o_t = (gate[3 * g:3 * g + 1, :] * o_cmp[:, cols] + gate[3 * g + 1:3 * g + 2, :] * o_sel[:, cols]
               + gate[3 * g + 2:3 * g + 3, :] * o_win[:, cols])
        outs.append(o_t.T)
    o_ref[...] = jnp.concatenate(outs, axis=1).astype(BF16)


def _overlap_t(seq):
    ncp = seq // CMP_STRIDE
    n_cmp = (seq - CMP_LEN) // CMP_STRIDE + 1
    n_slc = seq // SLC_LEN
    cs = np.arange(ncp)[None, :] * CMP_STRIDE
    ss = np.arange(SLC_PAD)[:, None] * SLC_LEN
    ov = np.maximum(np.minimum(cs + CMP_LEN, ss + SLC_LEN) - np.maximum(cs, ss), 0).astype(np.float32) / CMP_LEN
    ov = ov * (np.arange(ncp)[None, :] < n_cmp) * (np.arange(SLC_PAD)[:, None] < n_slc)
    return jnp.asarray(ov, BF16)


def _nsa(zq_t, kc, vc_t, nk, nv_t, gates_t, pat, utab, *, batch, seq):
    n = nk.shape[0]
    nq = seq // QT
    ncp = seq // CMP_STRIDE
    G = NSA_GROUP
    assert seq // SLC_LEN <= SLC_PAD and seq // SLC_LEN >= SLC_TOPK and seq >= WINDOW
    qcol = lambda b, hk, qi: (hk, b * nq + qi)
    return pl.pallas_call(
        _nsa_kernel,
        grid=(batch, NSA_KV_HEADS, nq),
        in_specs=[pl.BlockSpec((G * NSA_DIM, QT), qcol),
                  pl.BlockSpec((1, 1, ncp, NSA_DIM), lambda b, hk, qi: (b, hk, 0, 0)),
                  pl.BlockSpec((1, 1, NSA_DIM, ncp), lambda b, hk, qi: (b, hk, 0, 0)),
                  pl.BlockSpec((seq, LANE), lambda b, hk, qi: (b, hk)),
                  pl.BlockSpec((VROWS, seq), lambda b, hk, qi: (hk, b)),
                  pl.BlockSpec((seq, LANE), lambda b, hk, qi: (b, NSA_KV_HEADS + hk)),
                  pl.BlockSpec((VROWS, seq), lambda b, hk, qi: (NSA_KV_HEADS + hk, b)),
                  pl.BlockSpec((GATE_ROWS, QT), qcol),
                  pl.BlockSpec((G, 2 * ncp, QT), lambda b, hk, qi: (hk, 0, 0)),
                  pl.BlockSpec((1, 3, KP, G * QT), lambda b, hk, qi: (hk, 0, 0, 0)),
                  pl.BlockSpec((SLC_PAD, ncp), lambda b, hk, qi: (0, 0))],
        out_specs=pl.BlockSpec((QT, G * NSA_DIM), lambda b, hk, qi: (b * nq + qi, hk)),
        out_shape=jax.ShapeDtypeStruct((n, NSA_HEADS * NSA_DIM), BF16),
        scratch_shapes=[pltpu.VMEM((SLC_PAD, QT), F32),
                        pltpu.VMEM((VROWS, G * QT), F32),
                        pltpu.VMEM((1, G * QT), F32),
                        pltpu.VMEM((4 * KP, G * QT), F32),
                        pltpu.VMEM((4 * KP, G * QT), F32)],
        compiler_params=_params("parallel", "parallel", "arbitrary"),
        name="nsa_attention",
    )(zq_t, kc, vc_t, nk, nv_t, nk, nv_t, gates_t, pat, utab, _overlap_t(seq))


MLA_TQ = 512
MLA_HP = 4


def _mla_kernel(q_ref, k_ref, v_ref, o_ref, acc_ref, m_ref, sa_ref, sb_ref):
    qi = pl.program_id(2)
    tq = MLA_TQ
    HP = MLA_HP
    qs = [q_ref[hh * LANE:(hh + 1) * LANE, :] for hh in range(HP)]
    m_ref[...] = jnp.full(m_ref.shape, NEG, F32)
    acc_ref[...] = jnp.zeros(acc_ref.shape, F32)

    def qk(dst_ref, off, hh):
        dst_ref[hh] = _dot(k_ref[pl.ds(off, tq), hh * LANE:(hh + 1) * LANE], qs[hh])

    def consume(src_ref, off, mask, hh):
        s = src_ref[hh]
        if mask is not None:
            s = s + mask
        vt = v_ref[hh * VROWS:(hh + 1) * VROWS, pl.ds(off, tq)]
        m_old = m_ref[hh]
        m_new = jnp.maximum(m_old, jnp.max(s, axis=0, keepdims=True))
        p = jnp.exp2(s - m_new).astype(BF16)
        acc_ref[hh] = acc_ref[hh] * jnp.exp2(m_old - m_new) + _dot(vt, p)
        m_ref[hh] = m_new

    def stage(nxt_ref, nxt_off, cur_ref, cur_off, mask):
        for hh in range(HP):
            if nxt_ref is not None:
                qk(nxt_ref, nxt_off, hh)
            consume(cur_ref, cur_off, mask, hh)

    key_j = lax.broadcasted_iota(jnp.int32, (tq, tq), 0)
    qry_i = lax.broadcasted_iota(jnp.int32, (tq, tq), 1)
    causal = jnp.where(key_j <= qry_i, 0.0, NEG)
    for hh in range(HP):
        qk(sa_ref, 0, hh)

    def pair(u, carry):
        off = pl.multiple_of(u * (2 * tq), 2 * tq)
        stage(sb_ref, off + tq, sa_ref, off, None)
        stage(sa_ref, off + 2 * tq, sb_ref, off + tq, None)
        return carry

    lax.fori_loop(0, qi // 2, pair, 0)
    base = pl.multiple_of((qi // 2) * (2 * tq), 2 * tq)

    @pl.when(qi % 2 == 0)
    def _():
        stage(None, None, sa_ref, base, causal)

    @pl.when(qi % 2 == 1)
    def _():
        stage(sb_ref, base + tq, sa_ref, base, None)
        stage(None, None, sb_ref, base + tq, causal)

    outs = []
    for hh in range(HP):
        a = acc_ref[hh]
        outs.append((a[:MLA_V] / a[MLA_V:MLA_V + 1]).T)
    o_ref[...] = jnp.concatenate(outs, axis=1).astype(BF16)


def _mla(qm_t, km, vm_t, *, batch, seq):
    n = km.shape[0]
    tq = MLA_TQ
    nq = seq // tq
    HP = MLA_HP
    return pl.pallas_call(
        _mla_kernel,
        grid=(batch, MLA_HEADS // HP, nq),
        in_specs=[pl.BlockSpec((HP * LANE, tq), lambda b, hp, qi: (hp, b * nq + qi)),
                  pl.BlockSpec((seq, HP * LANE), lambda b, hp, qi: (b, hp)),
                  pl.BlockSpec((HP * VROWS, seq), lambda b, hp, qi: (hp, b))],
        out_specs=pl.BlockSpec((tq, HP * MLA_V), lambda b, hp, qi: (b * nq + qi, hp)),
        out_shape=jax.ShapeDtypeStruct((n, MLA_HEADS * MLA_V), BF16),
        scratch_shapes=[pltpu.VMEM((HP, VROWS, tq), F32),
                        pltpu.VMEM((HP, 1, tq), F32),
                        pltpu.VMEM((HP, tq, tq), F32),
                        pltpu.VMEM((HP, tq, tq), F32)],
        compiler_params=_params("parallel", "parallel", "arbitrary"),
        name="mla_attention",
    )(qm_t, km, vm_t)


def _tail_kernel(x_ref, om_ref, on_ref, p_ref, wo_ref, g1_ref, g2_ref, g3_ref,
                 wg_ref, wu_ref, cw_ref, cb_ref, wd_ref, pg_ref, pp_ref,
                 o_ref, carry_ref, act_ref, *, seq, tm):
    n_chunks = D_FF // FF_CHUNK
    half = om_ref.shape[1]
    y = _dot(om_ref[...], wo_ref[:half, :]) + _dot(on_ref[...], wo_ref[half:, :])
    x = x_ref[...] + _rms(y, g1_ref[...])

    h = _rms(x, g2_ref[...]).astype(BF16)

    @pl.when((pl.program_id(0) * tm) % seq == 0)
    def _():
        carry_ref[...] = jnp.zeros(carry_ref.shape, F32)

    SUB = 8
    row8 = lax.broadcasted_iota(jnp.int32, (SUB, 1), 0)
    for c in range(n_chunks):
        cols = slice(c * FF_CHUNK, (c + 1) * FF_CHUNK)
        g = _dot(h, wg_ref[:, cols])
        prev = carry_ref[:, cols]
        carry_ref[:, cols] = g[tm - SUB:, :]
        r1 = pltpu.roll(g, 1, 0)
        r2 = pltpu.roll(g, 2, 0)
        top1 = jnp.where(row8 == 0, prev[7:8, :], r1[:SUB])
        top2 = jnp.where(row8 == 0, prev[6:7, :], jnp.where(row8 == 1, prev[7:8, :], r2[:SUB]))
        g1 = jnp.concatenate([top1, r1[SUB:]], axis=0)
        g2 = jnp.concatenate([top2, r2[SUB:]], axis=0)
        conv = (cw_ref[0:1, cols] * g2 + cw_ref[1:2, cols] * g1 + cw_ref[2:3, cols] * g
                + cb_ref[:, cols])
        act_ref[:, cols] = (_gelu_tanh(conv) * _dot(h, wu_ref[:, cols])).astype(BF16)
    x = x + _rms(_dot(act_ref[...], wd_ref[...]), g3_ref[...])

    gate = _sigmoid(_dot(x.astype(BF16), pg_ref[...]))
    o_ref[...] = x + gate * _dot(p_ref[...].astype(BF16), pp_ref[...])


def _tail(x2, om, on, p2, layer, wo, g1, g2, g3, wg, wu, cw, cb, wd, pg, pp, *, seq):
    n = x2.shape[0]
    tm = 1024
    row = lambda r: (r, 0)
    half = om.shape[1]
    lr = functools.partial(_layer_resident, layer)
    return pl.pallas_call(
        functools.partial(_tail_kernel, seq=seq, tm=tm),
        grid=(n // tm,),
        in_specs=[pl.BlockSpec((tm, D_MODEL), row),
                  pl.BlockSpec((tm, half), row),
                  pl.BlockSpec((tm, half), row),
                  pl.BlockSpec((None, tm, PLE_DIM), lambda r: (layer, r, 0)),
                  lr((2 * half, D_MODEL)),
                  lr((1, D_MODEL)),
                  lr((1, D_MODEL)),
                  lr((1, D_MODEL)),
                  lr((D_MODEL, D_FF)),
                  lr((D_MODEL, D_FF)),
                  lr((CONV_WIDTH, D_FF)),
                  lr((1, D_FF)),
                  lr((D_FF, D_MODEL)),
                  lr((D_MODEL, D_MODEL)),
                  lr((PLE_DIM, D_MODEL))],
        out_specs=pl.BlockSpec((tm, D_MODEL), row),
        out_shape=jax.ShapeDtypeStruct((n, D_MODEL), F32),
        scratch_shapes=[pltpu.VMEM((8, D_FF), F32),
                        pltpu.VMEM((tm, D_FF), BF16)],
        compiler_params=_params("arbitrary"),
        name="layer_tail",
    )(x2, om, on, p2, wo, g1, g2, g3, wg, wu, cw, cb, wd, pg, pp)


def _rot_rows(w):
    half = w.shape[-2] // 2
    return jnp.concatenate([-w[..., half:, :], w[..., :half, :]], axis=-2)


def _prep_inproj(w):
    wt = w.T
    o = np.cumsum((0,) + IN_SPLITS)
    cq, ckv, kr, qn, kcmp, vcmp, kslc, vslc, kwin, vwin, gn = [wt[o[j]:o[j + 1]] for j in range(len(IN_SPLITS))]
    d = w.shape[0]

    def zeros(rows):
        return jnp.zeros((rows, d), F32)

    def kslots(m):
        return jnp.concatenate([m[:NSA_DIM], zeros(LANE - NSA_DIM), m[NSA_DIM:], zeros(LANE - NSA_DIM)], axis=0)

    def vslots(m):
        return jnp.concatenate([m[:NSA_DIM], zeros(VROWS - NSA_DIM), m[NSA_DIM:], zeros(VROWS - NSA_DIM)], axis=0)

    pad_rope = zeros(LANE - MLA_NOPE - MLA_ROPE)
    w1 = jnp.concatenate([cq, ckv,
                          zeros(MLA_NOPE), kr, pad_rope,
                          zeros(MLA_NOPE), _rot_rows(kr), pad_rope,
                          kslots(kslc), kslots(kwin), kcmp, vcmp], axis=0)
    assert w1.shape[0] == C_TOTAL
    per = 3 * NSA_GROUP
    w1t = jnp.concatenate([qn * NSA_DIM ** -0.5, vslots(vslc), vslots(vwin),
                           gn[:per], zeros(GATE_ROWS - per), gn[per:], zeros(GATE_ROWS - per)], axis=0)
    assert w1t.shape[0] == R_TOTAL
    return w1.astype(BF16).T, w1t.astype(BF16)


def _prep_mla(w_uq, w_ukv):
    r = w_uq.shape[0]
    dq = MLA_NOPE + MLA_ROPE
    ut = w_uq.T.reshape(MLA_HEADS, dq, r)
    pad = jnp.zeros((MLA_HEADS, LANE - dq, r), F32)
    wqa_t = jnp.concatenate([ut, pad], axis=1).reshape(MLA_HEADS * LANE, r)
    wqb_t = jnp.concatenate([jnp.zeros((MLA_HEADS, MLA_NOPE, r), F32), _rot_rows(ut[:, MLA_NOPE:]), pad],
                            axis=1).reshape(MLA_HEADS * LANE, r)
    rk = w_ukv.shape[0]
    kv = w_ukv.reshape(rk, MLA_HEADS, 2, MLA_NOPE)
    wk = jnp.concatenate([kv[:, :, 0, :], jnp.zeros((rk, MLA_HEADS, LANE - MLA_NOPE), F32)],
                         axis=-1).reshape(rk, MLA_HEADS * LANE)
    vt = kv[:, :, 1, :].transpose(1, 2, 0)
    wvt = jnp.concatenate([vt, jnp.zeros((MLA_HEADS, VROWS - MLA_V, rk), F32)], axis=1).reshape(MLA_HEADS * VROWS, rk)
    return wqa_t.astype(BF16), wqb_t.astype(BF16), wk.astype(BF16), wvt.astype(BF16)


def kernel(x, p, positions, rel_bias, attn_pre_norm, attn_post_norm, ffn_pre_norm, ffn_post_norm,
           w_in, mla_q_norm, mla_w_uq, mla_kv_norm, mla_w_ukv, nsa_cmp_pos, nsa_cmp_w1, nsa_cmp_w2,
           w_o, ffn_w_gate, ffn_w_up, ffn_conv_w, ffn_conv_b, ffn_w_down, ple_proj, ple_gate):
    batch, seq, d = x.shape
    depth = w_in.shape[0]
    n = batch * seq
    x2 = x.reshape(n, d)
    ctt, stt, ct, st = _rope_tables(positions)
    pat, utab = _bias_tables(rel_bias, seq)
    tail_params = (w_o.astype(BF16), attn_post_norm[:, None, :], ffn_pre_norm[:, None, :],
                   ffn_post_norm[:, None, :], ffn_w_gate.astype(BF16), ffn_w_up.astype(BF16),
                   ffn_conv_w, ffn_conv_b[:, None, :], ffn_w_down.astype(BF16),
                   ple_gate.astype(BF16), ple_proj.astype(BF16))
    for i in range(depth):
        w1, w1t = _prep_inproj(w_in[i])
        wqa, wqb, wk, wvt = _prep_mla(mla_w_uq[i], mla_w_ukv[i])
        qm_t, km, vm_t, zq_t, nk, nv_t, kcmp, vcmp, gates_t = _inproj(
            x2, attn_pre_norm[i][None, :], w1, w1t, mla_q_norm[i][None, :], wqa, wqb,
            mla_kv_norm[i][None, :], wk, wvt, ct, st, ctt, stt, batch=batch, seq=seq)
        kc, vc_t = _compress(kcmp, vcmp,
                             nsa_cmp_pos[i],
                             nsa_cmp_w1[i].astype(BF16), nsa_cmp_w2[i, 0].astype(BF16),
                             nsa_cmp_w2[i, 1].T.astype(BF16), batch=batch, seq=seq)
        o_nsa = _nsa(zq_t, kc, vc_t, nk, nv_t, gates_t, pat, utab, batch=batch, seq=seq)
        o_mla = _mla(qm_t, km, vm_t, batch=batch, seq=seq)
        x2 = _tail(x2, o_mla, o_nsa, p.reshape(depth, n, PLE_DIM), i, *tail_params, seq=seq)
    return x2.reshape(batch, seq, d)
```

```python
import functools
import math

import numpy as np
import jax
import jax.numpy as jnp
from jax import lax
from jax.experimental import pallas as pl
from jax.experimental.pallas import tpu as pltpu

F32 = jnp.float32
BF16 = jnp.bfloat16

D_MODEL = 1024
DEPTH = 2
MLA_HEADS = 8
MLA_NOPE = 64
MLA_ROPE = 32
MLA_V = 64
MLA_Q_RANK = 256
MLA_KV_RANK = 128
ROPE_BASE = 10000.0
NSA_HEADS = 8
NSA_KV_HEADS = 2
NSA_GROUP = NSA_HEADS // NSA_KV_HEADS
NSA_DIM = 64
CMP_LEN = 32
CMP_STRIDE = 16
CMP_HIDDEN = 128
SLC_LEN = 64
SLC_TOPK = 16
WINDOW = 512
REL_BUCKETS = 32
REL_MAX_DIST = 128
D_FF = 2816
CONV_WIDTH = 3
PLE_DIM = 256
EPS = 1e-6
NEG = -1e30
POS_BIG = 1e30
LOG2E = math.log2(math.e)

IN_SPLITS = (MLA_Q_RANK, MLA_KV_RANK, MLA_ROPE, NSA_HEADS * NSA_DIM,
             NSA_KV_HEADS * NSA_DIM, NSA_KV_HEADS * NSA_DIM,
             NSA_KV_HEADS * NSA_DIM, NSA_KV_HEADS * NSA_DIM,
             NSA_KV_HEADS * NSA_DIM, NSA_KV_HEADS * NSA_DIM,
             3 * NSA_HEADS)

LANE = 128
QT = 256
KP = 128
SLC_PAD = 64
FF_CHUNK = 256
ROW_TILE = 1024
VMEM_LIMIT = 56 * 1024 * 1024

VROWS = 80

C_CQ = 0
C_CKV = 256
C_KR = 384
C_KRROT = 512
C_KSLC = 640
C_KWIN = 896
C_KCMP = 1152
C_VCMP = 1280
C_TOTAL = 1408
R_QN = 0
R_VSLC = 512
R_VWIN = R_VSLC + NSA_KV_HEADS * VROWS
R_GATE = R_VWIN + NSA_KV_HEADS * VROWS
GATE_ROWS = 16
R_TOTAL = R_GATE + NSA_KV_HEADS * GATE_ROWS


def _dot(a, b):
    return jnp.dot(a, b, preferred_element_type=F32)


def _dot_nt(a, b):
    return lax.dot_general(a, b, (((1,), (1,)), ((), ())), preferred_element_type=F32)


def _rms(x, g):
    return x * lax.rsqrt(jnp.mean(x * x, axis=-1, keepdims=True) + EPS) * g


def _gelu_tanh(x):
    return 0.5 * x * (1.0 + jnp.tanh(math.sqrt(2.0 / math.pi) * (x + 0.044715 * (x * x * x))))


def _sigmoid(x):
    return 1.0 / (1.0 + jnp.exp(-x))


def _params(*sem):
    return pltpu.CompilerParams(dimension_semantics=sem, vmem_limit_bytes=VMEM_LIMIT)


def _resident(shape):
    nd = len(shape)
    return pl.BlockSpec(shape, lambda *_: (0,) * nd, pipeline_mode=pl.Buffered(1))


def _layer_resident(layer, shape):
    nd = len(shape)
    return pl.BlockSpec((None,) + tuple(shape), lambda *_: (layer,) + (0,) * nd, pipeline_mode=pl.Buffered(1))


def _rope_kernel(pos_ref, inv_ref, ctt_ref, stt_ref, ct_ref, st_ref):
    ang = inv_ref[...] * pos_ref[...].astype(F32)
    row = lax.broadcasted_iota(jnp.int32, ang.shape, 0)
    rope = (row >= MLA_NOPE) & (row < MLA_NOPE + MLA_ROPE)
    ct = jnp.where(rope, jnp.cos(ang), jnp.where(row < MLA_NOPE, 1.0, 0.0))
    st = jnp.where(rope, jnp.sin(ang), 0.0)
    ctt_ref[...] = ct
    stt_ref[...] = st
    ct_ref[...] = ct.T
    st_ref[...] = st.T


def _rope_tables(positions):
    n = positions.size
    tn = 2048
    half = MLA_ROPE // 2
    inv = ROPE_BASE ** (-jnp.arange(half, dtype=F32) / half)
    inv_slot = jnp.concatenate([jnp.zeros((MLA_NOPE,), F32), inv, inv,
                                jnp.zeros((LANE - MLA_NOPE - MLA_ROPE,), F32)])[:, None]
    return pl.pallas_call(
        _rope_kernel,
        grid=(n // tn,),
        in_specs=[pl.BlockSpec((1, tn), lambda r: (0, r)),
                  pl.BlockSpec((LANE, 1), lambda r: (0, 0))],
        out_specs=[pl.BlockSpec((LANE, tn), lambda r: (0, r))] * 2
                  + [pl.BlockSpec((tn, LANE), lambda r: (r, 0))] * 2,
        out_shape=[jax.ShapeDtypeStruct((LANE, n), F32)] * 2 + [jax.ShapeDtypeStruct((n, LANE), F32)] * 2,
        compiler_params=_params("parallel"),
        name="rope_tables",
    )(positions.reshape(1, n), inv_slot)


def _bucket_np(dist):
    n = np.maximum(dist, 0)
    max_exact = REL_BUCKETS // 2
    large = max_exact + (np.log(np.maximum(n, 1).astype(np.float32) / max_exact)
                         / math.log(REL_MAX_DIST / max_exact)
                         * (REL_BUCKETS - max_exact)).astype(np.int32)
    large = np.minimum(large, REL_BUCKETS - 1)
    return np.where(n < max_exact, n, large).astype(np.int32)


def _bias_kernel(table_ref, bpc_ref, bpu_ref, pat_ref, u_ref):
    h = pl.program_id(0)
    far = table_ref[REL_BUCKETS - 1, h]

    def lookup(bp, sub):
        acc = jnp.full(bp.shape, far - sub, F32)
        for b in range(REL_BUCKETS - 1):
            acc = jnp.where(bp == b, table_ref[b, h] - sub, acc)
        return jnp.where(bp < 0, NEG, acc * LOG2E)

    pat_ref[0] = lookup(bpc_ref[...], 0.0)
    for d in range(3):
        u_ref[0, d] = lookup(bpu_ref[d], far)


def _bias_tables(rel_bias, seq):
    ncp = seq // CMP_STRIDE
    i = np.arange(QT)[None, :]
    cprime = np.arange(2 * ncp)[:, None] - ncp
    dist_c = i - CMP_STRIDE * cprime - (CMP_LEN - 1)
    bpc = np.where(dist_c >= 0, _bucket_np(dist_c), -1).astype(np.int32)
    j = np.arange(KP)[:, None]
    bpu = []
    for delta in (-1, 0, 1):
        dist = i - KP * delta - j
        bpu.append(np.where(dist >= 0, _bucket_np(dist), -1))
    bpu = np.stack(bpu).astype(np.int32)
    return pl.pallas_call(
        _bias_kernel,
        grid=(NSA_HEADS,),
        in_specs=[pl.BlockSpec(memory_space=pltpu.SMEM),
                  pl.BlockSpec((2 * ncp, QT), lambda h: (0, 0)),
                  pl.BlockSpec((3, KP, QT), lambda h: (0, 0, 0))],
        out_specs=[pl.BlockSpec((1, 2 * ncp, QT), lambda h: (h, 0, 0)),
                   pl.BlockSpec((1, 3, KP, QT), lambda h: (h // NSA_GROUP, 0, 0, h % NSA_GROUP))],
        out_shape=[jax.ShapeDtypeStruct((NSA_HEADS, 2 * ncp, QT), F32),
                   jax.ShapeDtypeStruct((NSA_KV_HEADS, 3, KP, NSA_GROUP * QT), F32)],
        compiler_params=_params("parallel"),
        name="bias_tables",
    )(rel_bias.astype(F32), jnp.asarray(bpc), jnp.asarray(bpu))


def _inproj_kernel(x_ref, g_ref, w1_ref, w1t_ref, qn_ref, wqa_ref, wqb_ref, kvn_ref, wk_ref, wvt_ref,
                   ct_ref, st_ref, ctt_ref, stt_ref, ones_n_ref, ones_m_ref,
                   qm_ref, km_ref, vm_ref, zq_ref, nk_ref, nv_ref, kcmp_ref, vcmp_ref, gate_ref,
                   *, seq, tm):
    h = _rms(x_ref[...], g_ref[...]).astype(BF16)

    def proj(c0, width):
        return _dot(h, w1_ref[:, c0:c0 + width])

    cq = _rms(proj(C_CQ, MLA_Q_RANK), qn_ref[...]).astype(BF16)
    qa = _dot_nt(wqa_ref[...], cq)
    qb = _dot_nt(wqb_ref[...], cq)
    scale = (MLA_NOPE + MLA_ROPE) ** -0.5 * LOG2E
    cts = ctt_ref[...] * scale
    sts = stt_ref[...] * scale
    for hh in range(MLA_HEADS):
        sl = slice(hh * LANE, (hh + 1) * LANE)
        qm_ref[sl, :] = (qa[sl] * cts + qb[sl] * sts).astype(BF16)

    ckv = _rms(proj(C_CKV, MLA_KV_RANK), kvn_ref[...]).astype(BF16)
    kr = proj(C_KR, LANE) * ct_ref[...] + proj(C_KRROT, LANE) * st_ref[...]
    kn = _dot(ckv, wk_ref[...])
    for hh in range(MLA_HEADS):
        sl = slice(hh * LANE, (hh + 1) * LANE)
        km_ref[:, sl] = (kn[:, sl] + kr).astype(BF16)
    vm_ref[...] = (_dot_nt(wvt_ref[...], ckv) + ones_m_ref[...]).astype(BF16)

    zt = _dot_nt(w1t_ref[...], h)
    zq_ref[...] = (zt[R_QN:R_VSLC] * LOG2E).astype(BF16)
    nv_ref[...] = (zt[R_VSLC:R_GATE] + ones_n_ref[...]).astype(BF16)
    gate_ref[...] = _sigmoid(zt[R_GATE:R_TOTAL])

    lane = lax.broadcasted_iota(jnp.int32, (tm, LANE), 1)
    s0 = (pl.program_id(0) * tm) % seq
    row = lax.broadcasted_iota(jnp.int32, (tm, LANE), 0)
    onehot = (lane - NSA_DIM == (s0 + row) // SLC_LEN).astype(F32)
    ksl = proj(C_KSLC, 2 * LANE)
    for hk in range(NSA_KV_HEADS):
        sl = slice(hk * LANE, (hk + 1) * LANE)
        nk_ref[:, sl] = (ksl[:, sl] + onehot).astype(BF16)
    nk_ref[:, 2 * LANE:] = proj(C_KWIN, 2 * LANE).astype(BF16)

    kcmp = proj(C_KCMP, LANE)
    vcmp = proj(C_VCMP, LANE)
    for hk in range(NSA_KV_HEADS):
        kcmp_ref[0, hk] = kcmp[:, hk * NSA_DIM:(hk + 1) * NSA_DIM]
        vcmp_ref[0, hk] = vcmp[:, hk * NSA_DIM:(hk + 1) * NSA_DIM]


def _ones_rows(n_slots):
    r = np.arange(n_slots * VROWS) % VROWS
    return jnp.asarray((r >= NSA_DIM).astype(np.float32)[:, None])


def _inproj(x2, g_pre, w1, w1t, qn, wqa, wqb, kvn, wk, wvt, ct, st, ctt, stt, *, batch, seq):
    n = x2.shape[0]
    tm = ROW_TILE
    tiles_per_seq = seq // tm
    row = lambda r: (r, 0)
    col = lambda r: (0, r)
    hm = lambda r: (r // tiles_per_seq, 0, r % tiles_per_seq, 0)
    nv_rows = 2 * NSA_KV_HEADS * VROWS
    vm_rows = MLA_HEADS * VROWS
    return pl.pallas_call(
        functools.partial(_inproj_kernel, seq=seq, tm=tm),
        grid=(n // tm,),
        in_specs=[pl.BlockSpec((tm, D_MODEL), row),
                  _resident((1, D_MODEL)),
                  _resident((D_MODEL, C_TOTAL)),
                  _resident((R_TOTAL, D_MODEL)),
                  _resident((1, MLA_Q_RANK)),
                  _resident((MLA_HEADS * LANE, MLA_Q_RANK)),
                  _resident((MLA_HEADS * LANE, MLA_Q_RANK)),
                  _resident((1, MLA_KV_RANK)),
                  _resident((MLA_KV_RANK, MLA_HEADS * LANE)),
                  _resident((vm_rows, MLA_KV_RANK)),
                  pl.BlockSpec((tm, LANE), row),
                  pl.BlockSpec((tm, LANE), row),
                  pl.BlockSpec((LANE, tm), col),
                  pl.BlockSpec((LANE, tm), col),
                  _resident((nv_rows, 1)),
                  _resident((vm_rows, 1))],
        out_specs=[pl.BlockSpec((MLA_HEADS * LANE, tm), col),
                   pl.BlockSpec((tm, MLA_HEADS * LANE), row),
                   pl.BlockSpec((vm_rows, tm), col),
                   pl.BlockSpec((NSA_HEADS * NSA_DIM, tm), col),
                   pl.BlockSpec((tm, 4 * LANE), row),
                   pl.BlockSpec((nv_rows, tm), col),
                   pl.BlockSpec((1, NSA_KV_HEADS, tm, NSA_DIM), hm),
                   pl.BlockSpec((1, NSA_KV_HEADS, tm, NSA_DIM), hm),
                   pl.BlockSpec((NSA_KV_HEADS * GATE_ROWS, tm), col)],
        out_shape=[jax.ShapeDtypeStruct((MLA_HEADS * LANE, n), BF16),
                   jax.ShapeDtypeStruct((n, MLA_HEADS * LANE), BF16),
                   jax.ShapeDtypeStruct((vm_rows, n), BF16),
                   jax.ShapeDtypeStruct((NSA_HEADS * NSA_DIM, n), BF16),
                   jax.ShapeDtypeStruct((n, 4 * LANE), BF16),
                   jax.ShapeDtypeStruct((nv_rows, n), BF16),
                   jax.ShapeDtypeStruct((batch, NSA_KV_HEADS, seq, NSA_DIM), F32),
                   jax.ShapeDtypeStruct((batch, NSA_KV_HEADS, seq, NSA_DIM), F32),
                   jax.ShapeDtypeStruct((NSA_KV_HEADS * GATE_ROWS, n), F32)],
        compiler_params=_params("parallel"),
        name="in_proj",
    )(x2, g_pre, w1, w1t, qn, wqa, wqb, kvn, wk, wvt, ct, st, ctt, stt,
      _ones_rows(2 * NSA_KV_HEADS), _ones_rows(MLA_HEADS))


def _compress_kernel(k_ref, v_ref, pos_ref, w1_ref, w2k_ref, w2vt_ref, kc_ref, vc_ref):
    ncp = kc_ref.shape[2]

    def hidden(x_ref, j):
        u = jnp.zeros((ncp, CMP_HIDDEN), F32)
        low = jnp.zeros((ncp, CMP_HIDDEN), F32)
        for l in range(CMP_STRIDE):
            x = x_ref[0, 0, pl.ds(l, ncp, stride=CMP_STRIDE), :]
            top = (x + pos_ref[j, l:l + 1, :]).astype(BF16)
            bot = (x + pos_ref[j, CMP_STRIDE + l:CMP_STRIDE + l + 1, :]).astype(BF16)
            u = u + _dot(top, w1_ref[j, l * NSA_DIM:(l + 1) * NSA_DIM, :])
            low = low + _dot(bot, w1_ref[j, (CMP_STRIDE + l) * NSA_DIM:(CMP_STRIDE + l + 1) * NSA_DIM, :])
        nxt = jnp.concatenate([low[1:], jnp.zeros((1, CMP_HIDDEN), F32)], axis=0)
        return _gelu_tanh(u + nxt).astype(BF16)

    kc_ref[0, 0] = _dot(hidden(k_ref, 0), w2k_ref[...]).astype(BF16)
    vc_ref[0, 0] = _dot_nt(w2vt_ref[...], hidden(v_ref, 1)).astype(BF16)


def _compress(kcmp, vcmp, pos, w1, w2k, w2vt, *, batch, seq):
    ncp = seq // CMP_STRIDE
    blk = lambda b, hk: (b, hk, 0, 0)
    return pl.pallas_call(
        _compress_kernel,
        grid=(batch, NSA_KV_HEADS),
        in_specs=[pl.BlockSpec((1, 1, seq, NSA_DIM), blk),
                  pl.BlockSpec((1, 1, seq, NSA_DIM), blk),
                  pl.BlockSpec((2, CMP_LEN, NSA_DIM), lambda b, hk: (0, 0, 0)),
                  pl.BlockSpec((2, CMP_LEN * NSA_DIM, CMP_HIDDEN), lambda b, hk: (0, 0, 0)),
                  pl.BlockSpec((CMP_HIDDEN, NSA_DIM), lambda b, hk: (0, 0)),
                  pl.BlockSpec((NSA_DIM, CMP_HIDDEN), lambda b, hk: (0, 0))],
        out_specs=[pl.BlockSpec((1, 1, ncp, NSA_DIM), blk),
                   pl.BlockSpec((1, 1, NSA_DIM, ncp), blk)],
        out_shape=[jax.ShapeDtypeStruct((batch, NSA_KV_HEADS, ncp, NSA_DIM), BF16),
                   jax.ShapeDtypeStruct((batch, NSA_KV_HEADS, NSA_DIM, ncp), BF16)],
        compiler_params=_params("parallel", "parallel"),
        name="nsa_compress",
    )(kcmp, vcmp, pos, w1, w2k, w2vt)


def _nsa_kernel(zq_ref, kc_ref, vc_ref, ksl_ref, vsl_ref, kw_ref, vw_ref, gate_ref,
                pat_ref, u_ref, ovt_ref, o_ref, imp_ref, acc_ref, m_ref, sa_ref, sb_ref):
    qi = pl.program_id(2)
    G = NSA_GROUP
    M = G * QT
    PPT = QT // KP
    q4 = zq_ref[...]
    qs = jnp.concatenate([q4[g * NSA_DIM:(g + 1) * NSA_DIM, :] for g in range(G)], axis=1)
    kc = kc_ref[0, 0]
    vc_t = vc_ref[0, 0]
    ovt = ovt_ref[...]
    ncp = kc.shape[0]
    qcol = qi * QT + (lax.broadcasted_iota(jnp.int32, (1, M), 1) & (QT - 1))
    u_m1 = u_ref[0, 0]
    u_0 = u_ref[0, 1]
    u_p1 = u_ref[0, 2]

    n_back = WINDOW // KP
    n_w = n_back + PPT
    qw = jnp.concatenate([qs, jnp.zeros_like(qs)], axis=0)
    pad_v = jnp.where(lax.broadcasted_iota(jnp.int32, (VROWS, KP), 0) >= NSA_DIM, 1.0, 0.0).astype(BF16)
    k_rows, v_cols = [], []
    for w in range(n_w):
        piece = PPT * qi - n_back + w
        off = pl.multiple_of(jnp.maximum(piece, 0) * KP, KP)
        kt = kw_ref[pl.ds(off, KP), :]
        vt = vw_ref[:, pl.ds(off, KP)]
        if w < n_back:
            kt = jnp.where(piece >= 0, kt, jnp.zeros_like(kt))
            vt = jnp.where(piece >= 0, vt, pad_v)
        k_rows.append(kt)
        v_cols.append(vt)
    s_w = _dot(jnp.concatenate(k_rows, axis=0), qw)

    slide = QT // CMP_STRIDE
    off_b = pl.multiple_of(ncp - slide * qi, slide)
    bias = jnp.concatenate([pat_ref[g, pl.ds(off_b, ncp), :] for g in range(G)], axis=1)
    s = _dot(kc, qs) + bias
    e = jnp.exp2(s - jnp.max(s, axis=0, keepdims=True))
    scale = jnp.where(qcol >= CMP_LEN - 1, 1.0 / jnp.sum(e, axis=0, keepdims=True), 0.0)
    p_cmp = (e * scale).astype(BF16)

    key_j = lax.broadcasted_iota(jnp.int32, (KP, M), 0)
    qry_i = lax.broadcasted_iota(jnp.int32, (KP, M), 1) & (QT - 1)
    pieces = [s_w[w * KP:(w + 1) * KP] for w in range(n_w)]
    for w in range(PPT):
        pieces[w] = pieces[w] + jnp.where(key_j > qry_i - w * KP, 0.0, NEG)
    pieces[n_back - 1] = pieces[n_back - 1] + u_m1
    pieces[n_back] = pieces[n_back] + u_0
    pieces[n_back + 1] = pieces[n_back + 1] + u_p1
    s_w = jnp.concatenate(pieces, axis=0)
    p_w = jnp.exp2(s_w - jnp.max(s_w, axis=0, keepdims=True)).astype(BF16)

    o_cmp = _dot(vc_t, p_cmp)
    imp_t = jnp.zeros((SLC_PAD, QT), F32)
    for g in range(G):
        imp_t = imp_t + _dot(ovt, p_cmp[:, g * QT:(g + 1) * QT])

    a_w = _dot(jnp.concatenate(v_cols, axis=1), p_w)
    o_win = a_w[:NSA_DIM] / a_w[NSA_DIM:NSA_DIM + 1]

    n_id = lax.broadcasted_iota(jnp.int32, (SLC_PAD, QT), 0)
    q_blk = (qi * QT + lax.broadcasted_iota(jnp.int32, (SLC_PAD, QT), 1)) // SLC_LEN
    forced = (n_id == 0) | (n_id == q_blk) | (n_id == q_blk - 1)
    imp = jnp.where(forced, POS_BIG, jnp.where(n_id > q_blk, NEG, imp_t))
    imp_ref[...] = imp
    SUB = 8
    slabs = [imp[v * SUB:(v + 1) * SUB] for v in range(SLC_PAD // SUB)]
    ranks = [jnp.zeros((SUB, QT), jnp.int32) for _ in slabs]
    sub_id = lax.broadcasted_iota(jnp.int32, (SUB, QT), 0)
    for m in range(SLC_PAD):
        other = imp_ref[m:m + 1, :]
        for v, slab in enumerate(slabs):
            if v > m // SUB:
                beats = (other >= slab).astype(jnp.int32)
            elif v < m // SUB:
                beats = (other > slab).astype(jnp.int32)
            else:
                beats = jnp.where(sub_id > m % SUB, (other >= slab).astype(jnp.int32),
                                  (other > slab).astype(jnp.int32))
            ranks[v] = ranks[v] + beats
    rank = jnp.concatenate(ranks, axis=0)
    selb = jnp.where(rank < SLC_TOPK, 0.0, NEG).astype(BF16)
    qaug = jnp.concatenate([qs, jnp.concatenate([selb] * G, axis=1)], axis=0)

    PIECES = 4
    TK = PIECES * KP
    n_t = (PPT * qi + PPT - 1) // PIECES + 1
    at_start = (PPT * qi) % PIECES == 0
    last_bias = [jnp.where(at_start, u_0, 0.0), jnp.where(at_start, u_p1, u_m1),
                 jnp.where(at_start, NEG, u_0), jnp.where(at_start, NEG, u_p1)]
    prev_bias = [None, None, None, jnp.where(at_start, u_m1, 0.0)]
    m_ref[...] = jnp.full(m_ref.shape, NEG, F32)
    acc_ref[...] = jnp.zeros(acc_ref.shape, F32)

    HALVES = 2
    MH = M // HALVES

    def qk(dst_ref, t, hf):
        off = pl.multiple_of(t * TK, TK)
        lanes = slice(hf * MH, (hf + 1) * MH)
        dst_ref[:, lanes] = _dot(ksl_ref[pl.ds(off, TK), :], qaug[:, lanes])

    def consume(src_ref, t, bias, hf):
        off = pl.multiple_of(t * TK, TK)
        lanes = slice(hf * MH, (hf + 1) * MH)
        s = src_ref[:, lanes]
        if bias is not None:
            s = jnp.concatenate([s[j * KP:(j + 1) * KP] if bias[j] is None
                                 else s[j * KP:(j + 1) * KP] + bias[j][:, lanes]
                                 for j in range(PIECES)], axis=0)
        m_old = m_ref[:, lanes]
        m_new = jnp.maximum(m_old, jnp.max(s, axis=0, keepdims=True))
        p = jnp.exp2(s - m_new).astype(BF16)
        acc_ref[:, lanes] = acc_ref[:, lanes] * jnp.exp2(m_old - m_new) + _dot(vsl_ref[:, pl.ds(off, TK)], p)
        m_ref[:, lanes] = m_new

    def stage(nxt_ref, nxt_t, cur_ref, cur_t, bias):
        for hf in range(HALVES):
            if nxt_ref is not None:
                qk(nxt_ref, nxt_t, hf)
            consume(cur_ref, cur_t, bias, hf)

    n_plain = jnp.maximum(n_t - 2, 0)
    for hf in range(HALVES):
        qk(sa_ref, 0, hf)

    def pair(u, carry):
        t = 2 * u
        stage(sb_ref, t + 1, sa_ref, t, None)
        stage(sa_ref, t + 2, sb_ref, t + 1, None)
        return carry

    lax.fori_loop(0, n_plain // 2, pair, 0)
    tb = (n_plain // 2) * 2

    @pl.when(n_t == 1)
    def _():
        stage(None, None, sa_ref, 0, last_bias)

    @pl.when((n_t >= 2) & (n_plain % 2 == 0))
    def _():
        stage(sb_ref, tb + 1, sa_ref, tb, prev_bias)
        stage(None, None, sb_ref, tb + 1, last_bias)

    @pl.when(n_plain % 2 == 1)
    def _():
        stage(sb_ref, tb + 1, sa_ref, tb, None)
        stage(sa_ref, tb + 2, sb_ref, tb + 1, prev_bias)
        stage(None, None, sa_ref, tb + 2, last_bias)

    a_s = acc_ref[...]
    o_sel = a_s[:NSA_DIM] / a_s[NSA_DIM:NSA_DIM + 1]

    gate = gate_ref[...]
    outs = []
    for g in range(G):
        cols = slice(g * QT, (g + 1) * QT)
        o_t = (gate[3 * g:3 * g + 1, :] * o_cmp[:, cols] + gate[3 * g + 1:3 * g + 2, :] * o_sel[:, cols]
               + gate[3 * g + 2:3 * g + 3, :] * o_win[:, cols])
        outs.append(o_t.T)
    o_ref[...] = jnp.concatenate(outs, axis=1).astype(BF16)


def _overlap_t(seq):
    ncp = seq // CMP_STRIDE
    n_cmp = (seq - CMP_LEN) // CMP_STRIDE + 1
    n_slc = seq // SLC_LEN
    cs = np.arange(ncp)[None, :] * CMP_STRIDE
    ss = np.arange(SLC_PAD)[:, None] * SLC_LEN
    ov = np.maximum(np.minimum(cs + CMP_LEN, ss + SLC_LEN) - np.maximum(cs, ss), 0).astype(np.float32) / CMP_LEN
    ov = ov * (np.arange(ncp)[None, :] < n_cmp) * (np.arange(SLC_PAD)[:, None] < n_slc)
    return jnp.asarray(ov, BF16)


def _nsa(zq_t, kc, vc_t, nk, nv_t, gates_t, pat, utab, *, batch, seq):
    n = nk.shape[0]
    nq = seq // QT
    ncp = seq // CMP_STRIDE
    G = NSA_GROUP
    assert seq // SLC_LEN <= SLC_PAD and seq // SLC_LEN >= SLC_TOPK and seq >= WINDOW
    qcol = lambda b, hk, qi: (hk, b * nq + qi)
    return pl.pallas_call(
        _nsa_kernel,
        grid=(batch, NSA_KV_HEADS, nq),
        in_specs=[pl.BlockSpec((G * NSA_DIM, QT), qcol),
                  pl.BlockSpec((1, 1, ncp, NSA_DIM), lambda b, hk, qi: (b, hk, 0, 0)),
                  pl.BlockSpec((1, 1, NSA_DIM, ncp), lambda b, hk, qi: (b, hk, 0, 0)),
                  pl.BlockSpec((seq, LANE), lambda b, hk, qi: (b, hk)),
                  pl.BlockSpec((VROWS, seq), lambda b, hk, qi: (hk, b)),
                  pl.BlockSpec((seq, LANE), lambda b, hk, qi: (b, NSA_KV_HEADS + hk)),
                  pl.BlockSpec((VROWS, seq), lambda b, hk, qi: (NSA_KV_HEADS + hk, b)),
                  pl.BlockSpec((GATE_ROWS, QT), qcol),
                  pl.BlockSpec((G, 2 * ncp, QT), lambda b, hk, qi: (hk, 0, 0)),
                  pl.BlockSpec((1, 3, KP, G * QT), lambda b, hk, qi: (hk, 0, 0, 0)),
                  pl.BlockSpec((SLC_PAD, ncp), lambda b, hk, qi: (0, 0))],
        out_specs=pl.BlockSpec((QT, G * NSA_DIM), lambda b, hk, qi: (b * nq + qi, hk)),
        out_shape=jax.ShapeDtypeStruct((n, NSA_HEADS * NSA_DIM), BF16),
        scratch_shapes=[pltpu.VMEM((SLC_PAD, QT), F32),
                        pltpu.VMEM((VROWS, G * QT), F32),
                        pltpu.VMEM((1, G * QT), F32),
                        pltpu.VMEM((4 * KP, G * QT), F32),
                        pltpu.VMEM((4 * KP, G * QT), F32)],
        compiler_params=_params("parallel", "parallel", "arbitrary"),
        name="nsa_attention",
    )(zq_t, kc, vc_t, nk, nv_t, nk, nv_t, gates_t, pat, utab, _overlap_t(seq))


MLA_TQ = 512
MLA_HP = 8


def _mla_kernel(q_ref, k_ref, v_ref, o_ref, acc_ref, m_ref, sa_ref, sb_ref):
    qi = pl.program_id(2)
    tq = MLA_TQ
    HP = MLA_HP
    qs = [q_ref[hh * LANE:(hh + 1) * LANE, :] for hh in range(HP)]
    m_ref[...] = jnp.full(m_ref.shape, NEG, F32)
    acc_ref[...] = jnp.zeros(acc_ref.shape, F32)

    def qk(dst_ref, off, hh):
        dst_ref[hh] = _dot(k_ref[pl.ds(off, tq), hh * LANE:(hh + 1) * LANE], qs[hh])

    def consume(src_ref, off, mask, hh):
        s = src_ref[hh]
        if mask is not None:
            s = s + mask
        vt = v_ref[hh * VROWS:(hh + 1) * VROWS, pl.ds(off, tq)]
        m_old = m_ref[hh]
        m_new = jnp.maximum(m_old, jnp.max(s, axis=0, keepdims=True))
        p = jnp.exp2(s - m_new).astype(BF16)
        acc_ref[hh] = acc_ref[hh] * jnp.exp2(m_old - m_new) + _dot(vt, p)
        m_ref[hh] = m_new

    def stage(nxt_ref, nxt_off, cur_ref, cur_off, mask):
        for hh in range(HP):
            if nxt_ref is not None:
                qk(nxt_ref, nxt_off, hh)
            consume(cur_ref, cur_off, mask, hh)

    key_j = lax.broadcasted_iota(jnp.int32, (tq, tq), 0)
    qry_i = lax.broadcasted_iota(jnp.int32, (tq, tq), 1)
    causal = jnp.where(key_j <= qry_i, 0.0, NEG)
    for hh in range(HP):
        qk(sa_ref, 0, hh)

    def pair(u, carry):
        off = pl.multiple_of(u * (2 * tq), 2 * tq)
        stage(sb_ref, off + tq, sa_ref, off, None)
        stage(sa_ref, off + 2 * tq, sb_ref, off + tq, None)
        return carry

    lax.fori_loop(0, qi // 2, pair, 0)
    base = pl.multiple_of((qi // 2) * (2 * tq), 2 * tq)

    @pl.when(qi % 2 == 0)
    def _():
        stage(None, None, sa_ref, base, causal)

    @pl.when(qi % 2 == 1)
    def _():
        stage(sb_ref, base + tq, sa_ref, base, None)
        stage(None, None, sb_ref, base + tq, causal)

    outs = []
    for hh in range(HP):
        a = acc_ref[hh]
        outs.append((a[:MLA_V] / a[MLA_V:MLA_V + 1]).T)
    o_ref[...] = jnp.concatenate(outs, axis=1).astype(BF16)


def _mla(qm_t, km, vm_t, *, batch, seq):
    n = km.shape[0]
    tq = MLA_TQ
    nq = seq // tq
    HP = MLA_HP
    return pl.pallas_call(
        _mla_kernel,
        grid=(batch, MLA_HEADS // HP, nq),
        in_specs=[pl.BlockSpec((HP * LANE, tq), lambda b, hp, qi: (hp, b * nq + qi)),
                  pl.BlockSpec((seq, HP * LANE), lambda b, hp, qi: (b, hp)),
                  pl.BlockSpec((HP * VROWS, seq), lambda b, hp, qi: (hp, b))],
        out_specs=pl.BlockSpec((tq, HP * MLA_V), lambda b, hp, qi: (b * nq + qi, hp)),
        out_shape=jax.ShapeDtypeStruct((n, MLA_HEADS * MLA_V), BF16),
        scratch_shapes=[pltpu.VMEM((HP, VROWS, tq), F32),
                        pltpu.VMEM((HP, 1, tq), F32),
                        pltpu.VMEM((HP, tq, tq), F32),
                        pltpu.VMEM((HP, tq, tq), F32)],
        compiler_params=_params("parallel", "parallel", "arbitrary"),
        name="mla_attention",
    )(qm_t, km, vm_t)


def _tail_kernel(x_ref, om_ref, on_ref, p_ref, wo_ref, g1_ref, g2_ref, g3_ref,
                 wg_ref, wu_ref, cw_ref, cb_ref, wd_ref, pg_ref, pp_ref,
                 o_ref, carry_ref, act_ref, *, seq, tm):
    n_chunks = D_FF // FF_CHUNK
    half = om_ref.shape[1]
    y = _dot(om_ref[...], wo_ref[:half, :]) + _dot(on_ref[...], wo_ref[half:, :])
    x = x_ref[...] + _rms(y, g1_ref[...])

    h = _rms(x, g2_ref[...]).astype(BF16)

    @pl.when((pl.program_id(0) * tm) % seq == 0)
    def _():
        carry_ref[...] = jnp.zeros(carry_ref.shape, F32)

    SUB = 8
    row8 = lax.broadcasted_iota(jnp.int32, (SUB, 1), 0)
    for c in range(n_chunks):
        cols = slice(c * FF_CHUNK, (c + 1) * FF_CHUNK)
        g = _dot(h, wg_ref[:, cols])
        prev = carry_ref[:, cols]
        carry_ref[:, cols] = g[tm - SUB:, :]
        r1 = pltpu.roll(g, 1, 0)
        r2 = pltpu.roll(g, 2, 0)
        top1 = jnp.where(row8 == 0, prev[7:8, :], r1[:SUB])
        top2 = jnp.where(row8 == 0, prev[6:7, :], jnp.where(row8 == 1, prev[7:8, :], r2[:SUB]))
        g1 = jnp.concatenate([top1, r1[SUB:]], axis=0)
        g2 = jnp.concatenate([top2, r2[SUB:]], axis=0)
        conv = (cw_ref[0:1, cols] * g2 + cw_ref[1:2, cols] * g1 + cw_ref[2:3, cols] * g
                + cb_ref[:, cols])
        act_ref[:, cols] = (_gelu_tanh(conv) * _dot(h, wu_ref[:, cols])).astype(BF16)
    x = x + _rms(_dot(act_ref[...], wd_ref[...]), g3_ref[...])

    gate = _sigmoid(_dot(x.astype(BF16), pg_ref[...]))
    o_ref[...] = x + gate * _dot(p_ref[...].astype(BF16), pp_ref[...])


def _tail(x2, om, on, p2, layer, wo, g1, g2, g3, wg, wu, cw, cb, wd, pg, pp, *, seq):
    n = x2.shape[0]
    tm = ROW_TILE
    row = lambda r: (r, 0)
    half = om.shape[1]
    lr = functools.partial(_layer_resident, layer)
    return pl.pallas_call(
        functools.partial(_tail_kernel, seq=seq, tm=tm),
        grid=(n // tm,),
        in_specs=[pl.BlockSpec((tm, D_MODEL), row),
                  pl.BlockSpec((tm, half), row),
                  pl.BlockSpec((tm, half), row),
                  pl.BlockSpec((None, tm, PLE_DIM), lambda r: (layer, r, 0)),
                  lr((2 * half, D_MODEL)),
                  lr((1, D_MODEL)),
                  lr((1, D_MODEL)),
                  lr((1, D_MODEL)),
                  lr((D_MODEL, D_FF)),
                  lr((D_MODEL, D_FF)),
                  lr((CONV_WIDTH, D_FF)),
                  lr((1, D_FF)),
                  lr((D_FF, D_MODEL)),
                  lr((D_MODEL, D_MODEL)),
                  lr((PLE_DIM, D_MODEL))],
        out_specs=pl.BlockSpec((tm, D_MODEL), row),
        out_shape=jax.ShapeDtypeStruct((n, D_MODEL), F32),
        scratch_shapes=[pltpu.VMEM((8, D_FF), F32),
                        pltpu.VMEM((tm, D_FF), BF16)],
        compiler_params=_params("arbitrary"),
        name="layer_tail",
    )(x2, om, on, p2, wo, g1, g2, g3, wg, wu, cw, cb, wd, pg, pp)


def _rot_rows(w):
    half = w.shape[-2] // 2
    return jnp.concatenate([-w[..., half:, :], w[..., :half, :]], axis=-2)


def _prep_inproj(w):
    wt = w.T
    o = np.cumsum((0,) + IN_SPLITS)
    cq, ckv, kr, qn, kcmp, vcmp, kslc, vslc, kwin, vwin, gn = [wt[o[j]:o[j + 1]] for j in range(len(IN_SPLITS))]
    d = w.shape[0]

    def zeros(rows):
        return jnp.zeros((rows, d), F32)

    def kslots(m):
        return jnp.concatenate([m[:NSA_DIM], zeros(LANE - NSA_DIM), m[NSA_DIM:], zeros(LANE - NSA_DIM)], axis=0)

    def vslots(m):
        return jnp.concatenate([m[:NSA_DIM], zeros(VROWS - NSA_DIM), m[NSA_DIM:], zeros(VROWS - NSA_DIM)], axis=0)

    pad_rope = zeros(LANE - MLA_NOPE - MLA_ROPE)
    w1 = jnp.concatenate([cq, ckv,
                          zeros(MLA_NOPE), kr, pad_rope,
                          zeros(MLA_NOPE), _rot_rows(kr), pad_rope,
                          kslots(kslc), kslots(kwin), kcmp, vcmp], axis=0)
    assert w1.shape[0] == C_TOTAL
    per = 3 * NSA_GROUP
    w1t = jnp.concatenate([qn * NSA_DIM ** -0.5, vslots(vslc), vslots(vwin),
                           gn[:per], zeros(GATE_ROWS - per), gn[per:], zeros(GATE_ROWS - per)], axis=0)
    assert w1t.shape[0] == R_TOTAL
    return w1.astype(BF16).T, w1t.astype(BF16)


def _prep_mla(w_uq, w_ukv):
    r = w_uq.shape[0]
    dq = MLA_NOPE + MLA_ROPE
    ut = w_uq.T.reshape(MLA_HEADS, dq, r)
    pad = jnp.zeros((MLA_HEADS, LANE - dq, r), F32)
    wqa_t = jnp.concatenate([ut, pad], axis=1).reshape(MLA_HEADS * LANE, r)
    wqb_t = jnp.concatenate([jnp.zeros((MLA_HEADS, MLA_NOPE, r), F32), _rot_rows(ut[:, MLA_NOPE:]), pad],
                            axis=1).reshape(MLA_HEADS * LANE, r)
    rk = w_ukv.shape[0]
    kv = w_ukv.reshape(rk, MLA_HEADS, 2, MLA_NOPE)
    wk = jnp.concatenate([kv[:, :, 0, :], jnp.zeros((rk, MLA_HEADS, LANE - MLA_NOPE), F32)],
                         axis=-1).reshape(rk, MLA_HEADS * LANE)
    vt = kv[:, :, 1, :].transpose(1, 2, 0)
    wvt = jnp.concatenate([vt, jnp.zeros((MLA_HEADS, VROWS - MLA_V, rk), F32)], axis=1).reshape(MLA_HEADS * VROWS, rk)
    return wqa_t.astype(BF16), wqb_t.astype(BF16), wk.astype(BF16), wvt.astype(BF16)


def kernel(x, p, positions, rel_bias, attn_pre_norm, attn_post_norm, ffn_pre_norm, ffn_post_norm,
           w_in, mla_q_norm, mla_w_uq, mla_kv_norm, mla_w_ukv, nsa_cmp_pos, nsa_cmp_w1, nsa_cmp_w2,
           w_o, ffn_w_gate, ffn_w_up, ffn_conv_w, ffn_conv_b, ffn_w_down, ple_proj, ple_gate):
    batch, seq, d = x.shape
    depth = w_in.shape[0]
    n = batch * seq
    x2 = x.reshape(n, d)
    ctt, stt, ct, st = _rope_tables(positions)
    pat, utab = _bias_tables(rel_bias, seq)
    tail_params = (w_o.astype(BF16), attn_post_norm[:, None, :], ffn_pre_norm[:, None, :],
                   ffn_post_norm[:, None, :], ffn_w_gate.astype(BF16), ffn_w_up.astype(BF16),
                   ffn_conv_w, ffn_conv_b[:, None, :], ffn_w_down.astype(BF16),
                   ple_gate.astype(BF16), ple_proj.astype(BF16))
    for i in range(depth):
        w1, w1t = _prep_inproj(w_in[i])
        wqa, wqb, wk, wvt = _prep_mla(mla_w_uq[i], mla_w_ukv[i])
        qm_t, km, vm_t, zq_t, nk, nv_t, kcmp, vcmp, gates_t = _inproj(
            x2, attn_pre_norm[i][None, :], w1, w1t, mla_q_norm[i][None, :], wqa, wqb,
            mla_kv_norm[i][None, :], wk, wvt, ct, st, ctt, stt, batch=batch, seq=seq)
        kc, vc_t = _compress(kcmp, vcmp,
                             nsa_cmp_pos[i],
                             nsa_cmp_w1[i].astype(BF16), nsa_cmp_w2[i, 0].astype(BF16),
                             nsa_cmp_w2[i, 1].T.astype(BF16), batch=batch, seq=seq)
        o_nsa = _nsa(zq_t, kc, vc_t, nk, nv_t, gates_t, pat, utab, batch=batch, seq=seq)
        o_mla = _mla(qm_t, km, vm_t, batch=batch, seq=seq)
        x2 = _tail(x2, o_mla, o_nsa, p.reshape(depth, n, PLE_DIM), i, *tail_params, seq=seq)
    return x2.reshape(batch, seq, d)
```

```python
import functools
import math

import numpy as np
import jax
import jax.numpy as jnp
from jax import lax
from jax.experimental import pallas as pl
from jax.experimental.pallas import tpu as pltpu

F32 = jnp.float32
BF16 = jnp.bfloat16

D_MODEL = 1024
DEPTH = 2
MLA_HEADS = 8
MLA_NOPE = 64
MLA_ROPE = 32
MLA_V = 64
MLA_Q_RANK = 256
MLA_KV_RANK = 128
ROPE_BASE = 10000.0
NSA_HEADS = 8
NSA_KV_HEADS = 2
NSA_GROUP = NSA_HEADS // NSA_KV_HEADS
NSA_DIM = 64
CMP_LEN = 32
CMP_STRIDE = 16
CMP_HIDDEN = 128
SLC_LEN = 64
SLC_TOPK = 16
WINDOW = 512
REL_BUCKETS = 32
REL_MAX_DIST = 128
D_FF = 2816
CONV_WIDTH = 3
PLE_DIM = 256
EPS = 1e-6
NEG = -1e30
POS_BIG = 1e30
LOG2E = math.log2(math.e)

IN_SPLITS = (MLA_Q_RANK, MLA_KV_RANK, MLA_ROPE, NSA_HEADS * NSA_DIM,
             NSA_KV_HEADS * NSA_DIM, NSA_KV_HEADS * NSA_DIM,
             NSA_KV_HEADS * NSA_DIM, NSA_KV_HEADS * NSA_DIM,
             NSA_KV_HEADS * NSA_DIM, NSA_KV_HEADS * NSA_DIM,
             3 * NSA_HEADS)

LANE = 128
QT = 256
KP = 128
SLC_PAD = 64
FF_CHUNK = 256
ROW_TILE = 1024
VMEM_LIMIT = 56 * 1024 * 1024

VROWS = 80

C_CQ = 0
C_CKV = 256
C_KR = 384
C_KRROT = 512
C_KSLC = 640
C_KWIN = 896
C_KCMP = 1152
C_VCMP = 1280
C_TOTAL = 1408
R_QN = 0
R_VSLC = 512
R_VWIN = R_VSLC + NSA_KV_HEADS * VROWS
R_GATE = R_VWIN + NSA_KV_HEADS * VROWS
GATE_ROWS = 16
R_TOTAL = R_GATE + NSA_KV_HEADS * GATE_ROWS


def _dot(a, b):
    return jnp.dot(a, b, preferred_element_type=F32)


def _dot_nt(a, b):
    return lax.dot_general(a, b, (((1,), (1,)), ((), ())), preferred_element_type=F32)


def _rms(x, g):
    return x * lax.rsqrt(jnp.mean(x * x, axis=-1, keepdims=True) + EPS) * g


def _gelu_tanh(x):
    return 0.5 * x * (1.0 + jnp.tanh(math.sqrt(2.0 / math.pi) * (x + 0.044715 * (x * x * x))))


def _sigmoid(x):
    return 1.0 / (1.0 + jnp.exp(-x))


def _params(*sem):
    return pltpu.CompilerParams(dimension_semantics=sem, vmem_limit_bytes=VMEM_LIMIT)


def _resident(shape):
    nd = len(shape)
    return pl.BlockSpec(shape, lambda *_: (0,) * nd, pipeline_mode=pl.Buffered(1))


def _layer_resident(layer, shape):
    nd = len(shape)
    return pl.BlockSpec((None,) + tuple(shape), lambda *_: (layer,) + (0,) * nd, pipeline_mode=pl.Buffered(1))


def _rope_kernel(pos_ref, inv_ref, ctt_ref, stt_ref, ct_ref, st_ref):
    ang = inv_ref[...] * pos_ref[...].astype(F32)
    row = lax.broadcasted_iota(jnp.int32, ang.shape, 0)
    rope = (row >= MLA_NOPE) & (row < MLA_NOPE + MLA_ROPE)
    ct = jnp.where(rope, jnp.cos(ang), jnp.where(row < MLA_NOPE, 1.0, 0.0))
    st = jnp.where(rope, jnp.sin(ang), 0.0)
    ctt_ref[...] = ct
    stt_ref[...] = st
    ct_ref[...] = ct.T
    st_ref[...] = st.T


def _rope_tables(positions):
    n = positions.size
    tn = 2048
    half = MLA_ROPE // 2
    inv = ROPE_BASE ** (-jnp.arange(half, dtype=F32) / half)
    inv_slot = jnp.concatenate([jnp.zeros((MLA_NOPE,), F32), inv, inv,
                                jnp.zeros((LANE - MLA_NOPE - MLA_ROPE,), F32)])[:, None]
    return pl.pallas_call(
        _rope_kernel,
        grid=(n // tn,),
        in_specs=[pl.BlockSpec((1, tn), lambda r: (0, r)),
                  pl.BlockSpec((LANE, 1), lambda r: (0, 0))],
        out_specs=[pl.BlockSpec((LANE, tn), lambda r: (0, r))] * 2
                  + [pl.BlockSpec((tn, LANE), lambda r: (r, 0))] * 2,
        out_shape=[jax.ShapeDtypeStruct((LANE, n), F32)] * 2 + [jax.ShapeDtypeStruct((n, LANE), F32)] * 2,
        compiler_params=_params("parallel"),
        name="rope_tables",
    )(positions.reshape(1, n), inv_slot)


def _bucket_np(dist):
    n = np.maximum(dist, 0)
    max_exact = REL_BUCKETS // 2
    large = max_exact + (np.log(np.maximum(n, 1).astype(np.float32) / max_exact)
                         / math.log(REL_MAX_DIST / max_exact)
                         * (REL_BUCKETS - max_exact)).astype(np.int32)
    large = np.minimum(large, REL_BUCKETS - 1)
    return np.where(n < max_exact, n, large).astype(np.int32)


def _bias_kernel(table_ref, bpc_ref, bpu_ref, pat_ref, u_ref):
    h = pl.program_id(0)
    far = table_ref[REL_BUCKETS - 1, h]

    def lookup(bp, sub):
        acc = jnp.full(bp.shape, far - sub, F32)
        for b in range(REL_BUCKETS - 1):
            acc = jnp.where(bp == b, table_ref[b, h] - sub, acc)
        return jnp.where(bp < 0, NEG, acc * LOG2E)

    pat_ref[0] = lookup(bpc_ref[...], 0.0)
    for d in range(3):
        u_ref[0, d] = lookup(bpu_ref[d], far)


def _bias_tables(rel_bias, seq):
    ncp = seq // CMP_STRIDE
    i = np.arange(QT)[None, :]
    cprime = np.arange(2 * ncp)[:, None] - ncp
    dist_c = i - CMP_STRIDE * cprime - (CMP_LEN - 1)
    bpc = np.where(dist_c >= 0, _bucket_np(dist_c), -1).astype(np.int32)
    j = np.arange(KP)[:, None]
    bpu = []
    for delta in (-1, 0, 1):
        dist = i - KP * delta - j
        bpu.append(np.where(dist >= 0, _bucket_np(dist), -1))
    bpu = np.stack(bpu).astype(np.int32)
    return pl.pallas_call(
        _bias_kernel,
        grid=(NSA_HEADS,),
        in_specs=[pl.BlockSpec(memory_space=pltpu.SMEM),
                  pl.BlockSpec((2 * ncp, QT), lambda h: (0, 0)),
                  pl.BlockSpec((3, KP, QT), lambda h: (0, 0, 0))],
        out_specs=[pl.BlockSpec((1, 2 * ncp, QT), lambda h: (h, 0, 0)),
                   pl.BlockSpec((1, 3, KP, QT), lambda h: (h // NSA_GROUP, 0, 0, h % NSA_GROUP))],
        out_shape=[jax.ShapeDtypeStruct((NSA_HEADS, 2 * ncp, QT), F32),
                   jax.ShapeDtypeStruct((NSA_KV_HEADS, 3, KP, NSA_GROUP * QT), F32)],
        compiler_params=_params("parallel"),
        name="bias_tables",
    )(rel_bias.astype(F32), jnp.asarray(bpc), jnp.asarray(bpu))


def _inproj_kernel(x_ref, g_ref, w1_ref, w1t_ref, qn_ref, wqa_ref, wqb_ref, kvn_ref, wk_ref, wvt_ref,
                   ct_ref, st_ref, ctt_ref, stt_ref, ones_n_ref, ones_m_ref,
                   qm_ref, km_ref, vm_ref, zq_ref, nk_ref, nv_ref, kcmp_ref, vcmp_ref, gate_ref,
                   *, seq, tm):
    h = _rms(x_ref[...], g_ref[...]).astype(BF16)

    def proj(c0, width):
        return _dot(h, w1_ref[:, c0:c0 + width])

    cq = _rms(proj(C_CQ, MLA_Q_RANK), qn_ref[...]).astype(BF16)
    qa = _dot_nt(wqa_ref[...], cq)
    qb = _dot_nt(wqb_ref[...], cq)
    scale = (MLA_NOPE + MLA_ROPE) ** -0.5 * LOG2E
    cts = ctt_ref[...] * scale
    sts = stt_ref[...] * scale
    for hh in range(MLA_HEADS):
        sl = slice(hh * LANE, (hh + 1) * LANE)
        qm_ref[sl, :] = (qa[sl] * cts + qb[sl] * sts).astype(BF16)

    ckv = _rms(proj(C_CKV, MLA_KV_RANK), kvn_ref[...]).astype(BF16)
    kr = proj(C_KR, LANE) * ct_ref[...] + proj(C_KRROT, LANE) * st_ref[...]
    kn = _dot(ckv, wk_ref[...])
    for hh in range(MLA_HEADS):
        sl = slice(hh * LANE, (hh + 1) * LANE)
        km_ref[:, sl] = (kn[:, sl] + kr).astype(BF16)
    vm_ref[...] = (_dot_nt(wvt_ref[...], ckv) + ones_m_ref[...]).astype(BF16)

    zt = _dot_nt(w1t_ref[...], h)
    zq_ref[...] = (zt[R_QN:R_VSLC] * LOG2E).astype(BF16)
    nv_ref[...] = (zt[R_VSLC:R_GATE] + ones_n_ref[...]).astype(BF16)
    gate_ref[...] = _sigmoid(zt[R_GATE:R_TOTAL])

    lane = lax.broadcasted_iota(jnp.int32, (tm, LANE), 1)
    s0 = (pl.program_id(0) * tm) % seq
    row = lax.broadcasted_iota(jnp.int32, (tm, LANE), 0)
    onehot = (lane - NSA_DIM == (s0 + row) // SLC_LEN).astype(F32)
    ksl = proj(C_KSLC, 2 * LANE)
    for hk in range(NSA_KV_HEADS):
        sl = slice(hk * LANE, (hk + 1) * LANE)
        nk_ref[:, sl] = (ksl[:, sl] + onehot).astype(BF16)
    nk_ref[:, 2 * LANE:] = proj(C_KWIN, 2 * LANE).astype(BF16)

    kcmp = proj(C_KCMP, LANE)
    vcmp = proj(C_VCMP, LANE)
    for hk in range(NSA_KV_HEADS):
        kcmp_ref[0, hk] = kcmp[:, hk * NSA_DIM:(hk + 1) * NSA_DIM]
        vcmp_ref[0, hk] = vcmp[:, hk * NSA_DIM:(hk + 1) * NSA_DIM]


def _ones_rows(n_slots):
    r = np.arange(n_slots * VROWS) % VROWS
    return jnp.asarray((r >= NSA_DIM).astype(np.float32)[:, None])


def _inproj(x2, g_pre, w1, w1t, qn, wqa, wqb, kvn, wk, wvt, ct, st, ctt, stt, *, batch, seq):
    n = x2.shape[0]
    tm = ROW_TILE
    tiles_per_seq = seq // tm
    row = lambda r: (r, 0)
    col = lambda r: (0, r)
    hm = lambda r: (r // tiles_per_seq, 0, r % tiles_per_seq, 0)
    nv_rows = 2 * NSA_KV_HEADS * VROWS
    vm_rows = MLA_HEADS * VROWS
    return pl.pallas_call(
        functools.partial(_inproj_kernel, seq=seq, tm=tm),
        grid=(n // tm,),
        in_specs=[pl.BlockSpec((tm, D_MODEL), row),
                  _resident((1, D_MODEL)),
                  _resident((D_MODEL, C_TOTAL)),
                  _resident((R_TOTAL, D_MODEL)),
                  _resident((1, MLA_Q_RANK)),
                  _resident((MLA_HEADS * LANE, MLA_Q_RANK)),
                  _resident((MLA_HEADS * LANE, MLA_Q_RANK)),
                  _resident((1, MLA_KV_RANK)),
                  _resident((MLA_KV_RANK, MLA_HEADS * LANE)),
                  _resident((vm_rows, MLA_KV_RANK)),
                  pl.BlockSpec((tm, LANE), row),
                  pl.BlockSpec((tm, LANE), row),
                  pl.BlockSpec((LANE, tm), col),
                  pl.BlockSpec((LANE, tm), col),
                  _resident((nv_rows, 1)),
                  _resident((vm_rows, 1))],
        out_specs=[pl.BlockSpec((MLA_HEADS * LANE, tm), col),
                   pl.BlockSpec((tm, MLA_HEADS * LANE), row),
                   pl.BlockSpec((vm_rows, tm), col),
                   pl.BlockSpec((NSA_HEADS * NSA_DIM, tm), col),
                   pl.BlockSpec((tm, 4 * LANE), row),
                   pl.BlockSpec((nv_rows, tm), col),
                   pl.BlockSpec((1, NSA_KV_HEADS, tm, NSA_DIM), hm),
                   pl.BlockSpec((1, NSA_KV_HEADS, tm, NSA_DIM), hm),
                   pl.BlockSpec((NSA_KV_HEADS * GATE_ROWS, tm), col)],
        out_shape=[jax.ShapeDtypeStruct((MLA_HEADS * LANE, n), BF16),
                   jax.ShapeDtypeStruct((n, MLA_HEADS * LANE), BF16),
                   jax.ShapeDtypeStruct((vm_rows, n), BF16),
                   jax.ShapeDtypeStruct((NSA_HEADS * NSA_DIM, n), BF16),
                   jax.ShapeDtypeStruct((n, 4 * LANE), BF16),
                   jax.ShapeDtypeStruct((nv_rows, n), BF16),
                   jax.ShapeDtypeStruct((batch, NSA_KV_HEADS, seq, NSA_DIM), F32),
                   jax.ShapeDtypeStruct((batch, NSA_KV_HEADS, seq, NSA_DIM), F32),
                   jax.ShapeDtypeStruct((NSA_KV_HEADS * GATE_ROWS, n), F32)],
        compiler_params=_params("parallel"),
        name="in_proj",
    )(x2, g_pre, w1, w1t, qn, wqa, wqb, kvn, wk, wvt, ct, st, ctt, stt,
      _ones_rows(2 * NSA_KV_HEADS), _ones_rows(MLA_HEADS))


def _compress_kernel(k_ref, v_ref, pos_ref, w1_ref, w2k_ref, w2vt_ref, kc_ref, vc_ref):
    ncp = kc_ref.shape[2]

    def hidden(x_ref, j):
        u = jnp.zeros((ncp, CMP_HIDDEN), F32)
        low = jnp.zeros((ncp, CMP_HIDDEN), F32)
        for l in range(CMP_STRIDE):
            x = x_ref[0, 0, pl.ds(l, ncp, stride=CMP_STRIDE), :]
            top = (x + pos_ref[j, l:l + 1, :]).astype(BF16)
            bot = (x + pos_ref[j, CMP_STRIDE + l:CMP_STRIDE + l + 1, :]).astype(BF16)
            u = u + _dot(top, w1_ref[j, l * NSA_DIM:(l + 1) * NSA_DIM, :])
            low = low + _dot(bot, w1_ref[j, (CMP_STRIDE + l) * NSA_DIM:(CMP_STRIDE + l + 1) * NSA_DIM, :])
        nxt = jnp.concatenate([low[1:], jnp.zeros((1, CMP_HIDDEN), F32)], axis=0)
        return _gelu_tanh(u + nxt).astype(BF16)

    kc_ref[0, 0] = _dot(hidden(k_ref, 0), w2k_ref[...]).astype(BF16)
    vc_ref[0, 0] = _dot_nt(w2vt_ref[...], hidden(v_ref, 1)).astype(BF16)


def _compress(kcmp, vcmp, pos, w1, w2k, w2vt, *, batch, seq):
    ncp = seq // CMP_STRIDE
    blk = lambda b, hk: (b, hk, 0, 0)
    return pl.pallas_call(
        _compress_kernel,
        grid=(batch, NSA_KV_HEADS),
        in_specs=[pl.BlockSpec((1, 1, seq, NSA_DIM), blk),
                  pl.BlockSpec((1, 1, seq, NSA_DIM), blk),
                  pl.BlockSpec((2, CMP_LEN, NSA_DIM), lambda b, hk: (0, 0, 0)),
                  pl.BlockSpec((2, CMP_LEN * NSA_DIM, CMP_HIDDEN), lambda b, hk: (0, 0, 0)),
                  pl.BlockSpec((CMP_HIDDEN, NSA_DIM), lambda b, hk: (0, 0)),
                  pl.BlockSpec((NSA_DIM, CMP_HIDDEN), lambda b, hk: (0, 0))],
        out_specs=[pl.BlockSpec((1, 1, ncp, NSA_DIM), blk),
                   pl.BlockSpec((1, 1, NSA_DIM, ncp), blk)],
        out_shape=[jax.ShapeDtypeStruct((batch, NSA_KV_HEADS, ncp, NSA_DIM), BF16),
                   jax.ShapeDtypeStruct((batch, NSA_KV_HEADS, NSA_DIM, ncp), BF16)],
        compiler_params=_params("parallel", "parallel"),
        name="nsa_compress",
    )(kcmp, vcmp, pos, w1, w2k, w2vt)


def _nsa_kernel(*refs):
    for hk in range(NSA_KV_HEADS):
        _nsa_group(hk, *refs)


def _nsa_group(hk, zq_ref, kc_ref, vc_ref, ksl_ref, vsl_ref, kw_ref, vw_ref, gate_ref,
               pat_ref, u_ref, ovt_ref, o_ref, imp_ref, acc_ref, m_ref, sa_ref, sb_ref):
    qi = pl.program_id(1)
    G = NSA_GROUP
    kcols = slice(hk * LANE, (hk + 1) * LANE)
    vrows = slice(hk * VROWS, (hk + 1) * VROWS)
    M = G * QT
    PPT = QT // KP
    q4 = zq_ref[hk * G * NSA_DIM:(hk + 1) * G * NSA_DIM, :]
    qs = jnp.concatenate([q4[g * NSA_DIM:(g + 1) * NSA_DIM, :] for g in range(G)], axis=1)
    kc = kc_ref[0, hk]
    vc_t = vc_ref[0, hk]
    ovt = ovt_ref[...]
    ncp = kc.shape[0]
    qcol = qi * QT + (lax.broadcasted_iota(jnp.int32, (1, M), 1) & (QT - 1))
    u_m1 = u_ref[hk, 0]
    u_0 = u_ref[hk, 1]
    u_p1 = u_ref[hk, 2]

    n_back = WINDOW // KP
    n_w = n_back + PPT
    qw = jnp.concatenate([qs, jnp.zeros_like(qs)], axis=0)
    pad_v = jnp.where(lax.broadcasted_iota(jnp.int32, (VROWS, KP), 0) >= NSA_DIM, 1.0, 0.0).astype(BF16)
    k_rows, v_cols = [], []
    for w in range(n_w):
        piece = PPT * qi - n_back + w
        off = pl.multiple_of(jnp.maximum(piece, 0) * KP, KP)
        kt = kw_ref[pl.ds(off, KP), kcols]
        vt = vw_ref[vrows, pl.ds(off, KP)]
        if w < n_back:
            kt = jnp.where(piece >= 0, kt, jnp.zeros_like(kt))
            vt = jnp.where(piece >= 0, vt, pad_v)
        k_rows.append(kt)
        v_cols.append(vt)
    s_w = _dot(jnp.concatenate(k_rows, axis=0), qw)

    slide = QT // CMP_STRIDE
    off_b = pl.multiple_of(ncp - slide * qi, slide)
    bias = jnp.concatenate([pat_ref[hk * G + g, pl.ds(off_b, ncp), :] for g in range(G)], axis=1)
    s = _dot(kc, qs) + bias
    e = jnp.exp2(s - jnp.max(s, axis=0, keepdims=True))
    scale = jnp.where(qcol >= CMP_LEN - 1, 1.0 / jnp.sum(e, axis=0, keepdims=True), 0.0)
    p_cmp = (e * scale).astype(BF16)

    key_j = lax.broadcasted_iota(jnp.int32, (KP, M), 0)
    qry_i = lax.broadcasted_iota(jnp.int32, (KP, M), 1) & (QT - 1)
    pieces = [s_w[w * KP:(w + 1) * KP] for w in range(n_w)]
    for w in range(PPT):
        pieces[w] = pieces[w] + jnp.where(key_j > qry_i - w * KP, 0.0, NEG)
    pieces[n_back - 1] = pieces[n_back - 1] + u_m1
    pieces[n_back] = pieces[n_back] + u_0
    pieces[n_back + 1] = pieces[n_back + 1] + u_p1
    s_w = jnp.concatenate(pieces, axis=0)
    p_w = jnp.exp2(s_w - jnp.max(s_w, axis=0, keepdims=True)).astype(BF16)

    o_cmp = _dot(vc_t, p_cmp)
    imp_t = jnp.zeros((SLC_PAD, QT), F32)
    for g in range(G):
        imp_t = imp_t + _dot(ovt, p_cmp[:, g * QT:(g + 1) * QT])

    a_w = _dot(jnp.concatenate(v_cols, axis=1), p_w)
    o_win = a_w[:NSA_DIM] / a_w[NSA_DIM:NSA_DIM + 1]

    n_id = lax.broadcasted_iota(jnp.int32, (SLC_PAD, QT), 0)
    q_blk = (qi * QT + lax.broadcasted_iota(jnp.int32, (SLC_PAD, QT), 1)) // SLC_LEN
    forced = (n_id == 0) | (n_id == q_blk) | (n_id == q_blk - 1)
    imp = jnp.where(forced, POS_BIG, jnp.where(n_id > q_blk, NEG, imp_t))
    imp_ref[...] = imp
    SUB = 8
    slabs = [imp[v * SUB:(v + 1) * SUB] for v in range(SLC_PAD // SUB)]
    ranks = [jnp.zeros((SUB, QT), jnp.int32) for _ in slabs]
    sub_id = lax.broadcasted_iota(jnp.int32, (SUB, QT), 0)
    for m in range(SLC_PAD):
        other = imp_ref[m:m + 1, :]
        for v, slab in enumerate(slabs):
            if v > m // SUB:
                beats = (other >= slab).astype(jnp.int32)
            elif v < m // SUB:
                beats = (other > slab).astype(jnp.int32)
            else:
                beats = jnp.where(sub_id > m % SUB, (other >= slab).astype(jnp.int32),
                                  (other > slab).astype(jnp.int32))
            ranks[v] = ranks[v] + beats
    rank = jnp.concatenate(ranks, axis=0)
    selb = jnp.where(rank < SLC_TOPK, 0.0, NEG).astype(BF16)
    qaug = jnp.concatenate([qs, jnp.concatenate([selb] * G, axis=1)], axis=0)

    PIECES = 4
    TK = PIECES * KP
    n_t = (PPT * qi + PPT - 1) // PIECES + 1
    at_start = (PPT * qi) % PIECES == 0
    last_bias = [jnp.where(at_start, u_0, 0.0), jnp.where(at_start, u_p1, u_m1),
                 jnp.where(at_start, NEG, u_0), jnp.where(at_start, NEG, u_p1)]
    prev_bias = [None, None, None, jnp.where(at_start, u_m1, 0.0)]
    m_ref[...] = jnp.full(m_ref.shape, NEG, F32)
    acc_ref[...] = jnp.zeros(acc_ref.shape, F32)

    HALVES = 2
    MH = M // HALVES

    def qk(dst_ref, t, hf):
        off = pl.multiple_of(t * TK, TK)
        lanes = slice(hf * MH, (hf + 1) * MH)
        dst_ref[:, lanes] = _dot(ksl_ref[pl.ds(off, TK), kcols], qaug[:, lanes])

    def consume(src_ref, t, bias, hf):
        off = pl.multiple_of(t * TK, TK)
        lanes = slice(hf * MH, (hf + 1) * MH)
        s = src_ref[:, lanes]
        if bias is not None:
            s = jnp.concatenate([s[j * KP:(j + 1) * KP] if bias[j] is None
                                 else s[j * KP:(j + 1) * KP] + bias[j][:, lanes]
                                 for j in range(PIECES)], axis=0)
        m_old = m_ref[:, lanes]
        m_new = jnp.maximum(m_old, jnp.max(s, axis=0, keepdims=True))
        p = jnp.exp2(s - m_new).astype(BF16)
        acc_ref[:, lanes] = acc_ref[:, lanes] * jnp.exp2(m_old - m_new) + _dot(vsl_ref[vrows, pl.ds(off, TK)], p)
        m_ref[:, lanes] = m_new

    def stage(nxt_ref, nxt_t, cur_ref, cur_t, bias):
        for hf in range(HALVES):
            if nxt_ref is not None:
                qk(nxt_ref, nxt_t, hf)
            consume(cur_ref, cur_t, bias, hf)

    n_plain = jnp.maximum(n_t - 2, 0)
    for hf in range(HALVES):
        qk(sa_ref, 0, hf)

    def pair(u, carry):
        t = 2 * u
        stage(sb_ref, t + 1, sa_ref, t, None)
        stage(sa_ref, t + 2, sb_ref, t + 1, None)
        return carry

    lax.fori_loop(0, n_plain // 2, pair, 0)
    tb = (n_plain // 2) * 2

    @pl.when(n_t == 1)
    def _():
        stage(None, None, sa_ref, 0, last_bias)

    @pl.when((n_t >= 2) & (n_plain % 2 == 0))
    def _():
        stage(sb_ref, tb + 1, sa_ref, tb, prev_bias)
        stage(None, None, sb_ref, tb + 1, last_bias)

    @pl.when(n_plain % 2 == 1)
    def _():
        stage(sb_ref, tb + 1, sa_ref, tb, None)
        stage(sa_ref, tb + 2, sb_ref, tb + 1, prev_bias)
        stage(None, None, sa_ref, tb + 2, last_bias)

    a_s = acc_ref[...]
    o_sel = a_s[:NSA_DIM] / a_s[NSA_DIM:NSA_DIM + 1]

    gate = gate_ref[hk * GATE_ROWS:(hk + 1) * GATE_ROWS, :]
    outs = []
    for g in range(G):
        cols = slice(g * QT, (g + 1) * QT)
        o_t = (gate[3 * g:3 * g + 1, :] * o_cmp[:, cols] + gate[3 * g + 1:3 * g + 2, :] * o_sel[:, cols]
               + gate[3 * g + 2:3 * g + 3, :] * o_win[:, cols])
        outs.append(o_t.T)
    o_ref[:, hk * G * NSA_DIM:(hk + 1) * G * NSA_DIM] = jnp.concatenate(outs, axis=1).astype(BF16)


def _overlap_t(seq):
    ncp = seq // CMP_STRIDE
    n_cmp = (seq - CMP_LEN) // CMP_STRIDE + 1
    n_slc = seq // SLC_LEN
    cs = np.arange(ncp)[None, :] * CMP_STRIDE
    ss = np.arange(SLC_PAD)[:, None] * SLC_LEN
    ov = np.maximum(np.minimum(cs + CMP_LEN, ss + SLC_LEN) - np.maximum(cs, ss), 0).astype(np.float32) / CMP_LEN
    ov = ov * (np.arange(ncp)[None, :] < n_cmp) * (np.arange(SLC_PAD)[:, None] < n_slc)
    return jnp.asarray(ov, BF16)


def _nsa(zq_t, kc, vc_t, nk, nv_t, gates_t, pat, utab, *, batch, seq):
    n = nk.shape[0]
    nq = seq // QT
    ncp = seq // CMP_STRIDE
    G = NSA_GROUP
    HK = NSA_KV_HEADS
    assert seq // SLC_LEN <= SLC_PAD and seq // SLC_LEN >= SLC_TOPK and seq >= WINDOW
    qcol = lambda b, qi: (0, b * nq + qi)
    return pl.pallas_call(
        _nsa_kernel,
        grid=(batch, nq),
        in_specs=[pl.BlockSpec((NSA_HEADS * NSA_DIM, QT), qcol),
                  pl.BlockSpec((1, HK, ncp, NSA_DIM), lambda b, qi: (b, 0, 0, 0)),
                  pl.BlockSpec((1, HK, NSA_DIM, ncp), lambda b, qi: (b, 0, 0, 0)),
                  pl.BlockSpec((seq, HK * LANE), lambda b, qi: (b, 0)),
                  pl.BlockSpec((HK * VROWS, seq), lambda b, qi: (0, b)),
                  pl.BlockSpec((seq, HK * LANE), lambda b, qi: (b, 1)),
                  pl.BlockSpec((HK * VROWS, seq), lambda b, qi: (1, b)),
                  pl.BlockSpec((HK * GATE_ROWS, QT), qcol),
                  _resident((NSA_HEADS, 2 * ncp, QT)),
                  _resident((HK, 3, KP, G * QT)),
                  _resident((SLC_PAD, ncp))],
        out_specs=pl.BlockSpec((QT, NSA_HEADS * NSA_DIM), lambda b, qi: (b * nq + qi, 0)),
        out_shape=jax.ShapeDtypeStruct((n, NSA_HEADS * NSA_DIM), BF16),
        scratch_shapes=[pltpu.VMEM((SLC_PAD, QT), F32),
                        pltpu.VMEM((VROWS, G * QT), F32),
                        pltpu.VMEM((1, G * QT), F32),
                        pltpu.VMEM((4 * KP, G * QT), F32),
                        pltpu.VMEM((4 * KP, G * QT), F32)],
        compiler_params=_params("parallel", "arbitrary"),
        name="nsa_attention",
    )(zq_t, kc, vc_t, nk, nv_t, nk, nv_t, gates_t, pat, utab, _overlap_t(seq))


MLA_TQ = 512
MLA_HP = 8


def _mla_kernel(q_ref, k_ref, v_ref, o_ref, acc_ref, m_ref, sa_ref, sb_ref):
    qi = pl.program_id(2)
    tq = MLA_TQ
    HP = MLA_HP
    qs = [q_ref[hh * LANE:(hh + 1) * LANE, :] for hh in range(HP)]
    m_ref[...] = jnp.full(m_ref.shape, NEG, F32)
    acc_ref[...] = jnp.zeros(acc_ref.shape, F32)

    def qk(dst_ref, off, hh):
        dst_ref[hh] = _dot(k_ref[pl.ds(off, tq), hh * LANE:(hh + 1) * LANE], qs[hh])

    def consume(src_ref, off, mask, hh):
        s = src_ref[hh]
        if mask is not None:
            s = s + mask
        vt = v_ref[hh * VROWS:(hh + 1) * VROWS, pl.ds(off, tq)]
        m_old = m_ref[hh]
        m_new = jnp.maximum(m_old, jnp.max(s, axis=0, keepdims=True))
        p = jnp.exp2(s - m_new).astype(BF16)
        acc_ref[hh] = acc_ref[hh] * jnp.exp2(m_old - m_new) + _dot(vt, p)
        m_ref[hh] = m_new

    def stage(nxt_ref, nxt_off, cur_ref, cur_off, mask):
        for hh in range(HP):
            if nxt_ref is not None:
                qk(nxt_ref, nxt_off, hh)
            consume(cur_ref, cur_off, mask, hh)

    key_j = lax.broadcasted_iota(jnp.int32, (tq, tq), 0)
    qry_i = lax.broadcasted_iota(jnp.int32, (tq, tq), 1)
    causal = jnp.where(key_j <= qry_i, 0.0, NEG)
    for hh in range(HP):
        qk(sa_ref, 0, hh)

    def pair(u, carry):
        off = pl.multiple_of(u * (2 * tq), 2 * tq)
        stage(sb_ref, off + tq, sa_ref, off, None)
        stage(sa_ref, off + 2 * tq, sb_ref, off + tq, None)
        return carry

    lax.fori_loop(0, qi // 2, pair, 0)
    base = pl.multiple_of((qi // 2) * (2 * tq), 2 * tq)

    @pl.when(qi % 2 == 0)
    def _():
        stage(None, None, sa_ref, base, causal)

    @pl.when(qi % 2 == 1)
    def _():
        stage(sb_ref, base + tq, sa_ref, base, None)
        stage(None, None, sb_ref, base + tq, causal)

    outs = []
    for hh in range(HP):
        a = acc_ref[hh]
        outs.append((a[:MLA_V] / a[MLA_V:MLA_V + 1]).T)
    o_ref[...] = jnp.concatenate(outs, axis=1).astype(BF16)


def _mla(qm_t, km, vm_t, *, batch, seq):
    n = km.shape[0]
    tq = MLA_TQ
    nq = seq // tq
    HP = MLA_HP
    return pl.pallas_call(
        _mla_kernel,
        grid=(batch, MLA_HEADS // HP, nq),
        in_specs=[pl.BlockSpec((HP * LANE, tq), lambda b, hp, qi: (hp, b * nq + qi)),
                  pl.BlockSpec((seq, HP * LANE), lambda b, hp, qi: (b, hp)),
                  pl.BlockSpec((HP * VROWS, seq), lambda b, hp, qi: (hp, b))],
        out_specs=pl.BlockSpec((tq, HP * MLA_V), lambda b, hp, qi: (b * nq + qi, hp)),
        out_shape=jax.ShapeDtypeStruct((n, MLA_HEADS * MLA_V), BF16),
        scratch_shapes=[pltpu.VMEM((HP, VROWS, tq), F32),
                        pltpu.VMEM((HP, 1, tq), F32),
                        pltpu.VMEM((HP, tq, tq), F32),
                        pltpu.VMEM((HP, tq, tq), F32)],
        compiler_params=_params("parallel", "parallel", "arbitrary"),
        name="mla_attention",
    )(qm_t, km, vm_t)


def _tail_kernel(x_ref, om_ref, on_ref, p_ref, wo_ref, g1_ref, g2_ref, g3_ref,
                 wg_ref, wu_ref, cw_ref, cb_ref, wd_ref, pg_ref, pp_ref,
                 o_ref, carry_ref, act_ref, *, seq, tm):
    n_chunks = D_FF // FF_CHUNK
    half = om_ref.shape[1]
    y = _dot(om_ref[...], wo_ref[:half, :]) + _dot(on_ref[...], wo_ref[half:, :])
    x = x_ref[...] + _rms(y, g1_ref[...])

    h = _rms(x, g2_ref[...]).astype(BF16)

    @pl.when((pl.program_id(0) * tm) % seq == 0)
    def _():
        carry_ref[...] = jnp.zeros(carry_ref.shape, F32)

    SUB = 8
    row8 = lax.broadcasted_iota(jnp.int32, (SUB, 1), 0)
    for c in range(n_chunks):
        cols = slice(c * FF_CHUNK, (c + 1) * FF_CHUNK)
        g = _dot(h, wg_ref[:, cols])
        prev = carry_ref[:, cols]
        carry_ref[:, cols] = g[tm - SUB:, :]
        r1 = pltpu.roll(g, 1, 0)
        r2 = pltpu.roll(g, 2, 0)
        top1 = jnp.where(row8 == 0, prev[7:8, :], r1[:SUB])
        top2 = jnp.where(row8 == 0, prev[6:7, :], jnp.where(row8 == 1, prev[7:8, :], r2[:SUB]))
        g1 = jnp.concatenate([top1, r1[SUB:]], axis=0)
        g2 = jnp.concatenate([top2, r2[SUB:]], axis=0)
        conv = (cw_ref[0:1, cols] * g2 + cw_ref[1:2, cols] * g1 + cw_ref[2:3, cols] * g
                + cb_ref[:, cols])
        act_ref[:, cols] = (_gelu_tanh(conv) * _dot(h, wu_ref[:, cols])).astype(BF16)
    x = x + _rms(_dot(act_ref[...], wd_ref[...]), g3_ref[...])

    gate = _sigmoid(_dot(x.astype(BF16), pg_ref[...]))
    o_ref[...] = x + gate * _dot(p_ref[...].astype(BF16), pp_ref[...])


def _tail(x2, om, on, p2, layer, wo, g1, g2, g3, wg, wu, cw, cb, wd, pg, pp, *, seq):
    n = x2.shape[0]
    tm = ROW_TILE
    row = lambda r: (r, 0)
    half = om.shape[1]
    lr = functools.partial(_layer_resident, layer)
    return pl.pallas_call(
        functools.partial(_tail_kernel, seq=seq, tm=tm),
        grid=(n // tm,),
        in_specs=[pl.BlockSpec((tm, D_MODEL), row),
                  pl.BlockSpec((tm, half), row),
                  pl.BlockSpec((tm, half), row),
                  pl.BlockSpec((None, tm, PLE_DIM), lambda r: (layer, r, 0)),
                  lr((2 * half, D_MODEL)),
                  lr((1, D_MODEL)),
                  lr((1, D_MODEL)),
                  lr((1, D_MODEL)),
                  lr((D_MODEL, D_FF)),
                  lr((D_MODEL, D_FF)),
                  lr((CONV_WIDTH, D_FF)),
                  lr((1, D_FF)),
                  lr((D_FF, D_MODEL)),
                  lr((D_MODEL, D_MODEL)),
                  lr((PLE_DIM, D_MODEL))],
        out_specs=pl.BlockSpec((tm, D_MODEL), row),
        out_shape=jax.ShapeDtypeStruct((n, D_MODEL), F32),
        scratch_shapes=[pltpu.VMEM((8, D_FF), F32),
                        pltpu.VMEM((tm, D_FF), BF16)],
        compiler_params=_params("arbitrary"),
        name="layer_tail",
    )(x2, om, on, p2, wo, g1, g2, g3, wg, wu, cw, cb, wd, pg, pp)


def _rot_rows(w):
    half = w.shape[-2] // 2
    return jnp.concatenate([-w[..., half:, :], w[..., :half, :]], axis=-2)


def _prep_inproj(w):
    wt = w.T
    o = np.cumsum((0,) + IN_SPLITS)
    cq, ckv, kr, qn, kcmp, vcmp, kslc, vslc, kwin, vwin, gn = [wt[o[j]:o[j + 1]] for j in range(len(IN_SPLITS))]
    d = w.shape[0]

    def zeros(rows):
        return jnp.zeros((rows, d), F32)

    def kslots(m):
        return jnp.concatenate([m[:NSA_DIM], zeros(LANE - NSA_DIM), m[NSA_DIM:], zeros(LANE - NSA_DIM)], axis=0)

    def vslots(m):
        return jnp.concatenate([m[:NSA_DIM], zeros(VROWS - NSA_DIM), m[NSA_DIM:], zeros(VROWS - NSA_DIM)], axis=0)

    pad_rope = zeros(LANE - MLA_NOPE - MLA_ROPE)
    w1 = jnp.concatenate([cq, ckv,
                          zeros(MLA_NOPE), kr, pad_rope,
                          zeros(MLA_NOPE), _rot_rows(kr), pad_rope,
                          kslots(kslc), kslots(kwin), kcmp, vcmp], axis=0)
    assert w1.shape[0] == C_TOTAL
    per = 3 * NSA_GROUP
    w1t = jnp.concatenate([qn * NSA_DIM ** -0.5, vslots(vslc), vslots(vwin),
                           gn[:per], zeros(GATE_ROWS - per), gn[per:], zeros(GATE_ROWS - per)], axis=0)
    assert w1t.shape[0] == R_TOTAL
    return w1.astype(BF16).T, w1t.astype(BF16)


def _prep_mla(w_uq, w_ukv):
    r = w_uq.shape[0]
    dq = MLA_NOPE + MLA_ROPE
    ut = w_uq.T.reshape(MLA_HEADS, dq, r)
    pad = jnp.zeros((MLA_HEADS, LANE - dq, r), F32)
    wqa_t = jnp.concatenate([ut, pad], axis=1).reshape(MLA_HEADS * LANE, r)
    wqb_t = jnp.concatenate([jnp.zeros((MLA_HEADS, MLA_NOPE, r), F32), _rot_rows(ut[:, MLA_NOPE:]), pad],
                            axis=1).reshape(MLA_HEADS * LANE, r)
    rk = w_ukv.shape[0]
    kv = w_ukv.reshape(rk, MLA_HEADS, 2, MLA_NOPE)
    wk = jnp.concatenate([kv[:, :, 0, :], jnp.zeros((rk, MLA_HEADS, LANE - MLA_NOPE), F32)],
                         axis=-1).reshape(rk, MLA_HEADS * LANE)
    vt = kv[:, :, 1, :].transpose(1, 2, 0)
    wvt = jnp.concatenate([vt, jnp.zeros((MLA_HEADS, VROWS - MLA_V, rk), F32)], axis=1).reshape(MLA_HEADS * VROWS, rk)
    return wqa_t.astype(BF16), wqb_t.astype(BF16), wk.astype(BF16), wvt.astype(BF16)


def kernel(x, p, positions, rel_bias, attn_pre_norm, attn_post_norm, ffn_pre_norm, ffn_post_norm,
           w_in, mla_q_norm, mla_w_uq, mla_kv_norm, mla_w_ukv, nsa_cmp_pos, nsa_cmp_w1, nsa_cmp_w2,
           w_o, ffn_w_gate, ffn_w_up, ffn_conv_w, ffn_conv_b, ffn_w_down, ple_proj, ple_gate):
    batch, seq, d = x.shape
    depth = w_in.shape[0]
    n = batch * seq
    x2 = x.reshape(n, d)
    ctt, stt, ct, st = _rope_tables(positions)
    pat, utab = _bias_tables(rel_bias, seq)
    tail_params = (w_o.astype(BF16), attn_post_norm[:, None, :], ffn_pre_norm[:, None, :],
                   ffn_post_norm[:, None, :], ffn_w_gate.astype(BF16), ffn_w_up.astype(BF16),
                   ffn_conv_w, ffn_conv_b[:, None, :], ffn_w_down.astype(BF16),
                   ple_gate.astype(BF16), ple_proj.astype(BF16))
    for i in range(depth):
        w1, w1t = _prep_inproj(w_in[i])
        wqa, wqb, wk, wvt = _prep_mla(mla_w_uq[i], mla_w_ukv[i])
        qm_t, km, vm_t, zq_t, nk, nv_t, kcmp, vcmp, gates_t = _inproj(
            x2, attn_pre_norm[i][None, :], w1, w1t, mla_q_norm[i][None, :], wqa, wqb,
            mla_kv_norm[i][None, :], wk, wvt, ct, st, ctt, stt, batch=batch, seq=seq)
        kc, vc_t = _compress(kcmp, vcmp,
                             nsa_cmp_pos[i],
                             nsa_cmp_w1[i].astype(BF16), nsa_cmp_w2[i, 0].astype(BF16),
                             nsa_cmp_w2[i, 1].T.astype(BF16), batch=batch, seq=seq)
        o_nsa = _nsa(zq_t, kc, vc_t, nk, nv_t, gates_t, pat, utab, batch=batch, seq=seq)
        o_mla = _mla(qm_t, km, vm_t, batch=batch, seq=seq)
        x2 = _tail(x2, o_mla, o_nsa, p.reshape(depth, n, PLE_DIM), i, *tail_params, seq=seq)
    return x2.reshape(batch, seq, d)
```

```python
import functools
import math

import numpy as np
import jax
import jax.numpy as jnp
from jax import lax
from jax.experimental import pallas as pl
from jax.experimental.pallas import tpu as pltpu

F32 = jnp.float32
BF16 = jnp.bfloat16

D_MODEL = 1024
DEPTH = 2
MLA_HEADS = 8
MLA_NOPE = 64
MLA_ROPE = 32
MLA_V = 64
MLA_Q_RANK = 256
MLA_KV_RANK = 128
ROPE_BASE = 10000.0
NSA_HEADS = 8
NSA_KV_HEADS = 2
NSA_GROUP = NSA_HEADS // NSA_KV_HEADS
NSA_DIM = 64
CMP_LEN = 32
CMP_STRIDE = 16
CMP_HIDDEN = 128
SLC_LEN = 64
SLC_TOPK = 16
WINDOW = 512
REL_BUCKETS = 32
REL_MAX_DIST = 128
D_FF = 2816
CONV_WIDTH = 3
PLE_DIM = 256
EPS = 1e-6
NEG = -1e30
POS_BIG = 1e30
LOG2E = math.log2(math.e)

IN_SPLITS = (MLA_Q_RANK, MLA_KV_RANK, MLA_ROPE, NSA_HEADS * NSA_DIM,
             NSA_KV_HEADS * NSA_DIM, NSA_KV_HEADS * NSA_DIM,
             NSA_KV_HEADS * NSA_DIM, NSA_KV_HEADS * NSA_DIM,
             NSA_KV_HEADS * NSA_DIM, NSA_KV_HEADS * NSA_DIM,
             3 * NSA_HEADS)

LANE = 128
QT = 256
KP = 128
SLC_PAD = 64
FF_CHUNK = 256
ROW_TILE = 1024
VMEM_LIMIT = 56 * 1024 * 1024

VROWS = 80

C_CQ = 0
C_CKV = 256
C_KR = 384
C_KRROT = 512
C_KSLC = 640
C_KWIN = 896
C_KCMP = 1152
C_VCMP = 1280
C_TOTAL = 1408
R_QN = 0
R_VSLC = 512
R_VWIN = R_VSLC + NSA_KV_HEADS * VROWS
R_GATE = R_VWIN + NSA_KV_HEADS * VROWS
GATE_ROWS = 16
R_TOTAL = R_GATE + NSA_KV_HEADS * GATE_ROWS


def _dot(a, b):
    return jnp.dot(a, b, preferred_element_type=F32)


def _dot_nt(a, b):
    return lax.dot_general(a, b, (((1,), (1,)), ((), ())), preferred_element_type=F32)


def _rms(x, g):
    return x * lax.rsqrt(jnp.mean(x * x, axis=-1, keepdims=True) + EPS) * g


def _gelu_tanh(x):
    return 0.5 * x * (1.0 + jnp.tanh(math.sqrt(2.0 / math.pi) * (x + 0.044715 * (x * x * x))))


def _sigmoid(x):
    return 1.0 / (1.0 + jnp.exp(-x))


def _params(*sem):
    return pltpu.CompilerParams(dimension_semantics=sem, vmem_limit_bytes=VMEM_LIMIT)


def _resident(shape):
    nd = len(shape)
    return pl.BlockSpec(shape, lambda *_: (0,) * nd, pipeline_mode=pl.Buffered(1))


def _layer_resident(layer, shape):
    nd = len(shape)
    return pl.BlockSpec((None,) + tuple(shape), lambda *_: (layer,) + (0,) * nd, pipeline_mode=pl.Buffered(1))


def _rope_kernel(pos_ref, inv_ref, ctt_ref, stt_ref, ct_ref, st_ref):
    ang = inv_ref[...] * pos_ref[...].astype(F32)
    row = lax.broadcasted_iota(jnp.int32, ang.shape, 0)
    rope = (row >= MLA_NOPE) & (row < MLA_NOPE + MLA_ROPE)
    ct = jnp.where(rope, jnp.cos(ang), jnp.where(row < MLA_NOPE, 1.0, 0.0))
    st = jnp.where(rope, jnp.sin(ang), 0.0)
    ctt_ref[...] = ct
    stt_ref[...] = st
    ct_ref[...] = ct.T
    st_ref[...] = st.T


def _rope_tables(positions):
    n = positions.size
    tn = 2048
    half = MLA_ROPE // 2
    inv = ROPE_BASE ** (-jnp.arange(half, dtype=F32) / half)
    inv_slot = jnp.concatenate([jnp.zeros((MLA_NOPE,), F32), inv, inv,
                                jnp.zeros((LANE - MLA_NOPE - MLA_ROPE,), F32)])[:, None]
    return pl.pallas_call(
        _rope_kernel,
        grid=(n // tn,),
        in_specs=[pl.BlockSpec((1, tn), lambda r: (0, r)),
                  pl.BlockSpec((LANE, 1), lambda r: (0, 0))],
        out_specs=[pl.BlockSpec((LANE, tn), lambda r: (0, r))] * 2
                  + [pl.BlockSpec((tn, LANE), lambda r: (r, 0))] * 2,
        out_shape=[jax.ShapeDtypeStruct((LANE, n), F32)] * 2 + [jax.ShapeDtypeStruct((n, LANE), F32)] * 2,
        compiler_params=_params("parallel"),
        name="rope_tables",
    )(positions.reshape(1, n), inv_slot)


def _bucket_np(dist):
    n = np.maximum(dist, 0)
    max_exact = REL_BUCKETS // 2
    large = max_exact + (np.log(np.maximum(n, 1).astype(np.float32) / max_exact)
                         / math.log(REL_MAX_DIST / max_exact)
                         * (REL_BUCKETS - max_exact)).astype(np.int32)
    large = np.minimum(large, REL_BUCKETS - 1)
    return np.where(n < max_exact, n, large).astype(np.int32)


def _bias_kernel(table_ref, bpc_ref, bpu_ref, pat_ref, u_ref):
    h = pl.program_id(0)
    far = table_ref[REL_BUCKETS - 1, h]

    def lookup(bp, sub):
        acc = jnp.full(bp.shape, far - sub, F32)
        for b in range(REL_BUCKETS - 1):
            acc = jnp.where(bp == b, table_ref[b, h] - sub, acc)
        return jnp.where(bp < 0, NEG, acc * LOG2E)

    pat_ref[0] = lookup(bpc_ref[...], 0.0)
    for d in range(3):
        u_ref[0, d] = lookup(bpu_ref[d], far)


def _bias_tables(rel_bias, seq):
    ncp = seq // CMP_STRIDE
    i = np.arange(QT)[None, :]
    cprime = np.arange(2 * ncp)[:, None] - ncp
    dist_c = i - CMP_STRIDE * cprime - (CMP_LEN - 1)
    bpc = np.where(dist_c >= 0, _bucket_np(dist_c), -1).astype(np.int32)
    j = np.arange(KP)[:, None]
    bpu = []
    for delta in (-1, 0, 1):
        dist = i - KP * delta - j
        bpu.append(np.where(dist >= 0, _bucket_np(dist), -1))
    bpu = np.stack(bpu).astype(np.int32)
    return pl.pallas_call(
        _bias_kernel,
        grid=(NSA_HEADS,),
        in_specs=[pl.BlockSpec(memory_space=pltpu.SMEM),
                  pl.BlockSpec((2 * ncp, QT), lambda h: (0, 0)),
                  pl.BlockSpec((3, KP, QT), lambda h: (0, 0, 0))],
        out_specs=[pl.BlockSpec((1, 2 * ncp, QT), lambda h: (h, 0, 0)),
                   pl.BlockSpec((1, 3, KP, QT), lambda h: (h // NSA_GROUP, 0, 0, h % NSA_GROUP))],
        out_shape=[jax.ShapeDtypeStruct((NSA_HEADS, 2 * ncp, QT), F32),
                   jax.ShapeDtypeStruct((NSA_KV_HEADS, 3, KP, NSA_GROUP * QT), F32)],
        compiler_params=_params("parallel"),
        name="bias_tables",
    )(rel_bias.astype(F32), jnp.asarray(bpc), jnp.asarray(bpu))


def _inproj_kernel(x_ref, g_ref, w1_ref, w1t_ref, qn_ref, wqa_ref, wqb_ref, kvn_ref, wk_ref, wvt_ref,
                   ct_ref, st_ref, ctt_ref, stt_ref, ones_n_ref, ones_m_ref,
                   qm_ref, km_ref, vm_ref, zq_ref, nk_ref, nv_ref, kcmp_ref, vcmp_ref, gate_ref,
                   *, seq, tm):
    h = _rms(x_ref[...], g_ref[...]).astype(BF16)

    def proj(c0, width):
        return _dot(h, w1_ref[:, c0:c0 + width])

    cq = _rms(proj(C_CQ, MLA_Q_RANK), qn_ref[...]).astype(BF16)
    qa = _dot_nt(wqa_ref[...], cq)
    qb = _dot_nt(wqb_ref[...], cq)
    scale = (MLA_NOPE + MLA_ROPE) ** -0.5 * LOG2E
    cts = ctt_ref[...] * scale
    sts = stt_ref[...] * scale
    for hh in range(MLA_HEADS):
        sl = slice(hh * LANE, (hh + 1) * LANE)
        qm_ref[sl, :] = (qa[sl] * cts + qb[sl] * sts).astype(BF16)

    ckv = _rms(proj(C_CKV, MLA_KV_RANK), kvn_ref[...]).astype(BF16)
    kr = proj(C_KR, LANE) * ct_ref[...] + proj(C_KRROT, LANE) * st_ref[...]
    kn = _dot(ckv, wk_ref[...])
    for hh in range(MLA_HEADS):
        sl = slice(hh * LANE, (hh + 1) * LANE)
        km_ref[:, sl] = (kn[:, sl] + kr).astype(BF16)
    vm_ref[...] = (_dot_nt(wvt_ref[...], ckv) + ones_m_ref[...]).astype(BF16)

    zt = _dot_nt(w1t_ref[...], h)
    zq_ref[...] = (zt[R_QN:R_VSLC] * LOG2E).astype(BF16)
    nv_ref[...] = (zt[R_VSLC:R_GATE] + ones_n_ref[...]).astype(BF16)
    gate_ref[...] = _sigmoid(zt[R_GATE:R_TOTAL])

    lane = lax.broadcasted_iota(jnp.int32, (tm, LANE), 1)
    s0 = (pl.program_id(0) * tm) % seq
    row = lax.broadcasted_iota(jnp.int32, (tm, LANE), 0)
    onehot = (lane - NSA_DIM == (s0 + row) // SLC_LEN).astype(F32)
    ksl = proj(C_KSLC, 2 * LANE)
    for hk in range(NSA_KV_HEADS):
        sl = slice(hk * LANE, (hk + 1) * LANE)
        nk_ref[:, sl] = (ksl[:, sl] + onehot).astype(BF16)
    nk_ref[:, 2 * LANE:] = proj(C_KWIN, 2 * LANE).astype(BF16)

    kcmp = proj(C_KCMP, LANE)
    vcmp = proj(C_VCMP, LANE)
    for hk in range(NSA_KV_HEADS):
        kcmp_ref[0, hk] = kcmp[:, hk * NSA_DIM:(hk + 1) * NSA_DIM]
        vcmp_ref[0, hk] = vcmp[:, hk * NSA_DIM:(hk + 1) * NSA_DIM]


def _ones_rows(n_slots):
    r = np.arange(n_slots * VROWS) % VROWS
    return jnp.asarray((r >= NSA_DIM).astype(np.float32)[:, None])


def _inproj(x2, g_pre, w1, w1t, qn, wqa, wqb, kvn, wk, wvt, ct, st, ctt, stt, *, batch, seq):
    n = x2.shape[0]
    tm = ROW_TILE
    tiles_per_seq = seq // tm
    row = lambda r: (r, 0)
    col = lambda r: (0, r)
    hm = lambda r: (r // tiles_per_seq, 0, r % tiles_per_seq, 0)
    nv_rows = 2 * NSA_KV_HEADS * VROWS
    vm_rows = MLA_HEADS * VROWS
    return pl.pallas_call(
        functools.partial(_inproj_kernel, seq=seq, tm=tm),
        grid=(n // tm,),
        in_specs=[pl.BlockSpec((tm, D_MODEL), row),
                  _resident((1, D_MODEL)),
                  _resident((D_MODEL, C_TOTAL)),
                  _resident((R_TOTAL, D_MODEL)),
                  _resident((1, MLA_Q_RANK)),
                  _resident((MLA_HEADS * LANE, MLA_Q_RANK)),
                  _resident((MLA_HEADS * LANE, MLA_Q_RANK)),
                  _resident((1, MLA_KV_RANK)),
                  _resident((MLA_KV_RANK, MLA_HEADS * LANE)),
                  _resident((vm_rows, MLA_KV_RANK)),
                  pl.BlockSpec((tm, LANE), row),
                  pl.BlockSpec((tm, LANE), row),
                  pl.BlockSpec((LANE, tm), col),
                  pl.BlockSpec((LANE, tm), col),
                  _resident((nv_rows, 1)),
                  _resident((vm_rows, 1))],
        out_specs=[pl.BlockSpec((MLA_HEADS * LANE, tm), col),
                   pl.BlockSpec((tm, MLA_HEADS * LANE), row),
                   pl.BlockSpec((vm_rows, tm), col),
                   pl.BlockSpec((NSA_HEADS * NSA_DIM, tm), col),
                   pl.BlockSpec((tm, 4 * LANE), row),
                   pl.BlockSpec((nv_rows, tm), col),
                   pl.BlockSpec((1, NSA_KV_HEADS, tm, NSA_DIM), hm),
                   pl.BlockSpec((1, NSA_KV_HEADS, tm, NSA_DIM), hm),
                   pl.BlockSpec((NSA_KV_HEADS * GATE_ROWS, tm), col)],
        out_shape=[jax.ShapeDtypeStruct((MLA_HEADS * LANE, n), BF16),
                   jax.ShapeDtypeStruct((n, MLA_HEADS * LANE), BF16),
                   jax.ShapeDtypeStruct((vm_rows, n), BF16),
                   jax.ShapeDtypeStruct((NSA_HEADS * NSA_DIM, n), BF16),
                   jax.ShapeDtypeStruct((n, 4 * LANE), BF16),
                   jax.ShapeDtypeStruct((nv_rows, n), BF16),
                   jax.ShapeDtypeStruct((batch, NSA_KV_HEADS, seq, NSA_DIM), F32),
                   jax.ShapeDtypeStruct((batch, NSA_KV_HEADS, seq, NSA_DIM), F32),
                   jax.ShapeDtypeStruct((NSA_KV_HEADS * GATE_ROWS, n), F32)],
        compiler_params=_params("parallel"),
        name="in_proj",
    )(x2, g_pre, w1, w1t, qn, wqa, wqb, kvn, wk, wvt, ct, st, ctt, stt,
      _ones_rows(2 * NSA_KV_HEADS), _ones_rows(MLA_HEADS))


def _compress_kernel(k_ref, v_ref, pos_ref, w1_ref, w2k_ref, w2vt_ref, kc_ref, vc_ref):
    ncp = kc_ref.shape[2]

    def hidden(x_ref, j):
        u = jnp.zeros((ncp, CMP_HIDDEN), F32)
        low = jnp.zeros((ncp, CMP_HIDDEN), F32)
        for l in range(CMP_STRIDE):
            x = x_ref[0, 0, pl.ds(l, ncp, stride=CMP_STRIDE), :]
            top = (x + pos_ref[j, l:l + 1, :]).astype(BF16)
            bot = (x + pos_ref[j, CMP_STRIDE + l:CMP_STRIDE + l + 1, :]).astype(BF16)
            u = u + _dot(top, w1_ref[j, l * NSA_DIM:(l + 1) * NSA_DIM, :])
            low = low + _dot(bot, w1_ref[j, (CMP_STRIDE + l) * NSA_DIM:(CMP_STRIDE + l + 1) * NSA_DIM, :])
        nxt = jnp.concatenate([low[1:], jnp.zeros((1, CMP_HIDDEN), F32)], axis=0)
        return _gelu_tanh(u + nxt).astype(BF16)

    kc_ref[0, 0] = _dot(hidden(k_ref, 0), w2k_ref[...]).astype(BF16)
    vc_ref[0, 0] = _dot_nt(w2vt_ref[...], hidden(v_ref, 1)).astype(BF16)


def _compress(kcmp, vcmp, pos, w1, w2k, w2vt, *, batch, seq):
    ncp = seq // CMP_STRIDE
    blk = lambda b, hk: (b, hk, 0, 0)
    return pl.pallas_call(
        _compress_kernel,
        grid=(batch, NSA_KV_HEADS),
        in_specs=[pl.BlockSpec((1, 1, seq, NSA_DIM), blk),
                  pl.BlockSpec((1, 1, seq, NSA_DIM), blk),
                  pl.BlockSpec((2, CMP_LEN, NSA_DIM), lambda b, hk: (0, 0, 0)),
                  pl.BlockSpec((2, CMP_LEN * NSA_DIM, CMP_HIDDEN), lambda b, hk: (0, 0, 0)),
                  pl.BlockSpec((CMP_HIDDEN, NSA_DIM), lambda b, hk: (0, 0)),
                  pl.BlockSpec((NSA_DIM, CMP_HIDDEN), lambda b, hk: (0, 0))],
        out_specs=[pl.BlockSpec((1, 1, ncp, NSA_DIM), blk),
                   pl.BlockSpec((1, 1, NSA_DIM, ncp), blk)],
        out_shape=[jax.ShapeDtypeStruct((batch, NSA_KV_HEADS, ncp, NSA_DIM), BF16),
                   jax.ShapeDtypeStruct((batch, NSA_KV_HEADS, NSA_DIM, ncp), BF16)],
        compiler_params=_params("parallel", "parallel"),
        name="nsa_compress",
    )(kcmp, vcmp, pos, w1, w2k, w2vt)


def _nsa_kernel(*refs):
    for hk in range(NSA_KV_HEADS):
        _nsa_group(hk, *refs)


def _nsa_group(hk, zq_ref, kc_ref, vc_ref, ksl_ref, vsl_ref, kw_ref, vw_ref, gate_ref,
               pat_ref, u_ref, ovt_ref, o_ref, imp_ref, acc_ref, m_ref, sa_ref, sb_ref):
    qi = pl.program_id(1)
    G = NSA_GROUP
    kcols = slice(hk * LANE, (hk + 1) * LANE)
    vrows = slice(hk * VROWS, (hk + 1) * VROWS)
    M = G * QT
    PPT = QT // KP
    q4 = zq_ref[hk * G * NSA_DIM:(hk + 1) * G * NSA_DIM, :]
    qs = jnp.concatenate([q4[g * NSA_DIM:(g + 1) * NSA_DIM, :] for g in range(G)], axis=1)
    kc = kc_ref[0, hk]
    vc_t = vc_ref[0, hk]
    ovt = ovt_ref[...]
    ncp = kc.shape[0]
    qcol = qi * QT + (lax.broadcasted_iota(jnp.int32, (1, M), 1) & (QT - 1))
    u_m1 = u_ref[hk, 0]
    u_0 = u_ref[hk, 1]
    u_p1 = u_ref[hk, 2]

    n_back = WINDOW // KP
    n_w = n_back + PPT
    MS = G * KP

    def sub_lanes(x, a):
        return jnp.concatenate([x[:, g * QT + a * KP:g * QT + (a + 1) * KP] for g in range(G)], axis=1)

    qw = jnp.concatenate([qs, jnp.zeros_like(qs)], axis=0)
    pad_v = jnp.where(lax.broadcasted_iota(jnp.int32, (VROWS, KP), 0) >= NSA_DIM, 1.0, 0.0).astype(BF16)
    k_rows, v_cols = [], []
    for w in range(n_w):
        piece = PPT * qi - n_back + w
        off = pl.multiple_of(jnp.maximum(piece, 0) * KP, KP)
        kt = kw_ref[pl.ds(off, KP), kcols]
        vt = vw_ref[vrows, pl.ds(off, KP)]
        if w < n_back:
            kt = jnp.where(piece >= 0, kt, jnp.zeros_like(kt))
            vt = jnp.where(piece >= 0, vt, pad_v)
        k_rows.append(kt)
        v_cols.append(vt)
    s_sub = [_dot(jnp.concatenate(k_rows[a:a + n_back + 1], axis=0), sub_lanes(qw, a))
             for a in range(PPT)]

    slide = QT // CMP_STRIDE
    off_b = pl.multiple_of(ncp - slide * qi, slide)
    bias = jnp.concatenate([pat_ref[hk * G + g, pl.ds(off_b, ncp), :] for g in range(G)], axis=1)
    s = _dot(kc, qs) + bias
    e = jnp.exp2(s - jnp.max(s, axis=0, keepdims=True))
    scale = jnp.where(qcol >= CMP_LEN - 1, 1.0 / jnp.sum(e, axis=0, keepdims=True), 0.0)
    p_cmp = (e * scale).astype(BF16)

    key_j = lax.broadcasted_iota(jnp.int32, (KP, MS), 0)
    q_loc = lax.broadcasted_iota(jnp.int32, (KP, MS), 1) & (KP - 1)
    u_tabs = [u_m1, u_0, u_p1]
    p_sub = []
    for a in range(PPT):
        pieces = [s_sub[a][w * KP:(w + 1) * KP] for w in range(n_back + 1)]
        pieces[0] = pieces[0] + jnp.where(key_j > q_loc, 0.0, NEG)
        pieces[n_back - 1] = pieces[n_back - 1] + sub_lanes(u_tabs[a], a)
        pieces[n_back] = pieces[n_back] + sub_lanes(u_tabs[a + 1], a)
        s_a = jnp.concatenate(pieces, axis=0)
        p_sub.append(jnp.exp2(s_a - jnp.max(s_a, axis=0, keepdims=True)).astype(BF16))

    o_cmp = _dot(vc_t, p_cmp)
    imp_t = jnp.zeros((SLC_PAD, QT), F32)
    for g in range(G):
        imp_t = imp_t + _dot(ovt, p_cmp[:, g * QT:(g + 1) * QT])

    o_sub = []
    for a in range(PPT):
        a_w = _dot(jnp.concatenate(v_cols[a:a + n_back + 1], axis=1), p_sub[a])
        o_sub.append(a_w[:NSA_DIM] / a_w[NSA_DIM:NSA_DIM + 1])
    o_win = jnp.concatenate([o_sub[a][:, g * KP:(g + 1) * KP] for g in range(G) for a in range(PPT)], axis=1)

    n_id = lax.broadcasted_iota(jnp.int32, (SLC_PAD, QT), 0)
    q_blk = (qi * QT + lax.broadcasted_iota(jnp.int32, (SLC_PAD, QT), 1)) // SLC_LEN
    forced = (n_id == 0) | (n_id == q_blk) | (n_id == q_blk - 1)
    imp = jnp.where(forced, POS_BIG, jnp.where(n_id > q_blk, NEG, imp_t))
    imp_ref[...] = imp
    SUB = 8
    slabs = [imp[v * SUB:(v + 1) * SUB] for v in range(SLC_PAD // SUB)]
    ranks = [jnp.zeros((SUB, QT), jnp.int32) for _ in slabs]
    sub_id = lax.broadcasted_iota(jnp.int32, (SUB, QT), 0)
    for m in range(SLC_PAD):
        other = imp_ref[m:m + 1, :]
        for v, slab in enumerate(slabs):
            if v > m // SUB:
                beats = (other >= slab).astype(jnp.int32)
            elif v < m // SUB:
                beats = (other > slab).astype(jnp.int32)
            else:
                beats = jnp.where(sub_id > m % SUB, (other >= slab).astype(jnp.int32),
                                  (other > slab).astype(jnp.int32))
            ranks[v] = ranks[v] + beats
    rank = jnp.concatenate(ranks, axis=0)
    selb = jnp.where(rank < SLC_TOPK, 0.0, NEG).astype(BF16)
    qaug = jnp.concatenate([qs, jnp.concatenate([selb] * G, axis=1)], axis=0)

    PIECES = 4
    TK = PIECES * KP
    n_t = (PPT * qi + PPT - 1) // PIECES + 1
    at_start = (PPT * qi) % PIECES == 0
    last_bias = [jnp.where(at_start, u_0, 0.0), jnp.where(at_start, u_p1, u_m1),
                 jnp.where(at_start, NEG, u_0), jnp.where(at_start, NEG, u_p1)]
    prev_bias = [None, None, None, jnp.where(at_start, u_m1, 0.0)]
    m_ref[...] = jnp.full(m_ref.shape, NEG, F32)
    acc_ref[...] = jnp.zeros(acc_ref.shape, F32)

    HALVES = 2
    MH = M // HALVES

    def qk(dst_ref, t, hf):
        off = pl.multiple_of(t * TK, TK)
        lanes = slice(hf * MH, (hf + 1) * MH)
        dst_ref[:, lanes] = _dot(ksl_ref[pl.ds(off, TK), kcols], qaug[:, lanes])

    def consume(src_ref, t, bias, hf):
        off = pl.multiple_of(t * TK, TK)
        lanes = slice(hf * MH, (hf + 1) * MH)
        s = src_ref[:, lanes]
        if bias is not None:
            s = jnp.concatenate([s[j * KP:(j + 1) * KP] if bias[j] is None
                                 else s[j * KP:(j + 1) * KP] + bias[j][:, lanes]
                                 for j in range(PIECES)], axis=0)
        m_old = m_ref[:, lanes]
        m_new = jnp.maximum(m_old, jnp.max(s, axis=0, keepdims=True))
        p = jnp.exp2(s - m_new).astype(BF16)
        acc_ref[:, lanes] = acc_ref[:, lanes] * jnp.exp2(m_old - m_new) + _dot(vsl_ref[vrows, pl.ds(off, TK)], p)
        m_ref[:, lanes] = m_new

    def stage(nxt_ref, nxt_t, cur_ref, cur_t, bias):
        for hf in range(HALVES):
            if nxt_ref is not None:
                qk(nxt_ref, nxt_t, hf)
            consume(cur_ref, cur_t, bias, hf)

    n_plain = jnp.maximum(n_t - 2, 0)
    for hf in range(HALVES):
        qk(sa_ref, 0, hf)

    def pair(u, carry):
        t = 2 * u
        stage(sb_ref, t + 1, sa_ref, t, None)
        stage(sa_ref, t + 2, sb_ref, t + 1, None)
        return carry

    lax.fori_loop(0, n_plain // 2, pair, 0)
    tb = (n_plain // 2) * 2

    @pl.when(n_t == 1)
    def _():
        stage(None, None, sa_ref, 0, last_bias)

    @pl.when((n_t >= 2) & (n_plain % 2 == 0))
    def _():
        stage(sb_ref, tb + 1, sa_ref, tb, prev_bias)
        stage(None, None, sb_ref, tb + 1, last_bias)

    @pl.when(n_plain % 2 == 1)
    def _():
        stage(sb_ref, tb + 1, sa_ref, tb, None)
        stage(sa_ref, tb + 2, sb_ref, tb + 1, prev_bias)
        stage(None, None, sa_ref, tb + 2, last_bias)

    a_s = acc_ref[...]
    o_sel = a_s[:NSA_DIM] / a_s[NSA_DIM:NSA_DIM + 1]

    gate = gate_ref[hk * GATE_ROWS:(hk + 1) * GATE_ROWS, :]
    outs = []
    for g in range(G):
        cols = slice(g * QT, (g + 1) * QT)
        o_t = (gate[3 * g:3 * g + 1, :] * o_cmp[:, cols] + gate[3 * g + 1:3 * g + 2, :] * o_sel[:, cols]
               + gate[3 * g + 2:3 * g + 3, :] * o_win[:, cols])
        outs.append(o_t.T)
    o_ref[:, hk * G * NSA_DIM:(hk + 1) * G * NSA_DIM] = jnp.concatenate(outs, axis=1).astype(BF16)


def _overlap_t(seq):
    ncp = seq // CMP_STRIDE
    n_cmp = (seq - CMP_LEN) // CMP_STRIDE + 1
    n_slc = seq // SLC_LEN
    cs = np.arange(ncp)[None, :] * CMP_STRIDE
    ss = np.arange(SLC_PAD)[:, None] * SLC_LEN
    ov = np.maximum(np.minimum(cs + CMP_LEN, ss + SLC_LEN) - np.maximum(cs, ss), 0).astype(np.float32) / CMP_LEN
    ov = ov * (np.arange(ncp)[None, :] < n_cmp) * (np.arange(SLC_PAD)[:, None] < n_slc)
    return jnp.asarray(ov, BF16)


def _nsa(zq_t, kc, vc_t, nk, nv_t, gates_t, pat, utab, *, batch, seq):
    n = nk.shape[0]
    nq = seq // QT
    ncp = seq // CMP_STRIDE
    G = NSA_GROUP
    HK = NSA_KV_HEADS
    assert seq // SLC_LEN <= SLC_PAD and seq // SLC_LEN >= SLC_TOPK and seq >= WINDOW
    qcol = lambda b, qi: (0, b * nq + qi)
    return pl.pallas_call(
        _nsa_kernel,
        grid=(batch, nq),
        in_specs=[pl.BlockSpec((NSA_HEADS * NSA_DIM, QT), qcol),
                  pl.BlockSpec((1, HK, ncp, NSA_DIM), lambda b, qi: (b, 0, 0, 0)),
                  pl.BlockSpec((1, HK, NSA_DIM, ncp), lambda b, qi: (b, 0, 0, 0)),
                  pl.BlockSpec((seq, HK * LANE), lambda b, qi: (b, 0)),
                  pl.BlockSpec((HK * VROWS, seq), lambda b, qi: (0, b)),
                  pl.BlockSpec((seq, HK * LANE), lambda b, qi: (b, 1)),
                  pl.BlockSpec((HK * VROWS, seq), lambda b, qi: (1, b)),
                  pl.BlockSpec((HK * GATE_ROWS, QT), qcol),
                  _resident((NSA_HEADS, 2 * ncp, QT)),
                  _resident((HK, 3, KP, G * QT)),
                  _resident((SLC_PAD, ncp))],
        out_specs=pl.BlockSpec((QT, NSA_HEADS * NSA_DIM), lambda b, qi: (b * nq + qi, 0)),
        out_shape=jax.ShapeDtypeStruct((n, NSA_HEADS * NSA_DIM), BF16),
        scratch_shapes=[pltpu.VMEM((SLC_PAD, QT), F32),
                        pltpu.VMEM((VROWS, G * QT), F32),
                        pltpu.VMEM((1, G * QT), F32),
                        pltpu.VMEM((4 * KP, G * QT), F32),
                        pltpu.VMEM((4 * KP, G * QT), F32)],
        compiler_params=_params("parallel", "arbitrary"),
        name="nsa_attention",
    )(zq_t, kc, vc_t, nk, nv_t, nk, nv_t, gates_t, pat, utab, _overlap_t(seq))


MLA_TQ = 512
MLA_HP = 8


def _mla_kernel(q_ref, k_ref, v_ref, o_ref, acc_ref, m_ref, sa_ref, sb_ref):
    qi = pl.program_id(2)
    tq = MLA_TQ
    HP = MLA_HP
    qs = [q_ref[hh * LANE:(hh + 1) * LANE, :] for hh in range(HP)]
    m_ref[...] = jnp.full(m_ref.shape, NEG, F32)
    acc_ref[...] = jnp.zeros(acc_ref.shape, F32)

    def qk(dst_ref, off, hh):
        dst_ref[hh] = _dot(k_ref[pl.ds(off, tq), hh * LANE:(hh + 1) * LANE], qs[hh])

    def consume(src_ref, off, mask, hh):
        s = src_ref[hh]
        if mask is not None:
            s = s + mask
        vt = v_ref[hh * VROWS:(hh + 1) * VROWS, pl.ds(off, tq)]
        m_old = m_ref[hh]
        m_new = jnp.maximum(m_old, jnp.max(s, axis=0, keepdims=True))
        p = jnp.exp2(s - m_new).astype(BF16)
        acc_ref[hh] = acc_ref[hh] * jnp.exp2(m_old - m_new) + _dot(vt, p)
        m_ref[hh] = m_new

    def stage(nxt_ref, nxt_off, cur_ref, cur_off, mask):
        for hh in range(HP):
            if nxt_ref is not None:
                qk(nxt_ref, nxt_off, hh)
            consume(cur_ref, cur_off, mask, hh)

    key_j = lax.broadcasted_iota(jnp.int32, (tq, tq), 0)
    qry_i = lax.broadcasted_iota(jnp.int32, (tq, tq), 1)
    causal = jnp.where(key_j <= qry_i, 0.0, NEG)
    for hh in range(HP):
        qk(sa_ref, 0, hh)

    def pair(u, carry):
        off = pl.multiple_of(u * (2 * tq), 2 * tq)
        stage(sb_ref, off + tq, sa_ref, off, None)
        stage(sa_ref, off + 2 * tq, sb_ref, off + tq, None)
        return carry

    lax.fori_loop(0, qi // 2, pair, 0)
    base = pl.multiple_of((qi // 2) * (2 * tq), 2 * tq)

    @pl.when(qi % 2 == 0)
    def _():
        stage(None, None, sa_ref, base, causal)

    @pl.when(qi % 2 == 1)
    def _():
        stage(sb_ref, base + tq, sa_ref, base, None)
        stage(None, None, sb_ref, base + tq, causal)

    outs = []
    for hh in range(HP):
        a = acc_ref[hh]
        outs.append((a[:MLA_V] / a[MLA_V:MLA_V + 1]).T)
    o_ref[...] = jnp.concatenate(outs, axis=1).astype(BF16)


def _mla(qm_t, km, vm_t, *, batch, seq):
    n = km.shape[0]
    tq = MLA_TQ
    nq = seq // tq
    HP = MLA_HP
    return pl.pallas_call(
        _mla_kernel,
        grid=(batch, MLA_HEADS // HP, nq),
        in_specs=[pl.BlockSpec((HP * LANE, tq), lambda b, hp, qi: (hp, b * nq + qi)),
                  pl.BlockSpec((seq, HP * LANE), lambda b, hp, qi: (b, hp)),
                  pl.BlockSpec((HP * VROWS, seq), lambda b, hp, qi: (hp, b))],
        out_specs=pl.BlockSpec((tq, HP * MLA_V), lambda b, hp, qi: (b * nq + qi, hp)),
        out_shape=jax.ShapeDtypeStruct((n, MLA_HEADS * MLA_V), BF16),
        scratch_shapes=[pltpu.VMEM((HP, VROWS, tq), F32),
                        pltpu.VMEM((HP, 1, tq), F32),
                        pltpu.VMEM((HP, tq, tq), F32),
                        pltpu.VMEM((HP, tq, tq), F32)],
        compiler_params=_params("parallel", "parallel", "arbitrary"),
        name="mla_attention",
    )(qm_t, km, vm_t)


def _tail_kernel(x_ref, om_ref, on_ref, p_ref, wo_ref, g1_ref, g2_ref, g3_ref,
                 wg_ref, wu_ref, cw_ref, cb_ref, wd_ref, pg_ref, pp_ref,
                 o_ref, carry_ref, act_ref, *, seq, tm):
    n_chunks = D_FF // FF_CHUNK
    half = om_ref.shape[1]
    y = _dot(om_ref[...], wo_ref[:half, :]) + _dot(on_ref[...], wo_ref[half:, :])
    x = x_ref[...] + _rms(y, g1_ref[...])

    h = _rms(x, g2_ref[...]).astype(BF16)

    @pl.when((pl.program_id(0) * tm) % seq == 0)
    def _():
        carry_ref[...] = jnp.zeros(carry_ref.shape, F32)

    SUB = 8
    row8 = lax.broadcasted_iota(jnp.int32, (SUB, 1), 0)
    for c in range(n_chunks):
        cols = slice(c * FF_CHUNK, (c + 1) * FF_CHUNK)
        g = _dot(h, wg_ref[:, cols])
        prev = carry_ref[:, cols]
        carry_ref[:, cols] = g[tm - SUB:, :]
        r1 = pltpu.roll(g, 1, 0)
        r2 = pltpu.roll(g, 2, 0)
        top1 = jnp.where(row8 == 0, prev[7:8, :], r1[:SUB])
        top2 = jnp.where(row8 == 0, prev[6:7, :], jnp.where(row8 == 1, prev[7:8, :], r2[:SUB]))
        g1 = jnp.concatenate([top1, r1[SUB:]], axis=0)
        g2 = jnp.concatenate([top2, r2[SUB:]], axis=0)
        conv = (cw_ref[0:1, cols] * g2 + cw_ref[1:2, cols] * g1 + cw_ref[2:3, cols] * g
                + cb_ref[:, cols])
        act_ref[:, cols] = (_gelu_tanh(conv) * _dot(h, wu_ref[:, cols])).astype(BF16)
    x = x + _rms(_dot(act_ref[...], wd_ref[...]), g3_ref[...])

    gate = _sigmoid(_dot(x.astype(BF16), pg_ref[...]))
    o_ref[...] = x + gate * _dot(p_ref[...].astype(BF16), pp_ref[...])


def _tail(x2, om, on, p2, layer, wo, g1, g2, g3, wg, wu, cw, cb, wd, pg, pp, *, seq):
    n = x2.shape[0]
    tm = ROW_TILE
    row = lambda r: (r, 0)
    half = om.shape[1]
    lr = functools.partial(_layer_resident, layer)
    return pl.pallas_call(
        functools.partial(_tail_kernel, seq=seq, tm=tm),
        grid=(n // tm,),
        in_specs=[pl.BlockSpec((tm, D_MODEL), row),
                  pl.BlockSpec((tm, half), row),
                  pl.BlockSpec((tm, half), row),
                  pl.BlockSpec((None, tm, PLE_DIM), lambda r: (layer, r, 0)),
                  lr((2 * half, D_MODEL)),
                  lr((1, D_MODEL)),
                  lr((1, D_MODEL)),
                  lr((1, D_MODEL)),
                  lr((D_MODEL, D_FF)),
                  lr((D_MODEL, D_FF)),
                  lr((CONV_WIDTH, D_FF)),
                  lr((1, D_FF)),
                  lr((D_FF, D_MODEL)),
                  lr((D_MODEL, D_MODEL)),
                  lr((PLE_DIM, D_MODEL))],
        out_specs=pl.BlockSpec((tm, D_MODEL), row),
        out_shape=jax.ShapeDtypeStruct((n, D_MODEL), F32),
        scratch_shapes=[pltpu.VMEM((8, D_FF), F32),
                        pltpu.VMEM((tm, D_FF), BF16)],
        compiler_params=_params("arbitrary"),
        name="layer_tail",
    )(x2, om, on, p2, wo, g1, g2, g3, wg, wu, cw, cb, wd, pg, pp)


def _rot_rows(w):
    half = w.shape[-2] // 2
    return jnp.concatenate([-w[..., half:, :], w[..., :half, :]], axis=-2)


def _prep_inproj(w):
    wt = w.T
    o = np.cumsum((0,) + IN_SPLITS)
    cq, ckv, kr, qn, kcmp, vcmp, kslc, vslc, kwin, vwin, gn = [wt[o[j]:o[j + 1]] for j in range(len(IN_SPLITS))]
    d = w.shape[0]

    def zeros(rows):
        return jnp.zeros((rows, d), F32)

    def kslots(m):
        return jnp.concatenate([m[:NSA_DIM], zeros(LANE - NSA_DIM), m[NSA_DIM:], zeros(LANE - NSA_DIM)], axis=0)

    def vslots(m):
        return jnp.concatenate([m[:NSA_DIM], zeros(VROWS - NSA_DIM), m[NSA_DIM:], zeros(VROWS - NSA_DIM)], axis=0)

    pad_rope = zeros(LANE - MLA_NOPE - MLA_ROPE)
    w1 = jnp.concatenate([cq, ckv,
                          zeros(MLA_NOPE), kr, pad_rope,
                          zeros(MLA_NOPE), _rot_rows(kr), pad_rope,
                          kslots(kslc), kslots(kwin), kcmp, vcmp], axis=0)
    assert w1.shape[0] == C_TOTAL
    per = 3 * NSA_GROUP
    w1t = jnp.concatenate([qn * NSA_DIM ** -0.5, vslots(vslc), vslots(vwin),
                           gn[:per], zeros(GATE_ROWS - per), gn[per:], zeros(GATE_ROWS - per)], axis=0)
    assert w1t.shape[0] == R_TOTAL
    return w1.astype(BF16).T, w1t.astype(BF16)


def _prep_mla(w_uq, w_ukv):
    r = w_uq.shape[0]
    dq = MLA_NOPE + MLA_ROPE
    ut = w_uq.T.reshape(MLA_HEADS, dq, r)
    pad = jnp.zeros((MLA_HEADS, LANE - dq, r), F32)
    wqa_t = jnp.concatenate([ut, pad], axis=1).reshape(MLA_HEADS * LANE, r)
    wqb_t = jnp.concatenate([jnp.zeros((MLA_HEADS, MLA_NOPE, r), F32), _rot_rows(ut[:, MLA_NOPE:]), pad],
                            axis=1).reshape(MLA_HEADS * LANE, r)
    rk = w_ukv.shape[0]
    kv = w_ukv.reshape(rk, MLA_HEADS, 2, MLA_NOPE)
    wk = jnp.concatenate([kv[:, :, 0, :], jnp.zeros((rk, MLA_HEADS, LANE - MLA_NOPE), F32)],
                         axis=-1).reshape(rk, MLA_HEADS * LANE)
    vt = kv[:, :, 1, :].transpose(1, 2, 0)
    wvt = jnp.concatenate([vt, jnp.zeros((MLA_HEADS, VROWS - MLA_V, rk), F32)], axis=1).reshape(MLA_HEADS * VROWS, rk)
    return wqa_t.astype(BF16), wqb_t.astype(BF16), wk.astype(BF16), wvt.astype(BF16)


def kernel(x, p, positions, rel_bias, attn_pre_norm, attn_post_norm, ffn_pre_norm, ffn_post_norm,
           w_in, mla_q_norm, mla_w_uq, mla_kv_norm, mla_w_ukv, nsa_cmp_pos, nsa_cmp_w1, nsa_cmp_w2,
           w_o, ffn_w_gate, ffn_w_up, ffn_conv_w, ffn_conv_b, ffn_w_down, ple_proj, ple_gate):
    batch, seq, d = x.shape
    depth = w_in.shape[0]
    n = batch * seq
    x2 = x.reshape(n, d)
    ctt, stt, ct, st = _rope_tables(positions)
    pat, utab = _bias_tables(rel_bias, seq)
    tail_params = (w_o.astype(BF16), attn_post_norm[:, None, :], ffn_pre_norm[:, None, :],
                   ffn_post_norm[:, None, :], ffn_w_gate.astype(BF16), ffn_w_up.astype(BF16),
                   ffn_conv_w, ffn_conv_b[:, None, :], ffn_w_down.astype(BF16),
                   ple_gate.astype(BF16), ple_proj.astype(BF16))
    for i in range(depth):
        w1, w1t = _prep_inproj(w_in[i])
        wqa, wqb, wk, wvt = _prep_mla(mla_w_uq[i], mla_w_ukv[i])
        qm_t, km, vm_t, zq_t, nk, nv_t, kcmp, vcmp, gates_t = _inproj(
            x2, attn_pre_norm[i][None, :], w1, w1t, mla_q_norm[i][None, :], wqa, wqb,
            mla_kv_norm[i][None, :], wk, wvt, ct, st, ctt, stt, batch=batch, seq=seq)
        kc, vc_t = _compress(kcmp, vcmp,
                             nsa_cmp_pos[i],
                             nsa_cmp_w1[i].astype(BF16), nsa_cmp_w2[i, 0].astype(BF16),
                             nsa_cmp_w2[i, 1].T.astype(BF16), batch=batch, seq=seq)
        o_nsa = _nsa(zq_t, kc, vc_t, nk, nv_t, gates_t, pat, utab, batch=batch, seq=seq)
        o_mla = _mla(qm_t, km, vm_t, batch=batch, seq=seq)
        x2 = _tail(x2, o_mla, o_nsa, p.reshape(depth, n, PLE_DIM), i, *tail_params, seq=seq)
    return x2.reshape(batch, seq, d)
```

```python
import functools
import math

import numpy as np
import jax
import jax.numpy as jnp
from jax import lax
from jax.experimental import pallas as pl
from jax.experimental.pallas import tpu as pltpu

F32 = jnp.float32
BF16 = jnp.bfloat16

D_MODEL = 1024
DEPTH = 2
MLA_HEADS = 8
MLA_NOPE = 64
MLA_ROPE = 32
MLA_V = 64
MLA_Q_RANK = 256
MLA_KV_RANK = 128
ROPE_BASE = 10000.0
NSA_HEADS = 8
NSA_KV_HEADS = 2
NSA_GROUP = NSA_HEADS // NSA_KV_HEADS
NSA_DIM = 64
CMP_LEN = 32
CMP_STRIDE = 16
CMP_HIDDEN = 128
SLC_LEN = 64
SLC_TOPK = 16
WINDOW = 512
REL_BUCKETS = 32
REL_MAX_DIST = 128
D_FF = 2816
CONV_WIDTH = 3
PLE_DIM = 256
EPS = 1e-6
NEG = -1e30
POS_BIG = 1e30
LOG2E = math.log2(math.e)

IN_SPLITS = (MLA_Q_RANK, MLA_KV_RANK, MLA_ROPE, NSA_HEADS * NSA_DIM,
             NSA_KV_HEADS * NSA_DIM, NSA_KV_HEADS * NSA_DIM,
             NSA_KV_HEADS * NSA_DIM, NSA_KV_HEADS * NSA_DIM,
             NSA_KV_HEADS * NSA_DIM, NSA_KV_HEADS * NSA_DIM,
             3 * NSA_HEADS)

LANE = 128
QT = 256
KP = 128
SLC_PAD = 64
FF_CHUNK = 256
ROW_TILE = 1024
VMEM_LIMIT = 56 * 1024 * 1024

VROWS = 80

C_CQ = 0
C_CKV = 256
C_KR = 384
C_KRROT = 512
C_KSLC = 640
C_KWIN = 896
C_KCMP = 1152
C_VCMP = 1280
C_TOTAL = 1408
R_QN = 0
R_VSLC = 512
R_VWIN = R_VSLC + NSA_KV_HEADS * VROWS
R_GATE = R_VWIN + NSA_KV_HEADS * VROWS
GATE_ROWS = 16
R_TOTAL = R_GATE + NSA_KV_HEADS * GATE_ROWS


def _dot(a, b):
    return jnp.dot(a, b, preferred_element_type=F32)


def _dot_nt(a, b):
    return lax.dot_general(a, b, (((1,), (1,)), ((), ())), preferred_element_type=F32)


def _rms(x, g):
    return x * lax.rsqrt(jnp.mean(x * x, axis=-1, keepdims=True) + EPS) * g


def _gelu_tanh(x):
    return 0.5 * x * (1.0 + jnp.tanh(math.sqrt(2.0 / math.pi) * (x + 0.044715 * (x * x * x))))


def _sigmoid(x):
    return 1.0 / (1.0 + jnp.exp(-x))


def _params(*sem):
    return pltpu.CompilerParams(dimension_semantics=sem, vmem_limit_bytes=VMEM_LIMIT)


def _resident(shape):
    nd = len(shape)
    return pl.BlockSpec(shape, lambda *_: (0,) * nd, pipeline_mode=pl.Buffered(1))


def _layer_resident(layer, shape):
    nd = len(shape)
    return pl.BlockSpec((None,) + tuple(shape), lambda *_: (layer,) + (0,) * nd, pipeline_mode=pl.Buffered(1))


def _rope_kernel(pos_ref, inv_ref, ctt_ref, stt_ref, ct_ref, st_ref):
    ang = inv_ref[...] * pos_ref[...].astype(F32)
    row = lax.broadcasted_iota(jnp.int32, ang.shape, 0)
    rope = (row >= MLA_NOPE) & (row < MLA_NOPE + MLA_ROPE)
    ct = jnp.where(rope, jnp.cos(ang), jnp.where(row < MLA_NOPE, 1.0, 0.0))
    st = jnp.where(rope, jnp.sin(ang), 0.0)
    ctt_ref[...] = ct
    stt_ref[...] = st
    ct_ref[...] = ct.T
    st_ref[...] = st.T


def _rope_tables(positions):
    n = positions.size
    tn = 2048
    half = MLA_ROPE // 2
    inv = ROPE_BASE ** (-jnp.arange(half, dtype=F32) / half)
    inv_slot = jnp.concatenate([jnp.zeros((MLA_NOPE,), F32), inv, inv,
                                jnp.zeros((LANE - MLA_NOPE - MLA_ROPE,), F32)])[:, None]
    return pl.pallas_call(
        _rope_kernel,
        grid=(n // tn,),
        in_specs=[pl.BlockSpec((1, tn), lambda r: (0, r)),
                  pl.BlockSpec((LANE, 1), lambda r: (0, 0))],
        out_specs=[pl.BlockSpec((LANE, tn), lambda r: (0, r))] * 2
                  + [pl.BlockSpec((tn, LANE), lambda r: (r, 0))] * 2,
        out_shape=[jax.ShapeDtypeStruct((LANE, n), F32)] * 2 + [jax.ShapeDtypeStruct((n, LANE), F32)] * 2,
        compiler_params=_params("parallel"),
        name="rope_tables",
    )(positions.reshape(1, n), inv_slot)


def _bucket_np(dist):
    n = np.maximum(dist, 0)
    max_exact = REL_BUCKETS // 2
    large = max_exact + (np.log(np.maximum(n, 1).astype(np.float32) / max_exact)
                         / math.log(REL_MAX_DIST / max_exact)
                         * (REL_BUCKETS - max_exact)).astype(np.int32)
    large = np.minimum(large, REL_BUCKETS - 1)
    return np.where(n < max_exact, n, large).astype(np.int32)


def _bias_kernel(table_ref, bpc_ref, bpu_ref, pat_ref, u_ref):
    h = pl.program_id(0)
    far = table_ref[REL_BUCKETS - 1, h]

    def lookup(bp, sub):
        acc = jnp.full(bp.shape, far - sub, F32)
        for b in range(REL_BUCKETS - 1):
            acc = jnp.where(bp == b, table_ref[b, h] - sub, acc)
        return jnp.where(bp < 0, NEG, acc * LOG2E)

    pat_ref[0] = lookup(bpc_ref[...], 0.0)
    for d in range(3):
        u_ref[0, d] = lookup(bpu_ref[d], far)


def _bias_tables(rel_bias, seq):
    ncp = seq // CMP_STRIDE
    i = np.arange(QT)[None, :]
    cprime = np.arange(2 * ncp)[:, None] - ncp
    dist_c = i - CMP_STRIDE * cprime - (CMP_LEN - 1)
    bpc = np.where(dist_c >= 0, _bucket_np(dist_c), -1).astype(np.int32)
    j = np.arange(KP)[:, None]
    bpu = []
    for delta in (-1, 0, 1):
        dist = i - KP * delta - j
        bpu.append(np.where(dist >= 0, _bucket_np(dist), -1))
    bpu = np.stack(bpu).astype(np.int32)
    return pl.pallas_call(
        _bias_kernel,
        grid=(NSA_HEADS,),
        in_specs=[pl.BlockSpec(memory_space=pltpu.SMEM),
                  pl.BlockSpec((2 * ncp, QT), lambda h: (0, 0)),
                  pl.BlockSpec((3, KP, QT), lambda h: (0, 0, 0))],
        out_specs=[pl.BlockSpec((1, 2 * ncp, QT), lambda h: (h, 0, 0)),
                   pl.BlockSpec((1, 3, KP, QT), lambda h: (h // NSA_GROUP, 0, 0, h % NSA_GROUP))],
        out_shape=[jax.ShapeDtypeStruct((NSA_HEADS, 2 * ncp, QT), F32),
                   jax.ShapeDtypeStruct((NSA_KV_HEADS, 3, KP, NSA_GROUP * QT), F32)],
        compiler_params=_params("parallel"),
        name="bias_tables",
    )(rel_bias.astype(F32), jnp.asarray(bpc), jnp.asarray(bpu))


def _inproj_kernel(x_ref, g_ref, w1_ref, w1t_ref, qn_ref, wqa_ref, wqb_ref, kvn_ref, wk_ref, wvt_ref,
                   ct_ref, st_ref, ctt_ref, stt_ref, ones_n_ref, ones_m_ref,
                   qm_ref, km_ref, vm_ref, zq_ref, nk_ref, nv_ref, kcmp_ref, vcmp_ref, gate_ref,
                   *, seq, tm):
    h = _rms(x_ref[...], g_ref[...]).astype(BF16)

    def proj(c0, width):
        return _dot(h, w1_ref[:, c0:c0 + width])

    cq = _rms(proj(C_CQ, MLA_Q_RANK), qn_ref[...]).astype(BF16)
    qa = _dot_nt(wqa_ref[...], cq)
    qb = _dot_nt(wqb_ref[...], cq)
    scale = (MLA_NOPE + MLA_ROPE) ** -0.5 * LOG2E
    cts = ctt_ref[...] * scale
    sts = stt_ref[...] * scale
    for hh in range(MLA_HEADS):
        sl = slice(hh * LANE, (hh + 1) * LANE)
        qm_ref[sl, :] = (qa[sl] * cts + qb[sl] * sts).astype(BF16)

    ckv = _rms(proj(C_CKV, MLA_KV_RANK), kvn_ref[...]).astype(BF16)
    kr = proj(C_KR, LANE) * ct_ref[...] + proj(C_KRROT, LANE) * st_ref[...]
    kn = _dot(ckv, wk_ref[...])
    for hh in range(MLA_HEADS):
        sl = slice(hh * LANE, (hh + 1) * LANE)
        km_ref[:, sl] = (kn[:, sl] + kr).astype(BF16)
    vm_ref[...] = (_dot_nt(wvt_ref[...], ckv) + ones_m_ref[...]).astype(BF16)

    zt = _dot_nt(w1t_ref[...], h)
    zq_ref[...] = (zt[R_QN:R_VSLC] * LOG2E).astype(BF16)
    nv_ref[...] = (zt[R_VSLC:R_GATE] + ones_n_ref[...]).astype(BF16)
    gate_ref[...] = _sigmoid(zt[R_GATE:R_TOTAL])

    lane = lax.broadcasted_iota(jnp.int32, (tm, LANE), 1)
    s0 = (pl.program_id(0) * tm) % seq
    row = lax.broadcasted_iota(jnp.int32, (tm, LANE), 0)
    onehot = (lane - NSA_DIM == (s0 + row) // SLC_LEN).astype(F32)
    ksl = proj(C_KSLC, 2 * LANE)
    for hk in range(NSA_KV_HEADS):
        sl = slice(hk * LANE, (hk + 1) * LANE)
        nk_ref[:, sl] = (ksl[:, sl] + onehot).astype(BF16)
    nk_ref[:, 2 * LANE:] = proj(C_KWIN, 2 * LANE).astype(BF16)

    kcmp = proj(C_KCMP, LANE)
    vcmp = proj(C_VCMP, LANE)
    for hk in range(NSA_KV_HEADS):
        kcmp_ref[0, hk] = kcmp[:, hk * NSA_DIM:(hk + 1) * NSA_DIM]
        vcmp_ref[0, hk] = vcmp[:, hk * NSA_DIM:(hk + 1) * NSA_DIM]


def _ones_rows(n_slots):
    r = np.arange(n_slots * VROWS) % VROWS
    return jnp.asarray((r >= NSA_DIM).astype(np.float32)[:, None])


def _inproj(x2, g_pre, w1, w1t, qn, wqa, wqb, kvn, wk, wvt, ct, st, ctt, stt, *, batch, seq):
    n = x2.shape[0]
    tm = ROW_TILE
    tiles_per_seq = seq // tm
    row = lambda r: (r, 0)
    col = lambda r: (0, r)
    hm = lambda r: (r // tiles_per_seq, 0, r % tiles_per_seq, 0)
    nv_rows = 2 * NSA_KV_HEADS * VROWS
    vm_rows = MLA_HEADS * VROWS
    return pl.pallas_call(
        functools.partial(_inproj_kernel, seq=seq, tm=tm),
        grid=(n // tm,),
        in_specs=[pl.BlockSpec((tm, D_MODEL), row),
                  _resident((1, D_MODEL)),
                  _resident((D_MODEL, C_TOTAL)),
                  _resident((R_TOTAL, D_MODEL)),
                  _resident((1, MLA_Q_RANK)),
                  _resident((MLA_HEADS * LANE, MLA_Q_RANK)),
                  _resident((MLA_HEADS * LANE, MLA_Q_RANK)),
                  _resident((1, MLA_KV_RANK)),
                  _resident((MLA_KV_RANK, MLA_HEADS * LANE)),
                  _resident((vm_rows, MLA_KV_RANK)),
                  pl.BlockSpec((tm, LANE), row),
                  pl.BlockSpec((tm, LANE), row),
                  pl.BlockSpec((LANE, tm), col),
                  pl.BlockSpec((LANE, tm), col),
                  _resident((nv_rows, 1)),
                  _resident((vm_rows, 1))],
        out_specs=[pl.BlockSpec((MLA_HEADS * LANE, tm), col),
                   pl.BlockSpec((tm, MLA_HEADS * LANE), row),
                   pl.BlockSpec((vm_rows, tm), col),
                   pl.BlockSpec((NSA_HEADS * NSA_DIM, tm), col),
                   pl.BlockSpec((tm, 4 * LANE), row),
                   pl.BlockSpec((nv_rows, tm), col),
                   pl.BlockSpec((1, NSA_KV_HEADS, tm, NSA_DIM), hm),
                   pl.BlockSpec((1, NSA_KV_HEADS, tm, NSA_DIM), hm),
                   pl.BlockSpec((NSA_KV_HEADS * GATE_ROWS, tm), col)],
        out_shape=[jax.ShapeDtypeStruct((MLA_HEADS * LANE, n), BF16),
                   jax.ShapeDtypeStruct((n, MLA_HEADS * LANE), BF16),
                   jax.ShapeDtypeStruct((vm_rows, n), BF16),
                   jax.ShapeDtypeStruct((NSA_HEADS * NSA_DIM, n), BF16),
                   jax.ShapeDtypeStruct((n, 4 * LANE), BF16),
                   jax.ShapeDtypeStruct((nv_rows, n), BF16),
                   jax.ShapeDtypeStruct((batch, NSA_KV_HEADS, seq, NSA_DIM), F32),
                   jax.ShapeDtypeStruct((batch, NSA_KV_HEADS, seq, NSA_DIM), F32),
                   jax.ShapeDtypeStruct((NSA_KV_HEADS * GATE_ROWS, n), F32)],
        compiler_params=_params("parallel"),
        name="in_proj",
    )(x2, g_pre, w1, w1t, qn, wqa, wqb, kvn, wk, wvt, ct, st, ctt, stt,
      _ones_rows(2 * NSA_KV_HEADS), _ones_rows(MLA_HEADS))


def _compress_kernel(k_ref, v_ref, pos_ref, w1_ref, w2k_ref, w2vt_ref, kc_ref, vc_ref):
    ncp = kc_ref.shape[2]

    def hidden(x_ref, j):
        u = jnp.zeros((ncp, CMP_HIDDEN), F32)
        low = jnp.zeros((ncp, CMP_HIDDEN), F32)
        for l in range(CMP_STRIDE):
            x = x_ref[0, 0, pl.ds(l, ncp, stride=CMP_STRIDE), :]
            top = (x + pos_ref[j, l:l + 1, :]).astype(BF16)
            bot = (x + pos_ref[j, CMP_STRIDE + l:CMP_STRIDE + l + 1, :]).astype(BF16)
            u = u + _dot(top, w1_ref[j, l * NSA_DIM:(l + 1) * NSA_DIM, :])
            low = low + _dot(bot, w1_ref[j, (CMP_STRIDE + l) * NSA_DIM:(CMP_STRIDE + l + 1) * NSA_DIM, :])
        nxt = jnp.concatenate([low[1:], jnp.zeros((1, CMP_HIDDEN), F32)], axis=0)
        return _gelu_tanh(u + nxt).astype(BF16)

    kc_ref[0, 0] = _dot(hidden(k_ref, 0), w2k_ref[...]).astype(BF16)
    vc_ref[0, 0] = _dot_nt(w2vt_ref[...], hidden(v_ref, 1)).astype(BF16)


def _compress(kcmp, vcmp, pos, w1, w2k, w2vt, *, batch, seq):
    ncp = seq // CMP_STRIDE
    blk = lambda b, hk: (b, hk, 0, 0)
    return pl.pallas_call(
        _compress_kernel,
        grid=(batch, NSA_KV_HEADS),
        in_specs=[pl.BlockSpec((1, 1, seq, NSA_DIM), blk),
                  pl.BlockSpec((1, 1, seq, NSA_DIM), blk),
                  pl.BlockSpec((2, CMP_LEN, NSA_DIM), lambda b, hk: (0, 0, 0)),
                  pl.BlockSpec((2, CMP_LEN * NSA_DIM, CMP_HIDDEN), lambda b, hk: (0, 0, 0)),
                  pl.BlockSpec((CMP_HIDDEN, NSA_DIM), lambda b, hk: (0, 0)),
                  pl.BlockSpec((NSA_DIM, CMP_HIDDEN), lambda b, hk: (0, 0))],
        out_specs=[pl.BlockSpec((1, 1, ncp, NSA_DIM), blk),
                   pl.BlockSpec((1, 1, NSA_DIM, ncp), blk)],
        out_shape=[jax.ShapeDtypeStruct((batch, NSA_KV_HEADS, ncp, NSA_DIM), BF16),
                   jax.ShapeDtypeStruct((batch, NSA_KV_HEADS, NSA_DIM, ncp), BF16)],
        compiler_params=_params("parallel", "parallel"),
        name="nsa_compress",
    )(kcmp, vcmp, pos, w1, w2k, w2vt)


def _nsa_kernel(*refs):
    for hk in range(NSA_KV_HEADS):
        _nsa_group(hk, *refs)


def _nsa_group(hk, zq_ref, kc_ref, vc_ref, ksl_ref, vsl_ref, kw_ref, vw_ref, gate_ref,
               pat_ref, u_ref, ovt_ref, o_ref, imp_ref, acc_ref, m_ref, sa_ref, sb_ref):
    qi = pl.program_id(1)
    G = NSA_GROUP
    kcols = slice(hk * LANE, (hk + 1) * LANE)
    vrows = slice(hk * VROWS, (hk + 1) * VROWS)
    M = G * QT
    PPT = QT // KP
    q4 = zq_ref[hk * G * NSA_DIM:(hk + 1) * G * NSA_DIM, :]
    qs = jnp.concatenate([q4[g * NSA_DIM:(g + 1) * NSA_DIM, :] for g in range(G)], axis=1)
    kc = kc_ref[0, hk]
    vc_t = vc_ref[0, hk]
    ovt = ovt_ref[...]
    ncp = kc.shape[0]
    qcol = qi * QT + (lax.broadcasted_iota(jnp.int32, (1, M), 1) & (QT - 1))
    u_m1 = u_ref[hk, 0]
    u_0 = u_ref[hk, 1]
    u_p1 = u_ref[hk, 2]

    n_back = WINDOW // KP
    n_w = n_back + PPT
    MS = G * KP

    def sub_lanes(x, a):
        return jnp.concatenate([x[:, g * QT + a * KP:g * QT + (a + 1) * KP] for g in range(G)], axis=1)

    qw = jnp.concatenate([qs, jnp.zeros_like(qs)], axis=0)
    pad_v = jnp.where(lax.broadcasted_iota(jnp.int32, (VROWS, KP), 0) >= NSA_DIM, 1.0, 0.0).astype(BF16)
    k_rows, v_cols = [], []
    for w in range(n_w):
        piece = PPT * qi - n_back + w
        off = pl.multiple_of(jnp.maximum(piece, 0) * KP, KP)
        kt = kw_ref[pl.ds(off, KP), kcols]
        vt = vw_ref[vrows, pl.ds(off, KP)]
        if w < n_back:
            kt = jnp.where(piece >= 0, kt, jnp.zeros_like(kt))
            vt = jnp.where(piece >= 0, vt, pad_v)
        k_rows.append(kt)
        v_cols.append(vt)
    s_sub = [_dot(jnp.concatenate(k_rows[a:a + n_back + 1], axis=0), sub_lanes(qw, a))
             for a in range(PPT)]

    slide = QT // CMP_STRIDE
    off_b = pl.multiple_of(ncp - slide * qi, slide)
    bias = jnp.concatenate([pat_ref[hk * G + g, pl.ds(off_b, ncp), :] for g in range(G)], axis=1)
    s = _dot(kc, qs) + bias
    e = jnp.exp2(s - jnp.max(s, axis=0, keepdims=True))
    scale = jnp.where(qcol >= CMP_LEN - 1, 1.0 / jnp.sum(e, axis=0, keepdims=True), 0.0)
    p_cmp = (e * scale).astype(BF16)

    key_j = lax.broadcasted_iota(jnp.int32, (KP, MS), 0)
    q_loc = lax.broadcasted_iota(jnp.int32, (KP, MS), 1) & (KP - 1)
    u_tabs = [u_m1, u_0, u_p1]
    p_sub = []
    for a in range(PPT):
        pieces = [s_sub[a][w * KP:(w + 1) * KP] for w in range(n_back + 1)]
        pieces[0] = pieces[0] + jnp.where(key_j > q_loc, 0.0, NEG)
        pieces[n_back - 1] = pieces[n_back - 1] + sub_lanes(u_tabs[a], a)
        pieces[n_back] = pieces[n_back] + sub_lanes(u_tabs[a + 1], a)
        s_a = jnp.concatenate(pieces, axis=0)
        p_sub.append(jnp.exp2(s_a - jnp.max(s_a, axis=0, keepdims=True)).astype(BF16))

    o_cmp = _dot(vc_t, p_cmp)
    imp_t = jnp.zeros((SLC_PAD, QT), F32)
    for g in range(G):
        imp_t = imp_t + _dot(ovt, p_cmp[:, g * QT:(g + 1) * QT])

    o_sub = []
    for a in range(PPT):
        a_w = _dot(jnp.concatenate(v_cols[a:a + n_back + 1], axis=1), p_sub[a])
        o_sub.append(a_w[:NSA_DIM] / a_w[NSA_DIM:NSA_DIM + 1])
    o_win = jnp.concatenate([o_sub[a][:, g * KP:(g + 1) * KP] for g in range(G) for a in range(PPT)], axis=1)

    n_id = lax.broadcasted_iota(jnp.int32, (SLC_PAD, QT), 0)
    q_blk = (qi * QT + lax.broadcasted_iota(jnp.int32, (SLC_PAD, QT), 1)) // SLC_LEN
    forced = (n_id == 0) | (n_id == q_blk) | (n_id == q_blk - 1)
    imp = jnp.where(forced, POS_BIG, jnp.where(n_id > q_blk, NEG, imp_t))
    imp_ref[...] = imp
    SUB = 8
    slabs = [imp[v * SUB:(v + 1) * SUB] for v in range(SLC_PAD // SUB)]
    ranks = [jnp.zeros((SUB, QT), jnp.int32) for _ in slabs]
    sub_id = lax.broadcasted_iota(jnp.int32, (SUB, QT), 0)
    for m in range(SLC_PAD):
        other = imp_ref[m:m + 1, :]
        for v, slab in enumerate(slabs):
            if v > m // SUB:
                beats = (other >= slab).astype(jnp.int32)
            elif v < m // SUB:
                beats = (other > slab).astype(jnp.int32)
            else:
                beats = jnp.where(sub_id > m % SUB, (other >= slab).astype(jnp.int32),
                                  (other > slab).astype(jnp.int32))
            ranks[v] = ranks[v] + beats
    rank = jnp.concatenate(ranks, axis=0)
    selb = jnp.where(rank < SLC_TOPK, 0.0, NEG).astype(BF16)
    qaug = jnp.concatenate([qs, jnp.concatenate([selb] * G, axis=1)], axis=0)

    PIECES = 4
    TK = PIECES * KP
    n_t = (PPT * qi + PPT - 1) // PIECES + 1
    at_start = (PPT * qi) % PIECES == 0
    last_bias = [jnp.where(at_start, u_0, 0.0), jnp.where(at_start, u_p1, u_m1),
                 jnp.where(at_start, NEG, u_0), jnp.where(at_start, NEG, u_p1)]
    prev_bias = [None, None, None, jnp.where(at_start, u_m1, 0.0)]
    m_ref[...] = jnp.full(m_ref.shape, NEG, F32)
    acc_ref[...] = jnp.zeros(acc_ref.shape, F32)

    HALVES = 2
    MH = M // HALVES

    def qk(dst_ref, t, hf):
        off = pl.multiple_of(t * TK, TK)
        lanes = slice(hf * MH, (hf + 1) * MH)
        dst_ref[:, lanes] = _dot(ksl_ref[pl.ds(off, TK), kcols], qaug[:, lanes])

    def consume(src_ref, t, bias, hf):
        off = pl.multiple_of(t * TK, TK)
        lanes = slice(hf * MH, (hf + 1) * MH)
        s = src_ref[:, lanes]
        if bias is not None:
            s = jnp.concatenate([s[j * KP:(j + 1) * KP] if bias[j] is None
                                 else s[j * KP:(j + 1) * KP] + bias[j][:, lanes]
                                 for j in range(PIECES)], axis=0)
        m_old = m_ref[:, lanes]
        m_new = jnp.maximum(m_old, jnp.max(s, axis=0, keepdims=True))
        p = jnp.exp2(s - m_new).astype(BF16)
        acc_ref[:, lanes] = acc_ref[:, lanes] * jnp.exp2(m_old - m_new) + _dot(vsl_ref[vrows, pl.ds(off, TK)], p)
        m_ref[:, lanes] = m_new

    def stage(nxt_ref, nxt_t, cur_ref, cur_t, bias):
        for hf in range(HALVES):
            if nxt_ref is not None:
                qk(nxt_ref, nxt_t, hf)
            consume(cur_ref, cur_t, bias, hf)

    n_plain = jnp.maximum(n_t - 2, 0)
    for hf in range(HALVES):
        qk(sa_ref, 0, hf)

    def pair(u, carry):
        t = 2 * u
        stage(sb_ref, t + 1, sa_ref, t, None)
        stage(sa_ref, t + 2, sb_ref, t + 1, None)
        return carry

    lax.fori_loop(0, n_plain // 2, pair, 0)
    tb = (n_plain // 2) * 2

    @pl.when(n_t == 1)
    def _():
        stage(None, None, sa_ref, 0, last_bias)

    @pl.when((n_t >= 2) & (n_plain % 2 == 0))
    def _():
        stage(sb_ref, tb + 1, sa_ref, tb, prev_bias)
        stage(None, None, sb_ref, tb + 1, last_bias)

    @pl.when(n_plain % 2 == 1)
    def _():
        stage(sb_ref, tb + 1, sa_ref, tb, None)
        stage(sa_ref, tb + 2, sb_ref, tb + 1, prev_bias)
        stage(None, None, sa_ref, tb + 2, last_bias)

    a_s = acc_ref[...]
    o_sel = a_s[:NSA_DIM] / a_s[NSA_DIM:NSA_DIM + 1]

    gate = gate_ref[hk * GATE_ROWS:(hk + 1) * GATE_ROWS, :]
    outs = []
    for g in range(G):
        cols = slice(g * QT, (g + 1) * QT)
        o_t = (gate[3 * g:3 * g + 1, :] * o_cmp[:, cols] + gate[3 * g + 1:3 * g + 2, :] * o_sel[:, cols]
               + gate[3 * g + 2:3 * g + 3, :] * o_win[:, cols])
        outs.append(o_t.T)
    o_ref[:, hk * G * NSA_DIM:(hk + 1) * G * NSA_DIM] = jnp.concatenate(outs, axis=1).astype(BF16)


def _overlap_t(seq):
    ncp = seq // CMP_STRIDE
    n_cmp = (seq - CMP_LEN) // CMP_STRIDE + 1
    n_slc = seq // SLC_LEN
    cs = np.arange(ncp)[None, :] * CMP_STRIDE
    ss = np.arange(SLC_PAD)[:, None] * SLC_LEN
    ov = np.maximum(np.minimum(cs + CMP_LEN, ss + SLC_LEN) - np.maximum(cs, ss), 0).astype(np.float32) / CMP_LEN
    ov = ov * (np.arange(ncp)[None, :] < n_cmp) * (np.arange(SLC_PAD)[:, None] < n_slc)
    return jnp.asarray(ov, BF16)


def _nsa(zq_t, kc, vc_t, nk, nv_t, gates_t, pat, utab, *, batch, seq):
    n = nk.shape[0]
    nq = seq // QT
    ncp = seq // CMP_STRIDE
    G = NSA_GROUP
    HK = NSA_KV_HEADS
    assert seq // SLC_LEN <= SLC_PAD and seq // SLC_LEN >= SLC_TOPK and seq >= WINDOW
    qcol = lambda b, qi: (0, b * nq + qi)
    return pl.pallas_call(
        _nsa_kernel,
        grid=(batch, nq),
        in_specs=[pl.BlockSpec((NSA_HEADS * NSA_DIM, QT), qcol),
                  pl.BlockSpec((1, HK, ncp, NSA_DIM), lambda b, qi: (b, 0, 0, 0)),
                  pl.BlockSpec((1, HK, NSA_DIM, ncp), lambda b, qi: (b, 0, 0, 0)),
                  pl.BlockSpec((seq, HK * LANE), lambda b, qi: (b, 0)),
                  pl.BlockSpec((HK * VROWS, seq), lambda b, qi: (0, b)),
                  pl.BlockSpec((seq, HK * LANE), lambda b, qi: (b, 1)),
                  pl.BlockSpec((HK * VROWS, seq), lambda b, qi: (1, b)),
                  pl.BlockSpec((HK * GATE_ROWS, QT), qcol),
                  _resident((NSA_HEADS, 2 * ncp, QT)),
                  _resident((HK, 3, KP, G * QT)),
                  _resident((SLC_PAD, ncp))],
        out_specs=pl.BlockSpec((QT, NSA_HEADS * NSA_DIM), lambda b, qi: (b * nq + qi, 0)),
        out_shape=jax.ShapeDtypeStruct((n, NSA_HEADS * NSA_DIM), BF16),
        scratch_shapes=[pltpu.VMEM((SLC_PAD, QT), F32),
                        pltpu.VMEM((VROWS, G * QT), F32),
                        pltpu.VMEM((1, G * QT), F32),
                        pltpu.VMEM((4 * KP, G * QT), F32),
                        pltpu.VMEM((4 * KP, G * QT), F32)],
        compiler_params=_params("parallel", "arbitrary"),
        name="nsa_attention",
    )(zq_t, kc, vc_t, nk, nv_t, nk, nv_t, gates_t, pat, utab, _overlap_t(seq))


MLA_TQ = 512
MLA_HP = 8


def _mla_kernel(q_ref, k_ref, v_ref, o_ref, acc_ref, m_ref, sa_ref, sb_ref):
    qi = pl.program_id(2)
    tq = MLA_TQ
    HP = MLA_HP
    qs = [q_ref[hh * LANE:(hh + 1) * LANE, :] for hh in range(HP)]
    m_ref[...] = jnp.full(m_ref.shape, NEG, F32)
    acc_ref[...] = jnp.zeros(acc_ref.shape, F32)

    def qk(dst_ref, off, hh):
        dst_ref[hh] = _dot(k_ref[pl.ds(off, tq), hh * LANE:(hh + 1) * LANE], qs[hh])

    def update(hh, s, vt, lanes):
        m_old = m_ref[hh, :, lanes]
        m_new = jnp.maximum(m_old, jnp.max(s, axis=0, keepdims=True))
        p = jnp.exp2(s - m_new).astype(BF16)
        acc_ref[hh, :, lanes] = acc_ref[hh, :, lanes] * jnp.exp2(m_old - m_new) + _dot(vt, p)
        m_ref[hh, :, lanes] = m_new

    def consume(src_ref, off, mask, hh):
        vrows = slice(hh * VROWS, (hh + 1) * VROWS)
        if mask is None:
            update(hh, src_ref[hh], v_ref[vrows, pl.ds(off, tq)], slice(0, tq))
            return
        hq = tq // 2
        update(hh, src_ref[hh, :hq, :hq] + mask, v_ref[vrows, pl.ds(off, hq)], slice(0, hq))
        s_b = src_ref[hh, :, hq:]
        s_b = jnp.concatenate([s_b[:hq], s_b[hq:] + mask], axis=0)
        update(hh, s_b, v_ref[vrows, pl.ds(off, tq)], slice(hq, tq))

    def stage(nxt_ref, nxt_off, cur_ref, cur_off, mask):
        for hh in range(HP):
            if nxt_ref is not None:
                qk(nxt_ref, nxt_off, hh)
            consume(cur_ref, cur_off, mask, hh)

    key_j = lax.broadcasted_iota(jnp.int32, (tq // 2, tq // 2), 0)
    qry_i = lax.broadcasted_iota(jnp.int32, (tq // 2, tq // 2), 1)
    causal = jnp.where(key_j <= qry_i, 0.0, NEG)
    for hh in range(HP):
        qk(sa_ref, 0, hh)

    def pair(u, carry):
        off = pl.multiple_of(u * (2 * tq), 2 * tq)
        stage(sb_ref, off + tq, sa_ref, off, None)
        stage(sa_ref, off + 2 * tq, sb_ref, off + tq, None)
        return carry

    lax.fori_loop(0, qi // 2, pair, 0)
    base = pl.multiple_of((qi // 2) * (2 * tq), 2 * tq)

    @pl.when(qi % 2 == 0)
    def _():
        stage(None, None, sa_ref, base, causal)

    @pl.when(qi % 2 == 1)
    def _():
        stage(sb_ref, base + tq, sa_ref, base, None)
        stage(None, None, sb_ref, base + tq, causal)

    outs = []
    for hh in range(HP):
        a = acc_ref[hh]
        outs.append((a[:MLA_V] / a[MLA_V:MLA_V + 1]).T)
    o_ref[...] = jnp.concatenate(outs, axis=1).astype(BF16)


def _mla(qm_t, km, vm_t, *, batch, seq):
    n = km.shape[0]
    tq = MLA_TQ
    nq = seq // tq
    HP = MLA_HP
    return pl.pallas_call(
        _mla_kernel,
        grid=(batch, MLA_HEADS // HP, nq),
        in_specs=[pl.BlockSpec((HP * LANE, tq), lambda b, hp, qi: (hp, b * nq + qi)),
                  pl.BlockSpec((seq, HP * LANE), lambda b, hp, qi: (b, hp)),
                  pl.BlockSpec((HP * VROWS, seq), lambda b, hp, qi: (hp, b))],
        out_specs=pl.BlockSpec((tq, HP * MLA_V), lambda b, hp, qi: (b * nq + qi, hp)),
        out_shape=jax.ShapeDtypeStruct((n, MLA_HEADS * MLA_V), BF16),
        scratch_shapes=[pltpu.VMEM((HP, VROWS, tq), F32),
                        pltpu.VMEM((HP, 1, tq), F32),
                        pltpu.VMEM((HP, tq, tq), F32),
                        pltpu.VMEM((HP, tq, tq), F32)],
        compiler_params=_params("parallel", "parallel", "arbitrary"),
        name="mla_attention",
    )(qm_t, km, vm_t)


def _tail_kernel(x_ref, om_ref, on_ref, p_ref, wo_ref, g1_ref, g2_ref, g3_ref,
                 wg_ref, wu_ref, cw_ref, cb_ref, wd_ref, pg_ref, pp_ref,
                 o_ref, carry_ref, act_ref, *, seq, tm):
    n_chunks = D_FF // FF_CHUNK
    half = om_ref.shape[1]
    y = _dot(om_ref[...], wo_ref[:half, :]) + _dot(on_ref[...], wo_ref[half:, :])
    x = x_ref[...] + _rms(y, g1_ref[...])

    h = _rms(x, g2_ref[...]).astype(BF16)

    @pl.when((pl.program_id(0) * tm) % seq == 0)
    def _():
        carry_ref[...] = jnp.zeros(carry_ref.shape, F32)

    SUB = 8
    row8 = lax.broadcasted_iota(jnp.int32, (SUB, 1), 0)
    for c in range(n_chunks):
        cols = slice(c * FF_CHUNK, (c + 1) * FF_CHUNK)
        g = _dot(h, wg_ref[:, cols])
        prev = carry_ref[:, cols]
        carry_ref[:, cols] = g[tm - SUB:, :]
        r1 = pltpu.roll(g, 1, 0)
        r2 = pltpu.roll(g, 2, 0)
        top1 = jnp.where(row8 == 0, prev[7:8, :], r1[:SUB])
        top2 = jnp.where(row8 == 0, prev[6:7, :], jnp.where(row8 == 1, prev[7:8, :], r2[:SUB]))
        g1 = jnp.concatenate([top1, r1[SUB:]], axis=0)
        g2 = jnp.concatenate([top2, r2[SUB:]], axis=0)
        conv = (cw_ref[0:1, cols] * g2 + cw_ref[1:2, cols] * g1 + cw_ref[2:3, cols] * g
                + cb_ref[:, cols])
        act_ref[:, cols] = (_gelu_tanh(conv) * _dot(h, wu_ref[:, cols])).astype(BF16)
    x = x + _rms(_dot(act_ref[...], wd_ref[...]), g3_ref[...])

    gate = _sigmoid(_dot(x.astype(BF16), pg_ref[...]))
    o_ref[...] = x + gate * _dot(p_ref[...].astype(BF16), pp_ref[...])


def _tail(x2, om, on, p2, layer, wo, g1, g2, g3, wg, wu, cw, cb, wd, pg, pp, *, seq):
    n = x2.shape[0]
    tm = ROW_TILE
    row = lambda r: (r, 0)
    half = om.shape[1]
    lr = functools.partial(_layer_resident, layer)
    return pl.pallas_call(
        functools.partial(_tail_kernel, seq=seq, tm=tm),
        grid=(n // tm,),
        in_specs=[pl.BlockSpec((tm, D_MODEL), row),
                  pl.BlockSpec((tm, half), row),
                  pl.BlockSpec((tm, half), row),
                  pl.BlockSpec((None, tm, PLE_DIM), lambda r: (layer, r, 0)),
                  lr((2 * half, D_MODEL)),
                  lr((1, D_MODEL)),
                  lr((1, D_MODEL)),
                  lr((1, D_MODEL)),
                  lr((D_MODEL, D_FF)),
                  lr((D_MODEL, D_FF)),
                  lr((CONV_WIDTH, D_FF)),
                  lr((1, D_FF)),
                  lr((D_FF, D_MODEL)),
                  lr((D_MODEL, D_MODEL)),
                  lr((PLE_DIM, D_MODEL))],
        out_specs=pl.BlockSpec((tm, D_MODEL), row),
        out_shape=jax.ShapeDtypeStruct((n, D_MODEL), F32),
        scratch_shapes=[pltpu.VMEM((8, D_FF), F32),
                        pltpu.VMEM((tm, D_FF), BF16)],
        compiler_params=_params("arbitrary"),
        name="layer_tail",
    )(x2, om, on, p2, wo, g1, g2, g3, wg, wu, cw, cb, wd, pg, pp)


def _rot_rows(w):
    half = w.shape[-2] // 2
    return jnp.concatenate([-w[..., half:, :], w[..., :half, :]], axis=-2)


def _prep_inproj(w):
    wt = w.T
    o = np.cumsum((0,) + IN_SPLITS)
    cq, ckv, kr, qn, kcmp, vcmp, kslc, vslc, kwin, vwin, gn = [wt[o[j]:o[j + 1]] for j in range(len(IN_SPLITS))]
    d = w.shape[0]

    def zeros(rows):
        return jnp.zeros((rows, d), F32)

    def kslots(m):
        return jnp.concatenate([m[:NSA_DIM], zeros(LANE - NSA_DIM), m[NSA_DIM:], zeros(LANE - NSA_DIM)], axis=0)

    def vslots(m):
        return jnp.concatenate([m[:NSA_DIM], zeros(VROWS - NSA_DIM), m[NSA_DIM:], zeros(VROWS - NSA_DIM)], axis=0)

    pad_rope = zeros(LANE - MLA_NOPE - MLA_ROPE)
    w1 = jnp.concatenate([cq, ckv,
                          zeros(MLA_NOPE), kr, pad_rope,
                          zeros(MLA_NOPE), _rot_rows(kr), pad_rope,
                          kslots(kslc), kslots(kwin), kcmp, vcmp], axis=0)
    assert w1.shape[0] == C_TOTAL
    per = 3 * NSA_GROUP
    w1t = jnp.concatenate([qn * NSA_DIM ** -0.5, vslots(vslc), vslots(vwin),
                           gn[:per], zeros(GATE_ROWS - per), gn[per:], zeros(GATE_ROWS - per)], axis=0)
    assert w1t.shape[0] == R_TOTAL
    return w1.astype(BF16).T, w1t.astype(BF16)


def _prep_mla(w_uq, w_ukv):
    r = w_uq.shape[0]
    dq = MLA_NOPE + MLA_ROPE
    ut = w_uq.T.reshape(MLA_HEADS, dq, r)
    pad = jnp.zeros((MLA_HEADS, LANE - dq, r), F32)
    wqa_t = jnp.concatenate([ut, pad], axis=1).reshape(MLA_HEADS * LANE, r)
    wqb_t = jnp.concatenate([jnp.zeros((MLA_HEADS, MLA_NOPE, r), F32), _rot_rows(ut[:, MLA_NOPE:]), pad],
                            axis=1).reshape(MLA_HEADS * LANE, r)
    rk = w_ukv.shape[0]
    kv = w_ukv.reshape(rk, MLA_HEADS, 2, MLA_NOPE)
    wk = jnp.concatenate([kv[:, :, 0, :], jnp.zeros((rk, MLA_HEADS, LANE - MLA_NOPE), F32)],
                         axis=-1).reshape(rk, MLA_HEADS * LANE)
    vt = kv[:, :, 1, :].transpose(1, 2, 0)
    wvt = jnp.concatenate([vt, jnp.zeros((MLA_HEADS, VROWS - MLA_V, rk), F32)], axis=1).reshape(MLA_HEADS * VROWS, rk)
    return wqa_t.astype(BF16), wqb_t.astype(BF16), wk.astype(BF16), wvt.astype(BF16)


def kernel(x, p, positions, rel_bias, attn_pre_norm, attn_post_norm, ffn_pre_norm, ffn_post_norm,
           w_in, mla_q_norm, mla_w_uq, mla_kv_norm, mla_w_ukv, nsa_cmp_pos, nsa_cmp_w1, nsa_cmp_w2,
           w_o, ffn_w_gate, ffn_w_up, ffn_conv_w, ffn_conv_b, ffn_w_down, ple_proj, ple_gate):
    batch, seq, d = x.shape
    depth = w_in.shape[0]
    n = batch * seq
    x2 = x.reshape(n, d)
    ctt, stt, ct, st = _rope_tables(positions)
    pat, utab = _bias_tables(rel_bias, seq)
    tail_params = (w_o.astype(BF16), attn_post_norm[:, None, :], ffn_pre_norm[:, None, :],
                   ffn_post_norm[:, None, :], ffn_w_gate.astype(BF16), ffn_w_up.astype(BF16),
                   ffn_conv_w, ffn_conv_b[:, None, :], ffn_w_down.astype(BF16),
                   ple_gate.astype(BF16), ple_proj.astype(BF16))
    for i in range(depth):
        w1, w1t = _prep_inproj(w_in[i])
        wqa, wqb, wk, wvt = _prep_mla(mla_w_uq[i], mla_w_ukv[i])
        qm_t, km, vm_t, zq_t, nk, nv_t, kcmp, vcmp, gates_t = _inproj(
            x2, attn_pre_norm[i][None, :], w1, w1t, mla_q_norm[i][None, :], wqa, wqb,
            mla_kv_norm[i][None, :], wk, wvt, ct, st, ctt, stt, batch=batch, seq=seq)
        kc, vc_t = _compress(kcmp, vcmp,
                             nsa_cmp_pos[i],
                             nsa_cmp_w1[i].astype(BF16), nsa_cmp_w2[i, 0].astype(BF16),
                             nsa_cmp_w2[i, 1].T.astype(BF16), batch=batch, seq=seq)
        o_nsa = _nsa(zq_t, kc, vc_t, nk, nv_t, gates_t, pat, utab, batch=batch, seq=seq)
        o_mla = _mla(qm_t, km, vm_t, batch=batch, seq=seq)
        x2 = _tail(x2, o_mla, o_nsa, p.reshape(depth, n, PLE_DIM), i, *tail_params, seq=seq)
    return x2.reshape(batch, seq, d)
```

```python
import functools
import math

import numpy as np
import jax
import jax.numpy as jnp
from jax import lax
from jax.experimental import pallas as pl
from jax.experimental.pallas import tpu as pltpu

F32 = jnp.float32
BF16 = jnp.bfloat16

D_MODEL = 1024
MLA_HEADS = 8
MLA_NOPE = 64
MLA_ROPE = 32
MLA_V = 64
MLA_Q_RANK = 256
MLA_KV_RANK = 128
ROPE_BASE = 10000.0
NSA_HEADS = 8
NSA_KV_HEADS = 2
NSA_GROUP = NSA_HEADS // NSA_KV_HEADS
NSA_DIM = 64
CMP_LEN = 32
CMP_STRIDE = 16
CMP_HIDDEN = 128
SLC_LEN = 64
SLC_TOPK = 16
WINDOW = 512
REL_BUCKETS = 32
REL_MAX_DIST = 128
D_FF = 2816
CONV_WIDTH = 3
PLE_DIM = 256
EPS = 1e-6
NEG = -1e30
POS_BIG = 1e30
LOG2E = math.log2(math.e)

IN_SPLITS = (MLA_Q_RANK, MLA_KV_RANK, MLA_ROPE, NSA_HEADS * NSA_DIM,
             NSA_KV_HEADS * NSA_DIM, NSA_KV_HEADS * NSA_DIM,
             NSA_KV_HEADS * NSA_DIM, NSA_KV_HEADS * NSA_DIM,
             NSA_KV_HEADS * NSA_DIM, NSA_KV_HEADS * NSA_DIM,
             3 * NSA_HEADS)

LANE = 128
QT = 256
KP = 128
SLC_PAD = 64
FF_CHUNK = 256
ROW_TILE = 1024
VMEM_LIMIT = 56 * 1024 * 1024

VROWS = 80

C_CQ = 0
C_CKV = 256
C_KR = 384
C_KRROT = 512
C_KSLC = 640
C_KWIN = 896
C_KCMP = 1152
C_VCMP = 1280
C_TOTAL = 1408
R_QN = 0
R_VSLC = 512
R_VWIN = R_VSLC + NSA_KV_HEADS * VROWS
R_GATE = R_VWIN + NSA_KV_HEADS * VROWS
GATE_ROWS = 16
R_TOTAL = R_GATE + NSA_KV_HEADS * GATE_ROWS


def _dot(a, b):
    return jnp.dot(a, b, preferred_element_type=F32)


def _dot_nt(a, b):
    return lax.dot_general(a, b, (((1,), (1,)), ((), ())), preferred_element_type=F32)


def _rms(x, g):
    return x * lax.rsqrt(jnp.mean(x * x, axis=-1, keepdims=True) + EPS) * g


def _gelu_tanh(x):
    return 0.5 * x * (1.0 + jnp.tanh(math.sqrt(2.0 / math.pi) * (x + 0.044715 * (x * x * x))))


def _sigmoid(x):
    return 1.0 / (1.0 + jnp.exp(-x))


def _params(*sem):
    return pltpu.CompilerParams(dimension_semantics=sem, vmem_limit_bytes=VMEM_LIMIT)


def _resident(shape):
    nd = len(shape)
    return pl.BlockSpec(shape, lambda *_: (0,) * nd, pipeline_mode=pl.Buffered(1))


def _layer_resident(layer, shape):
    nd = len(shape)
    return pl.BlockSpec((None,) + tuple(shape), lambda *_: (layer,) + (0,) * nd, pipeline_mode=pl.Buffered(1))


def _rope_kernel(pos_ref, inv_ref, ctt_ref, stt_ref, ct_ref, st_ref):
    ang = inv_ref[...] * pos_ref[...].astype(F32)
    c = jnp.cos(ang)
    s = jnp.sin(ang)
    tn = ang.shape[1]
    pad = jnp.zeros((LANE - MLA_NOPE - MLA_ROPE, tn), F32)
    ct = jnp.concatenate([jnp.ones((MLA_NOPE, tn), F32), c, c, pad], axis=0)
    st = jnp.concatenate([jnp.zeros((MLA_NOPE, tn), F32), s, s, pad], axis=0)
    ctt_ref[...] = ct
    stt_ref[...] = st
    ct_ref[...] = ct.T
    st_ref[...] = st.T


def _rope_tables(positions):
    n = positions.size
    tn = 2048
    half = MLA_ROPE // 2
    inv = (ROPE_BASE ** (-jnp.arange(half, dtype=F32) / half))[:, None]
    return pl.pallas_call(
        _rope_kernel,
        grid=(n // tn,),
        in_specs=[pl.BlockSpec((1, tn), lambda r: (0, r)),
                  pl.BlockSpec((half, 1), lambda r: (0, 0))],
        out_specs=[pl.BlockSpec((LANE, tn), lambda r: (0, r))] * 2
                  + [pl.BlockSpec((tn, LANE), lambda r: (r, 0))] * 2,
        out_shape=[jax.ShapeDtypeStruct((LANE, n), F32)] * 2 + [jax.ShapeDtypeStruct((n, LANE), F32)] * 2,
        compiler_params=_params("parallel"),
        name="rope_tables",
    )(positions.reshape(1, n), inv)


def _bucket_np(dist):
    n = np.maximum(dist, 0)
    max_exact = REL_BUCKETS // 2
    large = max_exact + (np.log(np.maximum(n, 1).astype(np.float32) / max_exact)
                         / math.log(REL_MAX_DIST / max_exact)
                         * (REL_BUCKETS - max_exact)).astype(np.int32)
    large = np.minimum(large, REL_BUCKETS - 1)
    return np.where(n < max_exact, n, large).astype(np.int32)


def _bias_kernel(table_ref, bpc_ref, bpu_ref, pat_ref, u_ref, *, lo, hi):
    h = pl.program_id(0)
    far = table_ref[REL_BUCKETS - 1, h]

    def lookup(bp, sub):
        acc = jnp.full(bp.shape, far - sub, F32)
        for b in range(REL_BUCKETS - 1):
            acc = jnp.where(bp == b, table_ref[b, h] - sub, acc)
        return jnp.where(bp < 0, NEG, acc * LOG2E)

    rows = pat_ref.shape[1]
    pat_ref[0, :lo] = jnp.full((lo, QT), far * LOG2E, F32)
    pat_ref[0, lo:hi] = lookup(bpc_ref[...], 0.0)
    pat_ref[0, hi:] = jnp.full((rows - hi, QT), NEG, F32)
    for d in range(3):
        u_ref[0, d] = lookup(bpu_ref[d], far)


def _bias_tables(rel_bias, seq):
    ncp = seq // CMP_STRIDE
    i = np.arange(QT)[None, :]
    cprime = np.arange(2 * ncp)[:, None] - ncp
    dist_c = i - CMP_STRIDE * cprime - (CMP_LEN - 1)
    bpc = np.where(dist_c >= 0, _bucket_np(dist_c), -1).astype(np.int32)
    varied = np.nonzero(~(np.all(bpc == REL_BUCKETS - 1, axis=1) | np.all(bpc == -1, axis=1)))[0]
    lo = int(varied[0]) // 8 * 8
    hi = -(-(int(varied[-1]) + 1) // 8) * 8
    assert np.all(bpc[:lo] == REL_BUCKETS - 1) and np.all(bpc[hi:] == -1)
    j = np.arange(KP)[:, None]
    bpu = []
    for delta in (-1, 0, 1):
        dist = i - KP * delta - j
        bpu.append(np.where(dist >= 0, _bucket_np(dist), -1))
    bpu = np.stack(bpu).astype(np.int32)
    return pl.pallas_call(
        functools.partial(_bias_kernel, lo=lo, hi=hi),
        grid=(NSA_HEADS,),
        in_specs=[pl.BlockSpec(memory_space=pltpu.SMEM),
                  pl.BlockSpec((hi - lo, QT), lambda h: (0, 0)),
                  pl.BlockSpec((3, KP, QT), lambda h: (0, 0, 0))],
        out_specs=[pl.BlockSpec((1, 2 * ncp, QT), lambda h: (h, 0, 0)),
                   pl.BlockSpec((1, 3, KP, QT), lambda h: (h // NSA_GROUP, 0, 0, h % NSA_GROUP))],
        out_shape=[jax.ShapeDtypeStruct((NSA_HEADS, 2 * ncp, QT), F32),
                   jax.ShapeDtypeStruct((NSA_KV_HEADS, 3, KP, NSA_GROUP * QT), F32)],
        compiler_params=_params("parallel"),
        name="bias_tables",
    )(rel_bias.astype(F32), jnp.asarray(bpc[lo:hi]), jnp.asarray(bpu))


def _inproj_kernel(x_ref, g_ref, w1_ref, w1t_ref, qn_ref, wqa_ref, wqb_ref, kvn_ref, wk_ref, wvt_ref,
                   ct_ref, st_ref, ctt_ref, stt_ref, ones_n_ref, ones_m_ref,
                   qm_ref, km_ref, vm_ref, zq_ref, nk_ref, nv_ref, kcmp_ref, vcmp_ref, gate_ref,
                   *, seq, tm):
    h = _rms(x_ref[...], g_ref[...]).astype(BF16)

    def proj(c0, width):
        return _dot(h, w1_ref[:, c0:c0 + width])

    cq = _rms(proj(C_CQ, MLA_Q_RANK), qn_ref[...]).astype(BF16)
    qa = _dot_nt(wqa_ref[...], cq)
    qb = _dot_nt(wqb_ref[...], cq)
    scale = (MLA_NOPE + MLA_ROPE) ** -0.5 * LOG2E
    cts = ctt_ref[...] * scale
    sts = stt_ref[MLA_NOPE:MLA_NOPE + MLA_ROPE, :] * scale
    for hh in range(MLA_HEADS):
        sl = slice(hh * LANE, (hh + 1) * LANE)
        slot = qa[sl] * cts
        rope = slot[MLA_NOPE:MLA_NOPE + MLA_ROPE] + qb[hh * MLA_ROPE:(hh + 1) * MLA_ROPE] * sts
        qm_ref[sl, :] = jnp.concatenate([slot[:MLA_NOPE], rope, slot[MLA_NOPE + MLA_ROPE:]], axis=0).astype(BF16)

    ckv = _rms(proj(C_CKV, MLA_KV_RANK), kvn_ref[...]).astype(BF16)
    kr = proj(C_KR, LANE) * ct_ref[...] + proj(C_KRROT, LANE) * st_ref[...]
    kn = _dot(ckv, wk_ref[...])
    for hh in range(MLA_HEADS):
        sl = slice(hh * LANE, (hh + 1) * LANE)
        km_ref[:, sl] = (kn[:, sl] + kr).astype(BF16)
    vm_ref[...] = (_dot_nt(wvt_ref[...], ckv) + ones_m_ref[...]).astype(BF16)

    zt = _dot_nt(w1t_ref[...], h)
    zq_ref[...] = (zt[R_QN:R_VSLC] * LOG2E).astype(BF16)
    nv_ref[...] = (zt[R_VSLC:R_GATE] + ones_n_ref[...]).astype(BF16)
    gate_ref[...] = _sigmoid(zt[R_GATE:R_TOTAL])

    lane = lax.broadcasted_iota(jnp.int32, (tm, LANE), 1)
    s0 = (pl.program_id(0) * tm) % seq
    row = lax.broadcasted_iota(jnp.int32, (tm, LANE), 0)
    onehot = (lane - NSA_DIM == (s0 + row) // SLC_LEN).astype(F32)
    ksl = proj(C_KSLC, 2 * LANE)
    for hk in range(NSA_KV_HEADS):
        sl = slice(hk * LANE, (hk + 1) * LANE)
        nk_ref[:, sl] = (ksl[:, sl] + onehot).astype(BF16)
    nk_ref[:, 2 * LANE:] = proj(C_KWIN, 2 * LANE).astype(BF16)

    kcmp = proj(C_KCMP, LANE)
    vcmp = proj(C_VCMP, LANE)
    for hk in range(NSA_KV_HEADS):
        kcmp_ref[0, hk] = kcmp[:, hk * NSA_DIM:(hk + 1) * NSA_DIM]
        vcmp_ref[0, hk] = vcmp[:, hk * NSA_DIM:(hk + 1) * NSA_DIM]


def _ones_rows(n_slots):
    r = np.arange(n_slots * VROWS) % VROWS
    return jnp.asarray((r >= NSA_DIM).astype(np.float32)[:, None])


def _inproj(x2, g_pre, w1, w1t, qn, wqa, wqb, kvn, wk, wvt, ct, st, ctt, stt, *, batch, seq):
    n = x2.shape[0]
    tm = ROW_TILE
    tiles_per_seq = seq // tm
    row = lambda r: (r, 0)
    col = lambda r: (0, r)
    hm = lambda r: (r // tiles_per_seq, 0, r % tiles_per_seq, 0)
    nv_rows = 2 * NSA_KV_HEADS * VROWS
    vm_rows = MLA_HEADS * VROWS
    return pl.pallas_call(
        functools.partial(_inproj_kernel, seq=seq, tm=tm),
        grid=(n // tm,),
        in_specs=[pl.BlockSpec((tm, D_MODEL), row),
                  _resident((1, D_MODEL)),
                  _resident((D_MODEL, C_TOTAL)),
                  _resident((R_TOTAL, D_MODEL)),
                  _resident((1, MLA_Q_RANK)),
                  _resident((MLA_HEADS * LANE, MLA_Q_RANK)),
                  _resident((MLA_HEADS * MLA_ROPE, MLA_Q_RANK)),
                  _resident((1, MLA_KV_RANK)),
                  _resident((MLA_KV_RANK, MLA_HEADS * LANE)),
                  _resident((vm_rows, MLA_KV_RANK)),
                  pl.BlockSpec((tm, LANE), row),
                  pl.BlockSpec((tm, LANE), row),
                  pl.BlockSpec((LANE, tm), col),
                  pl.BlockSpec((LANE, tm), col),
                  _resident((nv_rows, 1)),
                  _resident((vm_rows, 1))],
        out_specs=[pl.BlockSpec((MLA_HEADS * LANE, tm), col),
                   pl.BlockSpec((tm, MLA_HEADS * LANE), row),
                   pl.BlockSpec((vm_rows, tm), col),
                   pl.BlockSpec((NSA_HEADS * NSA_DIM, tm), col),
                   pl.BlockSpec((tm, 4 * LANE), row),
                   pl.BlockSpec((nv_rows, tm), col),
                   pl.BlockSpec((1, NSA_KV_HEADS, tm, NSA_DIM), hm),
                   pl.BlockSpec((1, NSA_KV_HEADS, tm, NSA_DIM), hm),
                   pl.BlockSpec((NSA_KV_HEADS * GATE_ROWS, tm), col)],
        out_shape=[jax.ShapeDtypeStruct((MLA_HEADS * LANE, n), BF16),
                   jax.ShapeDtypeStruct((n, MLA_HEADS * LANE), BF16),
                   jax.ShapeDtypeStruct((vm_rows, n), BF16),
                   jax.ShapeDtypeStruct((NSA_HEADS * NSA_DIM, n), BF16),
                   jax.ShapeDtypeStruct((n, 4 * LANE), BF16),
                   jax.ShapeDtypeStruct((nv_rows, n), BF16),
                   jax.ShapeDtypeStruct((batch, NSA_KV_HEADS, seq, NSA_DIM), F32),
                   jax.ShapeDtypeStruct((batch, NSA_KV_HEADS, seq, NSA_DIM), F32),
                   jax.ShapeDtypeStruct((NSA_KV_HEADS * GATE_ROWS, n), F32)],
        compiler_params=_params("parallel"),
        name="in_proj",
    )(x2, g_pre, w1, w1t, qn, wqa, wqb, kvn, wk, wvt, ct, st, ctt, stt,
      _ones_rows(2 * NSA_KV_HEADS), _ones_rows(MLA_HEADS))


def _compress_kernel(k_ref, v_ref, pos_ref, w1_ref, w2k_ref, w2vt_ref, kc_ref, vc_ref):
    ncp = kc_ref.shape[2]

    def hidden(x_ref, j):
        u = jnp.zeros((ncp, CMP_HIDDEN), F32)
        low = jnp.zeros((ncp, CMP_HIDDEN), F32)
        for l in range(CMP_STRIDE):
            x = x_ref[0, 0, pl.ds(l, ncp, stride=CMP_STRIDE), :]
            top = (x + pos_ref[j, l:l + 1, :]).astype(BF16)
            bot = (x + pos_ref[j, CMP_STRIDE + l:CMP_STRIDE + l + 1, :]).astype(BF16)
            u = u + _dot(top, w1_ref[j, l * NSA_DIM:(l + 1) * NSA_DIM, :])
            low = low + _dot(bot, w1_ref[j, (CMP_STRIDE + l) * NSA_DIM:(CMP_STRIDE + l + 1) * NSA_DIM, :])
        nxt = jnp.concatenate([low[1:], jnp.zeros((1, CMP_HIDDEN), F32)], axis=0)
        return _gelu_tanh(u + nxt).astype(BF16)

    kc_ref[0, 0] = _dot(hidden(k_ref, 0), w2k_ref[...]).astype(BF16)
    vc_ref[0, 0] = _dot_nt(w2vt_ref[...], hidden(v_ref, 1)).astype(BF16)


def _compress(kcmp, vcmp, pos, w1, w2k, w2vt, *, batch, seq):
    ncp = seq // CMP_STRIDE
    blk = lambda b, hk: (b, hk, 0, 0)
    return pl.pallas_call(
        _compress_kernel,
        grid=(batch, NSA_KV_HEADS),
        in_specs=[pl.BlockSpec((1, 1, seq, NSA_DIM), blk),
                  pl.BlockSpec((1, 1, seq, NSA_DIM), blk),
                  pl.BlockSpec((2, CMP_LEN, NSA_DIM), lambda b, hk: (0, 0, 0)),
                  pl.BlockSpec((2, CMP_LEN * NSA_DIM, CMP_HIDDEN), lambda b, hk: (0, 0, 0)),
                  pl.BlockSpec((CMP_HIDDEN, NSA_DIM), lambda b, hk: (0, 0)),
                  pl.BlockSpec((NSA_DIM, CMP_HIDDEN), lambda b, hk: (0, 0))],
        out_specs=[pl.BlockSpec((1, 1, ncp, NSA_DIM), blk),
                   pl.BlockSpec((1, 1, NSA_DIM, ncp), blk)],
        out_shape=[jax.ShapeDtypeStruct((batch, NSA_KV_HEADS, ncp, NSA_DIM), BF16),
                   jax.ShapeDtypeStruct((batch, NSA_KV_HEADS, NSA_DIM, ncp), BF16)],
        compiler_params=_params("parallel", "parallel"),
        name="nsa_compress",
    )(kcmp, vcmp, pos, w1, w2k, w2vt)


def _nsa_kernel(*refs):
    for hk in range(NSA_KV_HEADS):
        _nsa_group(hk, *refs)


def _nsa_group(hk, zq_ref, kc_ref, vc_ref, ksl_ref, vsl_ref, kw_ref, vw_ref, gate_ref,
               pat_ref, u_ref, ovt_ref, o_ref, imp_ref, acc_ref, m_ref, sa_ref, sb_ref):
    qi = pl.program_id(1)
    G = NSA_GROUP
    kcols = slice(hk * LANE, (hk + 1) * LANE)
    vrows = slice(hk * VROWS, (hk + 1) * VROWS)
    M = G * QT
    PPT = QT // KP
    q4 = zq_ref[hk * G * NSA_DIM:(hk + 1) * G * NSA_DIM, :]
    qs = jnp.concatenate([q4[g * NSA_DIM:(g + 1) * NSA_DIM, :] for g in range(G)], axis=1)
    kc = kc_ref[0, hk]
    vc_t = vc_ref[0, hk]
    ovt = ovt_ref[...]
    ncp = kc.shape[0]
    qcol = qi * QT + (lax.broadcasted_iota(jnp.int32, (1, M), 1) & (QT - 1))
    u_m1 = u_ref[hk, 0]
    u_0 = u_ref[hk, 1]
    u_p1 = u_ref[hk, 2]

    n_back = WINDOW // KP
    n_w = n_back + PPT
    MS = G * KP

    def sub_lanes(x, a):
        return jnp.concatenate([x[:, g * QT + a * KP:g * QT + (a + 1) * KP] for g in range(G)], axis=1)

    qw = jnp.concatenate([qs, jnp.zeros_like(qs)], axis=0)
    pad_v = jnp.where(lax.broadcasted_iota(jnp.int32, (VROWS, KP), 0) >= NSA_DIM, 1.0, 0.0).astype(BF16)
    k_rows, v_cols = [], []
    for w in range(n_w):
        piece = PPT * qi - n_back + w
        off = pl.multiple_of(jnp.maximum(piece, 0) * KP, KP)
        kt = kw_ref[pl.ds(off, KP), kcols]
        vt = vw_ref[vrows, pl.ds(off, KP)]
        if w < n_back:
            kt = jnp.where(piece >= 0, kt, jnp.zeros_like(kt))
            vt = jnp.where(piece >= 0, vt, pad_v)
        k_rows.append(kt)
        v_cols.append(vt)
    s_sub = [_dot(jnp.concatenate(k_rows[a:a + n_back + 1], axis=0), sub_lanes(qw, a))
             for a in range(PPT)]

    slide = QT // CMP_STRIDE
    off_b = pl.multiple_of(ncp - slide * qi, slide)
    bias = jnp.concatenate([pat_ref[hk * G + g, pl.ds(off_b, ncp), :] for g in range(G)], axis=1)
    s = _dot(kc, qs) + bias
    e = jnp.exp2(s - jnp.max(s, axis=0, keepdims=True))
    scale = jnp.where(qcol >= CMP_LEN - 1, 1.0 / jnp.sum(e, axis=0, keepdims=True), 0.0)
    p_cmp = (e * scale).astype(BF16)

    key_j = lax.broadcasted_iota(jnp.int32, (KP, MS), 0)
    q_loc = lax.broadcasted_iota(jnp.int32, (KP, MS), 1) & (KP - 1)
    u_tabs = [u_m1, u_0, u_p1]
    p_sub = []
    for a in range(PPT):
        pieces = [s_sub[a][w * KP:(w + 1) * KP] for w in range(n_back + 1)]
        pieces[0] = pieces[0] + jnp.where(key_j > q_loc, 0.0, NEG)
        pieces[n_back - 1] = pieces[n_back - 1] + sub_lanes(u_tabs[a], a)
        pieces[n_back] = pieces[n_back] + sub_lanes(u_tabs[a + 1], a)
        s_a = jnp.concatenate(pieces, axis=0)
        p_sub.append(jnp.exp2(s_a - jnp.max(s_a, axis=0, keepdims=True)).astype(BF16))

    o_cmp = _dot(vc_t, p_cmp)
    imp_t = jnp.zeros((SLC_PAD, QT), F32)
    for g in range(G):
        imp_t = imp_t + _dot(ovt, p_cmp[:, g * QT:(g + 1) * QT])

    o_sub = []
    for a in range(PPT):
        a_w = _dot(jnp.concatenate(v_cols[a:a + n_back + 1], axis=1), p_sub[a])
        o_sub.append(a_w[:NSA_DIM] / a_w[NSA_DIM:NSA_DIM + 1])
    o_win = jnp.concatenate([o_sub[a][:, g * KP:(g + 1) * KP] for g in range(G) for a in range(PPT)], axis=1)

    n_id = lax.broadcasted_iota(jnp.int32, (SLC_PAD, QT), 0)
    q_blk = (qi * QT + lax.broadcasted_iota(jnp.int32, (SLC_PAD, QT), 1)) // SLC_LEN
    forced = (n_id == 0) | (n_id == q_blk) | (n_id == q_blk - 1)
    imp = jnp.where(forced, POS_BIG, jnp.where(n_id > q_blk, NEG, imp_t))
    imp_ref[...] = imp
    SUB = 8
    slabs = [imp[v * SUB:(v + 1) * SUB] for v in range(SLC_PAD // SUB)]
    ranks = [jnp.zeros((SUB, QT), jnp.int32) for _ in slabs]
    sub_id = lax.broadcasted_iota(jnp.int32, (SUB, QT), 0)
    for m in range(SLC_PAD):
        other = imp_ref[m:m + 1, :]
        for v, slab in enumerate(slabs):
            if v > m // SUB:
                beats = (other >= slab).astype(jnp.int32)
            elif v < m // SUB:
                beats = (other > slab).astype(jnp.int32)
            else:
                beats = jnp.where(sub_id > m % SUB, (other >= slab).astype(jnp.int32),
                                  (other > slab).astype(jnp.int32))
            ranks[v] = ranks[v] + beats
    rank = jnp.concatenate(ranks, axis=0)
    selb = jnp.where(rank < SLC_TOPK, 0.0, NEG).astype(BF16)
    qaug = jnp.concatenate([qs, jnp.concatenate([selb] * G, axis=1)], axis=0)

    PIECES = 4
    TK = PIECES * KP
    n_t = (PPT * qi + PPT - 1) // PIECES + 1
    at_start = (PPT * qi) % PIECES == 0
    last_bias = [jnp.where(at_start, u_0, 0.0), jnp.where(at_start, u_p1, u_m1),
                 jnp.where(at_start, NEG, u_0), jnp.where(at_start, NEG, u_p1)]
    prev_bias = [None, None, None, jnp.where(at_start, u_m1, 0.0)]
    m_ref[...] = jnp.full(m_ref.shape, NEG, F32)
    acc_ref[...] = jnp.zeros(acc_ref.shape, F32)

    HALVES = 2
    MH = M // HALVES

    def qk(dst_ref, t, hf):
        off = pl.multiple_of(t * TK, TK)
        lanes = slice(hf * MH, (hf + 1) * MH)
        dst_ref[:, lanes] = _dot(ksl_ref[pl.ds(off, TK), kcols], qaug[:, lanes])

    def consume(src_ref, t, bias, hf):
        off = pl.multiple_of(t * TK, TK)
        lanes = slice(hf * MH, (hf + 1) * MH)
        s = src_ref[:, lanes]
        if bias is not None:
            s = jnp.concatenate([s[j * KP:(j + 1) * KP] if bias[j] is None
                                 else s[j * KP:(j + 1) * KP] + bias[j][:, lanes]
                                 for j in range(PIECES)], axis=0)
        m_old = m_ref[:, lanes]
        m_new = jnp.maximum(m_old, jnp.max(s, axis=0, keepdims=True))
        p = jnp.exp2(s - m_new).astype(BF16)
        acc_ref[:, lanes] = acc_ref[:, lanes] * jnp.exp2(m_old - m_new) + _dot(vsl_ref[vrows, pl.ds(off, TK)], p)
        m_ref[:, lanes] = m_new

    def stage(nxt_ref, nxt_t, cur_ref, cur_t, bias):
        for hf in range(HALVES):
            if nxt_ref is not None:
                qk(nxt_ref, nxt_t, hf)
            consume(cur_ref, cur_t, bias, hf)

    n_plain = jnp.maximum(n_t - 2, 0)
    for hf in range(HALVES):
        qk(sa_ref, 0, hf)

    def pair(u, carry):
        t = 2 * u
        stage(sb_ref, t + 1, sa_ref, t, None)
        stage(sa_ref, t + 2, sb_ref, t + 1, None)
        return carry

    lax.fori_loop(0, n_plain // 2, pair, 0)
    tb = (n_plain // 2) * 2

    @pl.when(n_t == 1)
    def _():
        stage(None, None, sa_ref, 0, last_bias)

    @pl.when((n_t >= 2) & (n_plain % 2 == 0))
    def _():
        stage(sb_ref, tb + 1, sa_ref, tb, prev_bias)
        stage(None, None, sb_ref, tb + 1, last_bias)

    @pl.when(n_plain % 2 == 1)
    def _():
        stage(sb_ref, tb + 1, sa_ref, tb, None)
        stage(sa_ref, tb + 2, sb_ref, tb + 1, prev_bias)
        stage(None, None, sa_ref, tb + 2, last_bias)

    a_s = acc_ref[...]
    o_sel = a_s[:NSA_DIM] / a_s[NSA_DIM:NSA_DIM + 1]

    gate = gate_ref[hk * GATE_ROWS:(hk + 1) * GATE_ROWS, :]
    outs = []
    for g in range(G):
        cols = slice(g * QT, (g + 1) * QT)
        o_t = (gate[3 * g:3 * g + 1, :] * o_cmp[:, cols] + gate[3 * g + 1:3 * g + 2, :] * o_sel[:, cols]
               + gate[3 * g + 2:3 * g + 3, :] * o_win[:, cols])
        outs.append(o_t.T)
    o_ref[:, hk * G * NSA_DIM:(hk + 1) * G * NSA_DIM] = jnp.concatenate(outs, axis=1).astype(BF16)


def _overlap_t(seq):
    ncp = seq // CMP_STRIDE
    n_cmp = (seq - CMP_LEN) // CMP_STRIDE + 1
    n_slc = seq // SLC_LEN
    cs = np.arange(ncp)[None, :] * CMP_STRIDE
    ss = np.arange(SLC_PAD)[:, None] * SLC_LEN
    ov = np.maximum(np.minimum(cs + CMP_LEN, ss + SLC_LEN) - np.maximum(cs, ss), 0).astype(np.float32) / CMP_LEN
    ov = ov * (np.arange(ncp)[None, :] < n_cmp) * (np.arange(SLC_PAD)[:, None] < n_slc)
    return jnp.asarray(ov, BF16)


def _nsa(zq_t, kc, vc_t, nk, nv_t, gates_t, pat, utab, *, batch, seq):
    n = nk.shape[0]
    nq = seq // QT
    ncp = seq // CMP_STRIDE
    G = NSA_GROUP
    HK = NSA_KV_HEADS
    assert seq // SLC_LEN <= SLC_PAD and seq // SLC_LEN >= SLC_TOPK and seq >= WINDOW
    qcol = lambda b, qi: (0, b * nq + qi)
    return pl.pallas_call(
        _nsa_kernel,
        grid=(batch, nq),
        in_specs=[pl.BlockSpec((NSA_HEADS * NSA_DIM, QT), qcol),
                  pl.BlockSpec((1, HK, ncp, NSA_DIM), lambda b, qi: (b, 0, 0, 0)),
                  pl.BlockSpec((1, HK, NSA_DIM, ncp), lambda b, qi: (b, 0, 0, 0)),
                  pl.BlockSpec((seq, HK * LANE), lambda b, qi: (b, 0)),
                  pl.BlockSpec((HK * VROWS, seq), lambda b, qi: (0, b)),
                  pl.BlockSpec((seq, HK * LANE), lambda b, qi: (b, 1)),
                  pl.BlockSpec((HK * VROWS, seq), lambda b, qi: (1, b)),
                  pl.BlockSpec((HK * GATE_ROWS, QT), qcol),
                  _resident((NSA_HEADS, 2 * ncp, QT)),
                  _resident((HK, 3, KP, G * QT)),
                  _resident((SLC_PAD, ncp))],
        out_specs=pl.BlockSpec((QT, NSA_HEADS * NSA_DIM), lambda b, qi: (b * nq + qi, 0)),
        out_shape=jax.ShapeDtypeStruct((n, NSA_HEADS * NSA_DIM), BF16),
        scratch_shapes=[pltpu.VMEM((SLC_PAD, QT), F32),
                        pltpu.VMEM((VROWS, G * QT), F32),
                        pltpu.VMEM((1, G * QT), F32),
                        pltpu.VMEM((4 * KP, G * QT), F32),
                        pltpu.VMEM((4 * KP, G * QT), F32)],
        compiler_params=_params("parallel", "arbitrary"),
        name="nsa_attention",
    )(zq_t, kc, vc_t, nk, nv_t, nk, nv_t, gates_t, pat, utab, _overlap_t(seq))


MLA_TQ = 512
MLA_HP = 8


def _mla_kernel(q_ref, k_ref, v_ref, o_ref, acc_ref, m_ref, sa_ref, sb_ref):
    qi = pl.program_id(2)
    tq = MLA_TQ
    HP = MLA_HP
    qs = [q_ref[hh * LANE:(hh + 1) * LANE, :] for hh in range(HP)]
    m_ref[...] = jnp.full(m_ref.shape, NEG, F32)
    acc_ref[...] = jnp.zeros(acc_ref.shape, F32)

    def qk(dst_ref, off, hh):
        dst_ref[hh] = _dot(k_ref[pl.ds(off, tq), hh * LANE:(hh + 1) * LANE], qs[hh])

    def update(hh, s, vt, lanes):
        m_old = m_ref[hh, :, lanes]
        m_new = jnp.maximum(m_old, jnp.max(s, axis=0, keepdims=True))
        p = jnp.exp2(s - m_new).astype(BF16)
        acc_ref[hh, :, lanes] = acc_ref[hh, :, lanes] * jnp.exp2(m_old - m_new) + _dot(vt, p)
        m_ref[hh, :, lanes] = m_new

    def consume(src_ref, off, mask, hh):
        vrows = slice(hh * VROWS, (hh + 1) * VROWS)
        if mask is None:
            update(hh, src_ref[hh], v_ref[vrows, pl.ds(off, tq)], slice(0, tq))
            return
        hq = tq // 2
        update(hh, src_ref[hh, :hq, :hq] + mask, v_ref[vrows, pl.ds(off, hq)], slice(0, hq))
        s_b = src_ref[hh, :, hq:]
        s_b = jnp.concatenate([s_b[:hq], s_b[hq:] + mask], axis=0)
        update(hh, s_b, v_ref[vrows, pl.ds(off, tq)], slice(hq, tq))

    def stage(nxt_ref, nxt_off, cur_ref, cur_off, mask):
        for hh in range(HP):
            if nxt_ref is not None:
                qk(nxt_ref, nxt_off, hh)
            consume(cur_ref, cur_off, mask, hh)

    key_j = lax.broadcasted_iota(jnp.int32, (tq // 2, tq // 2), 0)
    qry_i = lax.broadcasted_iota(jnp.int32, (tq // 2, tq // 2), 1)
    causal = jnp.where(key_j <= qry_i, 0.0, NEG)
    for hh in range(HP):
        qk(sa_ref, 0, hh)

    def pair(u, carry):
        off = pl.multiple_of(u * (2 * tq), 2 * tq)
        stage(sb_ref, off + tq, sa_ref, off, None)
        stage(sa_ref, off + 2 * tq, sb_ref, off + tq, None)
        return carry

    lax.fori_loop(0, qi // 2, pair, 0)
    base = pl.multiple_of((qi // 2) * (2 * tq), 2 * tq)

    @pl.when(qi % 2 == 0)
    def _():
        stage(None, None, sa_ref, base, causal)

    @pl.when(qi % 2 == 1)
    def _():
        stage(sb_ref, base + tq, sa_ref, base, None)
        stage(None, None, sb_ref, base + tq, causal)

    outs = []
    for hh in range(HP):
        a = acc_ref[hh]
        outs.append((a[:MLA_V] / a[MLA_V:MLA_V + 1]).T)
    o_ref[...] = jnp.concatenate(outs, axis=1).astype(BF16)


def _mla(qm_t, km, vm_t, *, batch, seq):
    n = km.shape[0]
    tq = MLA_TQ
    nq = seq // tq
    HP = MLA_HP
    return pl.pallas_call(
        _mla_kernel,
        grid=(batch, MLA_HEADS // HP, nq),
        in_specs=[pl.BlockSpec((HP * LANE, tq), lambda b, hp, qi: (hp, b * nq + qi)),
                  pl.BlockSpec((seq, HP * LANE), lambda b, hp, qi: (b, hp)),
                  pl.BlockSpec((HP * VROWS, seq), lambda b, hp, qi: (hp, b))],
        out_specs=pl.BlockSpec((tq, HP * MLA_V), lambda b, hp, qi: (b * nq + qi, hp)),
        out_shape=jax.ShapeDtypeStruct((n, MLA_HEADS * MLA_V), BF16),
        scratch_shapes=[pltpu.VMEM((HP, VROWS, tq), F32),
                        pltpu.VMEM((HP, 1, tq), F32),
                        pltpu.VMEM((HP, tq, tq), F32),
                        pltpu.VMEM((HP, tq, tq), F32)],
        compiler_params=_params("parallel", "parallel", "arbitrary"),
        name="mla_attention",
    )(qm_t, km, vm_t)


def _tail_kernel(x_ref, om_ref, on_ref, p_ref, wo_ref, g1_ref, g2_ref, g3_ref,
                 wg_ref, wu_ref, cw_ref, cb_ref, wd_ref, pg_ref, pp_ref,
                 o_ref, carry_ref, act_ref, *, seq, tm):
    n_chunks = D_FF // FF_CHUNK
    half = om_ref.shape[1]
    y = _dot(om_ref[...], wo_ref[:half, :]) + _dot(on_ref[...], wo_ref[half:, :])
    x = x_ref[...] + _rms(y, g1_ref[...])

    h = _rms(x, g2_ref[...]).astype(BF16)

    @pl.when((pl.program_id(0) * tm) % seq == 0)
    def _():
        carry_ref[...] = jnp.zeros(carry_ref.shape, F32)

    SUB = 8
    row8 = lax.broadcasted_iota(jnp.int32, (SUB, 1), 0)
    for c in range(n_chunks):
        cols = slice(c * FF_CHUNK, (c + 1) * FF_CHUNK)
        g = _dot(h, wg_ref[:, cols])
        prev = carry_ref[:, cols]
        carry_ref[:, cols] = g[tm - SUB:, :]
        r1 = pltpu.roll(g, 1, 0)
        r2 = pltpu.roll(g, 2, 0)
        top1 = jnp.where(row8 == 0, prev[7:8, :], r1[:SUB])
        top2 = jnp.where(row8 == 0, prev[6:7, :], jnp.where(row8 == 1, prev[7:8, :], r2[:SUB]))
        g1 = jnp.concatenate([top1, r1[SUB:]], axis=0)
        g2 = jnp.concatenate([top2, r2[SUB:]], axis=0)
        conv = (cw_ref[0:1, cols] * g2 + cw_ref[1:2, cols] * g1 + cw_ref[2:3, cols] * g
                + cb_ref[:, cols])
        act_ref[:, cols] = (_gelu_tanh(conv) * _dot(h, wu_ref[:, cols])).astype(BF16)
    x = x + _rms(_dot(act_ref[...], wd_ref[...]), g3_ref[...])

    gate = _sigmoid(_dot(x.astype(BF16), pg_ref[...]))
    o_ref[...] = x + gate * _dot(p_ref[...].astype(BF16), pp_ref[...])


def _tail(x2, om, on, p2, layer, wo, g1, g2, g3, wg, wu, cw, cb, wd, pg, pp, *, seq):
    n = x2.shape[0]
    tm = ROW_TILE
    row = lambda r: (r, 0)
    half = om.shape[1]
    lr = functools.partial(_layer_resident, layer)
    return pl.pallas_call(
        functools.partial(_tail_kernel, seq=seq, tm=tm),
        grid=(n // tm,),
        in_specs=[pl.BlockSpec((tm, D_MODEL), row),
                  pl.BlockSpec((tm, half), row),
                  pl.BlockSpec((tm, half), row),
                  pl.BlockSpec((None, tm, PLE_DIM), lambda r: (layer, r, 0)),
                  lr((2 * half, D_MODEL)),
                  lr((1, D_MODEL)),
                  lr((1, D_MODEL)),
                  lr((1, D_MODEL)),
                  lr((D_MODEL, D_FF)),
                  lr((D_MODEL, D_FF)),
                  lr((CONV_WIDTH, D_FF)),
                  lr((1, D_FF)),
                  lr((D_FF, D_MODEL)),
                  lr((D_MODEL, D_MODEL)),
                  lr((PLE_DIM, D_MODEL))],
        out_specs=pl.BlockSpec((tm, D_MODEL), row),
        out_shape=jax.ShapeDtypeStruct((n, D_MODEL), F32),
        scratch_shapes=[pltpu.VMEM((8, D_FF), F32),
                        pltpu.VMEM((tm, D_FF), BF16)],
        compiler_params=_params("arbitrary"),
        name="layer_tail",
    )(x2, om, on, p2, wo, g1, g2, g3, wg, wu, cw, cb, wd, pg, pp)


def _rot_rows(w):
    half = w.shape[-2] // 2
    return jnp.concatenate([-w[..., half:, :], w[..., :half, :]], axis=-2)


def _prep_inproj(w):
    wt = w.T
    o = np.cumsum((0,) + IN_SPLITS)
    cq, ckv, kr, qn, kcmp, vcmp, kslc, vslc, kwin, vwin, gn = [wt[o[j]:o[j + 1]] for j in range(len(IN_SPLITS))]
    d = w.shape[0]

    def zeros(rows):
        return jnp.zeros((rows, d), F32)

    def kslots(m):
        return jnp.concatenate([m[:NSA_DIM], zeros(LANE - NSA_DIM), m[NSA_DIM:], zeros(LANE - NSA_DIM)], axis=0)

    def vslots(m):
        return jnp.concatenate([m[:NSA_DIM], zeros(VROWS - NSA_DIM), m[NSA_DIM:], zeros(VROWS - NSA_DIM)], axis=0)

    pad_rope = zeros(LANE - MLA_NOPE - MLA_ROPE)
    w1 = jnp.concatenate([cq, ckv,
                          zeros(MLA_NOPE), kr, pad_rope,
                          zeros(MLA_NOPE), _rot_rows(kr), pad_rope,
                          kslots(kslc), kslots(kwin), kcmp, vcmp], axis=0)
    assert w1.shape[0] == C_TOTAL
    per = 3 * NSA_GROUP
    w1t = jnp.concatenate([qn * NSA_DIM ** -0.5, vslots(vslc), vslots(vwin),
                           gn[:per], zeros(GATE_ROWS - per), gn[per:], zeros(GATE_ROWS - per)], axis=0)
    assert w1t.shape[0] == R_TOTAL
    return w1.astype(BF16).T, w1t.astype(BF16)


def _prep_mla(w_uq, w_ukv):
    r = w_uq.shape[0]
    dq = MLA_NOPE + MLA_ROPE
    ut = w_uq.T.reshape(MLA_HEADS, dq, r)
    pad = jnp.zeros((MLA_HEADS, LANE - dq, r), F32)
    wqa_t = jnp.concatenate([ut, pad], axis=1).reshape(MLA_HEADS * LANE, r)
    wqb_t = _rot_rows(ut[:, MLA_NOPE:]).reshape(MLA_HEADS * MLA_ROPE, r)
    rk = w_ukv.shape[0]
    kv = w_ukv.reshape(rk, MLA_HEADS, 2, MLA_NOPE)
    wk = jnp.concatenate([kv[:, :, 0, :], jnp.zeros((rk, MLA_HEADS, LANE - MLA_NOPE), F32)],
                         axis=-1).reshape(rk, MLA_HEADS * LANE)
    vt = kv[:, :, 1, :].transpose(1, 2, 0)
    wvt = jnp.concatenate([vt, jnp.zeros((MLA_HEADS, VROWS - MLA_V, rk), F32)], axis=1).reshape(MLA_HEADS * VROWS, rk)
    return wqa_t.astype(BF16), wqb_t.astype(BF16), wk.astype(BF16), wvt.astype(BF16)


def kernel(x, p, positions, rel_bias, attn_pre_norm, attn_post_norm, ffn_pre_norm, ffn_post_norm,
           w_in, mla_q_norm, mla_w_uq, mla_kv_norm, mla_w_ukv, nsa_cmp_pos, nsa_cmp_w1, nsa_cmp_w2,
           w_o, ffn_w_gate, ffn_w_up, ffn_conv_w, ffn_conv_b, ffn_w_down, ple_proj, ple_gate):
    batch, seq, d = x.shape
    depth = w_in.shape[0]
    n = batch * seq
    x2 = x.reshape(n, d)
    ctt, stt, ct, st = _rope_tables(positions)
    pat, utab = _bias_tables(rel_bias, seq)
    tail_params = (w_o.astype(BF16), attn_post_norm[:, None, :], ffn_pre_norm[:, None, :],
                   ffn_post_norm[:, None, :], ffn_w_gate.astype(BF16), ffn_w_up.astype(BF16),
                   ffn_conv_w, ffn_conv_b[:, None, :], ffn_w_down.astype(BF16),
                   ple_gate.astype(BF16), ple_proj.astype(BF16))
    for i in range(depth):
        w1, w1t = _prep_inproj(w_in[i])
        wqa, wqb, wk, wvt = _prep_mla(mla_w_uq[i], mla_w_ukv[i])
        qm_t, km, vm_t, zq_t, nk, nv_t, kcmp, vcmp, gates_t = _inproj(
            x2, attn_pre_norm[i][None, :], w1, w1t, mla_q_norm[i][None, :], wqa, wqb,
            mla_kv_norm[i][None, :], wk, wvt, ct, st, ctt, stt, batch=batch, seq=seq)
        kc, vc_t = _compress(kcmp, vcmp,
                             nsa_cmp_pos[i],
                             nsa_cmp_w1[i].astype(BF16), nsa_cmp_w2[i, 0].astype(BF16),
                             nsa_cmp_w2[i, 1].T.astype(BF16), batch=batch, seq=seq)
        o_nsa = _nsa(zq_t, kc, vc_t, nk, nv_t, gates_t, pat, utab, batch=batch, seq=seq)
        o_mla = _mla(qm_t, km, vm_t, batch=batch, seq=seq)
        x2 = _tail(x2, o_mla, o_nsa, p.reshape(depth, n, PLE_DIM), i, *tail_params, seq=seq)
    return x2.reshape(batch, seq, d)
```

```python
import functools
import math

import numpy as np
import jax
import jax.numpy as jnp
from jax import lax
from jax.experimental import pallas as pl
from jax.experimental.pallas import tpu as pltpu

F32 = jnp.float32
BF16 = jnp.bfloat16

D_MODEL = 1024
MLA_HEADS = 8
MLA_NOPE = 64
MLA_ROPE = 32
MLA_V = 64
MLA_Q_RANK = 256
MLA_KV_RANK = 128
ROPE_BASE = 10000.0
NSA_HEADS = 8
NSA_KV_HEADS = 2
NSA_GROUP = NSA_HEADS // NSA_KV_HEADS
NSA_DIM = 64
CMP_LEN = 32
CMP_STRIDE = 16
CMP_HIDDEN = 128
SLC_LEN = 64
SLC_TOPK = 16
WINDOW = 512
REL_BUCKETS = 32
REL_MAX_DIST = 128
D_FF = 2816
CONV_WIDTH = 3
PLE_DIM = 256
EPS = 1e-6
NEG = -1e30
POS_BIG = 1e30
LOG2E = math.log2(math.e)

IN_SPLITS = (MLA_Q_RANK, MLA_KV_RANK, MLA_ROPE, NSA_HEADS * NSA_DIM,
             NSA_KV_HEADS * NSA_DIM, NSA_KV_HEADS * NSA_DIM,
             NSA_KV_HEADS * NSA_DIM, NSA_KV_HEADS * NSA_DIM,
             NSA_KV_HEADS * NSA_DIM, NSA_KV_HEADS * NSA_DIM,
             3 * NSA_HEADS)

LANE = 128
QT = 256
KP = 128
SLC_PAD = 64
FF_CHUNK = 256
ROW_TILE = 1024
VMEM_LIMIT = 56 * 1024 * 1024

VROWS = 80

C_CQ = 0
C_CKV = 256
C_KR = 384
C_KRROT = 512
C_KSLC = 640
C_KWIN = 896
C_KCMP = 1152
C_VCMP = 1280
C_TOTAL = 1408
R_QN = 0
R_VSLC = 512
R_VWIN = R_VSLC + NSA_KV_HEADS * VROWS
R_GATE = R_VWIN + NSA_KV_HEADS * VROWS
GATE_ROWS = 16
R_TOTAL = R_GATE + NSA_KV_HEADS * GATE_ROWS


def _dot(a, b):
    return jnp.dot(a, b, preferred_element_type=F32)


def _dot_nt(a, b):
    return lax.dot_general(a, b, (((1,), (1,)), ((), ())), preferred_element_type=F32)


def _rms(x, g):
    return x * lax.rsqrt(jnp.mean(x * x, axis=-1, keepdims=True) + EPS) * g


def _gelu_tanh(x):
    return 0.5 * x * (1.0 + jnp.tanh(math.sqrt(2.0 / math.pi) * (x + 0.044715 * (x * x * x))))


def _sigmoid(x):
    return 1.0 / (1.0 + jnp.exp(-x))


def _params(*sem):
    return pltpu.CompilerParams(dimension_semantics=sem, vmem_limit_bytes=VMEM_LIMIT)


def _resident(shape):
    nd = len(shape)
    return pl.BlockSpec(shape, lambda *_: (0,) * nd, pipeline_mode=pl.Buffered(1))


def _layer_resident(layer, shape):
    nd = len(shape)
    return pl.BlockSpec((None,) + tuple(shape), lambda *_: (layer,) + (0,) * nd, pipeline_mode=pl.Buffered(1))


def _rope_kernel(pos_ref, inv_ref, ctt_ref, stt_ref, ct_ref, st_ref):
    ang = inv_ref[...] * pos_ref[...].astype(F32)
    c = jnp.cos(ang)
    s = jnp.sin(ang)
    tn = ang.shape[1]
    pad = jnp.zeros((LANE - MLA_NOPE - MLA_ROPE, tn), F32)
    ct = jnp.concatenate([jnp.ones((MLA_NOPE, tn), F32), c, c, pad], axis=0)
    st = jnp.concatenate([jnp.zeros((MLA_NOPE, tn), F32), s, s, pad], axis=0)
    ctt_ref[...] = ct
    stt_ref[...] = st
    ct_ref[...] = ct.T
    st_ref[...] = st.T


def _rope_tables(positions):
    n = positions.size
    tn = 2048
    half = MLA_ROPE // 2
    inv = (ROPE_BASE ** (-jnp.arange(half, dtype=F32) / half))[:, None]
    return pl.pallas_call(
        _rope_kernel,
        grid=(n // tn,),
        in_specs=[pl.BlockSpec((1, tn), lambda r: (0, r)),
                  pl.BlockSpec((half, 1), lambda r: (0, 0))],
        out_specs=[pl.BlockSpec((LANE, tn), lambda r: (0, r))] * 2
                  + [pl.BlockSpec((tn, LANE), lambda r: (r, 0))] * 2,
        out_shape=[jax.ShapeDtypeStruct((LANE, n), F32)] * 2 + [jax.ShapeDtypeStruct((n, LANE), F32)] * 2,
        compiler_params=_params("parallel"),
        name="rope_tables",
    )(positions.reshape(1, n), inv)


def _bucket_np(dist):
    n = np.maximum(dist, 0)
    max_exact = REL_BUCKETS // 2
    large = max_exact + (np.log(np.maximum(n, 1).astype(np.float32) / max_exact)
                         / math.log(REL_MAX_DIST / max_exact)
                         * (REL_BUCKETS - max_exact)).astype(np.int32)
    large = np.minimum(large, REL_BUCKETS - 1)
    return np.where(n < max_exact, n, large).astype(np.int32)


def _bias_kernel(table_ref, bpc_ref, bpu_ref, pat_ref, u_ref, *, lo, hi):
    h = pl.program_id(0)
    far = table_ref[REL_BUCKETS - 1, h]

    def lookup(bp, sub):
        acc = jnp.full(bp.shape, far - sub, F32)
        for b in range(REL_BUCKETS - 1):
            acc = jnp.where(bp == b, table_ref[b, h] - sub, acc)
        return jnp.where(bp < 0, NEG, acc * LOG2E)

    rows = pat_ref.shape[1]
    pat_ref[0, :lo] = jnp.full((lo, QT), far * LOG2E, F32)
    pat_ref[0, lo:hi] = lookup(bpc_ref[...], 0.0)
    pat_ref[0, hi:] = jnp.full((rows - hi, QT), NEG, F32)
    for d in range(3):
        u_ref[0, d] = lookup(bpu_ref[d], far)


def _bias_tables(rel_bias, seq):
    ncp = seq // CMP_STRIDE
    i = np.arange(QT)[None, :]
    cprime = np.arange(2 * ncp)[:, None] - ncp
    dist_c = i - CMP_STRIDE * cprime - (CMP_LEN - 1)
    bpc = np.where(dist_c >= 0, _bucket_np(dist_c), -1).astype(np.int32)
    varied = np.nonzero(~(np.all(bpc == REL_BUCKETS - 1, axis=1) | np.all(bpc == -1, axis=1)))[0]
    lo = int(varied[0]) // 8 * 8
    hi = -(-(int(varied[-1]) + 1) // 8) * 8
    assert np.all(bpc[:lo] == REL_BUCKETS - 1) and np.all(bpc[hi:] == -1)
    j = np.arange(KP)[:, None]
    bpu = []
    for delta in (-1, 0, 1):
        dist = i - KP * delta - j
        bpu.append(np.where(dist >= 0, _bucket_np(dist), -1))
    bpu = np.stack(bpu).astype(np.int32)
    return pl.pallas_call(
        functools.partial(_bias_kernel, lo=lo, hi=hi),
        grid=(NSA_HEADS,),
        in_specs=[pl.BlockSpec(memory_space=pltpu.SMEM),
                  pl.BlockSpec((hi - lo, QT), lambda h: (0, 0)),
                  pl.BlockSpec((3, KP, QT), lambda h: (0, 0, 0))],
        out_specs=[pl.BlockSpec((1, 2 * ncp, QT), lambda h: (h, 0, 0)),
                   pl.BlockSpec((1, 3, KP, QT), lambda h: (h // NSA_GROUP, 0, 0, h % NSA_GROUP))],
        out_shape=[jax.ShapeDtypeStruct((NSA_HEADS, 2 * ncp, QT), F32),
                   jax.ShapeDtypeStruct((NSA_KV_HEADS, 3, KP, NSA_GROUP * QT), F32)],
        compiler_params=_params("parallel"),
        name="bias_tables",
    )(rel_bias.astype(F32), jnp.asarray(bpc[lo:hi]), jnp.asarray(bpu))


def _inproj_kernel(x_ref, g_ref, w1_ref, w1t_ref, qn_ref, wqa_ref, wqb_ref, kvn_ref, wk_ref, wvt_ref,
                   ct_ref, st_ref, ctt_ref, stt_ref, ones_n_ref, ones_m_ref,
                   qm_ref, km_ref, vm_ref, zq_ref, nk_ref, nv_ref, kcmp_ref, vcmp_ref, gate_ref,
                   *, seq, tm):
    h = _rms(x_ref[...], g_ref[...]).astype(BF16)

    def proj(c0, width):
        return _dot(h, w1_ref[:, c0:c0 + width])

    cq = _rms(proj(C_CQ, MLA_Q_RANK), qn_ref[...]).astype(BF16)
    qa = _dot_nt(wqa_ref[...], cq)
    qb = _dot_nt(wqb_ref[...], cq)
    scale = (MLA_NOPE + MLA_ROPE) ** -0.5 * LOG2E
    cts = ctt_ref[...] * scale
    sts = stt_ref[MLA_NOPE:MLA_NOPE + MLA_ROPE, :] * scale
    for hh in range(MLA_HEADS):
        sl = slice(hh * LANE, (hh + 1) * LANE)
        slot = qa[sl] * cts
        rope = slot[MLA_NOPE:MLA_NOPE + MLA_ROPE] + qb[hh * MLA_ROPE:(hh + 1) * MLA_ROPE] * sts
        qm_ref[sl, :] = jnp.concatenate([slot[:MLA_NOPE], rope, slot[MLA_NOPE + MLA_ROPE:]], axis=0).astype(BF16)

    ckv = _rms(proj(C_CKV, MLA_KV_RANK), kvn_ref[...]).astype(BF16)
    kr = proj(C_KR, LANE) * ct_ref[...] + proj(C_KRROT, LANE) * st_ref[...]
    kn = _dot(ckv, wk_ref[...])
    for hh in range(MLA_HEADS):
        sl = slice(hh * LANE, (hh + 1) * LANE)
        km_ref[:, sl] = (kn[:, sl] + kr).astype(BF16)
    vm_ref[...] = (_dot_nt(wvt_ref[...], ckv) + ones_m_ref[...]).astype(BF16)

    zt = _dot_nt(w1t_ref[...], h)
    zq_ref[...] = (zt[R_QN:R_VSLC] * LOG2E).astype(BF16)
    nv_ref[...] = (zt[R_VSLC:R_GATE] + ones_n_ref[...]).astype(BF16)
    gate_ref[...] = _sigmoid(zt[R_GATE:R_TOTAL])

    lane = lax.broadcasted_iota(jnp.int32, (tm, LANE), 1)
    s0 = (pl.program_id(0) * tm) % seq
    row = lax.broadcasted_iota(jnp.int32, (tm, LANE), 0)
    onehot = (lane - NSA_DIM == (s0 + row) // SLC_LEN).astype(F32)
    ksl = proj(C_KSLC, 2 * LANE)
    for hk in range(NSA_KV_HEADS):
        sl = slice(hk * LANE, (hk + 1) * LANE)
        nk_ref[:, sl] = (ksl[:, sl] + onehot).astype(BF16)
    nk_ref[:, 2 * LANE:] = proj(C_KWIN, 2 * LANE).astype(BF16)

    kcmp = proj(C_KCMP, LANE)
    vcmp = proj(C_VCMP, LANE)
    for hk in range(NSA_KV_HEADS):
        kcmp_ref[0, hk] = kcmp[:, hk * NSA_DIM:(hk + 1) * NSA_DIM]
        vcmp_ref[0, hk] = vcmp[:, hk * NSA_DIM:(hk + 1) * NSA_DIM]


def _ones_rows(n_slots):
    r = np.arange(n_slots * VROWS) % VROWS
    return jnp.asarray((r >= NSA_DIM).astype(np.float32)[:, None])


def _inproj(x2, g_pre, w1, w1t, qn, wqa, wqb, kvn, wk, wvt, ct, st, ctt, stt, *, batch, seq):
    n = x2.shape[0]
    tm = ROW_TILE
    tiles_per_seq = seq // tm
    row = lambda r: (r, 0)
    col = lambda r: (0, r)
    hm = lambda r: (r // tiles_per_seq, 0, r % tiles_per_seq, 0)
    nv_rows = 2 * NSA_KV_HEADS * VROWS
    vm_rows = MLA_HEADS * VROWS
    return pl.pallas_call(
        functools.partial(_inproj_kernel, seq=seq, tm=tm),
        grid=(n // tm,),
        in_specs=[pl.BlockSpec((tm, D_MODEL), row),
                  _resident((1, D_MODEL)),
                  _resident((D_MODEL, C_TOTAL)),
                  _resident((R_TOTAL, D_MODEL)),
                  _resident((1, MLA_Q_RANK)),
                  _resident((MLA_HEADS * LANE, MLA_Q_RANK)),
                  _resident((MLA_HEADS * MLA_ROPE, MLA_Q_RANK)),
                  _resident((1, MLA_KV_RANK)),
                  _resident((MLA_KV_RANK, MLA_HEADS * LANE)),
                  _resident((vm_rows, MLA_KV_RANK)),
                  pl.BlockSpec((tm, LANE), row),
                  pl.BlockSpec((tm, LANE), row),
                  pl.BlockSpec((LANE, tm), col),
                  pl.BlockSpec((LANE, tm), col),
                  _resident((nv_rows, 1)),
                  _resident((vm_rows, 1))],
        out_specs=[pl.BlockSpec((MLA_HEADS * LANE, tm), col),
                   pl.BlockSpec((tm, MLA_HEADS * LANE), row),
                   pl.BlockSpec((vm_rows, tm), col),
                   pl.BlockSpec((NSA_HEADS * NSA_DIM, tm), col),
                   pl.BlockSpec((tm, 4 * LANE), row),
                   pl.BlockSpec((nv_rows, tm), col),
                   pl.BlockSpec((1, NSA_KV_HEADS, tm, NSA_DIM), hm),
                   pl.BlockSpec((1, NSA_KV_HEADS, tm, NSA_DIM), hm),
                   pl.BlockSpec((NSA_KV_HEADS * GATE_ROWS, tm), col)],
        out_shape=[jax.ShapeDtypeStruct((MLA_HEADS * LANE, n), BF16),
                   jax.ShapeDtypeStruct((n, MLA_HEADS * LANE), BF16),
                   jax.ShapeDtypeStruct((vm_rows, n), BF16),
                   jax.ShapeDtypeStruct((NSA_HEADS * NSA_DIM, n), BF16),
                   jax.ShapeDtypeStruct((n, 4 * LANE), BF16),
                   jax.ShapeDtypeStruct((nv_rows, n), BF16),
                   jax.ShapeDtypeStruct((batch, NSA_KV_HEADS, seq, NSA_DIM), F32),
                   jax.ShapeDtypeStruct((batch, NSA_KV_HEADS, seq, NSA_DIM), F32),
                   jax.ShapeDtypeStruct((NSA_KV_HEADS * GATE_ROWS, n), F32)],
        compiler_params=_params("parallel"),
        name="in_proj",
    )(x2, g_pre, w1, w1t, qn, wqa, wqb, kvn, wk, wvt, ct, st, ctt, stt,
      _ones_rows(2 * NSA_KV_HEADS), _ones_rows(MLA_HEADS))


def _compress_kernel(k_ref, v_ref, pos_ref, w1_ref, w2k_ref, w2vt_ref, kc_ref, vc_ref):
    ncp = kc_ref.shape[2]

    def hidden(x_ref, j):
        u = jnp.zeros((ncp, CMP_HIDDEN), F32)
        low = jnp.zeros((ncp, CMP_HIDDEN), F32)
        for l in range(CMP_STRIDE):
            x = x_ref[0, 0, pl.ds(l, ncp, stride=CMP_STRIDE), :]
            top = (x + pos_ref[j, l:l + 1, :]).astype(BF16)
            bot = (x + pos_ref[j, CMP_STRIDE + l:CMP_STRIDE + l + 1, :]).astype(BF16)
            u = u + _dot(top, w1_ref[j, l * NSA_DIM:(l + 1) * NSA_DIM, :])
            low = low + _dot(bot, w1_ref[j, (CMP_STRIDE + l) * NSA_DIM:(CMP_STRIDE + l + 1) * NSA_DIM, :])
        nxt = jnp.concatenate([low[1:], jnp.zeros((1, CMP_HIDDEN), F32)], axis=0)
        return _gelu_tanh(u + nxt).astype(BF16)

    kc_ref[0, 0] = _dot(hidden(k_ref, 0), w2k_ref[...]).astype(BF16)
    vc_ref[0, 0] = _dot_nt(w2vt_ref[...], hidden(v_ref, 1)).astype(BF16)


def _compress(kcmp, vcmp, pos, w1, w2k, w2vt, *, batch, seq):
    ncp = seq // CMP_STRIDE
    blk = lambda b, hk: (b, hk, 0, 0)
    return pl.pallas_call(
        _compress_kernel,
        grid=(batch, NSA_KV_HEADS),
        in_specs=[pl.BlockSpec((1, 1, seq, NSA_DIM), blk),
                  pl.BlockSpec((1, 1, seq, NSA_DIM), blk),
                  pl.BlockSpec((2, CMP_LEN, NSA_DIM), lambda b, hk: (0, 0, 0)),
                  pl.BlockSpec((2, CMP_LEN * NSA_DIM, CMP_HIDDEN), lambda b, hk: (0, 0, 0)),
                  pl.BlockSpec((CMP_HIDDEN, NSA_DIM), lambda b, hk: (0, 0)),
                  pl.BlockSpec((NSA_DIM, CMP_HIDDEN), lambda b, hk: (0, 0))],
        out_specs=[pl.BlockSpec((1, 1, ncp, NSA_DIM), blk),
                   pl.BlockSpec((1, 1, NSA_DIM, ncp), blk)],
        out_shape=[jax.ShapeDtypeStruct((batch, NSA_KV_HEADS, ncp, NSA_DIM), BF16),
                   jax.ShapeDtypeStruct((batch, NSA_KV_HEADS, NSA_DIM, ncp), BF16)],
        compiler_params=_params("parallel", "parallel"),
        name="nsa_compress",
    )(kcmp, vcmp, pos, w1, w2k, w2vt)


def _nsa_kernel(*refs):
    for hk in range(NSA_KV_HEADS):
        _nsa_group(hk, *refs)


def _nsa_group(hk, zq_ref, kc_ref, vc_ref, ksl_ref, vsl_ref, kw_ref, vw_ref, gate_ref,
               pat_ref, u_ref, ovt_ref, o_ref, imp_ref, acc_ref, m_ref, sa_ref, sb_ref):
    qi = pl.program_id(1)
    G = NSA_GROUP
    kcols = slice(hk * LANE, (hk + 1) * LANE)
    vrows = slice(hk * VROWS, (hk + 1) * VROWS)
    M = G * QT
    PPT = QT // KP
    q4 = zq_ref[hk * G * NSA_DIM:(hk + 1) * G * NSA_DIM, :]
    qs = jnp.concatenate([q4[g * NSA_DIM:(g + 1) * NSA_DIM, :] for g in range(G)], axis=1)
    kc = kc_ref[0, hk]
    vc_t = vc_ref[0, hk]
    ovt = ovt_ref[...]
    ncp = kc.shape[0]
    qcol = qi * QT + (lax.broadcasted_iota(jnp.int32, (1, M), 1) & (QT - 1))
    u_m1 = u_ref[hk, 0]
    u_0 = u_ref[hk, 1]
    u_p1 = u_ref[hk, 2]

    n_back = WINDOW // KP
    n_w = n_back + PPT
    MS = G * KP

    def sub_lanes(x, a):
        return jnp.concatenate([x[:, g * QT + a * KP:g * QT + (a + 1) * KP] for g in range(G)], axis=1)

    qw = jnp.concatenate([qs, jnp.zeros_like(qs)], axis=0)
    pad_v = jnp.where(lax.broadcasted_iota(jnp.int32, (VROWS, KP), 0) >= NSA_DIM, 1.0, 0.0).astype(BF16)
    k_rows, v_cols = [], []
    for w in range(n_w):
        piece = PPT * qi - n_back + w
        off = pl.multiple_of(jnp.maximum(piece, 0) * KP, KP)
        kt = kw_ref[pl.ds(off, KP), kcols]
        vt = vw_ref[vrows, pl.ds(off, KP)]
        if w < n_back:
            kt = jnp.where(piece >= 0, kt, jnp.zeros_like(kt))
            vt = jnp.where(piece >= 0, vt, pad_v)
        k_rows.append(kt)
        v_cols.append(vt)
    s_sub = [_dot(jnp.concatenate(k_rows[a:a + n_back + 1], axis=0), sub_lanes(qw, a))
             for a in range(PPT)]

    slide = QT // CMP_STRIDE
    off_b = pl.multiple_of(ncp - slide * qi, slide)
    bias = jnp.concatenate([pat_ref[hk * G + g, pl.ds(off_b, ncp), :] for g in range(G)], axis=1)
    s = _dot(kc, qs) + bias
    e = jnp.exp2(s - jnp.max(s, axis=0, keepdims=True))
    scale = jnp.where(qcol >= CMP_LEN - 1, 1.0 / jnp.sum(e, axis=0, keepdims=True), 0.0)
    p_cmp = (e * scale).astype(BF16)

    key_j = lax.broadcasted_iota(jnp.int32, (KP, MS), 0)
    q_loc = lax.broadcasted_iota(jnp.int32, (KP, MS), 1) & (KP - 1)
    u_tabs = [u_m1, u_0, u_p1]
    p_sub = []
    for a in range(PPT):
        pieces = [s_sub[a][w * KP:(w + 1) * KP] for w in range(n_back + 1)]
        pieces[0] = pieces[0] + jnp.where(key_j > q_loc, 0.0, NEG)
        pieces[n_back - 1] = pieces[n_back - 1] + sub_lanes(u_tabs[a], a)
        pieces[n_back] = pieces[n_back] + sub_lanes(u_tabs[a + 1], a)
        s_a = jnp.concatenate(pieces, axis=0)
        p_sub.append(jnp.exp2(s_a - jnp.max(s_a, axis=0, keepdims=True)).astype(BF16))

    o_cmp = _dot(vc_t, p_cmp)
    imp_t = jnp.zeros((SLC_PAD, QT), F32)
    for g in range(G):
        imp_t = imp_t + _dot(ovt, p_cmp[:, g * QT:(g + 1) * QT])

    o_sub = []
    for a in range(PPT):
        a_w = _dot(jnp.concatenate(v_cols[a:a + n_back + 1], axis=1), p_sub[a])
        o_sub.append(a_w[:NSA_DIM] / a_w[NSA_DIM:NSA_DIM + 1])
    o_win = jnp.concatenate([o_sub[a][:, g * KP:(g + 1) * KP] for g in range(G) for a in range(PPT)], axis=1)

    n_id = lax.broadcasted_iota(jnp.int32, (SLC_PAD, QT), 0)
    q_blk = (qi * QT + lax.broadcasted_iota(jnp.int32, (SLC_PAD, QT), 1)) // SLC_LEN
    forced = (n_id == 0) | (n_id == q_blk) | (n_id == q_blk - 1)
    imp = jnp.where(forced, POS_BIG, jnp.where(n_id > q_blk, NEG, imp_t))
    imp_ref[...] = imp
    SUB = 8
    slabs = [imp[v * SUB:(v + 1) * SUB] for v in range(SLC_PAD // SUB)]
    ranks = [jnp.zeros((SUB, QT), jnp.int32) for _ in slabs]
    sub_id = lax.broadcasted_iota(jnp.int32, (SUB, QT), 0)
    for m in range(SLC_PAD):
        other = imp_ref[m:m + 1, :]
        for v, slab in enumerate(slabs):
            if v > m // SUB:
                beats = (other >= slab).astype(jnp.int32)
            elif v < m // SUB:
                beats = (other > slab).astype(jnp.int32)
            else:
                beats = jnp.where(sub_id > m % SUB, (other >= slab).astype(jnp.int32),
                                  (other > slab).astype(jnp.int32))
            ranks[v] = ranks[v] + beats
    rank = jnp.concatenate(ranks, axis=0)
    selb = jnp.where(rank < SLC_TOPK, 0.0, NEG).astype(BF16)
    qaug = jnp.concatenate([qs, jnp.concatenate([selb] * G, axis=1)], axis=0)

    PIECES = 4
    TK = PIECES * KP
    n_t = (PPT * qi + PPT - 1) // PIECES + 1
    at_start = (PPT * qi) % PIECES == 0
    last_bias = [jnp.where(at_start, u_0, 0.0), jnp.where(at_start, u_p1, u_m1),
                 jnp.where(at_start, NEG, u_0), jnp.where(at_start, NEG, u_p1)]
    prev_bias = [None, None, None, jnp.where(at_start, u_m1, 0.0)]
    m_ref[...] = jnp.full(m_ref.shape, NEG, F32)
    acc_ref[...] = jnp.zeros(acc_ref.shape, F32)

    HALVES = 2
    MH = M // HALVES

    def qk(dst_ref, t, hf):
        off = pl.multiple_of(t * TK, TK)
        lanes = slice(hf * MH, (hf + 1) * MH)
        dst_ref[:, lanes] = _dot(ksl_ref[pl.ds(off, TK), kcols], qaug[:, lanes])

    def consume(src_ref, t, bias, hf):
        off = pl.multiple_of(t * TK, TK)
        lanes = slice(hf * MH, (hf + 1) * MH)
        s = src_ref[:, lanes]
        if bias is not None:
            s = jnp.concatenate([s[j * KP:(j + 1) * KP] if bias[j] is None
                                 else s[j * KP:(j + 1) * KP] + bias[j][:, lanes]
                                 for j in range(PIECES)], axis=0)
        m_old = m_ref[:, lanes]
        m_new = jnp.maximum(m_old, jnp.max(s, axis=0, keepdims=True))
        p = jnp.exp2(s - m_new).astype(BF16)
        acc_ref[:, lanes] = acc_ref[:, lanes] * jnp.exp2(m_old - m_new) + _dot(vsl_ref[vrows, pl.ds(off, TK)], p)
        m_ref[:, lanes] = m_new

    def stage(nxt_ref, nxt_t, cur_ref, cur_t, bias):
        for hf in range(HALVES):
            if nxt_ref is not None:
                qk(nxt_ref, nxt_t, hf)
            consume(cur_ref, cur_t, bias, hf)

    n_plain = jnp.maximum(n_t - 2, 0)
    for hf in range(HALVES):
        qk(sa_ref, 0, hf)

    def pair(u, carry):
        t = 2 * u
        stage(sb_ref, t + 1, sa_ref, t, None)
        stage(sa_ref, t + 2, sb_ref, t + 1, None)
        return carry

    lax.fori_loop(0, n_plain // 2, pair, 0)
    tb = (n_plain // 2) * 2

    @pl.when(n_t == 1)
    def _():
        stage(None, None, sa_ref, 0, last_bias)

    @pl.when((n_t >= 2) & (n_plain % 2 == 0))
    def _():
        stage(sb_ref, tb + 1, sa_ref, tb, prev_bias)
        stage(None, None, sb_ref, tb + 1, last_bias)

    @pl.when(n_plain % 2 == 1)
    def _():
        stage(sb_ref, tb + 1, sa_ref, tb, None)
        stage(sa_ref, tb + 2, sb_ref, tb + 1, prev_bias)
        stage(None, None, sa_ref, tb + 2, last_bias)

    a_s = acc_ref[...]
    o_sel = a_s[:NSA_DIM] / a_s[NSA_DIM:NSA_DIM + 1]

    gate = gate_ref[hk * GATE_ROWS:(hk + 1) * GATE_ROWS, :]
    outs = []
    for g in range(G):
        cols = slice(g * QT, (g + 1) * QT)
        o_t = (gate[3 * g:3 * g + 1, :] * o_cmp[:, cols] + gate[3 * g + 1:3 * g + 2, :] * o_sel[:, cols]
               + gate[3 * g + 2:3 * g + 3, :] * o_win[:, cols])
        outs.append(o_t.T)
    o_ref[:, hk * G * NSA_DIM:(hk + 1) * G * NSA_DIM] = jnp.concatenate(outs, axis=1).astype(BF16)


def _overlap_t(seq):
    ncp = seq // CMP_STRIDE
    n_cmp = (seq - CMP_LEN) // CMP_STRIDE + 1
    n_slc = seq // SLC_LEN
    cs = np.arange(ncp)[None, :] * CMP_STRIDE
    ss = np.arange(SLC_PAD)[:, None] * SLC_LEN
    ov = np.maximum(np.minimum(cs + CMP_LEN, ss + SLC_LEN) - np.maximum(cs, ss), 0).astype(np.float32) / CMP_LEN
    ov = ov * (np.arange(ncp)[None, :] < n_cmp) * (np.arange(SLC_PAD)[:, None] < n_slc)
    return jnp.asarray(ov, BF16)


def _nsa(zq_t, kc, vc_t, nk, nv_t, gates_t, pat, utab, *, batch, seq):
    n = nk.shape[0]
    nq = seq // QT
    ncp = seq // CMP_STRIDE
    G = NSA_GROUP
    HK = NSA_KV_HEADS
    assert seq // SLC_LEN <= SLC_PAD and seq // SLC_LEN >= SLC_TOPK and seq >= WINDOW
    qcol = lambda b, qi: (0, b * nq + qi)
    return pl.pallas_call(
        _nsa_kernel,
        grid=(batch, nq),
        in_specs=[pl.BlockSpec((NSA_HEADS * NSA_DIM, QT), qcol),
                  pl.BlockSpec((1, HK, ncp, NSA_DIM), lambda b, qi: (b, 0, 0, 0)),
                  pl.BlockSpec((1, HK, NSA_DIM, ncp), lambda b, qi: (b, 0, 0, 0)),
                  pl.BlockSpec((seq, HK * LANE), lambda b, qi: (b, 0)),
                  pl.BlockSpec((HK * VROWS, seq), lambda b, qi: (0, b)),
                  pl.BlockSpec((seq, HK * LANE), lambda b, qi: (b, 1)),
                  pl.BlockSpec((HK * VROWS, seq), lambda b, qi: (1, b)),
                  pl.BlockSpec((HK * GATE_ROWS, QT), qcol),
                  _resident((NSA_HEADS, 2 * ncp, QT)),
                  _resident((HK, 3, KP, G * QT)),
                  _resident((SLC_PAD, ncp))],
        out_specs=pl.BlockSpec((QT, NSA_HEADS * NSA_DIM), lambda b, qi: (b * nq + qi, 0)),
        out_shape=jax.ShapeDtypeStruct((n, NSA_HEADS * NSA_DIM), BF16),
        scratch_shapes=[pltpu.VMEM((SLC_PAD, QT), F32),
                        pltpu.VMEM((VROWS, G * QT), F32),
                        pltpu.VMEM((1, G * QT), F32),
                        pltpu.VMEM((4 * KP, G * QT), F32),
                        pltpu.VMEM((4 * KP, G * QT), F32)],
        compiler_params=_params("parallel", "arbitrary"),
        name="nsa_attention",
    )(zq_t, kc, vc_t, nk, nv_t, nk, nv_t, gates_t, pat, utab, _overlap_t(seq))


MLA_TQ = 512
MLA_HP = 8


def _mla_kernel(q_ref, k_ref, v_ref, o_ref, acc_ref, m_ref, sa_ref, sb_ref):
    qi = pl.program_id(2)
    tq = MLA_TQ
    HP = MLA_HP
    qs = [q_ref[hh * LANE:(hh + 1) * LANE, :] for hh in range(HP)]
    m_ref[...] = jnp.full(m_ref.shape, NEG, F32)
    acc_ref[...] = jnp.zeros(acc_ref.shape, F32)

    def qk(dst_ref, off, hh):
        dst_ref[hh] = _dot(k_ref[pl.ds(off, tq), hh * LANE:(hh + 1) * LANE], qs[hh])

    def update(hh, s, vt, lanes):
        m_old = m_ref[hh, :, lanes]
        m_new = jnp.maximum(m_old, jnp.max(s, axis=0, keepdims=True))
        p = jnp.exp2(s - m_new).astype(BF16)
        acc_ref[hh, :, lanes] = acc_ref[hh, :, lanes] * jnp.exp2(m_old - m_new) + _dot(vt, p)
        m_ref[hh, :, lanes] = m_new

    def consume(src_ref, off, mask, hh):
        vrows = slice(hh * VROWS, (hh + 1) * VROWS)
        if mask is None:
            update(hh, src_ref[hh], v_ref[vrows, pl.ds(off, tq)], slice(0, tq))
            return
        hq = tq // 2
        update(hh, src_ref[hh, :hq, :hq] + mask, v_ref[vrows, pl.ds(off, hq)], slice(0, hq))
        s_b = src_ref[hh, :, hq:]
        s_b = jnp.concatenate([s_b[:hq], s_b[hq:] + mask], axis=0)
        update(hh, s_b, v_ref[vrows, pl.ds(off, tq)], slice(hq, tq))

    def stage(nxt_ref, nxt_off, cur_ref, cur_off, mask):
        for hh in range(HP):
            if nxt_ref is not None:
                qk(nxt_ref, nxt_off, hh)
            consume(cur_ref, cur_off, mask, hh)

    key_j = lax.broadcasted_iota(jnp.int32, (tq // 2, tq // 2), 0)
    qry_i = lax.broadcasted_iota(jnp.int32, (tq // 2, tq // 2), 1)
    causal = jnp.where(key_j <= qry_i, 0.0, NEG)
    for hh in range(HP):
        qk(sa_ref, 0, hh)

    def pair(u, carry):
        off = pl.multiple_of(u * (2 * tq), 2 * tq)
        stage(sb_ref, off + tq, sa_ref, off, None)
        stage(sa_ref, off + 2 * tq, sb_ref, off + tq, None)
        return carry

    lax.fori_loop(0, qi // 2, pair, 0)
    base = pl.multiple_of((qi // 2) * (2 * tq), 2 * tq)

    @pl.when(qi % 2 == 0)
    def _():
        stage(None, None, sa_ref, base, causal)

    @pl.when(qi % 2 == 1)
    def _():
        stage(sb_ref, base + tq, sa_ref, base, None)
        stage(None, None, sb_ref, base + tq, causal)

    outs = []
    for hh in range(HP):
        a = acc_ref[hh]
        outs.append((a[:MLA_V] / a[MLA_V:MLA_V + 1]).T)
    o_ref[...] = jnp.concatenate(outs, axis=1).astype(BF16)


def _mla(qm_t, km, vm_t, *, batch, seq):
    n = km.shape[0]
    tq = MLA_TQ
    nq = seq // tq
    HP = MLA_HP
    return pl.pallas_call(
        _mla_kernel,
        grid=(batch, MLA_HEADS // HP, nq),
        in_specs=[pl.BlockSpec((HP * LANE, tq), lambda b, hp, qi: (hp, b * nq + qi)),
                  pl.BlockSpec((seq, HP * LANE), lambda b, hp, qi: (b, hp)),
                  pl.BlockSpec((HP * VROWS, seq), lambda b, hp, qi: (hp, b))],
        out_specs=pl.BlockSpec((tq, HP * MLA_V), lambda b, hp, qi: (b * nq + qi, hp)),
        out_shape=jax.ShapeDtypeStruct((n, MLA_HEADS * MLA_V), BF16),
        scratch_shapes=[pltpu.VMEM((HP, VROWS, tq), F32),
                        pltpu.VMEM((HP, 1, tq), F32),
                        pltpu.VMEM((HP, tq, tq), F32),
                        pltpu.VMEM((HP, tq, tq), F32)],
        compiler_params=_params("parallel", "parallel", "arbitrary"),
        name="mla_attention",
    )(qm_t, km, vm_t)


def _tail_kernel(x_ref, om_ref, on_ref, p_ref, wo_ref, g1_ref, g2_ref, g3_ref,
                 wg_ref, wu_ref, cw_ref, cb_ref, wd_ref, pg_ref, pp_ref,
                 o_ref, carry_ref, act_ref, *, seq, tm):
    n_chunks = D_FF // FF_CHUNK
    half = om_ref.shape[1]
    y = _dot(jnp.concatenate([om_ref[...], on_ref[...]], axis=1), wo_ref[...])
    x = x_ref[...] + _rms(y, g1_ref[...])

    h = _rms(x, g2_ref[...]).astype(BF16)

    @pl.when((pl.program_id(0) * tm) % seq == 0)
    def _():
        carry_ref[...] = jnp.zeros(carry_ref.shape, F32)

    SUB = 8
    row8 = lax.broadcasted_iota(jnp.int32, (SUB, 1), 0)
    for c in range(n_chunks):
        cols = slice(c * FF_CHUNK, (c + 1) * FF_CHUNK)
        g = _dot(h, wg_ref[:, cols])
        prev = carry_ref[:, cols]
        carry_ref[:, cols] = g[tm - SUB:, :]
        r1 = pltpu.roll(g, 1, 0)
        r2 = pltpu.roll(g, 2, 0)
        top1 = jnp.where(row8 == 0, prev[7:8, :], r1[:SUB])
        top2 = jnp.where(row8 == 0, prev[6:7, :], jnp.where(row8 == 1, prev[7:8, :], r2[:SUB]))
        g1 = jnp.concatenate([top1, r1[SUB:]], axis=0)
        g2 = jnp.concatenate([top2, r2[SUB:]], axis=0)
        conv = (cw_ref[0:1, cols] * g2 + cw_ref[1:2, cols] * g1 + cw_ref[2:3, cols] * g
                + cb_ref[:, cols])
        act_ref[:, cols] = (_gelu_tanh(conv) * _dot(h, wu_ref[:, cols])).astype(BF16)
    x = x + _rms(_dot(act_ref[...], wd_ref[...]), g3_ref[...])

    gate = _sigmoid(_dot(x.astype(BF16), pg_ref[...]))
    o_ref[...] = x + gate * _dot(p_ref[...].astype(BF16), pp_ref[...])


def _tail(x2, om, on, p2, layer, wo, g1, g2, g3, wg, wu, cw, cb, wd, pg, pp, *, seq):
    n = x2.shape[0]
    tm = ROW_TILE
    row = lambda r: (r, 0)
    half = om.shape[1]
    lr = functools.partial(_layer_resident, layer)
    return pl.pallas_call(
        functools.partial(_tail_kernel, seq=seq, tm=tm),
        grid=(n // tm,),
        in_specs=[pl.BlockSpec((tm, D_MODEL), row),
                  pl.BlockSpec((tm, half), row),
                  pl.BlockSpec((tm, half), row),
                  pl.BlockSpec((None, tm, PLE_DIM), lambda r: (layer, r, 0)),
                  lr((2 * half, D_MODEL)),
                  lr((1, D_MODEL)),
                  lr((1, D_MODEL)),
                  lr((1, D_MODEL)),
                  lr((D_MODEL, D_FF)),
                  lr((D_MODEL, D_FF)),
                  lr((CONV_WIDTH, D_FF)),
                  lr((1, D_FF)),
                  lr((D_FF, D_MODEL)),
                  lr((D_MODEL, D_MODEL)),
                  lr((PLE_DIM, D_MODEL))],
        out_specs=pl.BlockSpec((tm, D_MODEL), row),
        out_shape=jax.ShapeDtypeStruct((n, D_MODEL), F32),
        scratch_shapes=[pltpu.VMEM((8, D_FF), F32),
                        pltpu.VMEM((tm, D_FF), BF16)],
        compiler_params=_params("arbitrary"),
        name="layer_tail",
    )(x2, om, on, p2, wo, g1, g2, g3, wg, wu, cw, cb, wd, pg, pp)


def _rot_rows(w):
    half = w.shape[-2] // 2
    return jnp.concatenate([-w[..., half:, :], w[..., :half, :]], axis=-2)


def _prep_inproj(w):
    wt = w.T
    o = np.cumsum((0,) + IN_SPLITS)
    cq, ckv, kr, qn, kcmp, vcmp, kslc, vslc, kwin, vwin, gn = [wt[o[j]:o[j + 1]] for j in range(len(IN_SPLITS))]
    d = w.shape[0]

    def zeros(rows):
        return jnp.zeros((rows, d), F32)

    def kslots(m):
        return jnp.concatenate([m[:NSA_DIM], zeros(LANE - NSA_DIM), m[NSA_DIM:], zeros(LANE - NSA_DIM)], axis=0)

    def vslots(m):
        return jnp.concatenate([m[:NSA_DIM], zeros(VROWS - NSA_DIM), m[NSA_DIM:], zeros(VROWS - NSA_DIM)], axis=0)

    pad_rope = zeros(LANE - MLA_NOPE - MLA_ROPE)
    w1 = jnp.concatenate([cq, ckv,
                          zeros(MLA_NOPE), kr, pad_rope,
                          zeros(MLA_NOPE), _rot_rows(kr), pad_rope,
                          kslots(kslc), kslots(kwin), kcmp, vcmp], axis=0)
    assert w1.shape[0] == C_TOTAL
    per = 3 * NSA_GROUP
    w1t = jnp.concatenate([qn * NSA_DIM ** -0.5, vslots(vslc), vslots(vwin),
                           gn[:per], zeros(GATE_ROWS - per), gn[per:], zeros(GATE_ROWS - per)], axis=0)
    assert w1t.shape[0] == R_TOTAL
    return w1.astype(BF16).T, w1t.astype(BF16)


def _prep_mla(w_uq, w_ukv):
    r = w_uq.shape[0]
    dq = MLA_NOPE + MLA_ROPE
    ut = w_uq.T.reshape(MLA_HEADS, dq, r)
    pad = jnp.zeros((MLA_HEADS, LANE - dq, r), F32)
    wqa_t = jnp.concatenate([ut, pad], axis=1).reshape(MLA_HEADS * LANE, r)
    wqb_t = _rot_rows(ut[:, MLA_NOPE:]).reshape(MLA_HEADS * MLA_ROPE, r)
    rk = w_ukv.shape[0]
    kv = w_ukv.reshape(rk, MLA_HEADS, 2, MLA_NOPE)
    wk = jnp.concatenate([kv[:, :, 0, :], jnp.zeros((rk, MLA_HEADS, LANE - MLA_NOPE), F32)],
                         axis=-1).reshape(rk, MLA_HEADS * LANE)
    vt = kv[:, :, 1, :].transpose(1, 2, 0)
    wvt = jnp.concatenate([vt, jnp.zeros((MLA_HEADS, VROWS - MLA_V, rk), F32)], axis=1).reshape(MLA_HEADS * VROWS, rk)
    return wqa_t.astype(BF16), wqb_t.astype(BF16), wk.astype(BF16), wvt.astype(BF16)


def kernel(x, p, positions, rel_bias, attn_pre_norm, attn_post_norm, ffn_pre_norm, ffn_post_norm,
           w_in, mla_q_norm, mla_w_uq, mla_kv_norm, mla_w_ukv, nsa_cmp_pos, nsa_cmp_w1, nsa_cmp_w2,
           w_o, ffn_w_gate, ffn_w_up, ffn_conv_w, ffn_conv_b, ffn_w_down, ple_proj, ple_gate):
    batch, seq, d = x.shape
    depth = w_in.shape[0]
    n = batch * seq
    x2 = x.reshape(n, d)
    ctt, stt, ct, st = _rope_tables(positions)
    pat, utab = _bias_tables(rel_bias, seq)
    tail_params = (w_o.astype(BF16), attn_post_norm[:, None, :], ffn_pre_norm[:, None, :],
                   ffn_post_norm[:, None, :], ffn_w_gate.astype(BF16), ffn_w_up.astype(BF16),
                   ffn_conv_w, ffn_conv_b[:, None, :], ffn_w_down.astype(BF16),
                   ple_gate.astype(BF16), ple_proj.astype(BF16))
    for i in range(depth):
        w1, w1t = _prep_inproj(w_in[i])
        wqa, wqb, wk, wvt = _prep_mla(mla_w_uq[i], mla_w_ukv[i])
        qm_t, km, vm_t, zq_t, nk, nv_t, kcmp, vcmp, gates_t = _inproj(
            x2, attn_pre_norm[i][None, :], w1, w1t, mla_q_norm[i][None, :], wqa, wqb,
            mla_kv_norm[i][None, :], wk, wvt, ct, st, ctt, stt, batch=batch, seq=seq)
        kc, vc_t = _compress(kcmp, vcmp,
                             nsa_cmp_pos[i],
                             nsa_cmp_w1[i].astype(BF16), nsa_cmp_w2[i, 0].astype(BF16),
                             nsa_cmp_w2[i, 1].T.astype(BF16), batch=batch, seq=seq)
        o_nsa = _nsa(zq_t, kc, vc_t, nk, nv_t, gates_t, pat, utab, batch=batch, seq=seq)
        o_mla = _mla(qm_t, km, vm_t, batch=batch, seq=seq)
        x2 = _tail(x2, o_mla, o_nsa, p.reshape(depth, n, PLE_DIM), i, *tail_params, seq=seq)
    return x2.reshape(batch, seq, d)
```
